```python
import math
import jax
import jax.numpy as jnp
from jax import lax
import numpy as np

D_MODEL = 1024
BATCH = 8
SEQ = 2048
DEPTH = 4
DEC_BATCH = 32
DEC_SEQ = 4
PAST_LEN = 8192
PAGE_SIZE = 128

N_MIXERS = 3
N_NSA = (DEPTH + 2) // 3
N_RWKV = (DEPTH + 1) // 3
N_HGRN = DEPTH // 3
D_FF = 4 * D_MODEL
D_PLE = 256
NORM_EPS = 1e-6
ROPE_THETA = 10000.0
NEG_INF = -1e30

NSA_HEADS = 16
NSA_KV_HEADS = 4
NSA_HEAD_DIM = D_MODEL // NSA_HEADS
NSA_GROUP = NSA_HEADS // NSA_KV_HEADS
NSA_KV_WIDTH = NSA_KV_HEADS * NSA_HEAD_DIM
NSA_IN_WIDTH = NSA_HEADS * NSA_HEAD_DIM + 6 * NSA_KV_WIDTH + 3 * NSA_HEADS
CMP_BLOCK = 32
CMP_HIDDEN = 4 * NSA_HEAD_DIM
SEL_BLOCK = 64
N_SELECT = 16
WINDOW = 512
NSA_Q_BLOCK = 16
FORCE_SCORE = 1e4

RWKV_HEAD_DIM = 64
RWKV_HEADS = D_MODEL // RWKV_HEAD_DIM
DECAY_LORA = 64
AAA_LORA = 64
GATE_LORA = 160
LNX_EPS = 64e-5

HGRN_EXPAND = 128
HGRN_HEADS = D_MODEL // HGRN_EXPAND
HGRN_HEAD_DIM = D_MODEL // HGRN_HEADS
HGRN_CHUNK = 32

kernel_name = 'hybrid_nsa_rwkv7_hgrn2_step'


def rms(x):
    xf = x.astype(jnp.float32)
    return (xf * lax.rsqrt(jnp.mean(xf * xf, axis=-1, keepdims=True) + NORM_EPS)).astype(x.dtype)


def rmsnorm(x, g):
    return rms(x) * g


def rope(x, pos):
    half = x.shape[-1] // 2
    inv = ROPE_THETA ** (-jnp.arange(half, dtype=jnp.float32) / half)
    ang = pos.astype(jnp.float32)[:, None] * inv[None, :]
    cos = jnp.cos(ang)[None, :, None, :]
    sin = jnp.sin(ang)[None, :, None, :]
    xf = x.astype(jnp.float32)
    x1, x2 = xf[..., :half], xf[..., half:]
    return jnp.concatenate([x1 * cos - x2 * sin, x2 * cos + x1 * sin], axis=-1).astype(x.dtype)


def gather_pages(pool, page_table):
    g = pool[page_table]
    return g.reshape((page_table.shape[0], page_table.shape[1] * pool.shape[1]) + pool.shape[2:])


def nsa_project(u, pos, w_in, g_q, g_ks, g_kw):
    B, T, _ = u.shape
    sizes = [NSA_HEADS * NSA_HEAD_DIM] + [NSA_KV_WIDTH] * 6
    q, kc, vc, ks, vs, kw, vw, gl = jnp.split(u @ w_in, np.cumsum(sizes).tolist(), axis=-1)
    heads = lambda t: t.reshape(B, T, NSA_KV_HEADS, NSA_HEAD_DIM)
    q = rmsnorm(q.reshape(B, T, NSA_HEADS, NSA_HEAD_DIM), g_q)
    ks = rope(rmsnorm(heads(ks), g_ks), pos)
    kw = rope(rmsnorm(heads(kw), g_kw), pos)
    cmp_rows = jnp.stack([heads(kc), heads(vc)], axis=2)
    slc_rows = jnp.stack([ks, heads(vs)], axis=2)
    win_rows = jnp.stack([kw, heads(vw)], axis=2)
    gate = jax.nn.sigmoid(gl.reshape(B, T, NSA_HEADS, 3))
    return q, cmp_rows, slc_rows, win_rows, gate


def nsa_compress(cmp_rows, pe, w1, w2, g_kc):
    B, T = cmp_rows.shape[:2]
    nc = T // CMP_BLOCK
    blk = cmp_rows[:, :nc * CMP_BLOCK].reshape(B, nc, CMP_BLOCK, 2, NSA_KV_HEADS, NSA_HEAD_DIM)
    blk = blk + jnp.transpose(pe, (1, 0, 2))[None, None, :, :, None, :]
    flat = jnp.transpose(blk, (0, 1, 3, 4, 2, 5)).reshape(B, nc, 2, NSA_KV_HEADS, CMP_BLOCK * NSA_HEAD_DIM)
    hid = jax.nn.gelu(jnp.einsum('bnekf,efm->bnekm', flat, w1))
    out = jnp.einsum('bnekm,emd->bnekd', hid, w2)
    return rmsnorm(out[:, :, 0], g_kc), out[:, :, 1]


def sel_blocks(slc_rows):
    B, T = slc_rows.shape[:2]
    ns = -(-T // SEL_BLOCK)
    rows = jnp.pad(slc_rows, ((0, 0), (0, ns * SEL_BLOCK - T), (0, 0), (0, 0), (0, 0)))
    rows = rows.reshape(B, ns, SEL_BLOCK, 2, NSA_KV_HEADS, NSA_HEAD_DIM)
    return (jnp.transpose(rows[:, :, :, 0], (0, 3, 1, 2, 4)),
            jnp.transpose(rows[:, :, :, 1], (0, 3, 1, 2, 4)))


def nsa_core(q, q_pos, kcmp, vcmp, sel_k, sel_v, win_k, win_v, win_pos, gate):
    B, Tq = q.shape[:2]
    shp = (B, Tq, NSA_KV_HEADS, NSA_GROUP, NSA_HEAD_DIM)
    scale = NSA_HEAD_DIM ** -0.5
    qg = q.reshape(shp)
    qr = rope(q, q_pos).reshape(shp)
    nc = kcmp.shape[1]
    s = jnp.einsum('btghd,bcgd->bghtc', qg, kcmp).astype(jnp.float32) * scale
    ok = ((jnp.arange(nc) + 1) * CMP_BLOCK - 1)[None, :] <= q_pos[:, None]
    p_cmp = jnp.where(ok, jax.nn.softmax(jnp.where(ok, s, NEG_INF), axis=-1), 0.0)
    o_cmp = jnp.einsum('bghtc,bcgd->btghd', p_cmp.astype(q.dtype), vcmp)
    ns = sel_k.shape[2]
    ratio = SEL_BLOCK // CMP_BLOCK
    imp = jnp.sum(p_cmp, axis=2)
    imp = jnp.pad(imp, ((0, 0), (0, 0), (0, 0), (0, ns * ratio - nc)))
    imp = imp.reshape(B, NSA_KV_HEADS, Tq, ns, ratio).sum(-1)
    blk = jnp.arange(ns)[None, :]
    cur = (q_pos // SEL_BLOCK)[:, None]
    forced = (blk == 0) | (blk == cur) | (blk == cur - 1)
    score = jnp.where(blk <= cur, jnp.where(forced, FORCE_SCORE, imp), -1.0)
    n_sel = min(N_SELECT, ns)
    _, sel = lax.top_k(score, n_sel)
    flat_sel = sel.reshape(B, NSA_KV_HEADS, Tq * n_sel)
    take = jax.vmap(jax.vmap(lambda blocks, idx: blocks[idx]))
    n_keys = n_sel * SEL_BLOCK
    kg = take(sel_k, flat_sel).reshape(B, NSA_KV_HEADS, Tq, n_keys, NSA_HEAD_DIM)
    vg = take(sel_v, flat_sel).reshape(B, NSA_KV_HEADS, Tq, n_keys, NSA_HEAD_DIM)
    kpos = (sel[..., None] * SEL_BLOCK + jnp.arange(SEL_BLOCK)).reshape(B, NSA_KV_HEADS, Tq, n_keys)
    s = jnp.einsum('btghd,bgtkd->bghtk', qr, kg).astype(jnp.float32) * scale
    s = jnp.where((kpos <= q_pos[:, None])[:, :, None], s, NEG_INF)
    o_slc = jnp.einsum('bghtk,bgtkd->btghd', jax.nn.softmax(s, axis=-1).astype(q.dtype), vg)
    s = jnp.einsum('btghd,bkgd->bghtk', qr, win_k).astype(jnp.float32) * scale
    dist = q_pos[:, None] - win_pos[None, :]
    okw = (dist >= 0) & (dist <= WINDOW) & (win_pos[None, :] >= 0)
    p_w = jax.nn.softmax(jnp.where(okw, s, NEG_INF), axis=-1)
    o_win = jnp.einsum('bghtk,bkgd->btghd', p_w.astype(q.dtype), win_v)
    g = gate.reshape(B, Tq, NSA_KV_HEADS, NSA_GROUP, 3)
    o = o_cmp * g[..., 0:1] + o_slc * g[..., 1:2] + o_win * g[..., 2:3]
    return o.reshape(B, Tq, NSA_HEADS * NSA_HEAD_DIM)


def nsa_prompt(u, w_in, g_q, g_ks, g_kw, g_kc, pe, w1, w2, w_out):
    B, T, _ = u.shape
    q, cmp_rows, slc_rows, win_rows, gate = nsa_project(u, jnp.arange(T), w_in, g_q, g_ks, g_kw)
    kcmp, vcmp = nsa_compress(cmp_rows, pe, w1, w2, g_kc)
    sel_k, sel_v = sel_blocks(slc_rows)
    pad = ((0, 0), (WINDOW, 0), (0, 0), (0, 0))
    kw_pad = jnp.pad(win_rows[:, :, 0], pad)
    vw_pad = jnp.pad(win_rows[:, :, 1], pad)
    qb = math.gcd(T, NSA_Q_BLOCK)
    nb = T // qb

    def block(i):
        start = i * qb
        sl = lambda t, n: lax.dynamic_slice_in_dim(t, start, n, axis=1)
        q_pos = start + jnp.arange(qb)
        w_pos = start - WINDOW + jnp.arange(qb + WINDOW)
        return nsa_core(sl(q, qb), q_pos, kcmp, vcmp, sel_k, sel_v,
                        sl(kw_pad, qb + WINDOW), sl(vw_pad, qb + WINDOW), w_pos, sl(gate, qb))

    o = lax.map(block, jnp.arange(nb))
    o = jnp.transpose(o, (1, 0, 2, 3)).reshape(B, T, -1)
    keep = min(WINDOW, T)
    return o @ w_out, cmp_rows, slc_rows, win_rows[:, T - keep:]


def nsa_sample(u, past_cmp, past_slc, win_buf, w_in, g_q, g_ks, g_kw, g_kc, pe, w1, w2, w_out):
    B, T, _ = u.shape
    past_len = past_cmp.shape[1]
    q_pos = past_len + jnp.arange(T)
    q, cmp_new, slc_new, win_new, gate = nsa_project(u, q_pos, w_in, g_q, g_ks, g_kw)
    kcmp, vcmp = nsa_compress(jnp.concatenate([past_cmp, cmp_new], axis=1), pe, w1, w2, g_kc)
    sel_k, sel_v = sel_blocks(jnp.concatenate([past_slc, slc_new], axis=1))
    n_buf = win_buf.shape[1]
    win_all = jnp.concatenate([win_buf, win_new], axis=1)
    w_pos = past_len - n_buf + jnp.arange(n_buf + T)
    o = nsa_core(q, q_pos, kcmp, vcmp, sel_k, sel_v, win_all[:, :, 0], win_all[:, :, 1], w_pos, gate)
    return o @ w_out, cmp_new, slc_new, win_all[:, T:]


def rwkv7_scan(S0, r, w, k, v, a, b):
    def step(S, inp):
        rt, wt, kt, vt, at, bt = inp
        sa = jnp.einsum('bhij,bhj->bhi', S, at)
        S = S * wt[:, :, None, :] + sa[..., None] * bt[:, :, None, :] + vt[..., None] * kt[:, :, None, :]
        return S, jnp.einsum('bhij,bhj->bhi', S, rt)
    xs = tuple(jnp.moveaxis(t, 1, 0) for t in (r, w, k, v, a, b))
    S, y = lax.scan(step, S0, xs)
    return S, jnp.moveaxis(y, 0, 1)


def rwkv7_mix(u, shift, S0, mu, w_r, w_k, w_v, w_o, w0, w1, w2, a0, a1, a2, g1, g2, k_k, k_a, r_k, lnx_w, lnx_b):
    B, T, C = u.shape
    hd = lambda t: t.reshape(B, T, RWKV_HEADS, RWKV_HEAD_DIM).astype(jnp.float32)
    xx = jnp.concatenate([shift[:, None].astype(u.dtype), u[:, :-1]], axis=1) - u
    xr, xw, xk, xv, xa, xg = (u + xx * mu[j] for j in range(6))
    r = xr @ w_r
    w_log = -jax.nn.softplus(-(w0 + jnp.tanh(xw @ w1) @ w2)) - 0.5
    k = xk @ w_k
    v = xv @ w_v
    a = jax.nn.sigmoid(a0 + (xa @ a1) @ a2)
    g = jax.nn.sigmoid(xg @ g1) @ g2
    kk = hd(k * k_k)
    kk = kk / jnp.maximum(jnp.sqrt(jnp.sum(kk * kk, axis=-1, keepdims=True)), 1e-12)
    k = k * (1 + (a - 1) * k_a)
    rh, kh, vh = hd(r), hd(k), hd(v)
    decay = jnp.exp(-jnp.exp(hd(w_log)))
    S, y = rwkv7_scan(S0.astype(jnp.float32), rh, decay, kh, vh, -kk, kk * hd(a))
    mean = jnp.mean(y, axis=-1, keepdims=True)
    var = jnp.mean(jnp.square(y - mean), axis=-1, keepdims=True)
    y = ((y - mean) * lax.rsqrt(var + LNX_EPS)).reshape(B, T, C) * lnx_w + lnx_b
    y = y + (jnp.sum(rh * kh * r_k, axis=-1, keepdims=True) * vh).reshape(B, T, C)
    return (y.astype(u.dtype) * g) @ w_o, u[:, -1], S


def hgrn2_scan(S0, q, k, v, log_f):
    B, T, H, DK = q.shape
    DV = v.shape[-1]
    c = math.gcd(T, HGRN_CHUNK)
    n = T // c
    chunks = lambda t: jnp.transpose(t.reshape(B, n, c, H, t.shape[-1]), (1, 0, 3, 2, 4))
    tri = jnp.tril(jnp.ones((c, c), dtype=bool))[:, :, None]

    def step(S, inp):
        qc, kc, vc, gc = inp
        b = jnp.cumsum(gc, axis=2)
        o = jnp.einsum('bhtd,bhdv->bhtv', qc * jnp.exp(b), S)
        rel = b[:, :, :, None, :] - b[:, :, None, :, :]
        dec = jnp.where(tri, jnp.exp(jnp.where(tri, rel, 0.0)), 0.0)
        att = jnp.einsum('bhtd,bhtsd,bhsd->bhts', qc, dec, kc)
        o = o + jnp.einsum('bhts,bhsv->bhtv', att, vc)
        bl = b[:, :, -1:]
        S = jnp.exp(bl[:, :, 0])[..., None] * S + jnp.einsum('bhsd,bhsv->bhdv', kc * jnp.exp(bl - b), vc)
        return S, o

    S, o = lax.scan(step, S0, (chunks(q), chunks(k), chunks(v), chunks(log_f)))
    return S, jnp.transpose(o, (1, 0, 3, 2, 4)).reshape(B, T, H, DV)


def hgrn2_mix(u, S0, lb, w_in, gn, w_o):
    B, T, C = u.shape
    hd = lambda t: t.reshape(B, T, HGRN_HEADS, HGRN_HEAD_DIM).astype(jnp.float32)
    q, f, i, g = jnp.split(u @ w_in, 4, axis=-1)
    lbh = lb.reshape(HGRN_HEADS, HGRN_EXPAND)
    log_f = jnp.logaddexp(jnp.log(lbh), jnp.log1p(-lbh) + jax.nn.log_sigmoid(hd(f)))
    S, o = hgrn2_scan(S0.astype(jnp.float32), jax.nn.silu(hd(q)), -jnp.expm1(log_f), hd(i), log_f)
    o = rms(o) * gn * jax.nn.silu(hd(g))
    return o.reshape(B, T, C).astype(u.dtype) @ w_o, S


def channel_and_ple(h, p, g_ffn, w_up, w_down, w_ple, w_ple_gate):
    u = rmsnorm(h, g_ffn)
    h = h + jnp.square(jax.nn.relu(u @ w_up)) @ w_down
    return h + (p @ w_ple) * jax.nn.sigmoid(rms(h) @ w_ple_gate)


def setup_inputs(seed: int = 0) -> dict:
    key = jax.random.key(seed)
    keys = iter(jax.random.split(key, 64))
    f32 = jnp.float32

    def nrm(shape, scale):
        return jax.random.normal(next(keys), shape, f32) * scale

    def gain(shape, base=1.0):
        return base + 0.02 * jax.random.normal(next(keys), shape, f32)

    D = D_MODEL
    n_pages = PAST_LEN // PAGE_SIZE
    n_used = DEC_BATCH * n_pages
    n_phys = (5 * n_used + 3) // 4
    n_buf = min(WINDOW, PAST_LEN)
    kv_row = (2, NSA_KV_HEADS, NSA_HEAD_DIM)
    page_table = jax.random.permutation(next(keys), n_phys)[:n_used].reshape(DEC_BATCH, n_pages).astype(jnp.int32)
    return {
        'x_prompt': nrm((BATCH, SEQ, D), 1.0),
        'x_sample': nrm((DEC_BATCH, DEC_SEQ, D), 1.0),
        'cache_cmp': nrm((N_NSA, n_phys, PAGE_SIZE) + kv_row, 1.0),
        'cache_slc': nrm((N_NSA, n_phys, PAGE_SIZE) + kv_row, 1.0),
        'cache_win': nrm((N_NSA, DEC_BATCH, n_buf) + kv_row, 1.0),
        'state_rwkv_shift': nrm((N_RWKV, DEC_BATCH, D), 1.0),
        'state_rwkv_wkv': nrm((N_RWKV, DEC_BATCH, RWKV_HEADS, RWKV_HEAD_DIM, RWKV_HEAD_DIM), 0.5),
        'state_hgrn': nrm((N_HGRN, DEC_BATCH, HGRN_HEADS, HGRN_EXPAND, HGRN_HEAD_DIM), 0.5),
        'page_table': page_table,
        'p_prompt': nrm((DEPTH, BATCH, SEQ, D_PLE), 1.0),
        'p_sample': nrm((DEPTH, DEC_BATCH, DEC_SEQ, D_PLE), 1.0),
        'norm_mix': gain((DEPTH, D)),
        'norm_ffn': gain((DEPTH, D)),
        'w_up': nrm((DEPTH, D, D_FF), D ** -0.5),
        'w_down': nrm((DEPTH, D_FF, D), D_FF ** -0.5),
        'w_ple': nrm((DEPTH, D_PLE, D), D_PLE ** -0.5),
        'w_ple_gate': nrm((DEPTH, D, D), D ** -0.5),
        'nsa_w_in': nrm((N_NSA, D, NSA_IN_WIDTH), D ** -0.5),
        'nsa_g_q': gain((N_NSA, NSA_HEAD_DIM)),
        'nsa_g_ks': gain((N_NSA, NSA_HEAD_DIM)),
        'nsa_g_kw': gain((N_NSA, NSA_HEAD_DIM)),
        'nsa_g_kc': gain((N_NSA, NSA_HEAD_DIM)),
        'nsa_cmp_pe': nrm((N_NSA, 2, CMP_BLOCK, NSA_HEAD_DIM), 0.1),
        'nsa_cmp_w1': nrm((N_NSA, 2, CMP_BLOCK * NSA_HEAD_DIM, CMP_HIDDEN), (CMP_BLOCK * NSA_HEAD_DIM) ** -0.5),
        'nsa_cmp_w2': nrm((N_NSA, 2, CMP_HIDDEN, NSA_HEAD_DIM), CMP_HIDDEN ** -0.5),
        'nsa_w_out': nrm((N_NSA, D, D), D ** -0.5),
        'rwkv_mu': jax.random.uniform(next(keys), (N_RWKV, 6, D), f32),
        'rwkv_w_r': nrm((N_RWKV, D, D), D ** -0.5),
        'rwkv_w_k': nrm((N_RWKV, D, D), D ** -0.5),
        'rwkv_w_v': nrm((N_RWKV, D, D), D ** -0.5),
        'rwkv_w_o': nrm((N_RWKV, D, D), D ** -0.5),
        'rwkv_w0': -0.6 + nrm((N_RWKV, D), 0.3),
        'rwkv_w1': nrm((N_RWKV, D, DECAY_LORA), D ** -0.5),
        'rwkv_w2': nrm((N_RWKV, DECAY_LORA, D), DECAY_LORA ** -0.5),
        'rwkv_a0': nrm((N_RWKV, D), 0.1),
        'rwkv_a1': nrm((N_RWKV, D, AAA_LORA), D ** -0.5),
        'rwkv_a2': nrm((N_RWKV, AAA_LORA, D), AAA_LORA ** -0.5),
        'rwkv_g1': nrm((N_RWKV, D, GATE_LORA), D ** -0.5),
        'rwkv_g2': nrm((N_RWKV, GATE_LORA, D), GATE_LORA ** -0.5),
        'rwkv_k_k': gain((N_RWKV, D), 0.85),
        'rwkv_k_a': gain((N_RWKV, D)),
        'rwkv_r_k': nrm((N_RWKV, RWKV_HEADS, RWKV_HEAD_DIM), 0.1),
        'rwkv_lnx_w': gain((N_RWKV, D)),
        'rwkv_lnx_b': nrm((N_RWKV, D), 0.02),
        'hgrn_w_in': nrm((N_HGRN, D, 4 * D), D ** -0.5),
        'hgrn_gn': gain((N_HGRN, HGRN_HEAD_DIM)),
        'hgrn_w_o': nrm((N_HGRN, D, D), D ** -0.5),
        'hgrn_lower_bounds': 1.0 + nrm((DEPTH, D), 0.5),
    }


def reference(x_prompt, x_sample, cache_cmp, cache_slc, cache_win, state_rwkv_shift, state_rwkv_wkv, state_hgrn,
              page_table, p_prompt, p_sample, norm_mix, norm_ffn, w_up, w_down, w_ple, w_ple_gate,
              nsa_w_in, nsa_g_q, nsa_g_ks, nsa_g_kw, nsa_g_kc, nsa_cmp_pe, nsa_cmp_w1, nsa_cmp_w2, nsa_w_out,
              rwkv_mu, rwkv_w_r, rwkv_w_k, rwkv_w_v, rwkv_w_o, rwkv_w0, rwkv_w1, rwkv_w2, rwkv_a0, rwkv_a1,
              rwkv_a2, rwkv_g1, rwkv_g2, rwkv_k_k, rwkv_k_a, rwkv_r_k, rwkv_lnx_w, rwkv_lnx_b,
              hgrn_w_in, hgrn_gn, hgrn_w_o, hgrn_lower_bounds):
    f32 = jnp.float32
    lb_soft = jax.nn.softmax(hgrn_lower_bounds.astype(f32), axis=0)
    lower_bound = jnp.cumsum(lb_soft, axis=0) - lb_soft[0]
    hp, hs = x_prompt, x_sample
    cmp_p, cmp_s, slc_p, slc_s, win_p, win_s = [], [], [], [], [], []
    sh_p, sh_s, wkv_p, wkv_s, hg_p, hg_s = [], [], [], [], [], []
    for i in range(DEPTH):
        kind, n = i % N_MIXERS, i // N_MIXERS
        up = rmsnorm(hp, norm_mix[i])
        us = rmsnorm(hs, norm_mix[i])
        if kind == 0:
            prm = (nsa_w_in[n], nsa_g_q[n], nsa_g_ks[n], nsa_g_kw[n], nsa_g_kc[n],
                   nsa_cmp_pe[n], nsa_cmp_w1[n], nsa_cmp_w2[n], nsa_w_out[n])
            yp, rc, rs, rw = nsa_prompt(up, *prm)
            ys, nc_rows, ns_rows, nw_buf = nsa_sample(us, gather_pages(cache_cmp[n], page_table),
                                                      gather_pages(cache_slc[n], page_table), cache_win[n], *prm)
            cmp_p.append(rc); slc_p.append(rs); win_p.append(rw)
            cmp_s.append(nc_rows); slc_s.append(ns_rows); win_s.append(nw_buf)
        elif kind == 1:
            prm = (rwkv_mu[n], rwkv_w_r[n], rwkv_w_k[n], rwkv_w_v[n], rwkv_w_o[n], rwkv_w0[n], rwkv_w1[n],
                   rwkv_w2[n], rwkv_a0[n], rwkv_a1[n], rwkv_a2[n], rwkv_g1[n], rwkv_g2[n], rwkv_k_k[n],
                   rwkv_k_a[n], rwkv_r_k[n], rwkv_lnx_w[n], rwkv_lnx_b[n])
            s0 = jnp.zeros((up.shape[0], RWKV_HEADS, RWKV_HEAD_DIM, RWKV_HEAD_DIM), f32)
            yp, shp, Sp = rwkv7_mix(up, jnp.zeros_like(up[:, 0]), s0, *prm)
            ys, shs, Ss = rwkv7_mix(us, state_rwkv_shift[n], state_rwkv_wkv[n], *prm)
            sh_p.append(shp); sh_s.append(shs); wkv_p.append(Sp); wkv_s.append(Ss)
        else:
            prm = (hgrn_w_in[n], hgrn_gn[n], hgrn_w_o[n])
            s0 = jnp.zeros((up.shape[0], HGRN_HEADS, HGRN_EXPAND, HGRN_HEAD_DIM), f32)
            yp, Hp = hgrn2_mix(up, s0, lower_bound[i], *prm)
            ys, Hs = hgrn2_mix(us, state_hgrn[n], lower_bound[i], *prm)
            hg_p.append(Hp); hg_s.append(Hs)
        hp = channel_and_ple(hp + yp, p_prompt[i], norm_ffn[i], w_up[i], w_down[i], w_ple[i], w_ple_gate[i])
        hs = channel_and_ple(hs + ys, p_sample[i], norm_ffn[i], w_up[i], w_down[i], w_ple[i], w_ple_gate[i])
    return (hp, hs, jnp.stack(cmp_p), jnp.stack(cmp_s), jnp.stack(slc_p), jnp.stack(slc_s),
            jnp.stack(win_p), jnp.stack(win_s), jnp.stack(sh_p), jnp.stack(sh_s),
            jnp.stack(wkv_p), jnp.stack(wkv_s), jnp.stack(hg_p), jnp.stack(hg_s))
```

```python
import functools
import math

import jax
import jax.numpy as jnp
from jax import lax
from jax.experimental import pallas as pl
from jax.experimental.pallas import tpu as pltpu

F32 = jnp.float32
BF16 = jnp.bfloat16

NORM_EPS = 1e-6
ROPE_THETA = 10000.0
NEG_INF = -1e30
N_MIXERS = 3

NSA_HEADS = 16
NSA_KV_HEADS = 4
NSA_HEAD_DIM = 64
NSA_GROUP = NSA_HEADS // NSA_KV_HEADS
NSA_KV_WIDTH = NSA_KV_HEADS * NSA_HEAD_DIM
CMP_BLOCK = 32
SEL_BLOCK = 64
N_SELECT = 16
WINDOW = 512
FORCE_SCORE = 1e4

RWKV_HEAD_DIM = 64
LNX_EPS = 64e-5
HGRN_HEAD_DIM = 128

LANES = 128
VMEM_LIMIT = 56 * 1024 * 1024


def _cparams(n_axes):
    return pltpu.CompilerParams(dimension_semantics=("arbitrary",) * n_axes,
                                vmem_limit_bytes=VMEM_LIMIT)


def _resident(shape):
    zeros = (0,) * len(shape)
    return pl.BlockSpec(shape, lambda *_: zeros, pipeline_mode=pl.Buffered(1))


def _rms(x):
    return x * lax.rsqrt(jnp.mean(x * x, axis=-1, keepdims=True) + NORM_EPS)


def _dot(a, b):
    return jnp.dot(a, b, preferred_element_type=F32)


def _dot_nt(a, b):
    return lax.dot_general(a, b, (((1,), (1,)), ((), ())), preferred_element_type=F32)


def _per_chunk(fn, x, *rest):
    n = x.shape[-1] // LANES
    outs = [fn(*(a[:, c * LANES:(c + 1) * LANES] for a in (x,) + rest)) for c in range(n)]
    return outs[0] if n == 1 else jnp.concatenate(outs, axis=-1)


def _lane_iota():
    return lax.broadcasted_iota(jnp.int32, (1, LANES), 1)


def _group_sum(x, group):
    lane = _lane_iota()
    k = 1
    while k < group:
        up = pltpu.roll(x, LANES - k, axis=1)
        dn = pltpu.roll(x, k, axis=1)
        x = x + jnp.where((lane & k) == 0, up, dn)
        k *= 2
    return x


def _head_rms(x, head_dim):
    def one(c):
        return c * lax.rsqrt(_group_sum(c * c, head_dim) * (1.0 / head_dim) + NORM_EPS)
    return _per_chunk(one, x)


def _rope(x, cos, sin_signed):
    half = NSA_HEAD_DIM // 2
    lane = _lane_iota()

    def one(c):
        up = pltpu.roll(c, LANES - half, axis=1)
        dn = pltpu.roll(c, half, axis=1)
        rot = jnp.where((lane & half) == 0, up, dn)
        return c * cos + rot * sin_signed
    return _per_chunk(one, x)


def _ffn_kernel(mode, ff_chunk, *refs):
    if mode == "nsa":
        h_ref, o_ref, p_ref, wo_ref, gffn_ref, wup_ref, wdown_ref, wple_ref, wgate_ref, out_ref = refs
        o = o_ref[...]
    elif mode == "rwkv":
        h_ref, o_ref, aux_ref, p_ref, wo_ref, gffn_ref, wup_ref, wdown_ref, wple_ref, wgate_ref, out_ref = refs
        o = o_ref[...] * aux_ref[...]
    else:
        (h_ref, o_ref, aux_ref, gn_ref, p_ref, wo_ref, gffn_ref, wup_ref, wdown_ref, wple_ref, wgate_ref,
         out_ref) = refs
        gn = gn_ref[...]
        o = _per_chunk(lambda c: _rms(c) * gn, o_ref[...]) * aux_ref[...]
    h1 = h_ref[...] + _dot(o.astype(BF16), wo_ref[...])
    u = (_rms(h1) * gffn_ref[...]).astype(BF16)
    d_ff = wup_ref.shape[1]
    acc = jnp.zeros_like(h1)
    for j in range(d_ff // ff_chunk):
        a = _dot(u, wup_ref[:, j * ff_chunk:(j + 1) * ff_chunk])
        a = jnp.square(jnp.maximum(a, 0.0)).astype(BF16)
        acc = acc + _dot(a, wdown_ref[j * ff_chunk:(j + 1) * ff_chunk, :])
    h2 = h1 + acc
    gate = jax.nn.sigmoid(_dot(_rms(h2).astype(BF16), wgate_ref[...]))
    out_ref[...] = h2 + _dot(p_ref[...].astype(BF16), wple_ref[...]) * gate


def _ffn_call(mode, h, o, aux, gn, p, wo, gffn, wup, wdown, wple, wgate):
    m, d = h.shape
    tm = min(m, 256)
    row = lambda w: pl.BlockSpec((tm, w), lambda i: (i, 0))
    args, specs = [h, o], [row(d), row(d)]
    if mode != "nsa":
        args.append(aux)
        specs.append(row(d))
    if mode == "hgrn":
        args.append(gn)
        specs.append(_resident(gn.shape))
    args += [p, wo, gffn, wup, wdown, wple, wgate]
    specs += [row(p.shape[1])] + [_resident(a.shape) for a in (wo, gffn, wup, wdown, wple, wgate)]
    return pl.pallas_call(
        functools.partial(_ffn_kernel, mode, 1024),
        grid=(m // tm,),
        in_specs=specs,
        out_specs=row(d),
        out_shape=jax.ShapeDtypeStruct((m, d), F32),
        compiler_params=_cparams(1),
        name="ffn_" + mode,
    )(*args)


def _nsa_proj_kernel(with_flat, h_ref, gmix_ref, win_ref, gq_ref, gks_ref, gkw_ref, cos_ref, sin_ref, *outs):
    if with_flat:
        qn_ref, qr_ref, cmp_ref, slc_ref, wrow_ref, gate_ref, flat_ref = outs
    else:
        qn_ref, qr_ref, cmp_ref, slc_ref, wrow_ref, gate_ref = outs
    kvw = NSA_KV_WIDTH
    nq = NSA_HEADS * NSA_HEAD_DIM
    u = (_rms(h_ref[...]) * gmix_ref[...]).astype(BF16)
    z = _dot(u, win_ref[...])
    cos = cos_ref[...]
    sin = sin_ref[...]
    qn = _head_rms(z[:, :nq], NSA_HEAD_DIM) * gq_ref[...]
    qn_ref[...] = qn
    qr_ref[...] = _rope(qn, cos, sin)
    kv = z[:, nq:nq + 2 * kvw]
    cmp_ref[...] = kv
    o = nq + 2 * kvw
    slc_ref[:, :kvw] = _rope(_head_rms(z[:, o:o + kvw], NSA_HEAD_DIM) * gks_ref[...], cos, sin)
    slc_ref[:, kvw:] = z[:, o + kvw:o + 2 * kvw]
    o += 2 * kvw
    wrow_ref[:, :kvw] = _rope(_head_rms(z[:, o:o + kvw], NSA_HEAD_DIM) * gkw_ref[...], cos, sin)
    wrow_ref[:, kvw:] = z[:, o + kvw:o + 2 * kvw]
    o += 2 * kvw
    gate_ref[...] = jax.nn.sigmoid(z[:, o:])
    if with_flat:
        tm = kv.shape[0]
        for e in range(2):
            for k in range(NSA_KV_HEADS):
                c = (e * NSA_KV_HEADS + k) * NSA_HEAD_DIM
                flat_ref[0, e, k] = kv[:, c:c + NSA_HEAD_DIM].reshape(tm // CMP_BLOCK, CMP_BLOCK, NSA_HEAD_DIM)


def _nsa_proj_call(h, gmix, win, gq, gks, gkw, cos, sin, seq_len, with_flat):
    m, d = h.shape
    tm = min(m, 256)
    tiles_per_seq = max(seq_len // tm, 1)
    tab_tiles = cos.shape[0] // tm
    row = lambda w: pl.BlockSpec((tm, w), lambda i: (i, 0))
    tab = pl.BlockSpec((tm, LANES), lambda i: (i % tab_tiles, 0))
    nq = NSA_HEADS * NSA_HEAD_DIM
    out_shapes = [jax.ShapeDtypeStruct((m, nq), F32), jax.ShapeDtypeStruct((m, nq), F32)]
    out_shapes += [jax.ShapeDtypeStruct((m, 2 * NSA_KV_WIDTH), F32)] * 3
    out_shapes += [jax.ShapeDtypeStruct((m, 2 * LANES), F32)]
    out_specs = [row(nq), row(nq)] + [row(2 * NSA_KV_WIDTH)] * 3 + [row(2 * LANES)]
    if with_flat:
        nb = tm // CMP_BLOCK
        out_shapes.append(jax.ShapeDtypeStruct(
            (m // seq_len, 2, NSA_KV_HEADS, seq_len // CMP_BLOCK, CMP_BLOCK, NSA_HEAD_DIM), F32))
        out_specs.append(pl.BlockSpec(
            (1, 2, NSA_KV_HEADS, nb, CMP_BLOCK, NSA_HEAD_DIM),
            lambda i: (i // tiles_per_seq, 0, 0, i % tiles_per_seq, 0, 0)))
    return pl.pallas_call(
        functools.partial(_nsa_proj_kernel, with_flat),
        grid=(m // tm,),
        in_specs=[row(d), _resident(gmix.shape), _resident(win.shape), _resident(gq.shape),
                  _resident(gks.shape), _resident(gkw.shape), tab, tab],
        out_specs=out_specs,
        out_shape=out_shapes,
        compiler_params=_cparams(1),
        name="nsa_proj",
    )(h, gmix, win, gq, gks, gkw, cos, sin)


def _compress_kernel(x_ref, pe_ref, w1_ref, w2_ref, gkc_ref, out_ref):
    e = pl.program_id(0)
    x = (x_ref[0, 0] + pe_ref[0]).astype(BF16)
    hid = jax.nn.gelu(_dot(x, w1_ref[0]))
    out = _dot(hid.astype(BF16), w2_ref[0])
    out_ref[0, 0] = jnp.where(e == 0, _rms(out) * gkc_ref[...], out)


def _compress_call(flat, pe, w1, w2, gkc):
    b, _, r, f = flat.shape
    tr = min(r, 256)
    hd = w2.shape[-1]
    return pl.pallas_call(
        _compress_kernel,
        grid=(2, b, r // tr),
        in_specs=[pl.BlockSpec((1, 1, tr, f), lambda e, i, j: (i, e, j, 0)),
                  pl.BlockSpec((1, 1, f), lambda e, i, j: (e, 0, 0)),
                  pl.BlockSpec((1,) + w1.shape[1:], lambda e, i, j: (e, 0, 0)),
                  pl.BlockSpec((1,) + w2.shape[1:], lambda e, i, j: (e, 0, 0)),
                  pl.BlockSpec(gkc.shape, lambda e, i, j: (0, 0))],
        out_specs=pl.BlockSpec((1, 1, tr, hd), lambda e, i, j: (i, e, j, 0)),
        out_shape=jax.ShapeDtypeStruct((b, 2, r, hd), F32),
        compiler_params=_cparams(3),
        name="nsa_compress",
    )(flat, pe, w1, w2, gkc)


def _cmp_branch(q4, kc, vc, tpos, n_tok):
    c = kc.shape[0]
    scale = NSA_HEAD_DIM ** -0.5
    s = _dot_nt(q4.astype(BF16), kc.astype(BF16)) * scale
    cidx = lax.broadcasted_iota(jnp.int32, (1, c), 1)
    ok = ((cidx + 1) * CMP_BLOCK - 1 <= tpos).astype(F32)
    ok4 = jnp.concatenate([ok] * NSA_GROUP, axis=0)
    s = jnp.where(ok4 > 0.5, s, NEG_INF)
    e = jnp.exp(s - jnp.max(s, axis=-1, keepdims=True))
    p = e / jnp.sum(e, axis=-1, keepdims=True) * ok4
    o = _dot(p.astype(BF16), vc.astype(BF16))
    imp = p[:n_tok]
    for hh in range(1, NSA_GROUP):
        imp = imp + p[hh * n_tok:(hh + 1) * n_tok]
    return o, imp


def _select_blocks(imp, tpos, n_blocks):
    c = imp.shape[1]
    lane = lax.broadcasted_iota(jnp.int32, (1, c), 1)

    def pair(x):
        lane1 = _lane_iota()
        return x + jnp.where((lane1 & 1) == 0, pltpu.roll(x, LANES - 1, axis=1), pltpu.roll(x, 1, axis=1))
    imp2 = _per_chunk(pair, imp)
    blk = lane // 2
    cur = tpos // SEL_BLOCK
    forced = (blk == 0) | (blk == cur) | (blk == cur - 1)
    score = jnp.where(blk <= cur, jnp.where(forced, FORCE_SCORE, imp2), -1.0)
    score = jnp.where(blk < n_blocks, score, -2.0)
    rank = jnp.zeros(score.shape, F32)
    for j in range(n_blocks):
        col = score[:, 2 * j:2 * j + 1]
        ahead = (col > score) | ((col == score) & (j < blk))
        rank = rank + ahead.astype(F32)
    return (rank < float(min(N_SELECT, n_blocks))).astype(F32)


def _flash_step(q4, k, v, valid, carry):
    m, l, acc = carry
    scale = NSA_HEAD_DIM ** -0.5
    valid4 = jnp.concatenate([valid] * NSA_GROUP, axis=0)
    s = _dot_nt(q4, k.astype(BF16)) * scale
    s = jnp.where(valid4 > 0.5, s, NEG_INF)
    m_new = jnp.maximum(m, jnp.max(s, axis=-1, keepdims=True))
    alpha = jnp.exp(m - m_new)
    p = jnp.exp(s - m_new) * valid4
    l = alpha * l + jnp.sum(p, axis=-1, keepdims=True)
    acc = alpha * acc + _dot(p.astype(BF16), v.astype(BF16))
    return m_new, l, acc


def _flash_init(rows):
    return (jnp.full((rows, 1), NEG_INF, F32), jnp.zeros((rows, 1), F32),
            jnp.zeros((rows, NSA_HEAD_DIM), F32))


def _expand_sel(sel, k0, tk):
    c = sel.shape[1]
    cid = lax.broadcasted_iota(jnp.int32, (c, tk), 0)
    kblk = (k0 + lax.broadcasted_iota(jnp.int32, (c, tk), 1)) // CMP_BLOCK
    expand = jnp.where(cid == kblk, 1.0, 0.0).astype(BF16)
    return _dot(sel.astype(BF16), expand)


def _attn_prompt_kernel(tq, qn_ref, qr_ref, kvc_ref, sk_ref, sv_ref, wk_ref, wv_ref, gate_ref, o_ref):
    qi = pl.program_id(2)
    hd = NSA_HEAD_DIM
    t0 = qi * tq
    tpos = t0 + lax.broadcasted_iota(jnp.int32, (tq, 1), 0)
    n_sel_blocks = sk_ref.shape[1] // SEL_BLOCK
    gate = gate_ref[0]
    heads = []
    for gl in range(2):
        lo = gl * NSA_GROUP * hd
        stack = lambda ref: jnp.concatenate(
            [ref[0, :, lo + hh * hd:lo + (hh + 1) * hd] for hh in range(NSA_GROUP)], axis=0)
        qn4 = stack(qn_ref)
        qr4 = stack(qr_ref).astype(BF16)
        o_cmp, imp = _cmp_branch(qn4, kvc_ref[0, 0, gl], kvc_ref[0, 1, gl], tpos, tq)
        sel = _select_blocks(imp, tpos, n_sel_blocks)
        ksl = slice(gl * hd, (gl + 1) * hd)

        def slc_step(kt, carry):
            k0 = pl.multiple_of(kt * tq, tq)
            kpos = k0 + lax.broadcasted_iota(jnp.int32, (1, tq), 1)
            valid = jnp.where((_expand_sel(sel, k0, tq) > 0.5) & (kpos <= tpos), 1.0, 0.0)
            return _flash_step(qr4, sk_ref[0, pl.ds(k0, tq), ksl], sv_ref[0, pl.ds(k0, tq), ksl], valid, carry)

        def win_step(kt, carry):
            k0 = pl.multiple_of(kt * tq, tq)
            dist = tpos - (k0 + lax.broadcasted_iota(jnp.int32, (1, tq), 1))
            valid = jnp.where((dist >= 0) & (dist <= WINDOW), 1.0, 0.0)
            return _flash_step(qr4, wk_ref[0, pl.ds(k0, tq), ksl], wv_ref[0, pl.ds(k0, tq), ksl], valid, carry)

        _, l_s, acc_s = lax.fori_loop(0, qi + 1, slc_step, _flash_init(NSA_GROUP * tq))
        _, l_w, acc_w = lax.fori_loop(jnp.maximum(qi - WINDOW // tq, 0), qi + 1, win_step,
                                      _flash_init(NSA_GROUP * tq))
        o_slc = acc_s / l_s
        o_win = acc_w / l_w
        for hh in range(NSA_GROUP):
            rows = slice(hh * tq, (hh + 1) * tq)
            c = (gl * NSA_GROUP + hh) * 3
            heads.append(o_cmp[rows] * gate[:, c:c + 1] + o_slc[rows] * gate[:, c + 1:c + 2]
                         + o_win[rows] * gate[:, c + 2:c + 3])
    o_ref[0] = jnp.concatenate(heads, axis=-1)


def _attn_prompt_call(qn, qr, kvc, slc, win, gate, tq=128):
    b, t, nq = qn.shape
    pair_w = 2 * NSA_GROUP * NSA_HEAD_DIM
    c = kvc.shape[3]
    qspec = pl.BlockSpec((1, tq, pair_w), lambda i, p, j: (i, j, p))
    kspec = pl.BlockSpec((1, t, LANES), lambda i, p, j: (i, 0, p))
    vspec = pl.BlockSpec((1, t, LANES), lambda i, p, j: (i, 0, 2 + p))
    return pl.pallas_call(
        functools.partial(_attn_prompt_kernel, tq),
        grid=(b, 2, t // tq),
        in_specs=[qspec, qspec,
                  pl.BlockSpec((1, 2, 2, c, NSA_HEAD_DIM), lambda i, p, j: (i, 0, p, 0, 0)),
                  kspec, vspec, kspec, vspec,
                  pl.BlockSpec((1, tq, LANES), lambda i, p, j: (i, j, p))],
        out_specs=qspec,
        out_shape=jax.ShapeDtypeStruct((b, t, nq), F32),
        compiler_params=_cparams(3),
        name="nsa_attn_prompt",
    )(qn, qr, kvc, slc, slc, win, win, gate)


def _rope_tables(pos):
    half = NSA_HEAD_DIM // 2
    inv = ROPE_THETA ** (-jnp.arange(half, dtype=F32) / half)
    ang = pos.astype(F32)[:, None] * inv[None, :]
    cos, sin = jnp.cos(ang), jnp.sin(ang)
    reps = LANES // NSA_HEAD_DIM
    return jnp.tile(cos, (1, 2 * reps)), jnp.tile(jnp.concatenate([-sin, sin], axis=1), (1, reps))


def _nsa_params(w_in, g_q, g_ks, g_kw, g_kc, pe, w1, w2):
    d = w_in.shape[0]
    body = NSA_HEADS * NSA_HEAD_DIM + 6 * NSA_KV_WIDTH
    half = NSA_HEADS // 2 * 3
    zpad = jnp.zeros((d, LANES - half), w_in.dtype)
    w_pad = jnp.concatenate([w_in[:, :body], w_in[:, body:body + half], zpad, w_in[:, body + half:], zpad], axis=1)
    return dict(
        w_in=w_pad.astype(BF16),
        g_q=jnp.tile(g_q, NSA_HEADS)[None], g_ks=jnp.tile(g_ks, NSA_KV_HEADS)[None],
        g_kw=jnp.tile(g_kw, NSA_KV_HEADS)[None], g_kc=g_kc[None],
        pe=pe.reshape(2, 1, CMP_BLOCK * NSA_HEAD_DIM), w1=w1.astype(BF16), w2=w2.astype(BF16))


def _kv_rows(x, b, t):
    return x.reshape(b, t, 2, NSA_KV_HEADS, NSA_HEAD_DIM)


def _nsa_prompt_layer(h, gmix, prm):
    b, t, d = h.shape
    cos, sin = _rope_tables(jnp.arange(t))
    qn, qr, cmp, slc, wrow, gate, flat = _nsa_proj_call(
        h.reshape(b * t, d), gmix, prm["w_in"], prm["g_q"], prm["g_ks"], prm["g_kw"], cos, sin, t, True)
    nc = t // CMP_BLOCK
    kvc = _compress_call(flat.reshape(b, 2, NSA_KV_HEADS * nc, CMP_BLOCK * NSA_HEAD_DIM),
                         prm["pe"], prm["w1"], prm["w2"], prm["g_kc"])
    kvc = kvc.reshape(b, 2, NSA_KV_HEADS, nc, NSA_HEAD_DIM)
    c_pad = -nc % LANES
    kvc = jnp.pad(kvc, ((0, 0), (0, 0), (0, 0), (0, c_pad), (0, 0)))
    r3 = lambda x: x.reshape(b, t, x.shape[-1])
    o = _attn_prompt_call(r3(qn), r3(qr), kvc, r3(slc), r3(wrow), r3(gate))
    keep = min(WINDOW, t)
    return (o.reshape(b * t, -1), _kv_rows(cmp, b, t), _kv_rows(slc, b, t), _kv_rows(wrow, b, t)[:, t - keep:])


def _softplus(z):
    return jnp.maximum(z, 0.0) + jnp.log1p(jnp.exp(-jnp.abs(z)))


def _rwkv_proj_kernel(seq_len, tiles_per_seq, h_ref, shift_ref, gmix_ref, mu_ref, wr_ref, wk_ref, wv_ref,
                      w0_ref, w1_ref, w2_ref, a0_ref, a1_ref, a2_ref, g1_ref, g2_ref,
                      r_ref, dec_ref, k_ref, v_ref, a_ref, g_ref, u_ref, carry_ref):
    i = pl.program_id(0)
    u = _rms(h_ref[...]) * gmix_ref[...]
    tm = u.shape[0]
    rowid = lax.broadcasted_iota(jnp.int32, (tm, 1), 0)
    rolled = pltpu.roll(u, 1, axis=0)
    if tiles_per_seq:
        @pl.when(i == 0)
        def _():
            carry_ref[...] = jnp.zeros_like(carry_ref)
        first = jnp.where(i % tiles_per_seq == 0, shift_ref[0], carry_ref[...])
        prev = jnp.where(rowid == 0, first, rolled)
        carry_ref[...] = u[tm - 1:tm]
        u_ref[0] = u[tm - 8:tm]
    else:
        prev = jnp.where(rowid % seq_len == 0, shift_ref[...], rolled)
        u_ref[...] = u
    xx = prev - u
    mix = lambda j: (u + xx * mu_ref[j:j + 1]).astype(BF16)
    r_ref[...] = _dot(mix(0), wr_ref[...])
    wl = w0_ref[...] + _dot(jnp.tanh(_dot(mix(1), w1_ref[...])).astype(BF16), w2_ref[...])
    dec_ref[...] = jnp.exp(-jnp.exp(-_softplus(-wl) - 0.5))
    k_ref[...] = _dot(mix(2), wk_ref[...])
    v_ref[...] = _dot(mix(3), wv_ref[...])
    a_ref[...] = jax.nn.sigmoid(a0_ref[...] + _dot(_dot(mix(4), a1_ref[...]).astype(BF16), a2_ref[...]))
    g_ref[...] = _dot(jax.nn.sigmoid(_dot(mix(5), g1_ref[...])).astype(BF16), g2_ref[...])


def _rwkv_proj_call(h, shift, seq_len, gmix, prm):
    m, d = h.shape
    tm = min(m, 256)
    row = pl.BlockSpec((tm, d), lambda i: (i, 0))
    if seq_len >= tm:
        tiles_per_seq = seq_len // tm
        shift_spec = pl.BlockSpec((1, 1, d), lambda i: (i // tiles_per_seq, 0, 0))
        shift = shift[:, None]
        u_shape, u_spec = jax.ShapeDtypeStruct((m // tm, 8, d), F32), pl.BlockSpec((1, 8, d), lambda i: (i, 0, 0))
    else:
        tiles_per_seq = 0
        shift_spec = row
        shift = jnp.pad(shift[:, None], ((0, 0), (0, seq_len - 1), (0, 0))).reshape(m, d)
        u_shape, u_spec = jax.ShapeDtypeStruct((m, d), F32), row
    weights = [prm[n] for n in ("mu", "w_r", "w_k", "w_v", "w0", "w1", "w2", "a0", "a1", "a2", "g1", "g2")]
    outs = pl.pallas_call(
        functools.partial(_rwkv_proj_kernel, seq_len, tiles_per_seq),
        grid=(m // tm,),
        in_specs=[row, shift_spec, _resident(gmix.shape)] + [_resident(w.shape) for w in weights],
        out_specs=[row] * 6 + [u_spec],
        out_shape=[jax.ShapeDtypeStruct((m, d), F32)] * 6 + [u_shape],
        scratch_shapes=[pltpu.VMEM((1, d), F32)],
        compiler_params=_cparams(1),
        name="rwkv_proj",
    )(h, shift, gmix, *weights)
    u = outs[6]
    if tiles_per_seq:
        new_shift = u[tiles_per_seq - 1::tiles_per_seq, 7]
    else:
        new_shift = u[seq_len - 1::seq_len]
    return outs[:6], new_shift


def _rwkv_scan_kernel(tc, r_ref, dec_ref, k_ref, v_ref, a_ref, kk_ref, ka_ref, rk_ref, lw_ref, lb_ref, s0_ref,
                      y_ref, sout_ref, state_ref, vec_ref):
    c = pl.program_id(1)
    n = RWKV_HEAD_DIM

    @pl.when(c == 0)
    def _():
        state_ref[...] = s0_ref[...]

    def step(t, _):
        r = r_ref[t]
        k = k_ref[t]
        v = v_ref[t]
        a = a_ref[t]
        kk = k * kk_ref[...]
        kk = kk / jnp.maximum(jnp.sqrt(jnp.sum(kk * kk, axis=0, keepdims=True)), 1e-12)
        k2 = k * (1.0 + (a - 1.0) * ka_ref[...])
        vec_ref[0] = -kk
        vec_ref[1] = kk * a
        vec_ref[2] = dec_ref[t]
        vec_ref[3] = k2
        vec_ref[4] = r
        sa = jnp.zeros((n, LANES), F32)
        for j in range(n):
            sa = sa + state_ref[j] * vec_ref[0, j:j + 1, :]
        y = jnp.zeros((n, LANES), F32)
        for j in range(n):
            s_new = (state_ref[j] * vec_ref[2, j:j + 1, :] + sa * vec_ref[1, j:j + 1, :]
                     + v * vec_ref[3, j:j + 1, :])
            state_ref[j] = s_new
            y = y + s_new * vec_ref[4, j:j + 1, :]
        mean = jnp.mean(y, axis=0, keepdims=True)
        var = jnp.mean(jnp.square(y - mean), axis=0, keepdims=True)
        y = (y - mean) * lax.rsqrt(var + LNX_EPS) * lw_ref[...] + lb_ref[...]
        y_ref[t] = y + jnp.sum(r * k2 * rk_ref[...], axis=0, keepdims=True) * v
        return 0

    lax.fori_loop(0, tc, step, 0)

    @pl.when(c == pl.num_programs(1) - 1)
    def _():
        sout_ref[...] = state_ref[...]


def _rwkv_scan_call(r, dec, k, v, a, kk, ka, rk, lw, lb, s0):
    t, n, l = r.shape
    tc = math.gcd(t, 64)
    seq = pl.BlockSpec((tc, n, LANES), lambda g, c: (c, 0, g))
    par = pl.BlockSpec((n, LANES), lambda g, c: (0, g))
    st = pl.BlockSpec((n, n, LANES), lambda g, c: (0, 0, g))
    return pl.pallas_call(
        functools.partial(_rwkv_scan_kernel, tc),
        grid=(l // LANES, t // tc),
        in_specs=[seq] * 5 + [par] * 5 + [st],
        out_specs=[seq, st],
        out_shape=[jax.ShapeDtypeStruct((t, n, l), F32), jax.ShapeDtypeStruct((n, n, l), F32)],
        scratch_shapes=[pltpu.VMEM((n, n, LANES), F32), pltpu.VMEM((5, n, LANES), F32)],
        compiler_params=_cparams(2),
        name="rwkv_scan",
    )(r, dec, k, v, a, kk, ka, rk, lw, lb, s0)


def _rwkv_params(mu, w_r, w_k, w_v, w0, w1, w2, a0, a1, a2, g1, g2):
    gpad = -g1.shape[1] % LANES
    return dict(mu=mu, w_r=w_r.astype(BF16), w_k=w_k.astype(BF16), w_v=w_v.astype(BF16), w0=w0[None],
                w1=w1.astype(BF16), w2=w2.astype(BF16), a0=a0[None], a1=a1.astype(BF16), a2=a2.astype(BF16),
                g1=jnp.pad(g1, ((0, 0), (0, gpad))).astype(BF16), g2=jnp.pad(g2, ((0, gpad), (0, 0))).astype(BF16))


def _rwkv_layer(h, shift, s0, gmix, prm, k_k, k_a, r_k, lnx_w, lnx_b):
    b, t, d = h.shape
    n = RWKV_HEAD_DIM
    nh = d // n
    (r, dec, k, v, a, g), new_shift = _rwkv_proj_call(h.reshape(b * t, d), shift, t, gmix, prm)
    lanes = lambda x: x.reshape(b, t, nh, n).transpose(1, 3, 0, 2).reshape(t, n, b * nh)
    par = lambda x: jnp.tile(x.reshape(nh, n).T, (1, b))
    y, s = _rwkv_scan_call(lanes(r), lanes(dec), lanes(k), lanes(v), lanes(a), par(k_k), par(k_a), par(r_k),
                           par(lnx_w), par(lnx_b), s0.transpose(3, 2, 0, 1).reshape(n, n, b * nh))
    y = y.reshape(t, n, b, nh).transpose(2, 0, 3, 1).reshape(b * t, d)
    s = s.reshape(n, n, b, nh).transpose(2, 3, 1, 0)
    return y, g, new_shift, s


def _hgrn_proj_kernel(h_ref, gmix_ref, win_ref, lb_ref, omlb_ref, q_ref, f_ref, k_ref, i_ref, g_ref):
    d = h_ref.shape[1]
    u = (_rms(h_ref[...]) * gmix_ref[...]).astype(BF16)
    z = _dot(u, win_ref[...])
    q, f, g = z[:, :d], z[:, d:2 * d], z[:, 3 * d:]
    q_ref[...] = q * jax.nn.sigmoid(q)
    f_ref[...] = lb_ref[...] + omlb_ref[...] * jax.nn.sigmoid(f)
    k_ref[...] = omlb_ref[...] * jax.nn.sigmoid(-f)
    i_ref[...] = z[:, 2 * d:3 * d]
    g_ref[...] = g * jax.nn.sigmoid(g)


def _hgrn_proj_call(h, gmix, win, lb, omlb):
    m, d = h.shape
    tm = min(m, 256)
    row = pl.BlockSpec((tm, d), lambda i: (i, 0))
    return pl.pallas_call(
        _hgrn_proj_kernel,
        grid=(m // tm,),
        in_specs=[row, _resident(gmix.shape), _resident(win.shape), _resident(lb.shape), _resident(omlb.shape)],
        out_specs=[row] * 5,
        out_shape=[jax.ShapeDtypeStruct((m, d), F32)] * 5,
        compiler_params=_cparams(1),
        name="hgrn_proj",
    )(h, gmix, win, lb, omlb)


def _hgrn_scan_kernel(tc, q_ref, f_ref, k_ref, v_ref, s0_ref, o_ref, sout_ref, state_ref):
    c = pl.program_id(1)
    dk = state_ref.shape[0]

    @pl.when(c == 0)
    def _():
        state_ref[...] = s0_ref[...]

    def step(t, _):
        v = v_ref[t]
        o = jnp.zeros(v.shape, F32)
        for d in range(dk):
            s_new = state_ref[d] * f_ref[t, d:d + 1, :] + k_ref[t, d:d + 1, :] * v
            state_ref[d] = s_new
            o = o + s_new * q_ref[t, d:d + 1, :]
        o_ref[t] = o
        return 0

    lax.fori_loop(0, tc, step, 0)

    @pl.when(c == pl.num_programs(1) - 1)
    def _():
        sout_ref[...] = state_ref[...]


def _hgrn_scan_call(q, f, k, v, s0):
    t, dk, l = q.shape
    dv = v.shape[1]
    tc = math.gcd(t, 32)
    kspec = pl.BlockSpec((tc, dk, LANES), lambda g, c: (c, 0, g))
    vspec = pl.BlockSpec((tc, dv, LANES), lambda g, c: (c, 0, g))
    st = pl.BlockSpec((dk, dv, LANES), lambda g, c: (0, 0, g))
    return pl.pallas_call(
        functools.partial(_hgrn_scan_kernel, tc),
        grid=(l // LANES, t // tc),
        in_specs=[kspec] * 3 + [vspec, st],
        out_specs=[vspec, st],
        out_shape=[jax.ShapeDtypeStruct((t, dv, l), F32), jax.ShapeDtypeStruct((dk, dv, l), F32)],
        scratch_shapes=[pltpu.VMEM((dk, dv, LANES), F32)],
        compiler_params=_cparams(2),
        name="hgrn_scan",
    )(q, f, k, v, s0)


def _hgrn_layer(h, s0, gmix, win, lb, omlb):
    b, t, d = h.shape
    n = HGRN_HEAD_DIM
    nh = d // n
    vs = 2
    q, f, k, i, g = _hgrn_proj_call(h.reshape(b * t, d), gmix, win, lb, omlb)
    keys = lambda x: jnp.tile(x.reshape(b, t, nh, n).transpose(1, 3, 0, 2).reshape(t, n, b * nh), (1, 1, vs))
    vals = i.reshape(b, t, nh, vs, n // vs).transpose(1, 4, 3, 0, 2).reshape(t, n // vs, vs * b * nh)
    st = s0.reshape(b, nh, n, vs, n // vs).transpose(2, 4, 3, 0, 1).reshape(n, n // vs, vs * b * nh)
    o, s = _hgrn_scan_call(keys(q), keys(f), keys(k), vals, st)
    o = o.reshape(t, n // vs, vs, b, nh).transpose(3, 0, 4, 2, 1).reshape(b * t, d)
    s = s.reshape(n, n // vs, vs, b, nh).transpose(3, 4, 0, 2, 1).reshape(b, nh, n, n)
    return o, g, s


PAGES_PER_STEP = 8


def _page_specs(n, page_shape):
    return [pl.BlockSpec((1,) + page_shape, functools.partial(lambda i, b, c, pt: (pt[b, c * n + i], 0, 0), i))
            for i in range(n)]


def _gather_flat_kernel(pt_ref, *refs):
    pages, out_ref = refs[:-1], refs[-1]
    per_page = pages[0].shape[1] // CMP_BLOCK
    for i, pg in enumerate(pages):
        x = pg[0]
        for e in range(2):
            for k in range(NSA_KV_HEADS):
                c = (e * NSA_KV_HEADS + k) * NSA_HEAD_DIM
                out_ref[0, e, k, i * per_page:(i + 1) * per_page] = (
                    x[:, c:c + NSA_HEAD_DIM].reshape(per_page, CMP_BLOCK, NSA_HEAD_DIM))


def _gather_flat_call(pool, page_table):
    b, n_pages = page_table.shape
    page = pool.shape[1]
    n = PAGES_PER_STEP
    nb = n * page // CMP_BLOCK
    return pl.pallas_call(
        _gather_flat_kernel,
        grid_spec=pltpu.PrefetchScalarGridSpec(
            num_scalar_prefetch=1,
            grid=(b, n_pages // n),
            in_specs=_page_specs(n, pool.shape[1:]),
            out_specs=pl.BlockSpec((1, 2, NSA_KV_HEADS, nb, CMP_BLOCK, NSA_HEAD_DIM),
                                   lambda i, c, pt: (i, 0, 0, c, 0, 0)),
        ),
        out_shape=jax.ShapeDtypeStruct(
            (b, 2, NSA_KV_HEADS, n_pages * page // CMP_BLOCK, CMP_BLOCK, NSA_HEAD_DIM), F32),
        compiler_params=_cparams(2),
        name="nsa_gather_cmp",
    )(page_table, *([pool] * n))


def _attn_sample_kernel(n_step, t_real, past_len, pt_ref, qn_ref, qr_ref, kvc_ref, gate_ref, snew_ref, wbuf_ref,
                        wnew_ref, *rest):
    pages = rest[:n_step]
    o_ref, sel_ref, ocmp_ref, m_ref, l_ref, acc_ref = rest[n_step:]
    c = pl.program_id(1)
    hd = NSA_HEAD_DIM
    kvw = NSA_KV_WIDTH
    tq = qn_ref.shape[1]
    n_cmp = past_len // CMP_BLOCK
    tpos = past_len + lax.broadcasted_iota(jnp.int32, (tq, 1), 0) % t_real
    stack = lambda ref, g: jnp.concatenate(
        [ref[0, :, (g * NSA_GROUP + hh) * hd:(g * NSA_GROUP + hh + 1) * hd] for hh in range(NSA_GROUP)], axis=0)

    @pl.when(c == 0)
    def _():
        n_blocks = -(-(past_len + t_real) // SEL_BLOCK)
        for g in range(NSA_KV_HEADS):
            o_cmp, imp = _cmp_branch(stack(qn_ref, g), kvc_ref[0, 0, g], kvc_ref[0, 1, g], tpos, tq)
            imp = jnp.concatenate([imp, jnp.zeros((tq, LANES), F32)], axis=-1)
            sel_ref[g] = _select_blocks(imp, tpos, n_blocks)
            ocmp_ref[g] = o_cmp
        m_ref[...] = jnp.full(m_ref.shape, NEG_INF, F32)
        l_ref[...] = jnp.zeros(l_ref.shape, F32)
        acc_ref[...] = jnp.zeros(acc_ref.shape, F32)

    tk = n_step * pages[0].shape[1]
    k0 = c * tk
    kpos = k0 + lax.broadcasted_iota(jnp.int32, (1, tk), 1)
    for g in range(NSA_KV_HEADS):
        kk = jnp.concatenate([pg[0, :, g * hd:(g + 1) * hd] for pg in pages], axis=0)
        vv = jnp.concatenate([pg[0, :, kvw + g * hd:kvw + (g + 1) * hd] for pg in pages], axis=0)
        valid = jnp.where((_expand_sel(sel_ref[g][:, :n_cmp], k0, tk) > 0.5) & (kpos <= tpos), 1.0, 0.0)
        m, l, acc = _flash_step(stack(qr_ref, g).astype(BF16), kk, vv, valid, (m_ref[g], l_ref[g], acc_ref[g]))
        m_ref[g] = m
        l_ref[g] = l
        acc_ref[g] = acc

    @pl.when(c == pl.num_programs(1) - 1)
    def _():
        gate = gate_ref[0]
        rnew = lax.broadcasted_iota(jnp.int32, (1, tq), 1)
        newpos = past_len + rnew
        n_buf = wbuf_ref.shape[1]
        bpos = past_len - n_buf + lax.broadcasted_iota(jnp.int32, (1, n_buf), 1)
        new_lane = 2 * (past_len // SEL_BLOCK)
        heads = []
        for g in range(NSA_KV_HEADS):
            q4 = stack(qr_ref, g).astype(BF16)
            ks, vs = slice(g * hd, (g + 1) * hd), slice(kvw + g * hd, kvw + (g + 1) * hd)
            picked = sel_ref[g][:, new_lane:new_lane + 1] > 0.5
            valid = jnp.where(picked & (newpos <= tpos) & (rnew < t_real), 1.0, 0.0)
            _, l_s, acc_s = _flash_step(q4, snew_ref[0, :, ks], snew_ref[0, :, vs], valid,
                                        (m_ref[g], l_ref[g], acc_ref[g]))
            dist = tpos - bpos
            valid = jnp.where((dist >= 0) & (dist <= WINDOW) & (bpos >= 0), 1.0, 0.0)
            carry = _flash_step(q4, wbuf_ref[0, :, ks], wbuf_ref[0, :, vs], valid, _flash_init(NSA_GROUP * tq))
            dist = tpos - newpos
            valid = jnp.where((dist >= 0) & (dist <= WINDOW) & (rnew < t_real), 1.0, 0.0)
            _, l_w, acc_w = _flash_step(q4, wnew_ref[0, :, ks], wnew_ref[0, :, vs], valid, carry)
            o_cmp = ocmp_ref[g]
            o_slc = acc_s / l_s
            o_win = acc_w / l_w
            for hh in range(NSA_GROUP):
                head = g * NSA_GROUP + hh
                rows = slice(hh * tq, (hh + 1) * tq)
                col = (head // (NSA_HEADS // 2)) * LANES + (head % (NSA_HEADS // 2)) * 3
                heads.append(o_cmp[rows] * gate[:, col:col + 1] + o_slc[rows] * gate[:, col + 1:col + 2]
                             + o_win[rows] * gate[:, col + 2:col + 3])
        o_ref[0] = jnp.concatenate(heads, axis=-1)


def _attn_sample_call(qn, qr, kvc, gate, slc_new, win_buf, win_new, pool, page_table, t_real):
    b, tq, nq = qn.shape
    n_pages = page_table.shape[1]
    page = pool.shape[1]
    past_len = n_pages * page
    assert past_len % SEL_BLOCK == 0 and t_real <= SEL_BLOCK and past_len % (CMP_BLOCK * LANES) == 0
    n = PAGES_PER_STEP
    rows = NSA_GROUP * tq
    full = lambda a: pl.BlockSpec((1,) + a.shape[1:], lambda i, c, pt: (i,) + (0,) * (a.ndim - 1))
    sel_lanes = past_len // CMP_BLOCK + LANES
    return pl.pallas_call(
        functools.partial(_attn_sample_kernel, n, t_real, past_len),
        grid_spec=pltpu.PrefetchScalarGridSpec(
            num_scalar_prefetch=1,
            grid=(b, n_pages // n),
            in_specs=[full(a) for a in (qn, qr, kvc, gate, slc_new, win_buf, win_new)]
            + _page_specs(n, pool.shape[1:]),
            out_specs=pl.BlockSpec((1, tq, nq), lambda i, c, pt: (i, 0, 0)),
            scratch_shapes=[pltpu.VMEM((NSA_KV_HEADS, tq, sel_lanes), F32),
                            pltpu.VMEM((NSA_KV_HEADS, rows, NSA_HEAD_DIM), F32),
                            pltpu.VMEM((NSA_KV_HEADS, rows, 1), F32),
                            pltpu.VMEM((NSA_KV_HEADS, rows, 1), F32),
                            pltpu.VMEM((NSA_KV_HEADS, rows, NSA_HEAD_DIM), F32)],
        ),
        out_shape=jax.ShapeDtypeStruct((b, tq, nq), F32),
        compiler_params=_cparams(2),
        name="nsa_attn_sample",
    )(page_table, qn, qr, kvc, gate, slc_new, win_buf, win_new, *([pool] * n))


def _nsa_sample_layer(h, cache_cmp, cache_slc, win_buf, page_table, gmix, prm):
    b, t, d = h.shape
    n_phys, page = cache_cmp.shape[:2]
    past_len = page_table.shape[1] * page
    cos, sin = _rope_tables(jnp.tile(past_len + jnp.arange(t), b))
    qn, qr, cmp, slc, wrow, gate = _nsa_proj_call(
        h.reshape(b * t, d), gmix, prm["w_in"], prm["g_q"], prm["g_ks"], prm["g_kw"], cos, sin, t, False)
    width = 2 * NSA_KV_WIDTH
    flat = _gather_flat_call(cache_cmp.reshape(n_phys, page, width), page_table)
    nc = past_len // CMP_BLOCK
    kvc = _compress_call(flat.reshape(b, 2, NSA_KV_HEADS * nc, CMP_BLOCK * NSA_HEAD_DIM),
                         prm["pe"], prm["w1"], prm["w2"], prm["g_kc"])
    kvc = kvc.reshape(b, 2, NSA_KV_HEADS, nc, NSA_HEAD_DIM)
    tq = 8
    pad = lambda x: jnp.pad(x.reshape(b, t, x.shape[-1]), ((0, 0), (0, tq - t), (0, 0)))
    n_buf = win_buf.shape[1]
    o = _attn_sample_call(pad(qn), pad(qr), kvc, pad(gate), pad(slc), win_buf.reshape(b, n_buf, width), pad(wrow),
                          cache_slc.reshape(n_phys, page, width), page_table, t)
    new_rows = _kv_rows(wrow, b, t)
    new_win = jnp.concatenate([win_buf, new_rows], axis=1)[:, t:]
    return o[:, :t].reshape(b * t, -1), _kv_rows(cmp, b, t), _kv_rows(slc, b, t), new_win


def kernel(x_prompt, x_sample, cache_cmp, cache_slc, cache_win, state_rwkv_shift, state_rwkv_wkv, state_hgrn,
           page_table, p_prompt, p_sample, norm_mix, norm_ffn, w_up, w_down, w_ple, w_ple_gate,
           nsa_w_in, nsa_g_q, nsa_g_ks, nsa_g_kw, nsa_g_kc, nsa_cmp_pe, nsa_cmp_w1, nsa_cmp_w2, nsa_w_out,
           rwkv_mu, rwkv_w_r, rwkv_w_k, rwkv_w_v, rwkv_w_o, rwkv_w0, rwkv_w1, rwkv_w2, rwkv_a0, rwkv_a1,
           rwkv_a2, rwkv_g1, rwkv_g2, rwkv_k_k, rwkv_k_a, rwkv_r_k, rwkv_lnx_w, rwkv_lnx_b,
           hgrn_w_in, hgrn_gn, hgrn_w_o, hgrn_lower_bounds):
    depth = norm_mix.shape[0]
    bp, tp, d = x_prompt.shape
    bs, ts, _ = x_sample.shape
    lb_soft = jax.nn.softmax(hgrn_lower_bounds.astype(F32), axis=0)
    lower_bound = jnp.cumsum(lb_soft, axis=0) - lb_soft[0]
    hp, hs = x_prompt.reshape(bp * tp, d), x_sample.reshape(bs * ts, d)
    outs = [[] for _ in range(12)]
    for i in range(depth):
        kind, n = i % N_MIXERS, i // N_MIXERS
        gmix = norm_mix[i][None]
        hp3, hs3 = hp.reshape(bp, tp, d), hs.reshape(bs, ts, d)
        ffn = (p_prompt[i].reshape(bp * tp, -1), p_sample[i].reshape(bs * ts, -1))
        tail = (norm_ffn[i][None], w_up[i].astype(BF16), w_down[i].astype(BF16), w_ple[i].astype(BF16),
                w_ple_gate[i].astype(BF16))
        if kind == 0:
            prm = _nsa_params(nsa_w_in[n], nsa_g_q[n], nsa_g_ks[n], nsa_g_kw[n], nsa_g_kc[n], nsa_cmp_pe[n],
                              nsa_cmp_w1[n], nsa_cmp_w2[n])
            op, rc, rs, rw = _nsa_prompt_layer(hp3, gmix, prm)
            os_, nc_rows, ns_rows, nw_buf = _nsa_sample_layer(hs3, cache_cmp[n], cache_slc[n], cache_win[n],
                                                              page_table, gmix, prm)
            for lst, v in zip(outs[:6], (rc, nc_rows, rs, ns_rows, rw, nw_buf)):
                lst.append(v)
            wo = nsa_w_out[n].astype(BF16)
            hp = _ffn_call("nsa", hp, op, None, None, ffn[0], wo, *tail)
            hs = _ffn_call("nsa", hs, os_, None, None, ffn[1], wo, *tail)
        elif kind == 1:
            prm = _rwkv_params(rwkv_mu[n], rwkv_w_r[n], rwkv_w_k[n], rwkv_w_v[n], rwkv_w0[n], rwkv_w1[n],
                               rwkv_w2[n], rwkv_a0[n], rwkv_a1[n], rwkv_a2[n], rwkv_g1[n], rwkv_g2[n])
            vecs = (rwkv_k_k[n], rwkv_k_a[n], rwkv_r_k[n], rwkv_lnx_w[n], rwkv_lnx_b[n])
            nh = d // RWKV_HEAD_DIM
            zero_state = jnp.zeros((bp, nh, RWKV_HEAD_DIM, RWKV_HEAD_DIM), F32)
            yp, gp, shp, sp = _rwkv_layer(hp3, jnp.zeros((bp, d), F32), zero_state, gmix, prm, *vecs)
            ys, gs, shs, ss = _rwkv_layer(hs3, state_rwkv_shift[n], state_rwkv_wkv[n].astype(F32), gmix, prm, *vecs)
            for lst, v in zip(outs[6:10], (shp, shs, sp, ss)):
                lst.append(v)
            wo = rwkv_w_o[n].astype(BF16)
            hp = _ffn_call("rwkv", hp, yp, gp, None, ffn[0], wo, *tail)
            hs = _ffn_call("rwkv", hs, ys, gs, None, ffn[1], wo, *tail)
        else:
            lb = lower_bound[i][None]
            win = hgrn_w_in[n].astype(BF16)
            nh = d // HGRN_HEAD_DIM
            zero_state = jnp.zeros((bp, nh, HGRN_HEAD_DIM, HGRN_HEAD_DIM), F32)
            op, gp, sp = _hgrn_layer(hp3, zero_state, gmix, win, lb, 1.0 - lb)
            os_, gs, ss = _hgrn_layer(hs3, state_hgrn[n].astype(F32), gmix, win, lb, 1.0 - lb)
            outs[10].append(sp)
            outs[11].append(ss)
            wo, gn = hgrn_w_o[n].astype(BF16), hgrn_gn[n][None]
            hp = _ffn_call("hgrn", hp, op, gp, gn, ffn[0], wo, *tail)
            hs = _ffn_call("hgrn", hs, os_, gs, gn, ffn[1], wo, *tail)
    return (hp.reshape(bp, tp, d), hs.reshape(bs, ts, d)) + tuple(jnp.stack(o) for o in outs)
```

```python
import functools
import math

import jax
import jax.numpy as jnp
from jax import lax
from jax.experimental import pallas as pl
from jax.experimental.pallas import tpu as pltpu

F32 = jnp.float32
BF16 = jnp.bfloat16

NORM_EPS = 1e-6
ROPE_THETA = 10000.0
NEG_INF = -1e30
M_INIT = -1e29
N_MIXERS = 3

NSA_HEADS = 16
NSA_KV_HEADS = 4
NSA_HEAD_DIM = 64
NSA_GROUP = NSA_HEADS // NSA_KV_HEADS
NSA_KV_WIDTH = NSA_KV_HEADS * NSA_HEAD_DIM
NSA_SCALE = NSA_HEAD_DIM ** -0.5
CMP_BLOCK = 32
SEL_BLOCK = 64
N_SELECT = 16
WINDOW = 512
FORCE_SCORE = 1e4

RWKV_HEAD_DIM = 64
LNX_EPS = 64e-5
HGRN_HEAD_DIM = 128

LANES = 128
VMEM_LIMIT = 56 * 1024 * 1024


def _cparams(n_axes):
    return pltpu.CompilerParams(dimension_semantics=("arbitrary",) * n_axes,
                                vmem_limit_bytes=VMEM_LIMIT)


def _resident(shape):
    zeros = (0,) * len(shape)
    return pl.BlockSpec(shape, lambda *_: zeros, pipeline_mode=pl.Buffered(1))


def _rms(x):
    return x * lax.rsqrt(jnp.mean(x * x, axis=-1, keepdims=True) + NORM_EPS)


def _dot(a, b):
    return jnp.dot(a, b, preferred_element_type=F32)


def _dot_nt(a, b):
    return lax.dot_general(a, b, (((1,), (1,)), ((), ())), preferred_element_type=F32)


def _per_chunk(fn, x, *rest):
    n = x.shape[-1] // LANES
    outs = [fn(*(a[:, c * LANES:(c + 1) * LANES] for a in (x,) + rest)) for c in range(n)]
    return outs[0] if n == 1 else jnp.concatenate(outs, axis=-1)


def _lane_iota():
    return lax.broadcasted_iota(jnp.int32, (1, LANES), 1)


def _group_sum(x, group):
    lane = _lane_iota()
    k = 1
    while k < group:
        up = pltpu.roll(x, LANES - k, axis=1)
        dn = pltpu.roll(x, k, axis=1)
        x = x + jnp.where((lane & k) == 0, up, dn)
        k *= 2
    return x


def _head_rms(x, head_dim):
    def one(c):
        return c * lax.rsqrt(_group_sum(c * c, head_dim) * (1.0 / head_dim) + NORM_EPS)
    return _per_chunk(one, x)


def _rope(x, cos, sin_signed):
    half = NSA_HEAD_DIM // 2
    lane = _lane_iota()

    def one(c):
        up = pltpu.roll(c, LANES - half, axis=1)
        dn = pltpu.roll(c, half, axis=1)
        rot = jnp.where((lane & half) == 0, up, dn)
        return c * cos + rot * sin_signed
    return _per_chunk(one, x)


def _ffn_kernel(mode, ff_chunk, *refs):
    if mode == "nsa":
        h_ref, o_ref, p_ref, wo_ref, gffn_ref, wup_ref, wdown_ref, wple_ref, wgate_ref, out_ref = refs
        o = o_ref[...]
    elif mode == "rwkv":
        h_ref, o_ref, aux_ref, p_ref, wo_ref, gffn_ref, wup_ref, wdown_ref, wple_ref, wgate_ref, out_ref = refs
        o = o_ref[...] * aux_ref[...]
    else:
        (h_ref, o_ref, aux_ref, gn_ref, p_ref, wo_ref, gffn_ref, wup_ref, wdown_ref, wple_ref, wgate_ref,
         out_ref) = refs
        gn = gn_ref[...]
        o = _per_chunk(lambda c: _rms(c) * gn, o_ref[...]) * aux_ref[...]
    h1 = h_ref[...] + _dot(o.astype(BF16), wo_ref[...])
    u = (_rms(h1) * gffn_ref[...]).astype(BF16)
    d_ff = wup_ref.shape[1]
    acc = jnp.zeros_like(h1)
    for j in range(d_ff // ff_chunk):
        a = _dot(u, wup_ref[:, j * ff_chunk:(j + 1) * ff_chunk])
        a = jnp.square(jnp.maximum(a, 0.0)).astype(BF16)
        acc = acc + _dot(a, wdown_ref[j * ff_chunk:(j + 1) * ff_chunk, :])
    h2 = h1 + acc
    gate = jax.nn.sigmoid(_dot(_rms(h2).astype(BF16), wgate_ref[...]))
    out_ref[...] = h2 + _dot(p_ref[...].astype(BF16), wple_ref[...]) * gate


def _ffn_call(mode, h, o, aux, gn, p, wo, gffn, wup, wdown, wple, wgate):
    m, d = h.shape
    tm = min(m, 256)
    row = lambda w: pl.BlockSpec((tm, w), lambda i: (i, 0))
    args, specs = [h, o], [row(d), row(d)]
    if mode != "nsa":
        args.append(aux)
        specs.append(row(d))
    if mode == "hgrn":
        args.append(gn)
        specs.append(_resident(gn.shape))
    args += [p, wo, gffn, wup, wdown, wple, wgate]
    specs += [row(p.shape[1])] + [_resident(a.shape) for a in (wo, gffn, wup, wdown, wple, wgate)]
    return pl.pallas_call(
        functools.partial(_ffn_kernel, mode, 1024),
        grid=(m // tm,),
        in_specs=specs,
        out_specs=row(d),
        out_shape=jax.ShapeDtypeStruct((m, d), F32),
        compiler_params=_cparams(1),
        name="ffn_" + mode,
    )(*args)


def _nsa_proj_kernel(h_ref, gmix_ref, win_ref, gq_ref, gks_ref, gkw_ref, cos_ref, sin_ref,
                     qn_ref, qr_ref, cmp_ref, slc_ref, wrow_ref, gate_ref):
    kvw = NSA_KV_WIDTH
    nq = NSA_HEADS * NSA_HEAD_DIM
    u = (_rms(h_ref[...]) * gmix_ref[...]).astype(BF16)
    z = _dot(u, win_ref[...])
    cos = cos_ref[...]
    sin = sin_ref[...]
    qn = _head_rms(z[:, :nq], NSA_HEAD_DIM) * gq_ref[...]
    qn_ref[...] = qn
    qr_ref[...] = _rope(qn, cos, sin)
    cmp_ref[...] = z[:, nq:nq + 2 * kvw]
    o = nq + 2 * kvw
    slc_ref[:, :kvw] = _rope(_head_rms(z[:, o:o + kvw], NSA_HEAD_DIM) * gks_ref[...], cos, sin)
    slc_ref[:, kvw:] = z[:, o + kvw:o + 2 * kvw]
    o += 2 * kvw
    wrow_ref[:, :kvw] = _rope(_head_rms(z[:, o:o + kvw], NSA_HEAD_DIM) * gkw_ref[...], cos, sin)
    wrow_ref[:, kvw:] = z[:, o + kvw:o + 2 * kvw]
    o += 2 * kvw
    gate_ref[...] = jax.nn.sigmoid(z[:, o:])


def _nsa_proj_call(h, gmix, win, gq, gks, gkw, cos, sin):
    m, d = h.shape
    tm = min(m, 256)
    tab_tiles = cos.shape[0] // tm
    row = lambda w: pl.BlockSpec((tm, w), lambda i: (i, 0))
    tab = pl.BlockSpec((tm, LANES), lambda i: (i % tab_tiles, 0))
    nq = NSA_HEADS * NSA_HEAD_DIM
    widths = [nq, nq] + [2 * NSA_KV_WIDTH] * 3 + [2 * LANES]
    return pl.pallas_call(
        _nsa_proj_kernel,
        grid=(m // tm,),
        in_specs=[row(d), _resident(gmix.shape), _resident(win.shape), _resident(gq.shape),
                  _resident(gks.shape), _resident(gkw.shape), tab, tab],
        out_specs=[row(w) for w in widths],
        out_shape=[jax.ShapeDtypeStruct((m, w), F32) for w in widths],
        compiler_params=_cparams(1),
        name="nsa_proj",
    )(h, gmix, win, gq, gks, gkw, cos, sin)


def _compress_slot(load_pair, m, e, pe_ref, w1_ref, w2_ref, gkc_ref):
    lane = _lane_iota()
    low = lane < NSA_HEAD_DIM
    acc = jnp.zeros((NSA_KV_HEADS * m, w1_ref.shape[2]), F32)
    for s in range(CMP_BLOCK // 2):
        pe_row = pe_ref[e, s:s + 1, :]
        parts = []
        for kp in range(NSA_KV_HEADS // 2):
            a = load_pair(kp, 2 * s)
            b = load_pair(kp, 2 * s + 1)
            parts.append(jnp.where(low, a, pltpu.roll(b, NSA_HEAD_DIM, axis=1)))
            parts.append(jnp.where(low, pltpu.roll(a, NSA_HEAD_DIM, axis=1), b))
        x = (jnp.concatenate(parts, axis=0) + pe_row).astype(BF16)
        acc = acc + _dot(x, w1_ref[e, s * LANES:(s + 1) * LANES, :])
    out = _dot(jax.nn.gelu(acc).astype(BF16), w2_ref[e])
    if e == 0:
        out = _rms(out) * gkc_ref[...]
    return jnp.concatenate([out[k * m:(k + 1) * m] for k in range(NSA_KV_HEADS)], axis=-1)


def _compress_rows_kernel(*refs):
    cols = refs[:NSA_KV_HEADS]
    pe_ref, w1_ref, w2_ref, gkc_ref, out_ref = refs[NSA_KV_HEADS:]
    m = cols[0].shape[1] // CMP_BLOCK
    pad = out_ref.shape[2] - m
    for e in range(2):
        load = lambda kp, r: cols[2 * e + kp][0, pl.ds(r, m, stride=CMP_BLOCK), :]
        res = _compress_slot(load, m, e, pe_ref, w1_ref, w2_ref, gkc_ref)
        out_ref[0, e] = jnp.concatenate([res, jnp.zeros((pad, res.shape[1]), F32)], axis=0)


def _compress_rows_call(cmp, pe, w1, w2, gkc):
    b, t, w = cmp.shape
    c = -(-(t // CMP_BLOCK) // LANES) * LANES
    return pl.pallas_call(
        _compress_rows_kernel,
        grid=(b,),
        in_specs=[pl.BlockSpec((1, t, LANES), functools.partial(lambda j, i: (i, 0, j), j)) for j in range(w // LANES)]
        + [_resident(a.shape) for a in (pe, w1, w2, gkc)],
        out_specs=pl.BlockSpec((1, 2, c, NSA_KV_WIDTH), lambda i: (i, 0, 0, 0)),
        out_shape=jax.ShapeDtypeStruct((b, 2, c, NSA_KV_WIDTH), F32),
        compiler_params=_cparams(1),
        name="nsa_compress_rows",
    )(*([cmp] * (w // LANES)), pe, w1, w2, gkc)


PAGES_PER_STEP = 16


def _page_specs(n, layer, tail):
    zeros = (0,) * len(tail)
    return [pl.BlockSpec((1, 1) + tail,
                         functools.partial(lambda i, b, c, pt: (layer, pt[b, c * n + i]) + zeros, i))
            for i in range(n)]


def _compress_pages_kernel(pt_ref, *refs):
    n = len(refs) - 6
    pages = refs[:n]
    pe_ref, w1_ref, w2_ref, gkc_ref, out_ref, rows_ref = refs[n:]
    page = pages[0].shape[-1]
    m = n * page // CMP_BLOCK
    for e in range(2):
        for i, pg in enumerate(pages):
            for kp in range(NSA_KV_HEADS // 2):
                tile = pg[0, 0, e, 2 * kp:2 * kp + 2].reshape(LANES, page)
                rows_ref[kp, i * page:(i + 1) * page, :] = tile.T
        load = lambda kp, r: rows_ref[kp, pl.ds(r, m, stride=CMP_BLOCK), :]
        out_ref[0, e] = _compress_slot(load, m, e, pe_ref, w1_ref, w2_ref, gkc_ref)


def _compress_pages_call(pool_t, layer, page_table, pe, w1, w2, gkc):
    b, n_pages = page_table.shape
    page = pool_t.shape[-1]
    n = PAGES_PER_STEP
    m = n * page // CMP_BLOCK
    return pl.pallas_call(
        _compress_pages_kernel,
        grid_spec=pltpu.PrefetchScalarGridSpec(
            num_scalar_prefetch=1,
            grid=(b, n_pages // n),
            in_specs=_page_specs(n, layer, pool_t.shape[2:])
            + [pl.BlockSpec(a.shape, functools.partial(lambda nd, i, c, pt: (0,) * nd, a.ndim),
                            pipeline_mode=pl.Buffered(1)) for a in (pe, w1, w2, gkc)],
            out_specs=pl.BlockSpec((1, 2, m, NSA_KV_WIDTH), lambda i, c, pt: (i, 0, c, 0)),
            scratch_shapes=[pltpu.VMEM((NSA_KV_HEADS // 2, n * page, LANES), F32)],
        ),
        out_shape=jax.ShapeDtypeStruct((b, 2, n_pages * page // CMP_BLOCK, NSA_KV_WIDTH), F32),
        compiler_params=_cparams(2),
        name="nsa_compress_pages",
    )(page_table, *([pool_t] * n), pe, w1, w2, gkc)


def _select_blocks_t(imp, tpos, n_blocks):
    c = imp.shape[0]
    cid = lax.broadcasted_iota(jnp.int32, (c, 1), 0)
    imp2 = imp + jnp.where((cid & 1) == 0, pltpu.roll(imp, c - 1, axis=0), pltpu.roll(imp, 1, axis=0))
    blk = cid // 2
    cur = tpos // SEL_BLOCK
    forced = (blk == 0) | (blk == cur) | (blk == cur - 1)
    score = jnp.where(blk <= cur, jnp.where(forced, FORCE_SCORE, imp2), -1.0)
    rank = jnp.zeros(score.shape, F32)
    for j in range(n_blocks):
        row = score[2 * j:2 * j + 1, :]
        rank = rank + jnp.where((row > score) | ((row == score) & (j < blk)), 1.0, 0.0)
    return jnp.where(rank < float(min(N_SELECT, n_blocks)), 1.0, 0.0)


def _attn_prompt_kernel(tq, tk, qn_ref, qr_ref, kvc_ref, sk_ref, sv_ref, wk_ref, wv_ref, gate_ref, o_ref,
                        kb_ref, vt_ref, sel_ref):
    qi = pl.program_id(2)
    hd = NSA_HEAD_DIM
    t_len = sk_ref.shape[1]
    c_blocks = kvc_ref.shape[2]
    rows_per_tile = tk // CMP_BLOCK

    @pl.when(qi == 0)
    def _():
        for br, (k_ref, v_ref) in enumerate(((sk_ref, sv_ref), (wk_ref, wv_ref))):
            for gl in range(2):
                kb_ref[br, gl] = k_ref[0, :, gl * hd:(gl + 1) * hd].astype(BF16)
            for kt in range(t_len // tk):
                vt_ref[br, kt] = v_ref[0, kt * tk:(kt + 1) * tk, :].T.astype(BF16)

    t0 = qi * tq
    tpos = t0 + lax.broadcasted_iota(jnp.int32, (1, tq), 1)
    gate_t = gate_ref[0].T
    qn_t = [qn_ref[0, :, c * LANES:(c + 1) * LANES].T for c in range(2 * NSA_GROUP * hd // LANES)]
    qr_t = [qr_ref[0, :, c * LANES:(c + 1) * LANES].T for c in range(2 * NSA_GROUP * hd // LANES)]
    cid = lax.broadcasted_iota(jnp.int32, (c_blocks, 1), 0)
    ok = jnp.where((cid + 1) * CMP_BLOCK - 1 <= tpos, 1.0, 0.0)
    ok4 = jnp.concatenate([ok] * NSA_GROUP, axis=1)
    heads = []
    for gl in range(2):
        per_head = lambda parts: [parts[2 * gl + hh // 2][(hh % 2) * hd:(hh % 2 + 1) * hd] for hh in range(NSA_GROUP)]
        qn4 = (jnp.concatenate(per_head(qn_t), axis=1) * NSA_SCALE).astype(BF16)
        qr4 = (jnp.concatenate(per_head(qr_t), axis=1) * NSA_SCALE).astype(BF16)
        kc = kvc_ref[0, 0][:, gl * hd:(gl + 1) * hd]
        vc_t = kvc_ref[0, 1].T[gl * hd:(gl + 1) * hd]
        s = jnp.where(ok4 > 0.5, _dot(kc.astype(BF16), qn4), NEG_INF)
        e = jnp.exp(s - jnp.max(s, axis=0, keepdims=True))
        p = e / jnp.sum(e, axis=0, keepdims=True) * ok4
        o_cmp = _dot(vc_t.astype(BF16), p.astype(BF16))
        imp = p[:, :tq]
        for hh in range(1, NSA_GROUP):
            imp = imp + p[:, hh * tq:(hh + 1) * tq]
        sel_ref[gl] = _select_blocks_t(imp, tpos, t_len // SEL_BLOCK)

        def flash(br, lo, hi, valid_fn):
            def body(kt, carry):
                m, l, acc = carry
                k0 = pl.multiple_of(kt * tk, tk)
                kb = kb_ref[br, gl, pl.ds(k0, tk), :]
                vt = vt_ref[br, kt, gl * hd:(gl + 1) * hd, :]
                bias = jnp.where(valid_fn(kt, k0), 0.0, NEG_INF)
                s = _dot(kb, qr4) + jnp.concatenate([bias] * NSA_GROUP, axis=1)
                m_new = jnp.maximum(m, jnp.max(s, axis=0, keepdims=True))
                alpha = jnp.exp(m - m_new)
                p = jnp.exp(s - m_new)
                return (m_new, alpha * l + jnp.sum(p, axis=0, keepdims=True),
                        alpha * acc + _dot(vt, p.astype(BF16)))
            cols = NSA_GROUP * tq
            init = (jnp.full((1, cols), M_INIT, F32), jnp.zeros((1, cols), F32), jnp.zeros((hd, cols), F32))
            _, l, acc = lax.fori_loop(lo, hi, body, init)
            return acc / l

        def slc_valid(kt, k0):
            picked = sel_ref[gl, pl.ds(pl.multiple_of(kt * rows_per_tile, rows_per_tile), rows_per_tile), :]
            picked = jnp.concatenate([jnp.broadcast_to(picked[i:i + 1], (CMP_BLOCK, tq))
                                      for i in range(rows_per_tile)], axis=0)
            kpos = k0 + lax.broadcasted_iota(jnp.int32, (tk, 1), 0)
            return (picked > 0.5) & (kpos <= tpos)

        def win_valid(kt, k0):
            dist = tpos - (k0 + lax.broadcasted_iota(jnp.int32, (tk, 1), 0))
            return (dist >= 0) & (dist <= WINDOW)

        hi = (t0 + tq - 1) // tk + 1
        slc = flash(0, 0, hi, slc_valid)
        win = flash(1, jnp.maximum(t0 - WINDOW, 0) // tk, hi, win_valid)
        for hh in range(NSA_GROUP):
            c = (gl * NSA_GROUP + hh) * 3
            cols = slice(hh * tq, (hh + 1) * tq)
            heads.append(o_cmp[:, cols] * gate_t[c:c + 1] + slc[:, cols] * gate_t[c + 1:c + 2]
                         + win[:, cols] * gate_t[c + 2:c + 3])
    for c in range(len(heads) // 2):
        o_ref[0, :, c * LANES:(c + 1) * LANES] = jnp.concatenate(heads[2 * c:2 * c + 2], axis=0).T


def _attn_prompt_call(qn, qr, kvc, slc, win, gate, tq=128, tk=512):
    b, t, nq = qn.shape
    pair_w = 2 * NSA_GROUP * NSA_HEAD_DIM
    c = kvc.shape[2]
    assert tk % (8 * CMP_BLOCK) == 0 and t % tk == 0 and c % LANES == 0
    qspec = pl.BlockSpec((1, tq, pair_w), lambda i, p, j: (i, j, p))
    kspec = pl.BlockSpec((1, t, LANES), lambda i, p, j: (i, 0, p))
    vspec = pl.BlockSpec((1, t, LANES), lambda i, p, j: (i, 0, 2 + p))
    return pl.pallas_call(
        functools.partial(_attn_prompt_kernel, tq, tk),
        grid=(b, 2, t // tq),
        in_specs=[qspec, qspec,
                  pl.BlockSpec((1, 2, c, LANES), lambda i, p, j: (i, 0, 0, p)),
                  kspec, vspec, kspec, vspec,
                  pl.BlockSpec((1, tq, LANES), lambda i, p, j: (i, j, p))],
        out_specs=qspec,
        out_shape=jax.ShapeDtypeStruct((b, t, nq), F32),
        scratch_shapes=[pltpu.VMEM((2, 2, t, NSA_HEAD_DIM), BF16),
                        pltpu.VMEM((2, t // tk, LANES, tk), BF16),
                        pltpu.VMEM((2, c, tq), F32)],
        compiler_params=_cparams(3),
        name="nsa_attn_prompt",
    )(qn, qr, kvc, slc, slc, win, win, gate)


def _cmp_branch(q4, kc, vc, tpos, n_tok):
    c = kc.shape[0]
    s = _dot_nt((q4 * NSA_SCALE).astype(BF16), kc.astype(BF16))
    cidx = lax.broadcasted_iota(jnp.int32, (1, c), 1)
    ok = ((cidx + 1) * CMP_BLOCK - 1 <= tpos).astype(F32)
    ok4 = jnp.concatenate([ok] * NSA_GROUP, axis=0)
    s = jnp.where(ok4 > 0.5, s, NEG_INF)
    e = jnp.exp(s - jnp.max(s, axis=-1, keepdims=True))
    p = e / jnp.sum(e, axis=-1, keepdims=True) * ok4
    o = _dot(p.astype(BF16), vc.astype(BF16))
    imp = p[:n_tok]
    for hh in range(1, NSA_GROUP):
        imp = imp + p[hh * n_tok:(hh + 1) * n_tok]
    return o, imp


def _select_blocks(imp, tpos, n_blocks):
    c = imp.shape[1]
    lane = lax.broadcasted_iota(jnp.int32, (1, c), 1)

    def pair(x):
        lane1 = _lane_iota()
        return x + jnp.where((lane1 & 1) == 0, pltpu.roll(x, LANES - 1, axis=1), pltpu.roll(x, 1, axis=1))
    imp2 = _per_chunk(pair, imp)
    blk = lane // 2
    cur = tpos // SEL_BLOCK
    forced = (blk == 0) | (blk == cur) | (blk == cur - 1)
    score = jnp.where(blk <= cur, jnp.where(forced, FORCE_SCORE, imp2), -1.0)
    rank = jnp.zeros(score.shape, F32)
    for j in range(n_blocks):
        col = score[:, 2 * j:2 * j + 1]
        ahead = (col > score) | ((col == score) & (j < blk))
        rank = rank + ahead.astype(F32)
    return (rank < float(min(N_SELECT, n_blocks))).astype(F32)


def _flash_step(q4, k, v, valid, carry, transposed):
    m, l, acc = carry
    valid4 = jnp.concatenate([valid] * NSA_GROUP, axis=0)
    s = _dot(q4, k.astype(BF16)) if transposed else _dot_nt(q4, k.astype(BF16))
    s = jnp.where(valid4 > 0.5, s, NEG_INF)
    m_new = jnp.maximum(m, jnp.max(s, axis=-1, keepdims=True))
    alpha = jnp.exp(m - m_new)
    p = jnp.exp(s - m_new)
    l = alpha * l + jnp.sum(p, axis=-1, keepdims=True)
    pv = _dot_nt(p.astype(BF16), v.astype(BF16)) if transposed else _dot(p.astype(BF16), v.astype(BF16))
    return m_new, l, alpha * acc + pv


def _expand_sel(sel, k0, tk):
    c = sel.shape[1]
    cid = lax.broadcasted_iota(jnp.int32, (c, tk), 0)
    kblk = (k0 + lax.broadcasted_iota(jnp.int32, (c, tk), 1)) // CMP_BLOCK
    expand = jnp.where(cid == kblk, 1.0, 0.0).astype(BF16)
    return _dot(sel.astype(BF16), expand)


def _attn_sample_kernel(n_step, t_real, past_len, pt_ref, qn_ref, qr_ref, kvc_ref, gate_ref, snew_ref, wbuf_ref,
                        wnew_ref, *rest):
    pages = rest[:n_step]
    o_ref, sel_ref, ocmp_ref, m_ref, l_ref, acc_ref = rest[n_step:]
    c = pl.program_id(1)
    hd = NSA_HEAD_DIM
    kvw = NSA_KV_WIDTH
    tq = qn_ref.shape[1]
    n_cmp = past_len // CMP_BLOCK
    tpos = past_len + lax.broadcasted_iota(jnp.int32, (tq, 1), 0) % t_real
    stack = lambda ref, g: jnp.concatenate(
        [ref[0, :, (g * NSA_GROUP + hh) * hd:(g * NSA_GROUP + hh + 1) * hd] for hh in range(NSA_GROUP)], axis=0)
    rows = NSA_GROUP * tq

    @pl.when(c == 0)
    def _():
        n_blocks = -(-(past_len + t_real) // SEL_BLOCK)
        for g in range(NSA_KV_HEADS):
            ksl = slice(g * hd, (g + 1) * hd)
            o_cmp, imp = _cmp_branch(stack(qn_ref, g), kvc_ref[0, 0][:, ksl], kvc_ref[0, 1][:, ksl], tpos, tq)
            imp = jnp.concatenate([imp, jnp.zeros((tq, LANES), F32)], axis=-1)
            sel_ref[g] = _select_blocks(imp, tpos, n_blocks)
            ocmp_ref[g] = o_cmp
        m_ref[...] = jnp.full(m_ref.shape, M_INIT, F32)
        l_ref[...] = jnp.zeros(l_ref.shape, F32)
        acc_ref[...] = jnp.zeros(acc_ref.shape, F32)

    tk = n_step * pages[0].shape[-1]
    k0 = c * tk
    kpos = k0 + lax.broadcasted_iota(jnp.int32, (1, tk), 1)
    for g in range(NSA_KV_HEADS):
        kk = jnp.concatenate([pg[0, 0, 0, g] for pg in pages], axis=1)
        vv = jnp.concatenate([pg[0, 0, 1, g] for pg in pages], axis=1)
        valid = jnp.where((_expand_sel(sel_ref[g][:, :n_cmp], k0, tk) > 0.5) & (kpos <= tpos), 1.0, 0.0)
        q4 = (stack(qr_ref, g) * NSA_SCALE).astype(BF16)
        m, l, acc = _flash_step(q4, kk, vv, valid, (m_ref[g], l_ref[g], acc_ref[g]), True)
        m_ref[g] = m
        l_ref[g] = l
        acc_ref[g] = acc

    @pl.when(c == pl.num_programs(1) - 1)
    def _():
        gate = gate_ref[0]
        rnew = lax.broadcasted_iota(jnp.int32, (1, tq), 1)
        newpos = past_len + rnew
        n_buf = wbuf_ref.shape[-1]
        bpos = past_len - n_buf + lax.broadcasted_iota(jnp.int32, (1, n_buf), 1)
        new_lane = 2 * (past_len // SEL_BLOCK)
        heads = []
        for g in range(NSA_KV_HEADS):
            q4 = (stack(qr_ref, g) * NSA_SCALE).astype(BF16)
            ks, vs = slice(g * hd, (g + 1) * hd), slice(kvw + g * hd, kvw + (g + 1) * hd)
            picked = sel_ref[g][:, new_lane:new_lane + 1] > 0.5
            valid = jnp.where(picked & (newpos <= tpos) & (rnew < t_real), 1.0, 0.0)
            _, l_s, acc_s = _flash_step(q4, snew_ref[0, :, ks], snew_ref[0, :, vs], valid,
                                        (m_ref[g], l_ref[g], acc_ref[g]), False)
            dist = tpos - bpos
            valid = jnp.where((dist >= 0) & (dist <= WINDOW) & (bpos >= 0), 1.0, 0.0)
            init = (jnp.full((rows, 1), M_INIT, F32), jnp.zeros((rows, 1), F32), jnp.zeros((rows, hd), F32))
            carry = _flash_step(q4, wbuf_ref[0, 0, 0, g], wbuf_ref[0, 0, 1, g], valid, init, True)
            dist = tpos - newpos
            valid = jnp.where((dist >= 0) & (dist <= WINDOW) & (rnew < t_real), 1.0, 0.0)
            _, l_w, acc_w = _flash_step(q4, wnew_ref[0, :, ks], wnew_ref[0, :, vs], valid, carry, False)
            o_cmp = ocmp_ref[g]
            o_slc = acc_s / l_s
            o_win = acc_w / l_w
            for hh in range(NSA_GROUP):
                head = g * NSA_GROUP + hh
                hr = slice(hh * tq, (hh + 1) * tq)
                col = (head // (NSA_HEADS // 2)) * LANES + (head % (NSA_HEADS // 2)) * 3
                heads.append(o_cmp[hr] * gate[:, col:col + 1] + o_slc[hr] * gate[:, col + 1:col + 2]
                             + o_win[hr] * gate[:, col + 2:col + 3])
        o_ref[0] = jnp.concatenate(heads, axis=-1)


SAMPLE_PAGES_PER_STEP = 8


def _attn_sample_call(qn, qr, kvc, gate, slc_new, win_buf_t, win_new, pool_t, layer, page_table, t_real):
    b, tq, nq = qn.shape
    n_pages = page_table.shape[1]
    page = pool_t.shape[-1]
    past_len = n_pages * page
    assert past_len % SEL_BLOCK == 0 and t_real <= SEL_BLOCK and past_len % (CMP_BLOCK * LANES) == 0
    n = SAMPLE_PAGES_PER_STEP
    rows = NSA_GROUP * tq
    full = lambda a: pl.BlockSpec((1,) + a.shape[1:], lambda i, c, pt: (i,) + (0,) * (a.ndim - 1))
    wbuf_spec = pl.BlockSpec((1, 1) + win_buf_t.shape[2:], lambda i, c, pt: (layer, i, 0, 0, 0, 0))
    sel_lanes = past_len // CMP_BLOCK + LANES
    return pl.pallas_call(
        functools.partial(_attn_sample_kernel, n, t_real, past_len),
        grid_spec=pltpu.PrefetchScalarGridSpec(
            num_scalar_prefetch=1,
            grid=(b, n_pages // n),
            in_specs=[full(a) for a in (qn, qr, kvc, gate, slc_new)] + [wbuf_spec, full(win_new)]
            + _page_specs(n, layer, pool_t.shape[2:]),
            out_specs=pl.BlockSpec((1, tq, nq), lambda i, c, pt: (i, 0, 0)),
            scratch_shapes=[pltpu.VMEM((NSA_KV_HEADS, tq, sel_lanes), F32),
                            pltpu.VMEM((NSA_KV_HEADS, rows, NSA_HEAD_DIM), F32),
                            pltpu.VMEM((NSA_KV_HEADS, rows, 1), F32),
                            pltpu.VMEM((NSA_KV_HEADS, rows, 1), F32),
                            pltpu.VMEM((NSA_KV_HEADS, rows, NSA_HEAD_DIM), F32)],
        ),
        out_shape=jax.ShapeDtypeStruct((b, tq, nq), F32),
        compiler_params=_cparams(2),
        name="nsa_attn_sample",
    )(page_table, qn, qr, kvc, gate, slc_new, win_buf_t, win_new, *([pool_t] * n))


def _rope_tables(pos):
    half = NSA_HEAD_DIM // 2
    inv = ROPE_THETA ** (-jnp.arange(half, dtype=F32) / half)
    ang = pos.astype(F32)[:, None] * inv[None, :]
    cos, sin = jnp.cos(ang), jnp.sin(ang)
    reps = LANES // NSA_HEAD_DIM
    return jnp.tile(cos, (1, 2 * reps)), jnp.tile(jnp.concatenate([-sin, sin], axis=1), (1, reps))


def _nsa_params(w_in, g_q, g_ks, g_kw, g_kc, pe, w1, w2):
    d = w_in.shape[0]
    body = NSA_HEADS * NSA_HEAD_DIM + 6 * NSA_KV_WIDTH
    half = NSA_HEADS // 2 * 3
    zpad = jnp.zeros((d, LANES - half), w_in.dtype)
    w_pad = jnp.concatenate([w_in[:, :body], w_in[:, body:body + half], zpad, w_in[:, body + half:], zpad], axis=1)
    return dict(
        w_in=w_pad.astype(BF16),
        g_q=jnp.tile(g_q, NSA_HEADS)[None], g_ks=jnp.tile(g_ks, NSA_KV_HEADS)[None],
        g_kw=jnp.tile(g_kw, NSA_KV_HEADS)[None], g_kc=g_kc[None],
        pe=pe.reshape(2, CMP_BLOCK // 2, LANES), w1=w1.astype(BF16), w2=w2.astype(BF16))


def _kv_rows(x, b, t):
    return x.reshape(b, t, 2, NSA_KV_HEADS, NSA_HEAD_DIM)


def _nsa_prompt_layer(h, gmix, prm):
    b, t, d = h.shape
    cos, sin = _rope_tables(jnp.arange(t))
    qn, qr, cmp, slc, wrow, gate = _nsa_proj_call(
        h.reshape(b * t, d), gmix, prm["w_in"], prm["g_q"], prm["g_ks"], prm["g_kw"], cos, sin)
    r3 = lambda x: x.reshape(b, t, x.shape[-1])
    kvc = _compress_rows_call(r3(cmp), prm["pe"], prm["w1"], prm["w2"], prm["g_kc"])
    o = _attn_prompt_call(r3(qn), r3(qr), kvc, r3(slc), r3(wrow), r3(gate))
    keep = min(WINDOW, t)
    return (o.reshape(b * t, -1), _kv_rows(cmp, b, t), _kv_rows(slc, b, t), _kv_rows(wrow, b, t)[:, t - keep:])


def _nsa_sample_layer(h, cmp_pool_t, slc_pool_t, win_buf, win_buf_t, layer, page_table, gmix, prm):
    b, t, d = h.shape
    past_len = page_table.shape[1] * cmp_pool_t.shape[-1]
    cos, sin = _rope_tables(jnp.tile(past_len + jnp.arange(t), b))
    qn, qr, cmp, slc, wrow, gate = _nsa_proj_call(
        h.reshape(b * t, d), gmix, prm["w_in"], prm["g_q"], prm["g_ks"], prm["g_kw"], cos, sin)
    kvc = _compress_pages_call(cmp_pool_t, layer, page_table, prm["pe"], prm["w1"], prm["w2"], prm["g_kc"])
    tq = 8
    pad = lambda x: jnp.pad(x.reshape(b, t, x.shape[-1]), ((0, 0), (0, tq - t), (0, 0)))
    o = _attn_sample_call(pad(qn), pad(qr), kvc, pad(gate), pad(slc), win_buf_t, pad(wrow), slc_pool_t, layer,
                          page_table, t)
    new_win = jnp.concatenate([win_buf, _kv_rows(wrow, b, t)], axis=1)[:, t:]
    return o[:, :t].reshape(b * t, -1), _kv_rows(cmp, b, t), _kv_rows(slc, b, t), new_win


def _softplus(z):
    return jnp.maximum(z, 0.0) + jnp.log1p(jnp.exp(-jnp.abs(z)))


def _rwkv_proj_kernel(seq_len, tiles_per_seq, h_ref, shift_ref, gmix_ref, mu_ref, wr_ref, wk_ref, wv_ref,
                      w0_ref, w1_ref, w2_ref, a0_ref, a1_ref, a2_ref, g1_ref, g2_ref,
                      r_ref, dec_ref, k_ref, v_ref, a_ref, g_ref, u_ref, carry_ref):
    i = pl.program_id(0)
    u = _rms(h_ref[...]) * gmix_ref[...]
    tm = u.shape[0]
    rowid = lax.broadcasted_iota(jnp.int32, (tm, 1), 0)
    rolled = pltpu.roll(u, 1, axis=0)
    if tiles_per_seq:
        @pl.when(i == 0)
        def _():
            carry_ref[...] = jnp.zeros_like(carry_ref)
        first = jnp.where(i % tiles_per_seq == 0, shift_ref[0], carry_ref[...])
        prev = jnp.where(rowid == 0, first, rolled)
        carry_ref[...] = u[tm - 1:tm]
        u_ref[0] = u[tm - 8:tm]
    else:
        prev = jnp.where(rowid % seq_len == 0, shift_ref[...], rolled)
        u_ref[...] = u
    xx = prev - u
    mix = lambda j: (u + xx * mu_ref[j:j + 1]).astype(BF16)
    r_ref[...] = _dot(mix(0), wr_ref[...])
    wl = w0_ref[...] + _dot(jnp.tanh(_dot(mix(1), w1_ref[...])).astype(BF16), w2_ref[...])
    dec_ref[...] = jnp.exp(-jnp.exp(-_softplus(-wl) - 0.5))
    k_ref[...] = _dot(mix(2), wk_ref[...])
    v_ref[...] = _dot(mix(3), wv_ref[...])
    a_ref[...] = jax.nn.sigmoid(a0_ref[...] + _dot(_dot(mix(4), a1_ref[...]).astype(BF16), a2_ref[...]))
    g_ref[...] = _dot(jax.nn.sigmoid(_dot(mix(5), g1_ref[...])).astype(BF16), g2_ref[...])


def _rwkv_proj_call(h, shift, seq_len, gmix, prm):
    m, d = h.shape
    tm = min(m, 256)
    row = pl.BlockSpec((tm, d), lambda i: (i, 0))
    if seq_len >= tm:
        tiles_per_seq = seq_len // tm
        shift_spec = pl.BlockSpec((1, 1, d), lambda i: (i // tiles_per_seq, 0, 0))
        shift = shift[:, None]
        u_shape, u_spec = jax.ShapeDtypeStruct((m // tm, 8, d), F32), pl.BlockSpec((1, 8, d), lambda i: (i, 0, 0))
    else:
        tiles_per_seq = 0
        shift_spec = row
        shift = jnp.pad(shift[:, None], ((0, 0), (0, seq_len - 1), (0, 0))).reshape(m, d)
        u_shape, u_spec = jax.ShapeDtypeStruct((m, d), F32), row
    weights = [prm[n] for n in ("mu", "w_r", "w_k", "w_v", "w0", "w1", "w2", "a0", "a1", "a2", "g1", "g2")]
    outs = pl.pallas_call(
        functools.partial(_rwkv_proj_kernel, seq_len, tiles_per_seq),
        grid=(m // tm,),
        in_specs=[row, shift_spec, _resident(gmix.shape)] + [_resident(w.shape) for w in weights],
        out_specs=[row] * 6 + [u_spec],
        out_shape=[jax.ShapeDtypeStruct((m, d), F32)] * 6 + [u_shape],
        scratch_shapes=[pltpu.VMEM((1, d), F32)],
        compiler_params=_cparams(1),
        name="rwkv_proj",
    )(h, shift, gmix, *weights)
    u = outs[6]
    if tiles_per_seq:
        new_shift = u[tiles_per_seq - 1::tiles_per_seq, 7]
    else:
        new_shift = u[seq_len - 1::seq_len]
    return outs[:6], new_shift


def _rwkv_scan_kernel(tc, r_ref, dec_ref, k_ref, v_ref, a_ref, kk_ref, ka_ref, rk_ref, lw_ref, lb_ref, s0_ref,
                      y_ref, sout_ref, state_ref, vec_ref):
    c = pl.program_id(1)
    n = RWKV_HEAD_DIM

    @pl.when(c == 0)
    def _():
        state_ref[...] = s0_ref[...]

    def step(t, _):
        r = r_ref[t]
        k = k_ref[t]
        v = v_ref[t]
        a = a_ref[t]
        kk = k * kk_ref[...]
        kk = kk / jnp.maximum(jnp.sqrt(jnp.sum(kk * kk, axis=0, keepdims=True)), 1e-12)
        k2 = k * (1.0 + (a - 1.0) * ka_ref[...])
        vec_ref[0] = -kk
        vec_ref[1] = kk * a
        vec_ref[2] = dec_ref[t]
        vec_ref[3] = k2
        vec_ref[4] = r
        sa = jnp.zeros((n, LANES), F32)
        for j in range(n):
            sa = sa + state_ref[j] * vec_ref[0, j:j + 1, :]
        y = jnp.zeros((n, LANES), F32)
        for j in range(n):
            s_new = (state_ref[j] * vec_ref[2, j:j + 1, :] + sa * vec_ref[1, j:j + 1, :]
                     + v * vec_ref[3, j:j + 1, :])
            state_ref[j] = s_new
            y = y + s_new * vec_ref[4, j:j + 1, :]
        mean = jnp.mean(y, axis=0, keepdims=True)
        var = jnp.mean(jnp.square(y - mean), axis=0, keepdims=True)
        y = (y - mean) * lax.rsqrt(var + LNX_EPS) * lw_ref[...] + lb_ref[...]
        y_ref[t] = y + jnp.sum(r * k2 * rk_ref[...], axis=0, keepdims=True) * v
        return 0

    lax.fori_loop(0, tc, step, 0)

    @pl.when(c == pl.num_programs(1) - 1)
    def _():
        sout_ref[...] = state_ref[...]


def _rwkv_scan_call(r, dec, k, v, a, kk, ka, rk, lw, lb, s0):
    t, n, l = r.shape
    tc = math.gcd(t, 64)
    seq = pl.BlockSpec((tc, n, LANES), lambda g, c: (c, 0, g))
    par = pl.BlockSpec((n, LANES), lambda g, c: (0, g))
    st = pl.BlockSpec((n, n, LANES), lambda g, c: (0, 0, g))
    return pl.pallas_call(
        functools.partial(_rwkv_scan_kernel, tc),
        grid=(l // LANES, t // tc),
        in_specs=[seq] * 5 + [par] * 5 + [st],
        out_specs=[seq, st],
        out_shape=[jax.ShapeDtypeStruct((t, n, l), F32), jax.ShapeDtypeStruct((n, n, l), F32)],
        scratch_shapes=[pltpu.VMEM((n, n, LANES), F32), pltpu.VMEM((5, n, LANES), F32)],
        compiler_params=_cparams(2),
        name="rwkv_scan",
    )(r, dec, k, v, a, kk, ka, rk, lw, lb, s0)


def _rwkv_params(mu, w_r, w_k, w_v, w0, w1, w2, a0, a1, a2, g1, g2):
    gpad = -g1.shape[1] % LANES
    return dict(mu=mu, w_r=w_r.astype(BF16), w_k=w_k.astype(BF16), w_v=w_v.astype(BF16), w0=w0[None],
                w1=w1.astype(BF16), w2=w2.astype(BF16), a0=a0[None], a1=a1.astype(BF16), a2=a2.astype(BF16),
                g1=jnp.pad(g1, ((0, 0), (0, gpad))).astype(BF16), g2=jnp.pad(g2, ((0, gpad), (0, 0))).astype(BF16))


def _rwkv_layer(h, shift, s0, gmix, prm, k_k, k_a, r_k, lnx_w, lnx_b):
    b, t, d = h.shape
    n = RWKV_HEAD_DIM
    nh = d // n
    (r, dec, k, v, a, g), new_shift = _rwkv_proj_call(h.reshape(b * t, d), shift, t, gmix, prm)
    lanes = lambda x: x.reshape(b, t, nh, n).transpose(1, 3, 0, 2).reshape(t, n, b * nh)
    par = lambda x: jnp.tile(x.reshape(nh, n).T, (1, b))
    y, s = _rwkv_scan_call(lanes(r), lanes(dec), lanes(k), lanes(v), lanes(a), par(k_k), par(k_a), par(r_k),
                           par(lnx_w), par(lnx_b), s0.transpose(3, 2, 0, 1).reshape(n, n, b * nh))
    y = y.reshape(t, n, b, nh).transpose(2, 0, 3, 1).reshape(b * t, d)
    s = s.reshape(n, n, b, nh).transpose(2, 3, 1, 0)
    return y, g, new_shift, s


def _hgrn_proj_kernel(h_ref, gmix_ref, win_ref, lb_ref, omlb_ref, q_ref, f_ref, k_ref, i_ref, g_ref):
    d = h_ref.shape[1]
    u = (_rms(h_ref[...]) * gmix_ref[...]).astype(BF16)
    z = _dot(u, win_ref[...])
    q, f, g = z[:, :d], z[:, d:2 * d], z[:, 3 * d:]
    q_ref[...] = q * jax.nn.sigmoid(q)
    f_ref[...] = lb_ref[...] + omlb_ref[...] * jax.nn.sigmoid(f)
    k_ref[...] = omlb_ref[...] * jax.nn.sigmoid(-f)
    i_ref[...] = z[:, 2 * d:3 * d]
    g_ref[...] = g * jax.nn.sigmoid(g)


def _hgrn_proj_call(h, gmix, win, lb, omlb):
    m, d = h.shape
    tm = min(m, 256)
    row = pl.BlockSpec((tm, d), lambda i: (i, 0))
    return pl.pallas_call(
        _hgrn_proj_kernel,
        grid=(m // tm,),
        in_specs=[row, _resident(gmix.shape), _resident(win.shape), _resident(lb.shape), _resident(omlb.shape)],
        out_specs=[row] * 5,
        out_shape=[jax.ShapeDtypeStruct((m, d), F32)] * 5,
        compiler_params=_cparams(1),
        name="hgrn_proj",
    )(h, gmix, win, lb, omlb)


def _hgrn_scan_kernel(tc, q_ref, f_ref, k_ref, v_ref, s0_ref, o_ref, sout_ref, state_ref):
    c = pl.program_id(1)
    dk = state_ref.shape[0]

    @pl.when(c == 0)
    def _():
        state_ref[...] = s0_ref[...]

    def step(t, _):
        v = v_ref[t]
        o = jnp.zeros(v.shape, F32)
        for d in range(dk):
            s_new = state_ref[d] * f_ref[t, d:d + 1, :] + k_ref[t, d:d + 1, :] * v
            state_ref[d] = s_new
            o = o + s_new * q_ref[t, d:d + 1, :]
        o_ref[t] = o
        return 0

    lax.fori_loop(0, tc, step, 0)

    @pl.when(c == pl.num_programs(1) - 1)
    def _():
        sout_ref[...] = state_ref[...]


def _hgrn_scan_call(q, f, k, v, s0):
    t, dk, l = q.shape
    dv = v.shape[1]
    tc = math.gcd(t, 32)
    kspec = pl.BlockSpec((tc, dk, LANES), lambda g, c: (c, 0, g))
    vspec = pl.BlockSpec((tc, dv, LANES), lambda g, c: (c, 0, g))
    st = pl.BlockSpec((dk, dv, LANES), lambda g, c: (0, 0, g))
    return pl.pallas_call(
        functools.partial(_hgrn_scan_kernel, tc),
        grid=(l // LANES, t // tc),
        in_specs=[kspec] * 3 + [vspec, st],
        out_specs=[vspec, st],
        out_shape=[jax.ShapeDtypeStruct((t, dv, l), F32), jax.ShapeDtypeStruct((dk, dv, l), F32)],
        scratch_shapes=[pltpu.VMEM((dk, dv, LANES), F32)],
        compiler_params=_cparams(2),
        name="hgrn_scan",
    )(q, f, k, v, s0)


def _hgrn_layer(h, s0, gmix, win, lb, omlb):
    b, t, d = h.shape
    n = HGRN_HEAD_DIM
    nh = d // n
    vs = 2
    q, f, k, i, g = _hgrn_proj_call(h.reshape(b * t, d), gmix, win, lb, omlb)
    keys = lambda x: jnp.tile(x.reshape(b, t, nh, n).transpose(1, 3, 0, 2).reshape(t, n, b * nh), (1, 1, vs))
    vals = i.reshape(b, t, nh, vs, n // vs).transpose(1, 4, 3, 0, 2).reshape(t, n // vs, vs * b * nh)
    st = s0.reshape(b, nh, n, vs, n // vs).transpose(2, 4, 3, 0, 1).reshape(n, n // vs, vs * b * nh)
    o, s = _hgrn_scan_call(keys(q), keys(f), keys(k), vals, st)
    o = o.reshape(t, n // vs, vs, b, nh).transpose(3, 0, 4, 2, 1).reshape(b * t, d)
    s = s.reshape(n, n // vs, vs, b, nh).transpose(3, 4, 0, 2, 1).reshape(b, nh, n, n)
    return o, g, s


def kernel(x_prompt, x_sample, cache_cmp, cache_slc, cache_win, state_rwkv_shift, state_rwkv_wkv, state_hgrn,
           page_table, p_prompt, p_sample, norm_mix, norm_ffn, w_up, w_down, w_ple, w_ple_gate,
           nsa_w_in, nsa_g_q, nsa_g_ks, nsa_g_kw, nsa_g_kc, nsa_cmp_pe, nsa_cmp_w1, nsa_cmp_w2, nsa_w_out,
           rwkv_mu, rwkv_w_r, rwkv_w_k, rwkv_w_v, rwkv_w_o, rwkv_w0, rwkv_w1, rwkv_w2, rwkv_a0, rwkv_a1,
           rwkv_a2, rwkv_g1, rwkv_g2, rwkv_k_k, rwkv_k_a, rwkv_r_k, rwkv_lnx_w, rwkv_lnx_b,
           hgrn_w_in, hgrn_gn, hgrn_w_o, hgrn_lower_bounds):
    depth = norm_mix.shape[0]
    bp, tp, d = x_prompt.shape
    bs, ts, _ = x_sample.shape
    lb_soft = jax.nn.softmax(hgrn_lower_bounds.astype(F32), axis=0)
    lower_bound = jnp.cumsum(lb_soft, axis=0) - lb_soft[0]
    rows_minor = lambda x: jnp.transpose(x, (0, 1, 3, 4, 5, 2))
    cmp_pool_t, slc_pool_t, win_buf_t = rows_minor(cache_cmp), rows_minor(cache_slc), rows_minor(cache_win)
    hp, hs = x_prompt.reshape(bp * tp, d), x_sample.reshape(bs * ts, d)
    outs = [[] for _ in range(12)]
    for i in range(depth):
        kind, n = i % N_MIXERS, i // N_MIXERS
        gmix = norm_mix[i][None]
        hp3, hs3 = hp.reshape(bp, tp, d), hs.reshape(bs, ts, d)
        ffn = (p_prompt[i].reshape(bp * tp, -1), p_sample[i].reshape(bs * ts, -1))
        tail = (norm_ffn[i][None], w_up[i].astype(BF16), w_down[i].astype(BF16), w_ple[i].astype(BF16),
                w_ple_gate[i].astype(BF16))
        if kind == 0:
            prm = _nsa_params(nsa_w_in[n], nsa_g_q[n], nsa_g_ks[n], nsa_g_kw[n], nsa_g_kc[n], nsa_cmp_pe[n],
                              nsa_cmp_w1[n], nsa_cmp_w2[n])
            op, rc, rs, rw = _nsa_prompt_layer(hp3, gmix, prm)
            os_, nc_rows, ns_rows, nw_buf = _nsa_sample_layer(hs3, cmp_pool_t, slc_pool_t, cache_win[n], win_buf_t,
                                                              n, page_table, gmix, prm)
            for lst, v in zip(outs[:6], (rc, nc_rows, rs, ns_rows, rw, nw_buf)):
                lst.append(v)
            wo = nsa_w_out[n].astype(BF16)
            hp = _ffn_call("nsa", hp, op, None, None, ffn[0], wo, *tail)
            hs = _ffn_call("nsa", hs, os_, None, None, ffn[1], wo, *tail)
        elif kind == 1:
            prm = _rwkv_params(rwkv_mu[n], rwkv_w_r[n], rwkv_w_k[n], rwkv_w_v[n], rwkv_w0[n], rwkv_w1[n],
                               rwkv_w2[n], rwkv_a0[n], rwkv_a1[n], rwkv_a2[n], rwkv_g1[n], rwkv_g2[n])
            vecs = (rwkv_k_k[n], rwkv_k_a[n], rwkv_r_k[n], rwkv_lnx_w[n], rwkv_lnx_b[n])
            nh = d // RWKV_HEAD_DIM
            zero_state = jnp.zeros((bp, nh, RWKV_HEAD_DIM, RWKV_HEAD_DIM), F32)
            yp, gp, shp, sp = _rwkv_layer(hp3, jnp.zeros((bp, d), F32), zero_state, gmix, prm, *vecs)
            ys, gs, shs, ss = _rwkv_layer(hs3, state_rwkv_shift[n], state_rwkv_wkv[n].astype(F32), gmix, prm, *vecs)
            for lst, v in zip(outs[6:10], (shp, shs, sp, ss)):
                lst.append(v)
            wo = rwkv_w_o[n].astype(BF16)
            hp = _ffn_call("rwkv", hp, yp, gp, None, ffn[0], wo, *tail)
            hs = _ffn_call("rwkv", hs, ys, gs, None, ffn[1], wo, *tail)
        else:
            lb = lower_bound[i][None]
            win = hgrn_w_in[n].astype(BF16)
            nh = d // HGRN_HEAD_DIM
            zero_state = jnp.zeros((bp, nh, HGRN_HEAD_DIM, HGRN_HEAD_DIM), F32)
            op, gp, sp = _hgrn_layer(hp3, zero_state, gmix, win, lb, 1.0 - lb)
            os_, gs, ss = _hgrn_layer(hs3, state_hgrn[n].astype(F32), gmix, win, lb, 1.0 - lb)
            outs[10].append(sp)
            outs[11].append(ss)
            wo, gn = hgrn_w_o[n].astype(BF16), hgrn_gn[n][None]
            hp = _ffn_call("hgrn", hp, op, gp, gn, ffn[0], wo, *tail)
            hs = _ffn_call("hgrn", hs, os_, gs, gn, ffn[1], wo, *tail)
    return (hp.reshape(bp, tp, d), hs.reshape(bs, ts, d)) + tuple(jnp.stack(o) for o in outs)
```

```python
import functools
import math

import jax
import jax.numpy as jnp
from jax import lax
from jax.experimental import pallas as pl
from jax.experimental.pallas import tpu as pltpu

F32 = jnp.float32
BF16 = jnp.bfloat16

NORM_EPS = 1e-6
ROPE_THETA = 10000.0
NEG_INF = -1e30
M_INIT = -1e29
N_MIXERS = 3

NSA_HEADS = 16
NSA_KV_HEADS = 4
NSA_HEAD_DIM = 64
NSA_GROUP = NSA_HEADS // NSA_KV_HEADS
NSA_KV_WIDTH = NSA_KV_HEADS * NSA_HEAD_DIM
NSA_SCALE = NSA_HEAD_DIM ** -0.5
CMP_BLOCK = 32
SEL_BLOCK = 64
N_SELECT = 16
WINDOW = 512
FORCE_SCORE = 1e4

RWKV_HEAD_DIM = 64
LNX_EPS = 64e-5
HGRN_HEAD_DIM = 128

LANES = 128
VMEM_LIMIT = 56 * 1024 * 1024


def _cparams(n_axes):
    return pltpu.CompilerParams(dimension_semantics=("arbitrary",) * n_axes,
                                vmem_limit_bytes=VMEM_LIMIT)


def _resident(shape):
    zeros = (0,) * len(shape)
    return pl.BlockSpec(shape, lambda *_: zeros, pipeline_mode=pl.Buffered(1))


def _rms(x):
    return x * lax.rsqrt(jnp.mean(x * x, axis=-1, keepdims=True) + NORM_EPS)


def _dot(a, b):
    return jnp.dot(a, b, preferred_element_type=F32)


def _dot_nt(a, b):
    return lax.dot_general(a, b, (((1,), (1,)), ((), ())), preferred_element_type=F32)


def _per_chunk(fn, x, *rest):
    n = x.shape[-1] // LANES
    outs = [fn(*(a[:, c * LANES:(c + 1) * LANES] for a in (x,) + rest)) for c in range(n)]
    return outs[0] if n == 1 else jnp.concatenate(outs, axis=-1)


def _lane_iota():
    return lax.broadcasted_iota(jnp.int32, (1, LANES), 1)


def _group_ones(group):
    r = lax.broadcasted_iota(jnp.int32, (LANES, LANES), 0) // group
    c = lax.broadcasted_iota(jnp.int32, (LANES, LANES), 1) // group
    return jnp.where(r == c, 1.0, 0.0).astype(BF16)


def _head_rms(x, head_dim, ones):
    def one(c):
        sq = c * c
        hi = sq.astype(BF16)
        lo = (sq - hi.astype(F32)).astype(BF16)
        ss = _dot(hi, ones) + _dot(lo, ones)
        return c * lax.rsqrt(ss * (1.0 / head_dim) + NORM_EPS)
    return _per_chunk(one, x)


def _rope(x, cos, sin_signed):
    half = NSA_HEAD_DIM // 2
    lane = _lane_iota()

    def one(c):
        up = pltpu.roll(c, LANES - half, axis=1)
        dn = pltpu.roll(c, half, axis=1)
        rot = jnp.where((lane & half) == 0, up, dn)
        return c * cos + rot * sin_signed
    return _per_chunk(one, x)


def _ffn_kernel(mode, ff_chunk, *refs):
    if mode == "nsa":
        h_ref, o_ref, p_ref, wo_ref, gffn_ref, wup_ref, wdown_ref, wple_ref, wgate_ref, out_ref = refs
        o = o_ref[...]
    elif mode == "rwkv":
        h_ref, o_ref, aux_ref, p_ref, wo_ref, gffn_ref, wup_ref, wdown_ref, wple_ref, wgate_ref, out_ref = refs
        o = o_ref[...] * aux_ref[...]
    else:
        (h_ref, o_ref, aux_ref, gn_ref, p_ref, wo_ref, gffn_ref, wup_ref, wdown_ref, wple_ref, wgate_ref,
         out_ref) = refs
        gn = gn_ref[...]
        o = _per_chunk(lambda c: _rms(c) * gn, o_ref[...]) * aux_ref[...]
    h1 = h_ref[...] + _dot(o.astype(BF16), wo_ref[...])
    u = (_rms(h1) * gffn_ref[...]).astype(BF16)
    d_ff = wup_ref.shape[1]
    acc = jnp.zeros_like(h1)
    for j in range(d_ff // ff_chunk):
        a = _dot(u, wup_ref[:, j * ff_chunk:(j + 1) * ff_chunk])
        a = jnp.square(jnp.maximum(a, 0.0)).astype(BF16)
        acc = acc + _dot(a, wdown_ref[j * ff_chunk:(j + 1) * ff_chunk, :])
    h2 = h1 + acc
    gate = jax.nn.sigmoid(_dot(_rms(h2).astype(BF16), wgate_ref[...]))
    out_ref[...] = h2 + _dot(p_ref[...].astype(BF16), wple_ref[...]) * gate


def _ffn_call(mode, h, o, aux, gn, p, wo, gffn, wup, wdown, wple, wgate):
    m, d = h.shape
    tm = min(m, 256)
    row = lambda w: pl.BlockSpec((tm, w), lambda i: (i, 0))
    args, specs = [h, o], [row(d), row(d)]
    if mode != "nsa":
        args.append(aux)
        specs.append(row(d))
    if mode == "hgrn":
        args.append(gn)
        specs.append(_resident(gn.shape))
    args += [p, wo, gffn, wup, wdown, wple, wgate]
    specs += [row(p.shape[1])] + [_resident(a.shape) for a in (wo, gffn, wup, wdown, wple, wgate)]
    return pl.pallas_call(
        functools.partial(_ffn_kernel, mode, 1024),
        grid=(m // tm,),
        in_specs=specs,
        out_specs=row(d),
        out_shape=jax.ShapeDtypeStruct((m, d), F32),
        compiler_params=_cparams(1),
        name="ffn_" + mode,
    )(*args)


def _nsa_proj_kernel(h_ref, gmix_ref, win_ref, gq_ref, gks_ref, gkw_ref, cos_ref, sin_ref,
                     qn_ref, qr_ref, cmp_ref, slc_ref, wrow_ref, gate_ref):
    kvw = NSA_KV_WIDTH
    nq = NSA_HEADS * NSA_HEAD_DIM
    u = (_rms(h_ref[...]) * gmix_ref[...]).astype(BF16)
    z = _dot(u, win_ref[...])
    cos = cos_ref[...]
    sin = sin_ref[...]
    ones = _group_ones(NSA_HEAD_DIM)
    qn = _head_rms(z[:, :nq], NSA_HEAD_DIM, ones) * gq_ref[...]
    qn_ref[...] = qn
    qr_ref[...] = _rope(qn, cos, sin)
    cmp_ref[...] = z[:, nq:nq + 2 * kvw]
    o = nq + 2 * kvw
    slc_ref[:, :kvw] = _rope(_head_rms(z[:, o:o + kvw], NSA_HEAD_DIM, ones) * gks_ref[...], cos, sin)
    slc_ref[:, kvw:] = z[:, o + kvw:o + 2 * kvw]
    o += 2 * kvw
    wrow_ref[:, :kvw] = _rope(_head_rms(z[:, o:o + kvw], NSA_HEAD_DIM, ones) * gkw_ref[...], cos, sin)
    wrow_ref[:, kvw:] = z[:, o + kvw:o + 2 * kvw]
    o += 2 * kvw
    gate_ref[...] = jax.nn.sigmoid(z[:, o:])


def _nsa_proj_call(h, gmix, win, gq, gks, gkw, cos, sin):
    m, d = h.shape
    tm = min(m, 256)
    tab_tiles = cos.shape[0] // tm
    row = lambda w: pl.BlockSpec((tm, w), lambda i: (i, 0))
    tab = pl.BlockSpec((tm, LANES), lambda i: (i % tab_tiles, 0))
    nq = NSA_HEADS * NSA_HEAD_DIM
    widths = [nq, nq] + [2 * NSA_KV_WIDTH] * 3 + [2 * LANES]
    return pl.pallas_call(
        _nsa_proj_kernel,
        grid=(m // tm,),
        in_specs=[row(d), _resident(gmix.shape), _resident(win.shape), _resident(gq.shape),
                  _resident(gks.shape), _resident(gkw.shape), tab, tab],
        out_specs=[row(w) for w in widths],
        out_shape=[jax.ShapeDtypeStruct((m, w), F32) for w in widths],
        compiler_params=_cparams(1),
        name="nsa_proj",
    )(h, gmix, win, gq, gks, gkw, cos, sin)


def _compress_slot(load_pair, m, e, pe_ref, w1_ref, w2_ref, gkc_ref):
    lane = _lane_iota()
    low = lane < NSA_HEAD_DIM
    acc = jnp.zeros((NSA_KV_HEADS * m, w1_ref.shape[2]), F32)
    for s in range(CMP_BLOCK // 2):
        pe_row = pe_ref[e, s:s + 1, :]
        parts = []
        for kp in range(NSA_KV_HEADS // 2):
            a = load_pair(kp, 2 * s)
            b = load_pair(kp, 2 * s + 1)
            parts.append(jnp.where(low, a, pltpu.roll(b, NSA_HEAD_DIM, axis=1)))
            parts.append(jnp.where(low, pltpu.roll(a, NSA_HEAD_DIM, axis=1), b))
        x = (jnp.concatenate(parts, axis=0) + pe_row).astype(BF16)
        acc = acc + _dot(x, w1_ref[e, s * LANES:(s + 1) * LANES, :])
    out = _dot(jax.nn.gelu(acc).astype(BF16), w2_ref[e])
    if e == 0:
        out = _rms(out) * gkc_ref[...]
    return jnp.concatenate([out[k * m:(k + 1) * m] for k in range(NSA_KV_HEADS)], axis=-1)


def _compress_rows_kernel(*refs):
    cols = refs[:NSA_KV_HEADS]
    pe_ref, w1_ref, w2_ref, gkc_ref, out_ref = refs[NSA_KV_HEADS:]
    m = cols[0].shape[1] // CMP_BLOCK
    pad = out_ref.shape[2] - m
    for e in range(2):
        load = lambda kp, r: cols[2 * e + kp][0, pl.ds(r, m, stride=CMP_BLOCK), :]
        res = _compress_slot(load, m, e, pe_ref, w1_ref, w2_ref, gkc_ref)
        out_ref[0, e] = jnp.concatenate([res, jnp.zeros((pad, res.shape[1]), F32)], axis=0)


def _compress_rows_call(cmp, pe, w1, w2, gkc):
    b, t, w = cmp.shape
    c = -(-(t // CMP_BLOCK) // LANES) * LANES
    return pl.pallas_call(
        _compress_rows_kernel,
        grid=(b,),
        in_specs=[pl.BlockSpec((1, t, LANES), functools.partial(lambda j, i: (i, 0, j), j)) for j in range(w // LANES)]
        + [_resident(a.shape) for a in (pe, w1, w2, gkc)],
        out_specs=pl.BlockSpec((1, 2, c, NSA_KV_WIDTH), lambda i: (i, 0, 0, 0)),
        out_shape=jax.ShapeDtypeStruct((b, 2, c, NSA_KV_WIDTH), F32),
        compiler_params=_cparams(1),
        name="nsa_compress_rows",
    )(*([cmp] * (w // LANES)), pe, w1, w2, gkc)


PAGES_PER_STEP = 16


def _page_specs(n, layer, tail):
    zeros = (0,) * len(tail)
    return [pl.BlockSpec((1, 1) + tail,
                         functools.partial(lambda i, b, c, pt: (layer, pt[b, c * n + i]) + zeros, i))
            for i in range(n)]


def _compress_pages_kernel(pt_ref, *refs):
    n = len(refs) - 6
    pages = refs[:n]
    pe_ref, w1_ref, w2_ref, gkc_ref, out_ref, rows_ref = refs[n:]
    page = pages[0].shape[-1]
    m = n * page // CMP_BLOCK
    for e in range(2):
        for i, pg in enumerate(pages):
            for kp in range(NSA_KV_HEADS // 2):
                tile = pg[0, 0, e, 2 * kp:2 * kp + 2].reshape(LANES, page)
                rows_ref[kp, i * page:(i + 1) * page, :] = tile.T
        load = lambda kp, r: rows_ref[kp, pl.ds(r, m, stride=CMP_BLOCK), :]
        out_ref[0, e] = _compress_slot(load, m, e, pe_ref, w1_ref, w2_ref, gkc_ref)


def _compress_pages_call(pool_t, layer, page_table, pe, w1, w2, gkc):
    b, n_pages = page_table.shape
    page = pool_t.shape[-1]
    n = PAGES_PER_STEP
    m = n * page // CMP_BLOCK
    return pl.pallas_call(
        _compress_pages_kernel,
        grid_spec=pltpu.PrefetchScalarGridSpec(
            num_scalar_prefetch=1,
            grid=(b, n_pages // n),
            in_specs=_page_specs(n, layer, pool_t.shape[2:])
            + [pl.BlockSpec(a.shape, functools.partial(lambda nd, i, c, pt: (0,) * nd, a.ndim),
                            pipeline_mode=pl.Buffered(1)) for a in (pe, w1, w2, gkc)],
            out_specs=pl.BlockSpec((1, 2, m, NSA_KV_WIDTH), lambda i, c, pt: (i, 0, c, 0)),
            scratch_shapes=[pltpu.VMEM((NSA_KV_HEADS // 2, n * page, LANES), F32)],
        ),
        out_shape=jax.ShapeDtypeStruct((b, 2, n_pages * page // CMP_BLOCK, NSA_KV_WIDTH), F32),
        compiler_params=_cparams(2),
        name="nsa_compress_pages",
    )(page_table, *([pool_t] * n), pe, w1, w2, gkc)


def _select_blocks_t(imp, tpos):
    n_blocks = imp.shape[0]
    blk = lax.broadcasted_iota(jnp.int32, (n_blocks, 1), 0)
    cur = tpos // SEL_BLOCK
    forced = (blk == 0) | (blk == cur) | (blk == cur - 1)
    score = jnp.where(blk <= cur, jnp.where(forced, FORCE_SCORE, imp), -1.0)
    rank = jnp.zeros(score.shape, F32)
    for j in range(n_blocks):
        row = score[j:j + 1, :]
        below = jnp.where(j < blk, 1.0, 0.0)
        rank = rank + jnp.where(row > score, 1.0, jnp.where(row == score, below, 0.0))
    return jnp.where(rank < float(min(N_SELECT, n_blocks)), 1.0, 0.0)


def _attn_prompt_kernel(tq, tk, qn_ref, qr_ref, kvc_ref, sk_ref, sv_ref, wk_ref, wv_ref, gate_ref, o_ref,
                        kb_ref, vt_ref, sel_ref, imp_ref):
    qi = pl.program_id(2)
    hd = NSA_HEAD_DIM
    t_len = sk_ref.shape[1]
    c_blocks = kvc_ref.shape[2]
    n_sel = t_len // SEL_BLOCK
    ratio = SEL_BLOCK // CMP_BLOCK
    rows_per_tile = tk // SEL_BLOCK

    @pl.when(qi == 0)
    def _():
        for br, (k_ref, v_ref) in enumerate(((sk_ref, sv_ref), (wk_ref, wv_ref))):
            for gl in range(2):
                kb_ref[br, gl] = k_ref[0, :, gl * hd:(gl + 1) * hd].astype(BF16)
            for kt in range(t_len // tk):
                vt_ref[br, kt] = v_ref[0, kt * tk:(kt + 1) * tk, :].T.astype(BF16)

    t0 = qi * tq
    tpos = t0 + lax.broadcasted_iota(jnp.int32, (1, tq), 1)
    gate_t = gate_ref[0].T
    qn_t = [qn_ref[0, :, c * LANES:(c + 1) * LANES].T for c in range(2 * NSA_GROUP * hd // LANES)]
    qr_t = [qr_ref[0, :, c * LANES:(c + 1) * LANES].T for c in range(2 * NSA_GROUP * hd // LANES)]
    cid = lax.broadcasted_iota(jnp.int32, (c_blocks, 1), 0)
    ok = jnp.where((cid + 1) * CMP_BLOCK - 1 <= tpos, 1.0, 0.0)
    ok4 = jnp.concatenate([ok] * NSA_GROUP, axis=1)
    heads = []
    for gl in range(2):
        per_head = lambda parts: [parts[2 * gl + hh // 2][(hh % 2) * hd:(hh % 2 + 1) * hd] for hh in range(NSA_GROUP)]
        qn4 = (jnp.concatenate(per_head(qn_t), axis=1) * NSA_SCALE).astype(BF16)
        qr4 = (jnp.concatenate(per_head(qr_t), axis=1) * NSA_SCALE).astype(BF16)
        kc = kvc_ref[0, 0][:, gl * hd:(gl + 1) * hd]
        vc_t = kvc_ref[0, 1].T[gl * hd:(gl + 1) * hd]
        s = jnp.where(ok4 > 0.5, _dot(kc.astype(BF16), qn4), NEG_INF)
        e = jnp.exp(s - jnp.max(s, axis=0, keepdims=True))
        p = e / jnp.sum(e, axis=0, keepdims=True) * ok4
        o_cmp = _dot(vc_t.astype(BF16), p.astype(BF16))
        imp = p[:, :tq]
        for hh in range(1, NSA_GROUP):
            imp = imp + p[:, hh * tq:(hh + 1) * tq]
        imp_ref[...] = imp
        imp_sel = imp_ref[pl.ds(0, n_sel, stride=ratio), :]
        for i in range(1, ratio):
            imp_sel = imp_sel + imp_ref[pl.ds(i, n_sel, stride=ratio), :]
        sel_ref[gl] = _select_blocks_t(imp_sel, tpos)

        def flash(br, lo, hi, valid_fn):
            def body(kt, carry):
                m, l, acc = carry
                k0 = pl.multiple_of(kt * tk, tk)
                kb = kb_ref[br, gl, pl.ds(k0, tk), :]
                vt = vt_ref[br, kt, gl * hd:(gl + 1) * hd, :]
                bias = jnp.where(valid_fn(kt, k0), 0.0, NEG_INF)
                s = _dot(kb, qr4) + jnp.concatenate([bias] * NSA_GROUP, axis=1)
                m_new = jnp.maximum(m, jnp.max(s, axis=0, keepdims=True))
                alpha = jnp.exp(m - m_new)
                p = jnp.exp(s - m_new)
                return (m_new, alpha * l + jnp.sum(p, axis=0, keepdims=True),
                        alpha * acc + _dot(vt, p.astype(BF16)))
            cols = NSA_GROUP * tq
            init = (jnp.full((1, cols), M_INIT, F32), jnp.zeros((1, cols), F32), jnp.zeros((hd, cols), F32))
            _, l, acc = lax.fori_loop(lo, hi, body, init)
            return acc / l

        def slc_valid(kt, k0):
            picked = sel_ref[gl, pl.ds(pl.multiple_of(kt * rows_per_tile, rows_per_tile), rows_per_tile), :]
            picked = jnp.concatenate([jnp.broadcast_to(picked[i:i + 1], (SEL_BLOCK, tq))
                                      for i in range(rows_per_tile)], axis=0)
            kpos = k0 + lax.broadcasted_iota(jnp.int32, (tk, 1), 0)
            return (picked > 0.5) & (kpos <= tpos)

        def win_valid(kt, k0):
            dist = tpos - (k0 + lax.broadcasted_iota(jnp.int32, (tk, 1), 0))
            return (dist >= 0) & (dist <= WINDOW)

        hi = (t0 + tq - 1) // tk + 1
        slc = flash(0, 0, hi, slc_valid)
        win = flash(1, jnp.maximum(t0 - WINDOW, 0) // tk, hi, win_valid)
        for hh in range(NSA_GROUP):
            c = (gl * NSA_GROUP + hh) * 3
            cols = slice(hh * tq, (hh + 1) * tq)
            heads.append(o_cmp[:, cols] * gate_t[c:c + 1] + slc[:, cols] * gate_t[c + 1:c + 2]
                         + win[:, cols] * gate_t[c + 2:c + 3])
    for c in range(len(heads) // 2):
        o_ref[0, :, c * LANES:(c + 1) * LANES] = jnp.concatenate(heads[2 * c:2 * c + 2], axis=0).T


def _attn_prompt_call(qn, qr, kvc, slc, win, gate, tq=128, tk=512):
    b, t, nq = qn.shape
    pair_w = 2 * NSA_GROUP * NSA_HEAD_DIM
    c = kvc.shape[2]
    assert tk % (8 * SEL_BLOCK) == 0 and t % tk == 0 and c % LANES == 0 and tq == LANES
    qspec = pl.BlockSpec((1, tq, pair_w), lambda i, p, j: (i, j, p))
    kspec = pl.BlockSpec((1, t, LANES), lambda i, p, j: (i, 0, p))
    vspec = pl.BlockSpec((1, t, LANES), lambda i, p, j: (i, 0, 2 + p))
    return pl.pallas_call(
        functools.partial(_attn_prompt_kernel, tq, tk),
        grid=(b, 2, t // tq),
        in_specs=[qspec, qspec,
                  pl.BlockSpec((1, 2, c, LANES), lambda i, p, j: (i, 0, 0, p)),
                  kspec, vspec, kspec, vspec,
                  pl.BlockSpec((1, tq, LANES), lambda i, p, j: (i, j, p))],
        out_specs=qspec,
        out_shape=jax.ShapeDtypeStruct((b, t, nq), F32),
        scratch_shapes=[pltpu.VMEM((2, 2, t, NSA_HEAD_DIM), BF16),
                        pltpu.VMEM((2, t // tk, LANES, tk), BF16),
                        pltpu.VMEM((2, t // SEL_BLOCK, tq), F32),
                        pltpu.VMEM((c, tq), F32)],
        compiler_params=_cparams(3),
        name="nsa_attn_prompt",
    )(qn, qr, kvc, slc, slc, win, win, gate)


def _cmp_branch(q4, kc, vc, tpos, n_tok):
    c = kc.shape[0]
    s = _dot_nt((q4 * NSA_SCALE).astype(BF16), kc.astype(BF16))
    cidx = lax.broadcasted_iota(jnp.int32, (1, c), 1)
    ok = ((cidx + 1) * CMP_BLOCK - 1 <= tpos).astype(F32)
    ok4 = jnp.concatenate([ok] * NSA_GROUP, axis=0)
    s = jnp.where(ok4 > 0.5, s, NEG_INF)
    e = jnp.exp(s - jnp.max(s, axis=-1, keepdims=True))
    p = e / jnp.sum(e, axis=-1, keepdims=True) * ok4
    o = _dot(p.astype(BF16), vc.astype(BF16))
    imp = p[:n_tok]
    for hh in range(1, NSA_GROUP):
        imp = imp + p[hh * n_tok:(hh + 1) * n_tok]
    return o, imp


def _select_blocks(imp, tpos, n_blocks):
    c = imp.shape[1]
    lane = lax.broadcasted_iota(jnp.int32, (1, c), 1)

    def pair(x):
        lane1 = _lane_iota()
        return x + jnp.where((lane1 & 1) == 0, pltpu.roll(x, LANES - 1, axis=1), pltpu.roll(x, 1, axis=1))
    imp2 = _per_chunk(pair, imp)
    blk = lane // 2
    cur = tpos // SEL_BLOCK
    forced = (blk == 0) | (blk == cur) | (blk == cur - 1)
    score = jnp.where(blk <= cur, jnp.where(forced, FORCE_SCORE, imp2), -1.0)
    rank = jnp.zeros(score.shape, F32)
    for j in range(n_blocks):
        col = score[:, 2 * j:2 * j + 1]
        ahead = (col > score) | ((col == score) & (j < blk))
        rank = rank + ahead.astype(F32)
    return (rank < float(min(N_SELECT, n_blocks))).astype(F32)


def _flash_step(q4, k, v, valid, carry, transposed):
    m, l, acc = carry
    valid4 = jnp.concatenate([valid] * NSA_GROUP, axis=0)
    s = _dot(q4, k.astype(BF16)) if transposed else _dot_nt(q4, k.astype(BF16))
    s = jnp.where(valid4 > 0.5, s, NEG_INF)
    m_new = jnp.maximum(m, jnp.max(s, axis=-1, keepdims=True))
    alpha = jnp.exp(m - m_new)
    p = jnp.exp(s - m_new)
    l = alpha * l + jnp.sum(p, axis=-1, keepdims=True)
    pv = _dot_nt(p.astype(BF16), v.astype(BF16)) if transposed else _dot(p.astype(BF16), v.astype(BF16))
    return m_new, l, alpha * acc + pv


def _attn_sample_kernel(n_step, t_real, past_len, pt_ref, qn_ref, qr_ref, kvc_ref, gate_ref, snew_ref, wbuf_ref,
                        wnew_ref, expand_ref, *rest):
    pages = rest[:n_step]
    o_ref, sel_ref, ocmp_ref, m_ref, l_ref, acc_ref = rest[n_step:]
    c = pl.program_id(1)
    hd = NSA_HEAD_DIM
    kvw = NSA_KV_WIDTH
    tq = qn_ref.shape[1]
    n_cmp = past_len // CMP_BLOCK
    tpos = past_len + lax.broadcasted_iota(jnp.int32, (tq, 1), 0) % t_real
    stack = lambda ref, g: jnp.concatenate(
        [ref[0, :, (g * NSA_GROUP + hh) * hd:(g * NSA_GROUP + hh + 1) * hd] for hh in range(NSA_GROUP)], axis=0)
    rows = NSA_GROUP * tq

    @pl.when(c == 0)
    def _():
        n_blocks = -(-(past_len + t_real) // SEL_BLOCK)
        for g in range(NSA_KV_HEADS):
            ksl = slice(g * hd, (g + 1) * hd)
            o_cmp, imp = _cmp_branch(stack(qn_ref, g), kvc_ref[0, 0][:, ksl], kvc_ref[0, 1][:, ksl], tpos, tq)
            imp = jnp.concatenate([imp, jnp.zeros((tq, LANES), F32)], axis=-1)
            sel_ref[g] = _select_blocks(imp, tpos, n_blocks)
            ocmp_ref[g] = o_cmp
        m_ref[...] = jnp.full(m_ref.shape, M_INIT, F32)
        l_ref[...] = jnp.zeros(l_ref.shape, F32)
        acc_ref[...] = jnp.zeros(acc_ref.shape, F32)

    tk = n_step * pages[0].shape[-1]
    k0 = c * tk
    kpos = k0 + lax.broadcasted_iota(jnp.int32, (1, tk), 1)
    for g in range(NSA_KV_HEADS):
        kk = jnp.concatenate([pg[0, 0, 0, g] for pg in pages], axis=1)
        vv = jnp.concatenate([pg[0, 0, 1, g] for pg in pages], axis=1)
        picked = _dot(sel_ref[g][:, :n_cmp].astype(BF16), expand_ref[...])
        valid = jnp.where((picked > 0.5) & (kpos <= tpos), 1.0, 0.0)
        q4 = (stack(qr_ref, g) * NSA_SCALE).astype(BF16)
        m, l, acc = _flash_step(q4, kk, vv, valid, (m_ref[g], l_ref[g], acc_ref[g]), True)
        m_ref[g] = m
        l_ref[g] = l
        acc_ref[g] = acc

    @pl.when(c == pl.num_programs(1) - 1)
    def _():
        gate = gate_ref[0]
        rnew = lax.broadcasted_iota(jnp.int32, (1, tq), 1)
        newpos = past_len + rnew
        n_buf = wbuf_ref.shape[-1]
        bpos = past_len - n_buf + lax.broadcasted_iota(jnp.int32, (1, n_buf), 1)
        new_lane = 2 * (past_len // SEL_BLOCK)
        heads = []
        for g in range(NSA_KV_HEADS):
            q4 = (stack(qr_ref, g) * NSA_SCALE).astype(BF16)
            ks, vs = slice(g * hd, (g + 1) * hd), slice(kvw + g * hd, kvw + (g + 1) * hd)
            picked = sel_ref[g][:, new_lane:new_lane + 1] > 0.5
            valid = jnp.where(picked & (newpos <= tpos) & (rnew < t_real), 1.0, 0.0)
            _, l_s, acc_s = _flash_step(q4, snew_ref[0, :, ks], snew_ref[0, :, vs], valid,
                                        (m_ref[g], l_ref[g], acc_ref[g]), False)
            dist = tpos - bpos
            valid = jnp.where((dist >= 0) & (dist <= WINDOW) & (bpos >= 0), 1.0, 0.0)
            init = (jnp.full((rows, 1), M_INIT, F32), jnp.zeros((rows, 1), F32), jnp.zeros((rows, hd), F32))
            carry = _flash_step(q4, wbuf_ref[0, 0, 0, g], wbuf_ref[0, 0, 1, g], valid, init, True)
            dist = tpos - newpos
            valid = jnp.where((dist >= 0) & (dist <= WINDOW) & (rnew < t_real), 1.0, 0.0)
            _, l_w, acc_w = _flash_step(q4, wnew_ref[0, :, ks], wnew_ref[0, :, vs], valid, carry, False)
            o_cmp = ocmp_ref[g]
            o_slc = acc_s / l_s
            o_win = acc_w / l_w
            for hh in range(NSA_GROUP):
                head = g * NSA_GROUP + hh
                hr = slice(hh * tq, (hh + 1) * tq)
                col = (head // (NSA_HEADS // 2)) * LANES + (head % (NSA_HEADS // 2)) * 3
                heads.append(o_cmp[hr] * gate[:, col:col + 1] + o_slc[hr] * gate[:, col + 1:col + 2]
                             + o_win[hr] * gate[:, col + 2:col + 3])
        o_ref[0] = jnp.concatenate(heads, axis=-1)


SAMPLE_PAGES_PER_STEP = 8


def _attn_sample_call(qn, qr, kvc, gate, slc_new, win_buf_t, win_new, pool_t, layer, page_table, t_real):
    b, tq, nq = qn.shape
    n_pages = page_table.shape[1]
    page = pool_t.shape[-1]
    past_len = n_pages * page
    assert past_len % SEL_BLOCK == 0 and t_real <= SEL_BLOCK and past_len % (CMP_BLOCK * LANES) == 0
    n = SAMPLE_PAGES_PER_STEP
    rows = NSA_GROUP * tq
    full = lambda a: pl.BlockSpec((1,) + a.shape[1:], lambda i, c, pt: (i,) + (0,) * (a.ndim - 1))
    wbuf_spec = pl.BlockSpec((1, 1) + win_buf_t.shape[2:], lambda i, c, pt: (layer, i, 0, 0, 0, 0))
    n_cmp = past_len // CMP_BLOCK
    sel_lanes = n_cmp + LANES
    expand = (jnp.arange(n_cmp)[:, None] == (jnp.arange(past_len) // CMP_BLOCK)[None, :]).astype(BF16)
    return pl.pallas_call(
        functools.partial(_attn_sample_kernel, n, t_real, past_len),
        grid_spec=pltpu.PrefetchScalarGridSpec(
            num_scalar_prefetch=1,
            grid=(b, n_pages // n),
            in_specs=[full(a) for a in (qn, qr, kvc, gate, slc_new)] + [wbuf_spec, full(win_new)]
            + [pl.BlockSpec((n_cmp, n * page), lambda i, c, pt: (0, c))]
            + _page_specs(n, layer, pool_t.shape[2:]),
            out_specs=pl.BlockSpec((1, tq, nq), lambda i, c, pt: (i, 0, 0)),
            scratch_shapes=[pltpu.VMEM((NSA_KV_HEADS, tq, sel_lanes), F32),
                            pltpu.VMEM((NSA_KV_HEADS, rows, NSA_HEAD_DIM), F32),
                            pltpu.VMEM((NSA_KV_HEADS, rows, 1), F32),
                            pltpu.VMEM((NSA_KV_HEADS, rows, 1), F32),
                            pltpu.VMEM((NSA_KV_HEADS, rows, NSA_HEAD_DIM), F32)],
        ),
        out_shape=jax.ShapeDtypeStruct((b, tq, nq), F32),
        compiler_params=_cparams(2),
        name="nsa_attn_sample",
    )(page_table, qn, qr, kvc, gate, slc_new, win_buf_t, win_new, expand, *([pool_t] * n))


def _rope_tables(pos):
    half = NSA_HEAD_DIM // 2
    inv = ROPE_THETA ** (-jnp.arange(half, dtype=F32) / half)
    ang = pos.astype(F32)[:, None] * inv[None, :]
    cos, sin = jnp.cos(ang), jnp.sin(ang)
    reps = LANES // NSA_HEAD_DIM
    return jnp.tile(cos, (1, 2 * reps)), jnp.tile(jnp.concatenate([-sin, sin], axis=1), (1, reps))


def _nsa_params(w_in, g_q, g_ks, g_kw, g_kc, pe, w1, w2):
    d = w_in.shape[0]
    body = NSA_HEADS * NSA_HEAD_DIM + 6 * NSA_KV_WIDTH
    half = NSA_HEADS // 2 * 3
    zpad = jnp.zeros((d, LANES - half), w_in.dtype)
    w_pad = jnp.concatenate([w_in[:, :body], w_in[:, body:body + half], zpad, w_in[:, body + half:], zpad], axis=1)
    return dict(
        w_in=w_pad.astype(BF16),
        g_q=jnp.tile(g_q, NSA_HEADS)[None], g_ks=jnp.tile(g_ks, NSA_KV_HEADS)[None],
        g_kw=jnp.tile(g_kw, NSA_KV_HEADS)[None], g_kc=g_kc[None],
        pe=pe.reshape(2, CMP_BLOCK // 2, LANES), w1=w1.astype(BF16), w2=w2.astype(BF16))


def _kv_rows(x, b, t):
    return x.reshape(b, t, 2, NSA_KV_HEADS, NSA_HEAD_DIM)


def _nsa_prompt_layer(h, gmix, prm):
    b, t, d = h.shape
    cos, sin = _rope_tables(jnp.arange(t))
    qn, qr, cmp, slc, wrow, gate = _nsa_proj_call(
        h.reshape(b * t, d), gmix, prm["w_in"], prm["g_q"], prm["g_ks"], prm["g_kw"], cos, sin)
    r3 = lambda x: x.reshape(b, t, x.shape[-1])
    kvc = _compress_rows_call(r3(cmp), prm["pe"], prm["w1"], prm["w2"], prm["g_kc"])
    o = _attn_prompt_call(r3(qn), r3(qr), kvc, r3(slc), r3(wrow), r3(gate))
    keep = min(WINDOW, t)
    return (o.reshape(b * t, -1), _kv_rows(cmp, b, t), _kv_rows(slc, b, t), _kv_rows(wrow, b, t)[:, t - keep:])


def _nsa_sample_layer(h, cmp_pool_t, slc_pool_t, win_buf, win_buf_t, layer, page_table, gmix, prm):
    b, t, d = h.shape
    past_len = page_table.shape[1] * cmp_pool_t.shape[-1]
    cos, sin = _rope_tables(jnp.tile(past_len + jnp.arange(t), b))
    qn, qr, cmp, slc, wrow, gate = _nsa_proj_call(
        h.reshape(b * t, d), gmix, prm["w_in"], prm["g_q"], prm["g_ks"], prm["g_kw"], cos, sin)
    kvc = _compress_pages_call(cmp_pool_t, layer, page_table, prm["pe"], prm["w1"], prm["w2"], prm["g_kc"])
    tq = 8
    pad = lambda x: jnp.pad(x.reshape(b, t, x.shape[-1]), ((0, 0), (0, tq - t), (0, 0)))
    o = _attn_sample_call(pad(qn), pad(qr), kvc, pad(gate), pad(slc), win_buf_t, pad(wrow), slc_pool_t, layer,
                          page_table, t)
    new_win = jnp.concatenate([win_buf, _kv_rows(wrow, b, t)], axis=1)[:, t:]
    return o[:, :t].reshape(b * t, -1), _kv_rows(cmp, b, t), _kv_rows(slc, b, t), new_win


def _softplus(z):
    return jnp.maximum(z, 0.0) + jnp.log1p(jnp.exp(-jnp.abs(z)))


def _rwkv_proj_kernel(batch, h_ref, shift_ref, gmix_ref, mu_ref, wr_ref, wk_ref, wv_ref,
                      w0_ref, w1_ref, w2_ref, a0_ref, a1_ref, a2_ref, g1_ref, g2_ref,
                      r_ref, dec_ref, k_ref, v_ref, a_ref, g_ref, last_ref, carry_ref):
    i = pl.program_id(0)
    u = _rms(h_ref[...]) * gmix_ref[...]
    tm = u.shape[0]

    @pl.when(i == 0)
    def _():
        carry_ref[...] = shift_ref[...]
    prev = jnp.concatenate([carry_ref[...], u[:tm - batch]], axis=0)
    carry_ref[...] = u[tm - batch:]
    last_ref[...] = u[tm - batch:]
    xx = prev - u
    mix = lambda j: (u + xx * mu_ref[j:j + 1]).astype(BF16)
    r_ref[...] = _dot(mix(0), wr_ref[...])
    wl = w0_ref[...] + _dot(jnp.tanh(_dot(mix(1), w1_ref[...])).astype(BF16), w2_ref[...])
    dec_ref[...] = jnp.exp(-jnp.exp(-_softplus(-wl) - 0.5))
    k_ref[...] = _dot(mix(2), wk_ref[...])
    v_ref[...] = _dot(mix(3), wv_ref[...])
    a_ref[...] = jax.nn.sigmoid(a0_ref[...] + _dot(_dot(mix(4), a1_ref[...]).astype(BF16), a2_ref[...]))
    g_ref[...] = _dot(jax.nn.sigmoid(_dot(mix(5), g1_ref[...])).astype(BF16), g2_ref[...])


def _rwkv_proj_call(h, shift, gmix, prm):
    m, d = h.shape
    batch = shift.shape[0]
    tm = min(m, 256)
    assert tm % batch == 0 and tm > batch and batch % 8 == 0
    row = pl.BlockSpec((tm, d), lambda i: (i, 0))
    weights = [prm[n] for n in ("mu", "w_r", "w_k", "w_v", "w0", "w1", "w2", "a0", "a1", "a2", "g1", "g2")]
    outs = pl.pallas_call(
        functools.partial(_rwkv_proj_kernel, batch),
        grid=(m // tm,),
        in_specs=[row, _resident(shift.shape), _resident(gmix.shape)] + [_resident(w.shape) for w in weights],
        out_specs=[row] * 6 + [pl.BlockSpec((batch, d), lambda i: (0, 0))],
        out_shape=[jax.ShapeDtypeStruct((m, d), F32)] * 6 + [jax.ShapeDtypeStruct((batch, d), F32)],
        scratch_shapes=[pltpu.VMEM((batch, d), F32)],
        compiler_params=_cparams(1),
        name="rwkv_proj",
    )(h, shift, gmix, *weights)
    return outs[:6], outs[6]


def _rwkv_scan_kernel(tc, r_ref, dec_ref, k_ref, v_ref, a_ref, kk_ref, ka_ref, rk_ref, lw_ref, lb_ref, s0_ref,
                      y_ref, sout_ref, state_ref, vec_ref):
    c = pl.program_id(1)
    n = RWKV_HEAD_DIM

    @pl.when(c == 0)
    def _():
        state_ref[...] = s0_ref[...]

    def step(r, dec, k, v, a):
        kk = k * kk_ref[...]
        kk = kk / jnp.maximum(jnp.sqrt(jnp.sum(kk * kk, axis=0, keepdims=True)), 1e-12)
        k2 = k * (1.0 + (a - 1.0) * ka_ref[...])
        vec_ref[0] = -kk
        vec_ref[1] = kk * a
        vec_ref[2] = dec
        vec_ref[3] = k2
        vec_ref[4] = r
        def sweep_sa(jb, sa):
            for u in range(8):
                j = jb * 8 + u
                sa = sa + state_ref[j] * vec_ref[0, pl.ds(j, 1), :]
            return sa
        sa = lax.fori_loop(0, n // 8, sweep_sa, jnp.zeros((n, LANES), F32))

        def sweep_update(jb, y):
            for u in range(8):
                j = jb * 8 + u
                row = lambda i: vec_ref[i, pl.ds(j, 1), :]
                s_new = state_ref[j] * row(2) + sa * row(1) + v * row(3)
                state_ref[j] = s_new
                y = y + s_new * row(4)
            return y
        y = lax.fori_loop(0, n // 8, sweep_update, jnp.zeros((n, LANES), F32))
        mean = jnp.mean(y, axis=0, keepdims=True)
        var = jnp.mean(jnp.square(y - mean), axis=0, keepdims=True)
        y = (y - mean) * lax.rsqrt(var + LNX_EPS) * lw_ref[...] + lb_ref[...]
        return y + jnp.sum(r * k2 * rk_ref[...], axis=0, keepdims=True) * v

    def pair(tp, _):
        t = 2 * tp
        ins = [_rows_to_lanes(ref[t], ref[t + 1]) for ref in (r_ref, dec_ref, k_ref, v_ref, a_ref)]
        ya = step(*(x[0] for x in ins))
        yb = step(*(x[1] for x in ins))
        _lanes_to_rows(ya, yb, y_ref, t)
        return 0

    lax.fori_loop(0, tc // 2, pair, 0)

    @pl.when(c == pl.num_programs(1) - 1)
    def _():
        sout_ref[...] = state_ref[...]


def _rows_to_lanes(xa, xb):
    chunks = xa.shape[1] // LANES
    stack = jnp.concatenate([x[:, c * LANES:(c + 1) * LANES] for x in (xa, xb) for c in range(chunks)], axis=0)
    tr = stack.T
    half = LANES // 2
    top, bot = tr[:half], tr[half:]
    low = _lane_iota() < half
    return (jnp.where(low, top, pltpu.roll(bot, half, axis=1)),
            jnp.where(low, pltpu.roll(top, half, axis=1), bot))


def _lanes_to_rows(ya, yb, out_ref, t):
    half = LANES // 2
    low = _lane_iota() < half
    top = jnp.where(low, ya, pltpu.roll(yb, half, axis=1))
    bot = jnp.where(low, pltpu.roll(ya, half, axis=1), yb)
    tr = jnp.concatenate([top, bot], axis=0).T
    chunks = out_ref.shape[2] // LANES
    for tok in range(2):
        for c in range(chunks):
            r0 = (tok * chunks + c) * 8
            out_ref[t + tok, :, c * LANES:(c + 1) * LANES] = tr[r0:r0 + 8]


def _lane_params(x, heads, width):
    per_chunk = LANES // width if width < LANES else 1
    x = x.reshape(heads // per_chunk, per_chunk, width)
    x = jnp.transpose(x, (2, 1, 0))
    return jnp.repeat(x[..., None], 8, axis=-1).reshape(width, LANES)


def _rwkv_scan_call(r, dec, k, v, a, kk, ka, rk, lw, lb, s0):
    t, b, d = r.shape
    n = RWKV_HEAD_DIM
    assert d == 8 * LANES and b % 8 == 0 and t % 2 == 0
    tc = math.gcd(t, 32)
    seq = pl.BlockSpec((tc, 8, d), lambda g, c: (c, g, 0))
    par = pl.BlockSpec((n, LANES), lambda g, c: (0, 0))
    st = pl.BlockSpec((n, n, LANES), lambda g, c: (0, 0, g))
    return pl.pallas_call(
        functools.partial(_rwkv_scan_kernel, tc),
        grid=(b // 8, t // tc),
        in_specs=[seq] * 5 + [par] * 5 + [st],
        out_specs=[seq, st],
        out_shape=[jax.ShapeDtypeStruct((t, b, d), F32), jax.ShapeDtypeStruct((n, n, b // 8 * LANES), F32)],
        scratch_shapes=[pltpu.VMEM((n, n, LANES), F32), pltpu.VMEM((5, n, LANES), F32)],
        compiler_params=_cparams(2),
        name="rwkv_scan",
    )(r, dec, k, v, a, kk, ka, rk, lw, lb, s0)


def _rwkv_params(mu, w_r, w_k, w_v, w0, w1, w2, a0, a1, a2, g1, g2):
    gpad = -g1.shape[1] % LANES
    return dict(mu=mu, w_r=w_r.astype(BF16), w_k=w_k.astype(BF16), w_v=w_v.astype(BF16), w0=w0[None],
                w1=w1.astype(BF16), w2=w2.astype(BF16), a0=a0[None], a1=a1.astype(BF16), a2=a2.astype(BF16),
                g1=jnp.pad(g1, ((0, 0), (0, gpad))).astype(BF16), g2=jnp.pad(g2, ((0, gpad), (0, 0))).astype(BF16))


def _rwkv_layer(h, batch, shift, s0, gmix, prm, k_k, k_a, r_k, lnx_w, lnx_b):
    m, d = h.shape
    t = m // batch
    n = RWKV_HEAD_DIM
    nh = d // n
    g8 = batch // 8
    (r, dec, k, v, a, g), new_shift = _rwkv_proj_call(h, shift, gmix, prm)
    rows = lambda x: x.reshape(t, batch, d)
    par = lambda x: _lane_params(x.reshape(-1), nh, n)
    st = s0.reshape(g8, 8, nh // 2, 2, n, n).transpose(5, 4, 0, 3, 2, 1).reshape(n, n, g8 * LANES)
    y, s = _rwkv_scan_call(rows(r), rows(dec), rows(k), rows(v), rows(a), par(k_k), par(k_a), par(r_k),
                           par(lnx_w), par(lnx_b), st)
    s = s.reshape(n, n, g8, 2, nh // 2, 8).transpose(2, 5, 4, 3, 1, 0).reshape(batch, nh, n, n)
    return y.reshape(m, d), g, new_shift, s


def _hgrn_proj_kernel(h_ref, gmix_ref, win_ref, lb_ref, omlb_ref, q_ref, f_ref, k_ref, i_ref, g_ref):
    d = h_ref.shape[1]
    u = (_rms(h_ref[...]) * gmix_ref[...]).astype(BF16)
    z = _dot(u, win_ref[...])
    q, f, g = z[:, :d], z[:, d:2 * d], z[:, 3 * d:]
    q_ref[...] = q * jax.nn.sigmoid(q)
    f_ref[...] = lb_ref[...] + omlb_ref[...] * jax.nn.sigmoid(f)
    k_ref[...] = omlb_ref[...] * jax.nn.sigmoid(-f)
    i_ref[...] = z[:, 2 * d:3 * d]
    g_ref[...] = g * jax.nn.sigmoid(g)


def _hgrn_proj_call(h, gmix, win, lb, omlb):
    m, d = h.shape
    tm = min(m, 256)
    row = pl.BlockSpec((tm, d), lambda i: (i, 0))
    return pl.pallas_call(
        _hgrn_proj_kernel,
        grid=(m // tm,),
        in_specs=[row, _resident(gmix.shape), _resident(win.shape), _resident(lb.shape), _resident(omlb.shape)],
        out_specs=[row] * 5,
        out_shape=[jax.ShapeDtypeStruct((m, d), F32)] * 5,
        compiler_params=_cparams(1),
        name="hgrn_proj",
    )(h, gmix, win, lb, omlb)


def _rows_to_lanes_dup(xa, xb):
    chunks = xa.shape[1] // LANES
    stack = jnp.concatenate([x[:, c * LANES:(c + 1) * LANES] for x in (xa, xb) for c in range(chunks)], axis=0)
    tr = stack.T
    half = LANES // 2
    low = _lane_iota() < half
    other = pltpu.roll(tr, half, axis=1)
    return jnp.where(low, tr, other), jnp.where(low, other, tr)


def _hgrn_scan_kernel(tc, q_ref, f_ref, k_ref, v_ref, s0_ref, o_ref, sout_ref, state_ref, vec_ref):
    c = pl.program_id(1)
    dk = state_ref.shape[0]

    @pl.when(c == 0)
    def _():
        state_ref[...] = s0_ref[...]

    def step(q, f, k, v):
        vec_ref[0] = q
        vec_ref[1] = f
        vec_ref[2] = k
        def sweep(db, o):
            for u in range(8):
                d = db * 8 + u
                row = lambda i: vec_ref[i, pl.ds(d, 1), :]
                s_new = state_ref[d] * row(1) + row(2) * v
                state_ref[d] = s_new
                o = o + s_new * row(0)
            return o
        return lax.fori_loop(0, dk // 8, sweep, jnp.zeros(v.shape, F32))

    def pair(tp, _):
        t = 2 * tp
        q, f, k = (_rows_to_lanes_dup(ref[t], ref[t + 1]) for ref in (q_ref, f_ref, k_ref))
        v = _rows_to_lanes(v_ref[t], v_ref[t + 1])
        oa = step(q[0], f[0], k[0], v[0])
        ob = step(q[1], f[1], k[1], v[1])
        _lanes_to_rows(oa, ob, o_ref, t)
        return 0

    lax.fori_loop(0, tc // 2, pair, 0)

    @pl.when(c == pl.num_programs(1) - 1)
    def _():
        sout_ref[...] = state_ref[...]


def _hgrn_scan_call(q, f, k, v, s0):
    t, b, d = q.shape
    dk = HGRN_HEAD_DIM
    dv = dk // 2
    assert d == 8 * LANES and b % 8 == 0 and t % 2 == 0
    tc = math.gcd(t, 32)
    seq = pl.BlockSpec((tc, 8, d), lambda g, c: (c, g, 0))
    st = pl.BlockSpec((dk, dv, LANES), lambda g, c: (0, 0, g))
    return pl.pallas_call(
        functools.partial(_hgrn_scan_kernel, tc),
        grid=(b // 8, t // tc),
        in_specs=[seq] * 4 + [st],
        out_specs=[seq, st],
        out_shape=[jax.ShapeDtypeStruct((t, b, d), F32), jax.ShapeDtypeStruct((dk, dv, b // 8 * LANES), F32)],
        scratch_shapes=[pltpu.VMEM((dk, dv, LANES), F32), pltpu.VMEM((3, dk, LANES), F32)],
        compiler_params=_cparams(2),
        name="hgrn_scan",
    )(q, f, k, v, s0)


def _hgrn_layer(h, batch, s0, gmix, win, lb, omlb):
    m, d = h.shape
    t = m // batch
    n = HGRN_HEAD_DIM
    nh = d // n
    g8 = batch // 8
    q, f, k, i, g = _hgrn_proj_call(h, gmix, win, lb, omlb)
    rows = lambda x: x.reshape(t, batch, d)
    st = s0.reshape(g8, 8, nh, n, 2, n // 2).transpose(3, 5, 0, 4, 2, 1).reshape(n, n // 2, g8 * LANES)
    o, s = _hgrn_scan_call(rows(q), rows(f), rows(k), rows(i), st)
    s = s.reshape(n, n // 2, g8, 2, nh, 8).transpose(2, 5, 4, 0, 3, 1).reshape(batch, nh, n, n)
    return o.reshape(m, d), g, s


def kernel(x_prompt, x_sample, cache_cmp, cache_slc, cache_win, state_rwkv_shift, state_rwkv_wkv, state_hgrn,
           page_table, p_prompt, p_sample, norm_mix, norm_ffn, w_up, w_down, w_ple, w_ple_gate,
           nsa_w_in, nsa_g_q, nsa_g_ks, nsa_g_kw, nsa_g_kc, nsa_cmp_pe, nsa_cmp_w1, nsa_cmp_w2, nsa_w_out,
           rwkv_mu, rwkv_w_r, rwkv_w_k, rwkv_w_v, rwkv_w_o, rwkv_w0, rwkv_w1, rwkv_w2, rwkv_a0, rwkv_a1,
           rwkv_a2, rwkv_g1, rwkv_g2, rwkv_k_k, rwkv_k_a, rwkv_r_k, rwkv_lnx_w, rwkv_lnx_b,
           hgrn_w_in, hgrn_gn, hgrn_w_o, hgrn_lower_bounds):
    depth = norm_mix.shape[0]
    bp, tp, d = x_prompt.shape
    bs, ts, _ = x_sample.shape
    lb_soft = jax.nn.softmax(hgrn_lower_bounds.astype(F32), axis=0)
    lower_bound = jnp.cumsum(lb_soft, axis=0) - lb_soft[0]
    rows_minor = lambda x: jnp.transpose(x, (0, 1, 3, 4, 5, 2))
    cmp_pool_t, slc_pool_t, win_buf_t = rows_minor(cache_cmp), rows_minor(cache_slc), rows_minor(cache_win)
    hp, hs = x_prompt.reshape(bp * tp, d), x_sample.reshape(bs * ts, d)
    swap = lambda x, a, b: x.reshape(a, b, x.shape[-1]).transpose(1, 0, 2).reshape(a * b, x.shape[-1])
    time_major = False
    outs = [[] for _ in range(12)]
    for i in range(depth):
        kind, n = i % N_MIXERS, i // N_MIXERS
        gmix = norm_mix[i][None]
        ffn = (p_prompt[i].reshape(bp * tp, -1), p_sample[i].reshape(bs * ts, -1))
        if (kind != 0) != time_major:
            hp = swap(hp, tp, bp) if time_major else swap(hp, bp, tp)
            hs = swap(hs, ts, bs) if time_major else swap(hs, bs, ts)
            time_major = not time_major
        if time_major:
            ffn = (swap(ffn[0], bp, tp), swap(ffn[1], bs, ts))
        hp3, hs3 = hp.reshape(bp, tp, d), hs.reshape(bs, ts, d)
        tail = (norm_ffn[i][None], w_up[i].astype(BF16), w_down[i].astype(BF16), w_ple[i].astype(BF16),
                w_ple_gate[i].astype(BF16))
        if kind == 0:
            prm = _nsa_params(nsa_w_in[n], nsa_g_q[n], nsa_g_ks[n], nsa_g_kw[n], nsa_g_kc[n], nsa_cmp_pe[n],
                              nsa_cmp_w1[n], nsa_cmp_w2[n])
            op, rc, rs, rw = _nsa_prompt_layer(hp3, gmix, prm)
            os_, nc_rows, ns_rows, nw_buf = _nsa_sample_layer(hs3, cmp_pool_t, slc_pool_t, cache_win[n], win_buf_t,
                                                              n, page_table, gmix, prm)
            for lst, v in zip(outs[:6], (rc, nc_rows, rs, ns_rows, rw, nw_buf)):
                lst.append(v)
            wo = nsa_w_out[n].astype(BF16)
            hp = _ffn_call("nsa", hp, op, None, None, ffn[0], wo, *tail)
            hs = _ffn_call("nsa", hs, os_, None, None, ffn[1], wo, *tail)
        elif kind == 1:
            prm = _rwkv_params(rwkv_mu[n], rwkv_w_r[n], rwkv_w_k[n], rwkv_w_v[n], rwkv_w0[n], rwkv_w1[n],
                               rwkv_w2[n], rwkv_a0[n], rwkv_a1[n], rwkv_a2[n], rwkv_g1[n], rwkv_g2[n])
            vecs = (rwkv_k_k[n], rwkv_k_a[n], rwkv_r_k[n], rwkv_lnx_w[n], rwkv_lnx_b[n])
            nh = d // RWKV_HEAD_DIM
            zero_state = jnp.zeros((bp, nh, RWKV_HEAD_DIM, RWKV_HEAD_DIM), F32)
            yp, gp, shp, sp = _rwkv_layer(hp, bp, jnp.zeros((bp, d), F32), zero_state, gmix, prm, *vecs)
            ys, gs, shs, ss = _rwkv_layer(hs, bs, state_rwkv_shift[n], state_rwkv_wkv[n].astype(F32), gmix, prm,
                                          *vecs)
            for lst, v in zip(outs[6:10], (shp, shs, sp, ss)):
                lst.append(v)
            wo = rwkv_w_o[n].astype(BF16)
            hp = _ffn_call("rwkv", hp, yp, gp, None, ffn[0], wo, *tail)
            hs = _ffn_call("rwkv", hs, ys, gs, None, ffn[1], wo, *tail)
        else:
            lb = lower_bound[i][None]
            win = hgrn_w_in[n].astype(BF16)
            nh = d // HGRN_HEAD_DIM
            zero_state = jnp.zeros((bp, nh, HGRN_HEAD_DIM, HGRN_HEAD_DIM), F32)
            op, gp, sp = _hgrn_layer(hp, bp, zero_state, gmix, win, lb, 1.0 - lb)
            os_, gs, ss = _hgrn_layer(hs, bs, state_hgrn[n].astype(F32), gmix, win, lb, 1.0 - lb)
            outs[10].append(sp)
            outs[11].append(ss)
            wo, gn = hgrn_w_o[n].astype(BF16), hgrn_gn[n][None]
            hp = _ffn_call("hgrn", hp, op, gp, gn, ffn[0], wo, *tail)
            hs = _ffn_call("hgrn", hs, os_, gs, gn, ffn[1], wo, *tail)
    if time_major:
        hp, hs = swap(hp, tp, bp), swap(hs, ts, bs)
    return (hp.reshape(bp, tp, d), hs.reshape(bs, ts, d)) + tuple(jnp.stack(o) for o in outs)
```

```python
import functools
import math

import jax
import jax.numpy as jnp
from jax import lax
from jax.experimental import pallas as pl
from jax.experimental.pallas import tpu as pltpu

F32 = jnp.float32
BF16 = jnp.bfloat16

NORM_EPS = 1e-6
ROPE_THETA = 10000.0
NEG_INF = -1e30
M_INIT = -1e29
N_MIXERS = 3

NSA_HEADS = 16
NSA_KV_HEADS = 4
NSA_HEAD_DIM = 64
NSA_GROUP = NSA_HEADS // NSA_KV_HEADS
NSA_KV_WIDTH = NSA_KV_HEADS * NSA_HEAD_DIM
NSA_SCALE = NSA_HEAD_DIM ** -0.5
CMP_BLOCK = 32
SEL_BLOCK = 64
N_SELECT = 16
WINDOW = 512
FORCE_SCORE = 1e4

RWKV_HEAD_DIM = 64
LNX_EPS = 64e-5
HGRN_HEAD_DIM = 128

LANES = 128
VMEM_LIMIT = 56 * 1024 * 1024


def _cparams(n_axes):
    return pltpu.CompilerParams(dimension_semantics=("arbitrary",) * n_axes,
                                vmem_limit_bytes=VMEM_LIMIT)


def _resident(shape):
    zeros = (0,) * len(shape)
    return pl.BlockSpec(shape, lambda *_: zeros, pipeline_mode=pl.Buffered(1))


def _rms(x):
    return x * lax.rsqrt(jnp.mean(x * x, axis=-1, keepdims=True) + NORM_EPS)


def _dot(a, b):
    return jnp.dot(a, b, preferred_element_type=F32)


def _dot_nt(a, b):
    return lax.dot_general(a, b, (((1,), (1,)), ((), ())), preferred_element_type=F32)


def _per_chunk(fn, x, *rest):
    n = x.shape[-1] // LANES
    outs = [fn(*(a[:, c * LANES:(c + 1) * LANES] for a in (x,) + rest)) for c in range(n)]
    return outs[0] if n == 1 else jnp.concatenate(outs, axis=-1)


def _lane_iota():
    return lax.broadcasted_iota(jnp.int32, (1, LANES), 1)


def _group_ones(group):
    r = lax.broadcasted_iota(jnp.int32, (LANES, LANES), 0) // group
    c = lax.broadcasted_iota(jnp.int32, (LANES, LANES), 1) // group
    return jnp.where(r == c, 1.0, 0.0).astype(BF16)


def _head_rms(x, head_dim, ones):
    def one(c):
        sq = c * c
        hi = sq.astype(BF16)
        lo = (sq - hi.astype(F32)).astype(BF16)
        ss = _dot(hi, ones) + _dot(lo, ones)
        return c * lax.rsqrt(ss * (1.0 / head_dim) + NORM_EPS)
    return _per_chunk(one, x)


def _rope(x, cos, sin_signed):
    half = NSA_HEAD_DIM // 2
    lane = _lane_iota()

    def one(c):
        up = pltpu.roll(c, LANES - half, axis=1)
        dn = pltpu.roll(c, half, axis=1)
        rot = jnp.where((lane & half) == 0, up, dn)
        return c * cos + rot * sin_signed
    return _per_chunk(one, x)


def _ffn_kernel(mode, ff_chunk, *refs):
    if mode == "nsa":
        h_ref, o_ref, p_ref, wo_ref, gffn_ref, wup_ref, wdown_ref, wple_ref, wgate_ref, out_ref = refs
        o = o_ref[...]
    elif mode == "rwkv":
        h_ref, o_ref, aux_ref, p_ref, wo_ref, gffn_ref, wup_ref, wdown_ref, wple_ref, wgate_ref, out_ref = refs
        o = o_ref[...] * aux_ref[...]
    else:
        (h_ref, o_ref, aux_ref, gn_ref, p_ref, wo_ref, gffn_ref, wup_ref, wdown_ref, wple_ref, wgate_ref,
         out_ref) = refs
        gn = gn_ref[...]
        o = _per_chunk(lambda c: _rms(c) * gn, o_ref[...]) * aux_ref[...]
    h1 = h_ref[...] + _dot(o.astype(BF16), wo_ref[...])
    u = (_rms(h1) * gffn_ref[...]).astype(BF16)
    d_ff = wup_ref.shape[1]
    acc = jnp.zeros_like(h1)
    for j in range(d_ff // ff_chunk):
        a = _dot(u, wup_ref[:, j * ff_chunk:(j + 1) * ff_chunk])
        a = jnp.square(jnp.maximum(a, 0.0)).astype(BF16)
        acc = acc + _dot(a, wdown_ref[j * ff_chunk:(j + 1) * ff_chunk, :])
    h2 = h1 + acc
    gate = jax.nn.sigmoid(_dot(_rms(h2).astype(BF16), wgate_ref[...]))
    out_ref[...] = h2 + _dot(p_ref[...].astype(BF16), wple_ref[...]) * gate


def _ffn_call(mode, h, o, aux, gn, p, wo, gffn, wup, wdown, wple, wgate):
    m, d = h.shape
    tm = min(m, 512)
    row = lambda w: pl.BlockSpec((tm, w), lambda i: (i, 0))
    args, specs = [h, o], [row(d), row(d)]
    if mode != "nsa":
        args.append(aux)
        specs.append(row(d))
    if mode == "hgrn":
        args.append(gn)
        specs.append(_resident(gn.shape))
    args += [p, wo, gffn, wup, wdown, wple, wgate]
    specs += [row(p.shape[1])] + [_resident(a.shape) for a in (wo, gffn, wup, wdown, wple, wgate)]
    return pl.pallas_call(
        functools.partial(_ffn_kernel, mode, 1024),
        grid=(m // tm,),
        in_specs=specs,
        out_specs=row(d),
        out_shape=jax.ShapeDtypeStruct((m, d), F32),
        compiler_params=_cparams(1),
        name="ffn_" + mode,
    )(*args)


def _nsa_proj_kernel(h_ref, gmix_ref, win_ref, gq_ref, gks_ref, gkw_ref, cos_ref, sin_ref,
                     qn_ref, qr_ref, cmp_ref, slc_ref, wrow_ref, gate_ref):
    kvw = NSA_KV_WIDTH
    nq = NSA_HEADS * NSA_HEAD_DIM
    u = (_rms(h_ref[...]) * gmix_ref[...]).astype(BF16)
    z = _dot(u, win_ref[...])
    cos = cos_ref[...]
    sin = sin_ref[...]
    ones = _group_ones(NSA_HEAD_DIM)
    qn = _head_rms(z[:, :nq], NSA_HEAD_DIM, ones) * gq_ref[...]
    qn_ref[...] = qn
    qr_ref[...] = _rope(qn, cos, sin)
    cmp_ref[...] = z[:, nq:nq + 2 * kvw]
    o = nq + 2 * kvw
    slc_ref[:, :kvw] = _rope(_head_rms(z[:, o:o + kvw], NSA_HEAD_DIM, ones) * gks_ref[...], cos, sin)
    slc_ref[:, kvw:] = z[:, o + kvw:o + 2 * kvw]
    o += 2 * kvw
    wrow_ref[:, :kvw] = _rope(_head_rms(z[:, o:o + kvw], NSA_HEAD_DIM, ones) * gkw_ref[...], cos, sin)
    wrow_ref[:, kvw:] = z[:, o + kvw:o + 2 * kvw]
    o += 2 * kvw
    gate_ref[...] = jax.nn.sigmoid(z[:, o:])


def _nsa_proj_call(h, gmix, win, gq, gks, gkw, cos, sin):
    m, d = h.shape
    tm = min(m, 256)
    tab_tiles = cos.shape[0] // tm
    row = lambda w: pl.BlockSpec((tm, w), lambda i: (i, 0))
    tab = pl.BlockSpec((tm, LANES), lambda i: (i % tab_tiles, 0))
    nq = NSA_HEADS * NSA_HEAD_DIM
    widths = [nq, nq] + [2 * NSA_KV_WIDTH] * 3 + [2 * LANES]
    return pl.pallas_call(
        _nsa_proj_kernel,
        grid=(m // tm,),
        in_specs=[row(d), _resident(gmix.shape), _resident(win.shape), _resident(gq.shape),
                  _resident(gks.shape), _resident(gkw.shape), tab, tab],
        out_specs=[row(w) for w in widths],
        out_shape=[jax.ShapeDtypeStruct((m, w), F32) for w in widths],
        compiler_params=_cparams(1),
        name="nsa_proj",
    )(h, gmix, win, gq, gks, gkw, cos, sin)


def _compress_slot(load_pair, m, e, pe_ref, w1_ref, w2_ref, gkc_ref):
    lane = _lane_iota()
    low = lane < NSA_HEAD_DIM
    acc = jnp.zeros((NSA_KV_HEADS * m, w1_ref.shape[2]), F32)
    for s in range(CMP_BLOCK // 2):
        pe_row = pe_ref[e, s:s + 1, :]
        parts = []
        for kp in range(NSA_KV_HEADS // 2):
            a = load_pair(kp, 2 * s)
            b = load_pair(kp, 2 * s + 1)
            parts.append(jnp.where(low, a, pltpu.roll(b, NSA_HEAD_DIM, axis=1)))
            parts.append(jnp.where(low, pltpu.roll(a, NSA_HEAD_DIM, axis=1), b))
        x = (jnp.concatenate(parts, axis=0) + pe_row).astype(BF16)
        acc = acc + _dot(x, w1_ref[e, s * LANES:(s + 1) * LANES, :])
    out = _dot(jax.nn.gelu(acc).astype(BF16), w2_ref[e])
    if e == 0:
        out = _rms(out) * gkc_ref[...]
    return jnp.concatenate([out[k * m:(k + 1) * m] for k in range(NSA_KV_HEADS)], axis=-1)


def _compress_rows_kernel(*refs):
    cols = refs[:NSA_KV_HEADS]
    pe_ref, w1_ref, w2_ref, gkc_ref, out_ref = refs[NSA_KV_HEADS:]
    m = cols[0].shape[1] // CMP_BLOCK
    pad = out_ref.shape[2] - m
    for e in range(2):
        load = lambda kp, r: cols[2 * e + kp][0, pl.ds(r, m, stride=CMP_BLOCK), :]
        res = _compress_slot(load, m, e, pe_ref, w1_ref, w2_ref, gkc_ref)
        out_ref[0, e] = jnp.concatenate([res, jnp.zeros((pad, res.shape[1]), F32)], axis=0)


def _compress_rows_call(cmp, pe, w1, w2, gkc):
    b, t, w = cmp.shape
    c = -(-(t // CMP_BLOCK) // LANES) * LANES
    return pl.pallas_call(
        _compress_rows_kernel,
        grid=(b,),
        in_specs=[pl.BlockSpec((1, t, LANES), functools.partial(lambda j, i: (i, 0, j), j)) for j in range(w // LANES)]
        + [_resident(a.shape) for a in (pe, w1, w2, gkc)],
        out_specs=pl.BlockSpec((1, 2, c, NSA_KV_WIDTH), lambda i: (i, 0, 0, 0)),
        out_shape=jax.ShapeDtypeStruct((b, 2, c, NSA_KV_WIDTH), F32),
        compiler_params=_cparams(1),
        name="nsa_compress_rows",
    )(*([cmp] * (w // LANES)), pe, w1, w2, gkc)


PAGES_PER_STEP = 16


def _page_specs(n, layer, tail):
    zeros = (0,) * len(tail)
    return [pl.BlockSpec((1, 1) + tail,
                         functools.partial(lambda i, b, c, pt: (layer, pt[b, c * n + i]) + zeros, i))
            for i in range(n)]


def _compress_pages_kernel(pt_ref, *refs):
    n = len(refs) - 6
    pages = refs[:n]
    pe_ref, w1_ref, w2_ref, gkc_ref, out_ref, rows_ref = refs[n:]
    page = pages[0].shape[-1]
    m = n * page // CMP_BLOCK
    for e in range(2):
        for i, pg in enumerate(pages):
            for kp in range(NSA_KV_HEADS // 2):
                tile = pg[0, 0, e, 2 * kp:2 * kp + 2].reshape(LANES, page)
                rows_ref[kp, i * page:(i + 1) * page, :] = tile.T
        load = lambda kp, r: rows_ref[kp, pl.ds(r, m, stride=CMP_BLOCK), :]
        out_ref[0, e] = _compress_slot(load, m, e, pe_ref, w1_ref, w2_ref, gkc_ref)


def _compress_pages_call(pool_t, layer, page_table, pe, w1, w2, gkc):
    b, n_pages = page_table.shape
    page = pool_t.shape[-1]
    n = PAGES_PER_STEP
    m = n * page // CMP_BLOCK
    return pl.pallas_call(
        _compress_pages_kernel,
        grid_spec=pltpu.PrefetchScalarGridSpec(
            num_scalar_prefetch=1,
            grid=(b, n_pages // n),
            in_specs=_page_specs(n, layer, pool_t.shape[2:])
            + [pl.BlockSpec(a.shape, functools.partial(lambda nd, i, c, pt: (0,) * nd, a.ndim),
                            pipeline_mode=pl.Buffered(1)) for a in (pe, w1, w2, gkc)],
            out_specs=pl.BlockSpec((1, 2, m, NSA_KV_WIDTH), lambda i, c, pt: (i, 0, c, 0)),
            scratch_shapes=[pltpu.VMEM((NSA_KV_HEADS // 2, n * page, LANES), F32)],
        ),
        out_shape=jax.ShapeDtypeStruct((b, 2, n_pages * page // CMP_BLOCK, NSA_KV_WIDTH), F32),
        compiler_params=_cparams(2),
        name="nsa_compress_pages",
    )(page_table, *([pool_t] * n), pe, w1, w2, gkc)


def _select_blocks_t(imp, tpos):
    n_blocks = imp.shape[0]
    blk = lax.broadcasted_iota(jnp.int32, (n_blocks, 1), 0)
    cur = tpos // SEL_BLOCK
    forced = (blk == 0) | (blk == cur) | (blk == cur - 1)
    score = jnp.where(blk <= cur, jnp.where(forced, FORCE_SCORE, imp), -1.0)
    rank = jnp.zeros(score.shape, F32)
    for j in range(n_blocks):
        row = score[j:j + 1, :]
        below = jnp.where(j < blk, 1.0, 0.0)
        rank = rank + jnp.where(row > score, 1.0, jnp.where(row == score, below, 0.0))
    return jnp.where(rank < float(min(N_SELECT, n_blocks)), 1.0, 0.0)


def _attn_prompt_kernel(tq, tk, qn_ref, qr_ref, kvc_ref, sk_ref, sv_ref, wk_ref, wv_ref, gate_ref, o_ref,
                        kb_ref, vt_ref, sel_ref, imp_ref):
    qi = pl.program_id(2)
    hd = NSA_HEAD_DIM
    t_len = sk_ref.shape[1]
    c_blocks = kvc_ref.shape[2]
    n_sel = t_len // SEL_BLOCK
    ratio = SEL_BLOCK // CMP_BLOCK
    rows_per_tile = tk // SEL_BLOCK

    @pl.when(qi == 0)
    def _():
        for br, (k_ref, v_ref) in enumerate(((sk_ref, sv_ref), (wk_ref, wv_ref))):
            for gl in range(2):
                kb_ref[br, gl] = k_ref[0, :, gl * hd:(gl + 1) * hd].astype(BF16)
            for kt in range(t_len // tk):
                vt_ref[br, kt] = v_ref[0, kt * tk:(kt + 1) * tk, :].T.astype(BF16)

    t0 = qi * tq
    tpos = t0 + lax.broadcasted_iota(jnp.int32, (1, tq), 1)
    gate_t = gate_ref[0].T
    qn_t = [qn_ref[0, :, c * LANES:(c + 1) * LANES].T for c in range(2 * NSA_GROUP * hd // LANES)]
    qr_t = [qr_ref[0, :, c * LANES:(c + 1) * LANES].T for c in range(2 * NSA_GROUP * hd // LANES)]
    cid = lax.broadcasted_iota(jnp.int32, (c_blocks, 1), 0)
    ok = jnp.where((cid + 1) * CMP_BLOCK - 1 <= tpos, 1.0, 0.0)
    ok4 = jnp.concatenate([ok] * NSA_GROUP, axis=1)
    heads = []
    for gl in range(2):
        per_head = lambda parts: [parts[2 * gl + hh // 2][(hh % 2) * hd:(hh % 2 + 1) * hd] for hh in range(NSA_GROUP)]
        qn4 = (jnp.concatenate(per_head(qn_t), axis=1) * NSA_SCALE).astype(BF16)
        qr4 = (jnp.concatenate(per_head(qr_t), axis=1) * NSA_SCALE).astype(BF16)
        kc = kvc_ref[0, 0][:, gl * hd:(gl + 1) * hd]
        vc_t = kvc_ref[0, 1].T[gl * hd:(gl + 1) * hd]
        s = jnp.where(ok4 > 0.5, _dot(kc.astype(BF16), qn4), NEG_INF)
        e = jnp.exp(s - jnp.max(s, axis=0, keepdims=True))
        p = e / jnp.sum(e, axis=0, keepdims=True) * ok4
        o_cmp = _dot(vc_t.astype(BF16), p.astype(BF16))
        imp = p[:, :tq]
        for hh in range(1, NSA_GROUP):
            imp = imp + p[:, hh * tq:(hh + 1) * tq]
        imp_ref[...] = imp
        imp_sel = imp_ref[pl.ds(0, n_sel, stride=ratio), :]
        for i in range(1, ratio):
            imp_sel = imp_sel + imp_ref[pl.ds(i, n_sel, stride=ratio), :]
        sel_ref[gl] = _select_blocks_t(imp_sel, tpos)

        def flash(br, lo, hi, valid_fn):
            def body(kt, carry):
                m, l, acc = carry
                k0 = pl.multiple_of(kt * tk, tk)
                kb = kb_ref[br, gl, pl.ds(k0, tk), :]
                vt = vt_ref[br, kt, gl * hd:(gl + 1) * hd, :]
                bias = jnp.where(valid_fn(kt, k0), 0.0, NEG_INF)
                s = _dot(kb, qr4) + jnp.concatenate([bias] * NSA_GROUP, axis=1)
                m_new = jnp.maximum(m, jnp.max(s, axis=0, keepdims=True))
                alpha = jnp.exp(m - m_new)
                p = jnp.exp(s - m_new)
                return (m_new, alpha * l + jnp.sum(p, axis=0, keepdims=True),
                        alpha * acc + _dot(vt, p.astype(BF16)))
            cols = NSA_GROUP * tq
            init = (jnp.full((1, cols), M_INIT, F32), jnp.zeros((1, cols), F32), jnp.zeros((hd, cols), F32))
            _, l, acc = lax.fori_loop(lo, hi, body, init)
            return acc / l

        def slc_valid(kt, k0):
            picked = sel_ref[gl, pl.ds(pl.multiple_of(kt * rows_per_tile, rows_per_tile), rows_per_tile), :]
            picked = jnp.concatenate([jnp.broadcast_to(picked[i:i + 1], (SEL_BLOCK, tq))
                                      for i in range(rows_per_tile)], axis=0)
            kpos = k0 + lax.broadcasted_iota(jnp.int32, (tk, 1), 0)
            return (picked > 0.5) & (kpos <= tpos)

        def win_valid(kt, k0):
            dist = tpos - (k0 + lax.broadcasted_iota(jnp.int32, (tk, 1), 0))
            return (dist >= 0) & (dist <= WINDOW)

        hi = (t0 + tq - 1) // tk + 1
        slc = flash(0, 0, hi, slc_valid)
        win = flash(1, jnp.maximum(t0 - WINDOW, 0) // tk, hi, win_valid)
        for hh in range(NSA_GROUP):
            c = (gl * NSA_GROUP + hh) * 3
            cols = slice(hh * tq, (hh + 1) * tq)
            heads.append(o_cmp[:, cols] * gate_t[c:c + 1] + slc[:, cols] * gate_t[c + 1:c + 2]
                         + win[:, cols] * gate_t[c + 2:c + 3])
    for c in range(len(heads) // 2):
        o_ref[0, :, c * LANES:(c + 1) * LANES] = jnp.concatenate(heads[2 * c:2 * c + 2], axis=0).T


def _attn_prompt_call(qn, qr, kvc, slc, win, gate, tq=128, tk=512):
    b, t, nq = qn.shape
    pair_w = 2 * NSA_GROUP * NSA_HEAD_DIM
    c = kvc.shape[2]
    assert tk % (8 * SEL_BLOCK) == 0 and t % tk == 0 and c % LANES == 0 and tq == LANES
    qspec = pl.BlockSpec((1, tq, pair_w), lambda i, p, j: (i, j, p))
    kspec = pl.BlockSpec((1, t, LANES), lambda i, p, j: (i, 0, p))
    vspec = pl.BlockSpec((1, t, LANES), lambda i, p, j: (i, 0, 2 + p))
    return pl.pallas_call(
        functools.partial(_attn_prompt_kernel, tq, tk),
        grid=(b, 2, t // tq),
        in_specs=[qspec, qspec,
                  pl.BlockSpec((1, 2, c, LANES), lambda i, p, j: (i, 0, 0, p)),
                  kspec, vspec, kspec, vspec,
                  pl.BlockSpec((1, tq, LANES), lambda i, p, j: (i, j, p))],
        out_specs=qspec,
        out_shape=jax.ShapeDtypeStruct((b, t, nq), F32),
        scratch_shapes=[pltpu.VMEM((2, 2, t, NSA_HEAD_DIM), BF16),
                        pltpu.VMEM((2, t // tk, LANES, tk), BF16),
                        pltpu.VMEM((2, t // SEL_BLOCK, tq), F32),
                        pltpu.VMEM((c, tq), F32)],
        compiler_params=_cparams(3),
        name="nsa_attn_prompt",
    )(qn, qr, kvc, slc, slc, win, win, gate)


def _cmp_branch(q4, kc, vc, tpos, n_tok):
    c = kc.shape[0]
    s = _dot_nt((q4 * NSA_SCALE).astype(BF16), kc.astype(BF16))
    cidx = lax.broadcasted_iota(jnp.int32, (1, c), 1)
    ok = ((cidx + 1) * CMP_BLOCK - 1 <= tpos).astype(F32)
    ok4 = jnp.concatenate([ok] * NSA_GROUP, axis=0)
    s = jnp.where(ok4 > 0.5, s, NEG_INF)
    e = jnp.exp(s - jnp.max(s, axis=-1, keepdims=True))
    p = e / jnp.sum(e, axis=-1, keepdims=True) * ok4
    o = _dot(p.astype(BF16), vc.astype(BF16))
    imp = p[:n_tok]
    for hh in range(1, NSA_GROUP):
        imp = imp + p[hh * n_tok:(hh + 1) * n_tok]
    return o, imp


def _select_blocks(imp, tpos, n_blocks):
    c = imp.shape[1]
    lane = lax.broadcasted_iota(jnp.int32, (1, c), 1)

    def pair(x):
        lane1 = _lane_iota()
        return x + jnp.where((lane1 & 1) == 0, pltpu.roll(x, LANES - 1, axis=1), pltpu.roll(x, 1, axis=1))
    imp2 = _per_chunk(pair, imp)
    blk = lane // 2
    cur = tpos // SEL_BLOCK
    forced = (blk == 0) | (blk == cur) | (blk == cur - 1)
    score = jnp.where(blk <= cur, jnp.where(forced, FORCE_SCORE, imp2), -1.0)
    rank = jnp.zeros(score.shape, F32)
    for j in range(n_blocks):
        col = score[:, 2 * j:2 * j + 1]
        ahead = (col > score) | ((col == score) & (j < blk))
        rank = rank + ahead.astype(F32)
    return (rank < float(min(N_SELECT, n_blocks))).astype(F32)


def _flash_step(q4, k, v, valid, carry, transposed):
    m, l, acc = carry
    valid4 = jnp.concatenate([valid] * NSA_GROUP, axis=0)
    s = _dot(q4, k.astype(BF16)) if transposed else _dot_nt(q4, k.astype(BF16))
    s = jnp.where(valid4 > 0.5, s, NEG_INF)
    m_new = jnp.maximum(m, jnp.max(s, axis=-1, keepdims=True))
    alpha = jnp.exp(m - m_new)
    p = jnp.exp(s - m_new)
    l = alpha * l + jnp.sum(p, axis=-1, keepdims=True)
    pv = _dot_nt(p.astype(BF16), v.astype(BF16)) if transposed else _dot(p.astype(BF16), v.astype(BF16))
    return m_new, l, alpha * acc + pv


def _attn_sample_kernel(n_step, t_real, past_len, pt_ref, qn_ref, qr_ref, kvc_ref, gate_ref, snew_ref, wbuf_ref,
                        wnew_ref, expand_ref, *rest):
    pages = rest[:n_step]
    o_ref, sel_ref, ocmp_ref, m_ref, l_ref, acc_ref, qbd_ref = rest[n_step:]
    c = pl.program_id(1)
    hd = NSA_HEAD_DIM
    kvw = NSA_KV_WIDTH
    tq = qn_ref.shape[1]
    n_cmp = past_len // CMP_BLOCK
    tpos = past_len + lax.broadcasted_iota(jnp.int32, (tq, 1), 0) % t_real
    stack = lambda ref, g: jnp.concatenate(
        [ref[0, :, (g * NSA_GROUP + hh) * hd:(g * NSA_GROUP + hh + 1) * hd] for hh in range(NSA_GROUP)], axis=0)
    rows = NSA_GROUP * tq

    @pl.when(c == 0)
    def _():
        n_blocks = -(-(past_len + t_real) // SEL_BLOCK)
        for g in range(NSA_KV_HEADS):
            ksl = slice(g * hd, (g + 1) * hd)
            o_cmp, imp = _cmp_branch(stack(qn_ref, g), kvc_ref[0, 0][:, ksl], kvc_ref[0, 1][:, ksl], tpos, tq)
            imp = jnp.concatenate([imp, jnp.zeros((tq, LANES), F32)], axis=-1)
            sel_ref[g] = _select_blocks(imp, tpos, n_blocks)
            ocmp_ref[g] = o_cmp
            q4 = (stack(qr_ref, g) * NSA_SCALE).astype(BF16)
            zero = jnp.zeros((rows, hd), BF16)
            qbd_ref[g * rows:(g + 1) * rows, :] = jnp.concatenate(
                [q4 if gg == g else zero for gg in range(NSA_KV_HEADS)], axis=1)
        m_ref[...] = jnp.full(m_ref.shape, M_INIT, F32)
        l_ref[...] = jnp.zeros(l_ref.shape, F32)
        acc_ref[...] = jnp.zeros(acc_ref.shape, F32)

    tk = n_step * pages[0].shape[-1]
    k0 = c * tk
    kpos = k0 + lax.broadcasted_iota(jnp.int32, (1, tk), 1)
    kt_all = jnp.concatenate([pg[0, 0, 0].reshape(kvw, -1) for pg in pages], axis=1).astype(BF16)
    vt_all = jnp.concatenate([pg[0, 0, 1].reshape(kvw, -1) for pg in pages], axis=1).astype(BF16)
    sel_all = jnp.concatenate([sel_ref[g][:, :n_cmp] for g in range(NSA_KV_HEADS)], axis=0).astype(BF16)
    picked = _dot(sel_all, expand_ref[...])
    tpos_all = jnp.concatenate([tpos] * NSA_KV_HEADS, axis=0)
    bias = jnp.where((picked > 0.5) & (kpos <= tpos_all), 0.0, NEG_INF)
    bias = jnp.concatenate([bias[g * tq:(g + 1) * tq] for g in range(NSA_KV_HEADS) for _ in range(NSA_GROUP)],
                           axis=0)
    s = _dot(qbd_ref[...], kt_all) + bias
    m_old = m_ref[...]
    m_new = jnp.maximum(m_old, jnp.max(s, axis=-1, keepdims=True))
    alpha = jnp.exp(m_old - m_new)
    p = jnp.exp(s - m_new)
    m_ref[...] = m_new
    l_ref[...] = alpha * l_ref[...] + jnp.sum(p, axis=-1, keepdims=True)
    acc_ref[...] = alpha * acc_ref[...] + _dot_nt(p.astype(BF16), vt_all)

    @pl.when(c == pl.num_programs(1) - 1)
    def _():
        gate = gate_ref[0]
        rnew = lax.broadcasted_iota(jnp.int32, (1, tq), 1)
        newpos = past_len + rnew
        n_buf = wbuf_ref.shape[-1]
        bpos = past_len - n_buf + lax.broadcasted_iota(jnp.int32, (1, n_buf), 1)
        new_lane = 2 * (past_len // SEL_BLOCK)
        heads = []
        for g in range(NSA_KV_HEADS):
            q4 = (stack(qr_ref, g) * NSA_SCALE).astype(BF16)
            ks, vs = slice(g * hd, (g + 1) * hd), slice(kvw + g * hd, kvw + (g + 1) * hd)
            picked = sel_ref[g][:, new_lane:new_lane + 1] > 0.5
            valid = jnp.where(picked & (newpos <= tpos) & (rnew < t_real), 1.0, 0.0)
            gr = slice(g * rows, (g + 1) * rows)
            _, l_s, acc_s = _flash_step(q4, snew_ref[0, :, ks], snew_ref[0, :, vs], valid,
                                        (m_ref[gr, :], l_ref[gr, :], acc_ref[gr, ks]), False)
            dist = tpos - bpos
            valid = jnp.where((dist >= 0) & (dist <= WINDOW) & (bpos >= 0), 1.0, 0.0)
            init = (jnp.full((rows, 1), M_INIT, F32), jnp.zeros((rows, 1), F32), jnp.zeros((rows, hd), F32))
            carry = _flash_step(q4, wbuf_ref[0, 0, 0, g], wbuf_ref[0, 0, 1, g], valid, init, True)
            dist = tpos - newpos
            valid = jnp.where((dist >= 0) & (dist <= WINDOW) & (rnew < t_real), 1.0, 0.0)
            _, l_w, acc_w = _flash_step(q4, wnew_ref[0, :, ks], wnew_ref[0, :, vs], valid, carry, False)
            o_cmp = ocmp_ref[g]
            o_slc = acc_s / l_s
            o_win = acc_w / l_w
            for hh in range(NSA_GROUP):
                head = g * NSA_GROUP + hh
                hr = slice(hh * tq, (hh + 1) * tq)
                col = (head // (NSA_HEADS // 2)) * LANES + (head % (NSA_HEADS // 2)) * 3
                heads.append(o_cmp[hr] * gate[:, col:col + 1] + o_slc[hr] * gate[:, col + 1:col + 2]
                             + o_win[hr] * gate[:, col + 2:col + 3])
        o_ref[0] = jnp.concatenate(heads, axis=-1)


SAMPLE_PAGES_PER_STEP = 8


def _attn_sample_call(qn, qr, kvc, gate, slc_new, win_buf_t, win_new, pool_t, layer, page_table, t_real):
    b, tq, nq = qn.shape
    n_pages = page_table.shape[1]
    page = pool_t.shape[-1]
    past_len = n_pages * page
    assert past_len % SEL_BLOCK == 0 and t_real <= SEL_BLOCK and past_len % (CMP_BLOCK * LANES) == 0
    n = SAMPLE_PAGES_PER_STEP
    rows = NSA_GROUP * tq
    full = lambda a: pl.BlockSpec((1,) + a.shape[1:], lambda i, c, pt: (i,) + (0,) * (a.ndim - 1))
    wbuf_spec = pl.BlockSpec((1, 1) + win_buf_t.shape[2:], lambda i, c, pt: (layer, i, 0, 0, 0, 0))
    n_cmp = past_len // CMP_BLOCK
    sel_lanes = n_cmp + LANES
    expand = (jnp.arange(n_cmp)[:, None] == (jnp.arange(past_len) // CMP_BLOCK)[None, :]).astype(BF16)
    return pl.pallas_call(
        functools.partial(_attn_sample_kernel, n, t_real, past_len),
        grid_spec=pltpu.PrefetchScalarGridSpec(
            num_scalar_prefetch=1,
            grid=(b, n_pages // n),
            in_specs=[full(a) for a in (qn, qr, kvc, gate, slc_new)] + [wbuf_spec, full(win_new)]
            + [pl.BlockSpec((n_cmp, n * page), lambda i, c, pt: (0, c))]
            + _page_specs(n, layer, pool_t.shape[2:]),
            out_specs=pl.BlockSpec((1, tq, nq), lambda i, c, pt: (i, 0, 0)),
            scratch_shapes=[pltpu.VMEM((NSA_KV_HEADS, tq, sel_lanes), F32),
                            pltpu.VMEM((NSA_KV_HEADS, rows, NSA_HEAD_DIM), F32),
                            pltpu.VMEM((NSA_KV_HEADS * rows, 1), F32),
                            pltpu.VMEM((NSA_KV_HEADS * rows, 1), F32),
                            pltpu.VMEM((NSA_KV_HEADS * rows, NSA_KV_WIDTH), F32),
                            pltpu.VMEM((NSA_KV_HEADS * rows, NSA_KV_WIDTH), BF16)],
        ),
        out_shape=jax.ShapeDtypeStruct((b, tq, nq), F32),
        compiler_params=_cparams(2),
        name="nsa_attn_sample",
    )(page_table, qn, qr, kvc, gate, slc_new, win_buf_t, win_new, expand, *([pool_t] * n))


def _rope_tables(pos):
    half = NSA_HEAD_DIM // 2
    inv = ROPE_THETA ** (-jnp.arange(half, dtype=F32) / half)
    ang = pos.astype(F32)[:, None] * inv[None, :]
    cos, sin = jnp.cos(ang), jnp.sin(ang)
    reps = LANES // NSA_HEAD_DIM
    return jnp.tile(cos, (1, 2 * reps)), jnp.tile(jnp.concatenate([-sin, sin], axis=1), (1, reps))


def _nsa_params(w_in, g_q, g_ks, g_kw, g_kc, pe, w1, w2):
    d = w_in.shape[0]
    body = NSA_HEADS * NSA_HEAD_DIM + 6 * NSA_KV_WIDTH
    half = NSA_HEADS // 2 * 3
    zpad = jnp.zeros((d, LANES - half), w_in.dtype)
    w_pad = jnp.concatenate([w_in[:, :body], w_in[:, body:body + half], zpad, w_in[:, body + half:], zpad], axis=1)
    return dict(
        w_in=w_pad.astype(BF16),
        g_q=jnp.tile(g_q, NSA_HEADS)[None], g_ks=jnp.tile(g_ks, NSA_KV_HEADS)[None],
        g_kw=jnp.tile(g_kw, NSA_KV_HEADS)[None], g_kc=g_kc[None],
        pe=pe.reshape(2, CMP_BLOCK // 2, LANES), w1=w1.astype(BF16), w2=w2.astype(BF16))


def _kv_rows(x, b, t):
    return x.reshape(b, t, 2, NSA_KV_HEADS, NSA_HEAD_DIM)


def _nsa_prompt_layer(h, gmix, prm):
    b, t, d = h.shape
    cos, sin = _rope_tables(jnp.arange(t))
    qn, qr, cmp, slc, wrow, gate = _nsa_proj_call(
        h.reshape(b * t, d), gmix, prm["w_in"], prm["g_q"], prm["g_ks"], prm["g_kw"], cos, sin)
    r3 = lambda x: x.reshape(b, t, x.shape[-1])
    kvc = _compress_rows_call(r3(cmp), prm["pe"], prm["w1"], prm["w2"], prm["g_kc"])
    o = _attn_prompt_call(r3(qn), r3(qr), kvc, r3(slc), r3(wrow), r3(gate))
    keep = min(WINDOW, t)
    return (o.reshape(b * t, -1), _kv_rows(cmp, b, t), _kv_rows(slc, b, t), _kv_rows(wrow, b, t)[:, t - keep:])


def _nsa_sample_layer(h, cmp_pool_t, slc_pool_t, win_buf, win_buf_t, layer, page_table, gmix, prm):
    b, t, d = h.shape
    past_len = page_table.shape[1] * cmp_pool_t.shape[-1]
    cos, sin = _rope_tables(jnp.tile(past_len + jnp.arange(t), b))
    qn, qr, cmp, slc, wrow, gate = _nsa_proj_call(
        h.reshape(b * t, d), gmix, prm["w_in"], prm["g_q"], prm["g_ks"], prm["g_kw"], cos, sin)
    kvc = _compress_pages_call(cmp_pool_t, layer, page_table, prm["pe"], prm["w1"], prm["w2"], prm["g_kc"])
    tq = 8
    pad = lambda x: jnp.pad(x.reshape(b, t, x.shape[-1]), ((0, 0), (0, tq - t), (0, 0)))
    o = _attn_sample_call(pad(qn), pad(qr), kvc, pad(gate), pad(slc), win_buf_t, pad(wrow), slc_pool_t, layer,
                          page_table, t)
    new_win = jnp.concatenate([win_buf, _kv_rows(wrow, b, t)], axis=1)[:, t:]
    return o[:, :t].reshape(b * t, -1), _kv_rows(cmp, b, t), _kv_rows(slc, b, t), new_win


def _softplus(z):
    return jnp.maximum(z, 0.0) + jnp.log1p(jnp.exp(-jnp.abs(z)))


def _rwkv_proj_kernel(batch, h_ref, shift_ref, gmix_ref, mu_ref, wr_ref, wk_ref, wv_ref,
                      w0_ref, w1_ref, w2_ref, a0_ref, a1_ref, a2_ref, g1_ref, g2_ref,
                      r_ref, dec_ref, k_ref, v_ref, a_ref, g_ref, last_ref, carry_ref):
    i = pl.program_id(0)
    u = _rms(h_ref[...]) * gmix_ref[...]
    tm = u.shape[0]

    @pl.when(i == 0)
    def _():
        carry_ref[...] = shift_ref[...]
    prev = jnp.concatenate([carry_ref[...], u[:tm - batch]], axis=0)
    carry_ref[...] = u[tm - batch:]
    last_ref[...] = u[tm - batch:]
    xx = prev - u
    mix = lambda j: (u + xx * mu_ref[j:j + 1]).astype(BF16)
    r_ref[...] = _dot(mix(0), wr_ref[...])
    wl = w0_ref[...] + _dot(jnp.tanh(_dot(mix(1), w1_ref[...])).astype(BF16), w2_ref[...])
    dec_ref[...] = jnp.exp(-jnp.exp(-_softplus(-wl) - 0.5))
    k_ref[...] = _dot(mix(2), wk_ref[...])
    v_ref[...] = _dot(mix(3), wv_ref[...])
    a_ref[...] = jax.nn.sigmoid(a0_ref[...] + _dot(_dot(mix(4), a1_ref[...]).astype(BF16), a2_ref[...]))
    g_ref[...] = _dot(jax.nn.sigmoid(_dot(mix(5), g1_ref[...])).astype(BF16), g2_ref[...])


def _rwkv_proj_call(h, shift, gmix, prm):
    m, d = h.shape
    batch = shift.shape[0]
    tm = min(m, 256)
    assert tm % batch == 0 and tm > batch and batch % 8 == 0
    row = pl.BlockSpec((tm, d), lambda i: (i, 0))
    weights = [prm[n] for n in ("mu", "w_r", "w_k", "w_v", "w0", "w1", "w2", "a0", "a1", "a2", "g1", "g2")]
    outs = pl.pallas_call(
        functools.partial(_rwkv_proj_kernel, batch),
        grid=(m // tm,),
        in_specs=[row, _resident(shift.shape), _resident(gmix.shape)] + [_resident(w.shape) for w in weights],
        out_specs=[row] * 6 + [pl.BlockSpec((batch, d), lambda i: (0, 0))],
        out_shape=[jax.ShapeDtypeStruct((m, d), F32)] * 6 + [jax.ShapeDtypeStruct((batch, d), F32)],
        scratch_shapes=[pltpu.VMEM((batch, d), F32)],
        compiler_params=_cparams(1),
        name="rwkv_proj",
    )(h, shift, gmix, *weights)
    return outs[:6], outs[6]


def _rwkv_scan_kernel(tc, r_ref, dec_ref, k_ref, v_ref, a_ref, kk_ref, ka_ref, rk_ref, lw_ref, lb_ref, s0_ref,
                      y_ref, sout_ref, state_ref, vec_ref, stage_ref, ystage_ref):
    c = pl.program_id(1)
    n = RWKV_HEAD_DIM

    @pl.when(c == 0)
    def _():
        state_ref[...] = s0_ref[...]

    def stage_in(tp, _):
        t = 2 * tp
        for i, ref in enumerate((r_ref, dec_ref, k_ref, v_ref, a_ref)):
            xa, xb = _rows_to_lanes(ref[t], ref[t + 1])
            stage_ref[i, t] = xa
            stage_ref[i, t + 1] = xb
        return 0

    lax.fori_loop(0, tc // 2, stage_in, 0)

    def step(t, _):
        r = stage_ref[0, t]
        k = stage_ref[2, t]
        v = stage_ref[3, t]
        a = stage_ref[4, t]
        kk = k * kk_ref[...]
        kk = kk / jnp.maximum(jnp.sqrt(jnp.sum(kk * kk, axis=0, keepdims=True)), 1e-12)
        k2 = k * (1.0 + (a - 1.0) * ka_ref[...])
        vec_ref[0] = -kk
        vec_ref[1] = kk * a
        vec_ref[2] = k2
        sa = jnp.zeros((n, LANES), F32)
        for j in range(n):
            sa = sa + state_ref[j] * vec_ref[0, j:j + 1, :]
        y = jnp.zeros((n, LANES), F32)
        for j in range(n):
            s_new = (state_ref[j] * stage_ref[1, t, j:j + 1, :] + sa * vec_ref[1, j:j + 1, :]
                     + v * vec_ref[2, j:j + 1, :])
            state_ref[j] = s_new
            y = y + s_new * stage_ref[0, t, j:j + 1, :]
        mean = jnp.mean(y, axis=0, keepdims=True)
        var = jnp.mean(jnp.square(y - mean), axis=0, keepdims=True)
        y = (y - mean) * lax.rsqrt(var + LNX_EPS) * lw_ref[...] + lb_ref[...]
        ystage_ref[t] = y + jnp.sum(r * k2 * rk_ref[...], axis=0, keepdims=True) * v
        return 0

    lax.fori_loop(0, tc, step, 0)

    def stage_out(tp, _):
        t = 2 * tp
        _lanes_to_rows(ystage_ref[t], ystage_ref[t + 1], y_ref, t)
        return 0

    lax.fori_loop(0, tc // 2, stage_out, 0)

    @pl.when(c == pl.num_programs(1) - 1)
    def _():
        sout_ref[...] = state_ref[...]


def _rows_to_lanes(xa, xb):
    chunks = xa.shape[1] // LANES
    stack = jnp.concatenate([x[:, c * LANES:(c + 1) * LANES] for x in (xa, xb) for c in range(chunks)], axis=0)
    tr = stack.T
    half = LANES // 2
    top, bot = tr[:half], tr[half:]
    low = _lane_iota() < half
    return (jnp.where(low, top, pltpu.roll(bot, half, axis=1)),
            jnp.where(low, pltpu.roll(top, half, axis=1), bot))


def _lanes_to_rows(ya, yb, out_ref, t):
    half = LANES // 2
    low = _lane_iota() < half
    top = jnp.where(low, ya, pltpu.roll(yb, half, axis=1))
    bot = jnp.where(low, pltpu.roll(ya, half, axis=1), yb)
    tr = jnp.concatenate([top, bot], axis=0).T
    chunks = out_ref.shape[2] // LANES
    for tok in range(2):
        for c in range(chunks):
            r0 = (tok * chunks + c) * 8
            out_ref[t + tok, :, c * LANES:(c + 1) * LANES] = tr[r0:r0 + 8]


def _lane_params(x, heads, width):
    per_chunk = LANES // width if width < LANES else 1
    x = x.reshape(heads // per_chunk, per_chunk, width)
    x = jnp.transpose(x, (2, 1, 0))
    return jnp.repeat(x[..., None], 8, axis=-1).reshape(width, LANES)


def _rwkv_scan_call(r, dec, k, v, a, kk, ka, rk, lw, lb, s0):
    t, b, d = r.shape
    n = RWKV_HEAD_DIM
    assert d == 8 * LANES and b % 8 == 0 and t % 2 == 0
    tc = math.gcd(t, 32)
    seq = pl.BlockSpec((tc, 8, d), lambda g, c: (c, g, 0))
    par = pl.BlockSpec((n, LANES), lambda g, c: (0, 0))
    st = pl.BlockSpec((n, n, LANES), lambda g, c: (0, 0, g))
    return pl.pallas_call(
        functools.partial(_rwkv_scan_kernel, tc),
        grid=(b // 8, t // tc),
        in_specs=[seq] * 5 + [par] * 5 + [st],
        out_specs=[seq, st],
        out_shape=[jax.ShapeDtypeStruct((t, b, d), F32), jax.ShapeDtypeStruct((n, n, b // 8 * LANES), F32)],
        scratch_shapes=[pltpu.VMEM((n, n, LANES), F32), pltpu.VMEM((3, n, LANES), F32),
                        pltpu.VMEM((5, tc, n, LANES), F32), pltpu.VMEM((tc, n, LANES), F32)],
        compiler_params=_cparams(2),
        name="rwkv_scan",
    )(r, dec, k, v, a, kk, ka, rk, lw, lb, s0)


def _rwkv_params(mu, w_r, w_k, w_v, w0, w1, w2, a0, a1, a2, g1, g2):
    gpad = -g1.shape[1] % LANES
    return dict(mu=mu, w_r=w_r.astype(BF16), w_k=w_k.astype(BF16), w_v=w_v.astype(BF16), w0=w0[None],
                w1=w1.astype(BF16), w2=w2.astype(BF16), a0=a0[None], a1=a1.astype(BF16), a2=a2.astype(BF16),
                g1=jnp.pad(g1, ((0, 0), (0, gpad))).astype(BF16), g2=jnp.pad(g2, ((0, gpad), (0, 0))).astype(BF16))


def _rwkv_layer(h, batch, shift, s0, gmix, prm, k_k, k_a, r_k, lnx_w, lnx_b):
    m, d = h.shape
    t = m // batch
    n = RWKV_HEAD_DIM
    nh = d // n
    g8 = batch // 8
    (r, dec, k, v, a, g), new_shift = _rwkv_proj_call(h, shift, gmix, prm)
    rows = lambda x: x.reshape(t, batch, d)
    par = lambda x: _lane_params(x.reshape(-1), nh, n)
    st = s0.reshape(g8, 8, nh // 2, 2, n, n).transpose(5, 4, 0, 3, 2, 1).reshape(n, n, g8 * LANES)
    y, s = _rwkv_scan_call(rows(r), rows(dec), rows(k), rows(v), rows(a), par(k_k), par(k_a), par(r_k),
                           par(lnx_w), par(lnx_b), st)
    s = s.reshape(n, n, g8, 2, nh // 2, 8).transpose(2, 5, 4, 3, 1, 0).reshape(batch, nh, n, n)
    return y.reshape(m, d), g, new_shift, s


def _hgrn_proj_kernel(h_ref, gmix_ref, win_ref, lb_ref, omlb_ref, q_ref, f_ref, k_ref, i_ref, g_ref):
    d = h_ref.shape[1]
    u = (_rms(h_ref[...]) * gmix_ref[...]).astype(BF16)
    z = _dot(u, win_ref[...])
    q, f, g = z[:, :d], z[:, d:2 * d], z[:, 3 * d:]
    q_ref[...] = q * jax.nn.sigmoid(q)
    f_ref[...] = lb_ref[...] + omlb_ref[...] * jax.nn.sigmoid(f)
    k_ref[...] = omlb_ref[...] * jax.nn.sigmoid(-f)
    i_ref[...] = z[:, 2 * d:3 * d]
    g_ref[...] = g * jax.nn.sigmoid(g)


def _hgrn_proj_call(h, gmix, win, lb, omlb):
    m, d = h.shape
    tm = min(m, 256)
    row = pl.BlockSpec((tm, d), lambda i: (i, 0))
    return pl.pallas_call(
        _hgrn_proj_kernel,
        grid=(m // tm,),
        in_specs=[row, _resident(gmix.shape), _resident(win.shape), _resident(lb.shape), _resident(omlb.shape)],
        out_specs=[row] * 5,
        out_shape=[jax.ShapeDtypeStruct((m, d), F32)] * 5,
        compiler_params=_cparams(1),
        name="hgrn_proj",
    )(h, gmix, win, lb, omlb)


def _rows_to_lanes_dup(xa, xb):
    chunks = xa.shape[1] // LANES
    stack = jnp.concatenate([x[:, c * LANES:(c + 1) * LANES] for x in (xa, xb) for c in range(chunks)], axis=0)
    tr = stack.T
    half = LANES // 2
    low = _lane_iota() < half
    other = pltpu.roll(tr, half, axis=1)
    return jnp.where(low, tr, other), jnp.where(low, other, tr)


def _hgrn_scan_kernel(tc, q_ref, f_ref, k_ref, v_ref, s0_ref, o_ref, sout_ref, state_ref, keys_ref, vals_ref):
    c = pl.program_id(1)
    dk = state_ref.shape[0]

    @pl.when(c == 0)
    def _():
        state_ref[...] = s0_ref[...]

    def stage_in(tp, _):
        t = 2 * tp
        for i, ref in enumerate((q_ref, f_ref, k_ref)):
            xa, xb = _rows_to_lanes_dup(ref[t], ref[t + 1])
            keys_ref[i, t] = xa
            keys_ref[i, t + 1] = xb
        va, vb = _rows_to_lanes(v_ref[t], v_ref[t + 1])
        vals_ref[t] = va
        vals_ref[t + 1] = vb
        return 0

    lax.fori_loop(0, tc // 2, stage_in, 0)

    def step(t, _):
        v = vals_ref[t]
        o = jnp.zeros(v.shape, F32)
        for d in range(dk):
            s_new = state_ref[d] * keys_ref[1, t, d:d + 1, :] + keys_ref[2, t, d:d + 1, :] * v
            state_ref[d] = s_new
            o = o + s_new * keys_ref[0, t, d:d + 1, :]
        vals_ref[t] = o
        return 0

    lax.fori_loop(0, tc, step, 0)

    def stage_out(tp, _):
        t = 2 * tp
        _lanes_to_rows(vals_ref[t], vals_ref[t + 1], o_ref, t)
        return 0

    lax.fori_loop(0, tc // 2, stage_out, 0)

    @pl.when(c == pl.num_programs(1) - 1)
    def _():
        sout_ref[...] = state_ref[...]


def _hgrn_scan_call(q, f, k, v, s0):
    t, b, d = q.shape
    dk = HGRN_HEAD_DIM
    dv = dk // 2
    assert d == 8 * LANES and b % 8 == 0 and t % 2 == 0
    tc = math.gcd(t, 32)
    seq = pl.BlockSpec((tc, 8, d), lambda g, c: (c, g, 0))
    st = pl.BlockSpec((dk, dv, LANES), lambda g, c: (0, 0, g))
    return pl.pallas_call(
        functools.partial(_hgrn_scan_kernel, tc),
        grid=(b // 8, t // tc),
        in_specs=[seq] * 4 + [st],
        out_specs=[seq, st],
        out_shape=[jax.ShapeDtypeStruct((t, b, d), F32), jax.ShapeDtypeStruct((dk, dv, b // 8 * LANES), F32)],
        scratch_shapes=[pltpu.VMEM((dk, dv, LANES), F32), pltpu.VMEM((3, tc, dk, LANES), F32),
                        pltpu.VMEM((tc, dv, LANES), F32)],
        compiler_params=_cparams(2),
        name="hgrn_scan",
    )(q, f, k, v, s0)


def _hgrn_layer(h, batch, s0, gmix, win, lb, omlb):
    m, d = h.shape
    t = m // batch
    n = HGRN_HEAD_DIM
    nh = d // n
    g8 = batch // 8
    q, f, k, i, g = _hgrn_proj_call(h, gmix, win, lb, omlb)
    rows = lambda x: x.reshape(t, batch, d)
    st = s0.reshape(g8, 8, nh, n, 2, n // 2).transpose(3, 5, 0, 4, 2, 1).reshape(n, n // 2, g8 * LANES)
    o, s = _hgrn_scan_call(rows(q), rows(f), rows(k), rows(i), st)
    s = s.reshape(n, n // 2, g8, 2, nh, 8).transpose(2, 5, 4, 0, 3, 1).reshape(batch, nh, n, n)
    return o.reshape(m, d), g, s


def kernel(x_prompt, x_sample, cache_cmp, cache_slc, cache_win, state_rwkv_shift, state_rwkv_wkv, state_hgrn,
           page_table, p_prompt, p_sample, norm_mix, norm_ffn, w_up, w_down, w_ple, w_ple_gate,
           nsa_w_in, nsa_g_q, nsa_g_ks, nsa_g_kw, nsa_g_kc, nsa_cmp_pe, nsa_cmp_w1, nsa_cmp_w2, nsa_w_out,
           rwkv_mu, rwkv_w_r, rwkv_w_k, rwkv_w_v, rwkv_w_o, rwkv_w0, rwkv_w1, rwkv_w2, rwkv_a0, rwkv_a1,
           rwkv_a2, rwkv_g1, rwkv_g2, rwkv_k_k, rwkv_k_a, rwkv_r_k, rwkv_lnx_w, rwkv_lnx_b,
           hgrn_w_in, hgrn_gn, hgrn_w_o, hgrn_lower_bounds):
    depth = norm_mix.shape[0]
    bp, tp, d = x_prompt.shape
    bs, ts, _ = x_sample.shape
    lb_soft = jax.nn.softmax(hgrn_lower_bounds.astype(F32), axis=0)
    lower_bound = jnp.cumsum(lb_soft, axis=0) - lb_soft[0]
    rows_minor = lambda x: jnp.transpose(x, (0, 1, 3, 4, 5, 2))
    cmp_pool_t, slc_pool_t, win_buf_t = rows_minor(cache_cmp), rows_minor(cache_slc), rows_minor(cache_win)
    hp, hs = x_prompt.reshape(bp * tp, d), x_sample.reshape(bs * ts, d)
    swap = lambda x, a, b: x.reshape(a, b, x.shape[-1]).transpose(1, 0, 2).reshape(a * b, x.shape[-1])
    time_major = False
    outs = [[] for _ in range(12)]
    for i in range(depth):
        kind, n = i % N_MIXERS, i // N_MIXERS
        gmix = norm_mix[i][None]
        ffn = (p_prompt[i].reshape(bp * tp, -1), p_sample[i].reshape(bs * ts, -1))
        if (kind != 0) != time_major:
            hp = swap(hp, tp, bp) if time_major else swap(hp, bp, tp)
            hs = swap(hs, ts, bs) if time_major else swap(hs, bs, ts)
            time_major = not time_major
        if time_major:
            ffn = (swap(ffn[0], bp, tp), swap(ffn[1], bs, ts))
        hp3, hs3 = hp.reshape(bp, tp, d), hs.reshape(bs, ts, d)
        tail = (norm_ffn[i][None], w_up[i].astype(BF16), w_down[i].astype(BF16), w_ple[i].astype(BF16),
                w_ple_gate[i].astype(BF16))
        if kind == 0:
            prm = _nsa_params(nsa_w_in[n], nsa_g_q[n], nsa_g_ks[n], nsa_g_kw[n], nsa_g_kc[n], nsa_cmp_pe[n],
                              nsa_cmp_w1[n], nsa_cmp_w2[n])
            op, rc, rs, rw = _nsa_prompt_layer(hp3, gmix, prm)
            os_, nc_rows, ns_rows, nw_buf = _nsa_sample_layer(hs3, cmp_pool_t, slc_pool_t, cache_win[n], win_buf_t,
                                                              n, page_table, gmix, prm)
            for lst, v in zip(outs[:6], (rc, nc_rows, rs, ns_rows, rw, nw_buf)):
                lst.append(v)
            wo = nsa_w_out[n].astype(BF16)
            hp = _ffn_call("nsa", hp, op, None, None, ffn[0], wo, *tail)
            hs = _ffn_call("nsa", hs, os_, None, None, ffn[1], wo, *tail)
        elif kind == 1:
            prm = _rwkv_params(rwkv_mu[n], rwkv_w_r[n], rwkv_w_k[n], rwkv_w_v[n], rwkv_w0[n], rwkv_w1[n],
                               rwkv_w2[n], rwkv_a0[n], rwkv_a1[n], rwkv_a2[n], rwkv_g1[n], rwkv_g2[n])
            vecs = (rwkv_k_k[n], rwkv_k_a[n], rwkv_r_k[n], rwkv_lnx_w[n], rwkv_lnx_b[n])
            nh = d // RWKV_HEAD_DIM
            zero_state = jnp.zeros((bp, nh, RWKV_HEAD_DIM, RWKV_HEAD_DIM), F32)
            yp, gp, shp, sp = _rwkv_layer(hp, bp, jnp.zeros((bp, d), F32), zero_state, gmix, prm, *vecs)
            ys, gs, shs, ss = _rwkv_layer(hs, bs, state_rwkv_shift[n], state_rwkv_wkv[n].astype(F32), gmix, prm,
                                          *vecs)
            for lst, v in zip(outs[6:10], (shp, shs, sp, ss)):
                lst.append(v)
            wo = rwkv_w_o[n].astype(BF16)
            hp = _ffn_call("rwkv", hp, yp, gp, None, ffn[0], wo, *tail)
            hs = _ffn_call("rwkv", hs, ys, gs, None, ffn[1], wo, *tail)
        else:
            lb = lower_bound[i][None]
            win = hgrn_w_in[n].astype(BF16)
            nh = d // HGRN_HEAD_DIM
            zero_state = jnp.zeros((bp, nh, HGRN_HEAD_DIM, HGRN_HEAD_DIM), F32)
            op, gp, sp = _hgrn_layer(hp, bp, zero_state, gmix, win, lb, 1.0 - lb)
            os_, gs, ss = _hgrn_layer(hs, bs, state_hgrn[n].astype(F32), gmix, win, lb, 1.0 - lb)
            outs[10].append(sp)
            outs[11].append(ss)
            wo, gn = hgrn_w_o[n].astype(BF16), hgrn_gn[n][None]
            hp = _ffn_call("hgrn", hp, op, gp, gn, ffn[0], wo, *tail)
            hs = _ffn_call("hgrn", hs, os_, gs, gn, ffn[1], wo, *tail)
    if time_major:
        hp, hs = swap(hp, tp, bp), swap(hs, ts, bs)
    return (hp.reshape(bp, tp, d), hs.reshape(bs, ts, d)) + tuple(jnp.stack(o) for o in outs)
```

```python
import functools
import math

import jax
import jax.numpy as jnp
from jax import lax
from jax.experimental import pallas as pl
from jax.experimental.pallas import tpu as pltpu

F32 = jnp.float32
BF16 = jnp.bfloat16

NORM_EPS = 1e-6
ROPE_THETA = 10000.0
NEG_INF = -1e30
M_INIT = -1e29
N_MIXERS = 3

NSA_HEADS = 16
NSA_KV_HEADS = 4
NSA_HEAD_DIM = 64
NSA_GROUP = NSA_HEADS // NSA_KV_HEADS
NSA_KV_WIDTH = NSA_KV_HEADS * NSA_HEAD_DIM
NSA_SCALE = NSA_HEAD_DIM ** -0.5
LOG2E = math.log2(math.e)
CMP_BLOCK = 32
SEL_BLOCK = 64
N_SELECT = 16
WINDOW = 512
FORCE_SCORE = 1e4

RWKV_HEAD_DIM = 64
LNX_EPS = 64e-5
HGRN_HEAD_DIM = 128

LANES = 128
VMEM_LIMIT = 56 * 1024 * 1024


def _cparams(n_axes):
    return pltpu.CompilerParams(dimension_semantics=("arbitrary",) * n_axes,
                                vmem_limit_bytes=VMEM_LIMIT)


def _resident(shape):
    zeros = (0,) * len(shape)
    return pl.BlockSpec(shape, lambda *_: zeros, pipeline_mode=pl.Buffered(1))


def _rms(x):
    return x * lax.rsqrt(jnp.mean(x * x, axis=-1, keepdims=True) + NORM_EPS)


def _dot(a, b):
    return jnp.dot(a, b, preferred_element_type=F32)


def _dot_nt(a, b):
    return lax.dot_general(a, b, (((1,), (1,)), ((), ())), preferred_element_type=F32)


def _per_chunk(fn, x, *rest):
    n = x.shape[-1] // LANES
    outs = [fn(*(a[:, c * LANES:(c + 1) * LANES] for a in (x,) + rest)) for c in range(n)]
    return outs[0] if n == 1 else jnp.concatenate(outs, axis=-1)


def _lane_iota():
    return lax.broadcasted_iota(jnp.int32, (1, LANES), 1)


def _group_ones(group):
    r = lax.broadcasted_iota(jnp.int32, (LANES, LANES), 0) // group
    c = lax.broadcasted_iota(jnp.int32, (LANES, LANES), 1) // group
    return jnp.where(r == c, 1.0, 0.0).astype(BF16)


def _head_rms(x, head_dim, ones):
    def one(c):
        sq = c * c
        hi = sq.astype(BF16)
        lo = (sq - hi.astype(F32)).astype(BF16)
        ss = _dot(hi, ones) + _dot(lo, ones)
        return c * lax.rsqrt(ss * (1.0 / head_dim) + NORM_EPS)
    return _per_chunk(one, x)


def _rope(x, cos, sin_signed):
    half = NSA_HEAD_DIM // 2
    lane = _lane_iota()

    def one(c):
        up = pltpu.roll(c, LANES - half, axis=1)
        dn = pltpu.roll(c, half, axis=1)
        rot = jnp.where((lane & half) == 0, up, dn)
        return c * cos + rot * sin_signed
    return _per_chunk(one, x)


def _ffn_kernel(mode, ff_chunk, *refs):
    if mode == "nsa":
        h_ref, o_ref, p_ref, wo_ref, gffn_ref, wup_ref, wdown_ref, wple_ref, wgate_ref, out_ref = refs
        o = o_ref[...]
    elif mode == "rwkv":
        h_ref, o_ref, aux_ref, p_ref, wo_ref, gffn_ref, wup_ref, wdown_ref, wple_ref, wgate_ref, out_ref = refs
        o = o_ref[...] * aux_ref[...]
    else:
        (h_ref, o_ref, aux_ref, gn_ref, p_ref, wo_ref, gffn_ref, wup_ref, wdown_ref, wple_ref, wgate_ref,
         out_ref) = refs
        gn = gn_ref[...]
        o = _per_chunk(lambda c: _rms(c) * gn, o_ref[...]) * aux_ref[...]
    h1 = h_ref[...] + _dot(o.astype(BF16), wo_ref[...])
    u = (_rms(h1) * gffn_ref[...]).astype(BF16)
    d_ff = wup_ref.shape[1]
    acc = jnp.zeros_like(h1)
    for j in range(d_ff // ff_chunk):
        a = _dot(u, wup_ref[:, j * ff_chunk:(j + 1) * ff_chunk])
        a = jnp.square(jnp.maximum(a, 0.0)).astype(BF16)
        acc = acc + _dot(a, wdown_ref[j * ff_chunk:(j + 1) * ff_chunk, :])
    h2 = h1 + acc
    gate = jax.nn.sigmoid(_dot(_rms(h2).astype(BF16), wgate_ref[...]))
    out_ref[...] = h2 + _dot(p_ref[...].astype(BF16), wple_ref[...]) * gate


def _ffn_call(mode, h, o, aux, gn, p, wo, gffn, wup, wdown, wple, wgate):
    m, d = h.shape
    tm = min(m, 512)
    row = lambda w: pl.BlockSpec((tm, w), lambda i: (i, 0))
    args, specs = [h, o], [row(d), row(d)]
    if mode != "nsa":
        args.append(aux)
        specs.append(row(d))
    if mode == "hgrn":
        args.append(gn)
        specs.append(_resident(gn.shape))
    args += [p, wo, gffn, wup, wdown, wple, wgate]
    specs += [row(p.shape[1])] + [_resident(a.shape) for a in (wo, gffn, wup, wdown, wple, wgate)]
    return pl.pallas_call(
        functools.partial(_ffn_kernel, mode, 1024),
        grid=(m // tm,),
        in_specs=specs,
        out_specs=row(d),
        out_shape=jax.ShapeDtypeStruct((m, d), F32),
        compiler_params=_cparams(1),
        name="ffn_" + mode,
    )(*args)


def _nsa_proj_kernel(h_ref, gmix_ref, win_ref, gq_ref, gks_ref, gkw_ref, cos_ref, sin_ref,
                     qn_ref, qr_ref, cmp_ref, slc_ref, wrow_ref, gate_ref):
    kvw = NSA_KV_WIDTH
    nq = NSA_HEADS * NSA_HEAD_DIM
    u = (_rms(h_ref[...]) * gmix_ref[...]).astype(BF16)
    z = _dot(u, win_ref[...])
    cos = cos_ref[...]
    sin = sin_ref[...]
    ones = _group_ones(NSA_HEAD_DIM)
    qn = _head_rms(z[:, :nq], NSA_HEAD_DIM, ones) * gq_ref[...]
    qn_ref[...] = qn
    qr_ref[...] = _rope(qn, cos, sin)
    cmp_ref[...] = z[:, nq:nq + 2 * kvw]
    o = nq + 2 * kvw
    slc_ref[:, :kvw] = _rope(_head_rms(z[:, o:o + kvw], NSA_HEAD_DIM, ones) * gks_ref[...], cos, sin)
    slc_ref[:, kvw:] = z[:, o + kvw:o + 2 * kvw]
    o += 2 * kvw
    wrow_ref[:, :kvw] = _rope(_head_rms(z[:, o:o + kvw], NSA_HEAD_DIM, ones) * gkw_ref[...], cos, sin)
    wrow_ref[:, kvw:] = z[:, o + kvw:o + 2 * kvw]
    o += 2 * kvw
    gate_ref[...] = jax.nn.sigmoid(z[:, o:])


def _nsa_proj_call(h, gmix, win, gq, gks, gkw, cos, sin):
    m, d = h.shape
    tm = min(m, 256)
    tab_tiles = cos.shape[0] // tm
    row = lambda w: pl.BlockSpec((tm, w), lambda i: (i, 0))
    tab = pl.BlockSpec((tm, LANES), lambda i: (i % tab_tiles, 0))
    nq = NSA_HEADS * NSA_HEAD_DIM
    widths = [nq, nq] + [2 * NSA_KV_WIDTH] * 3 + [2 * LANES]
    return pl.pallas_call(
        _nsa_proj_kernel,
        grid=(m // tm,),
        in_specs=[row(d), _resident(gmix.shape), _resident(win.shape), _resident(gq.shape),
                  _resident(gks.shape), _resident(gkw.shape), tab, tab],
        out_specs=[row(w) for w in widths],
        out_shape=[jax.ShapeDtypeStruct((m, w), F32) for w in widths],
        compiler_params=_cparams(1),
        name="nsa_proj",
    )(h, gmix, win, gq, gks, gkw, cos, sin)


def _compress_slot(load_pair, m, e, pe_ref, w1_ref, w2_ref, gkc_ref):
    lane = _lane_iota()
    low = lane < NSA_HEAD_DIM
    acc = jnp.zeros((NSA_KV_HEADS * m, w1_ref.shape[2]), F32)
    for s in range(CMP_BLOCK // 2):
        pe_row = pe_ref[e, s:s + 1, :]
        parts = []
        for kp in range(NSA_KV_HEADS // 2):
            a = load_pair(kp, 2 * s)
            b = load_pair(kp, 2 * s + 1)
            parts.append(jnp.where(low, a, pltpu.roll(b, NSA_HEAD_DIM, axis=1)))
            parts.append(jnp.where(low, pltpu.roll(a, NSA_HEAD_DIM, axis=1), b))
        x = (jnp.concatenate(parts, axis=0) + pe_row).astype(BF16)
        acc = acc + _dot(x, w1_ref[e, s * LANES:(s + 1) * LANES, :])
    out = _dot(jax.nn.gelu(acc).astype(BF16), w2_ref[e])
    if e == 0:
        out = _rms(out) * gkc_ref[...]
    return jnp.concatenate([out[k * m:(k + 1) * m] for k in range(NSA_KV_HEADS)], axis=-1)


def _compress_rows_kernel(*refs):
    cols = refs[:NSA_KV_HEADS]
    pe_ref, w1_ref, w2_ref, gkc_ref, out_ref = refs[NSA_KV_HEADS:]
    m = cols[0].shape[1] // CMP_BLOCK
    pad = out_ref.shape[2] - m
    for e in range(2):
        load = lambda kp, r: cols[2 * e + kp][0, pl.ds(r, m, stride=CMP_BLOCK), :]
        res = _compress_slot(load, m, e, pe_ref, w1_ref, w2_ref, gkc_ref)
        out_ref[0, e] = jnp.concatenate([res, jnp.zeros((pad, res.shape[1]), F32)], axis=0)


def _compress_rows_call(cmp, pe, w1, w2, gkc):
    b, t, w = cmp.shape
    c = -(-(t // CMP_BLOCK) // LANES) * LANES
    return pl.pallas_call(
        _compress_rows_kernel,
        grid=(b,),
        in_specs=[pl.BlockSpec((1, t, LANES), functools.partial(lambda j, i: (i, 0, j), j)) for j in range(w // LANES)]
        + [_resident(a.shape) for a in (pe, w1, w2, gkc)],
        out_specs=pl.BlockSpec((1, 2, c, NSA_KV_WIDTH), lambda i: (i, 0, 0, 0)),
        out_shape=jax.ShapeDtypeStruct((b, 2, c, NSA_KV_WIDTH), F32),
        compiler_params=_cparams(1),
        name="nsa_compress_rows",
    )(*([cmp] * (w // LANES)), pe, w1, w2, gkc)


PAGES_PER_STEP = 16


def _page_specs(n, layer, tail):
    zeros = (0,) * len(tail)
    return [pl.BlockSpec((1, 1) + tail,
                         functools.partial(lambda i, b, c, pt: (layer, pt[b, c * n + i]) + zeros, i))
            for i in range(n)]


def _compress_pages_kernel(pt_ref, *refs):
    n = len(refs) - 6
    pages = refs[:n]
    pe_ref, w1_ref, w2_ref, gkc_ref, out_ref, rows_ref = refs[n:]
    page = pages[0].shape[-1]
    m = n * page // CMP_BLOCK
    for e in range(2):
        for i, pg in enumerate(pages):
            for kp in range(NSA_KV_HEADS // 2):
                tile = pg[0, 0, e, 2 * kp:2 * kp + 2].reshape(LANES, page)
                rows_ref[kp, i * page:(i + 1) * page, :] = tile.T
        load = lambda kp, r: rows_ref[kp, pl.ds(r, m, stride=CMP_BLOCK), :]
        out_ref[0, e] = _compress_slot(load, m, e, pe_ref, w1_ref, w2_ref, gkc_ref)


def _compress_pages_call(pool_t, layer, page_table, pe, w1, w2, gkc):
    b, n_pages = page_table.shape
    page = pool_t.shape[-1]
    n = PAGES_PER_STEP
    m = n * page // CMP_BLOCK
    return pl.pallas_call(
        _compress_pages_kernel,
        grid_spec=pltpu.PrefetchScalarGridSpec(
            num_scalar_prefetch=1,
            grid=(b, n_pages // n),
            in_specs=_page_specs(n, layer, pool_t.shape[2:])
            + [pl.BlockSpec(a.shape, functools.partial(lambda nd, i, c, pt: (0,) * nd, a.ndim),
                            pipeline_mode=pl.Buffered(1)) for a in (pe, w1, w2, gkc)],
            out_specs=pl.BlockSpec((1, 2, m, NSA_KV_WIDTH), lambda i, c, pt: (i, 0, c, 0)),
            scratch_shapes=[pltpu.VMEM((NSA_KV_HEADS // 2, n * page, LANES), F32)],
        ),
        out_shape=jax.ShapeDtypeStruct((b, 2, n_pages * page // CMP_BLOCK, NSA_KV_WIDTH), F32),
        compiler_params=_cparams(2),
        name="nsa_compress_pages",
    )(page_table, *([pool_t] * n), pe, w1, w2, gkc)


def _select_blocks_t(imp, tpos):
    n_blocks = imp.shape[0]
    blk = lax.broadcasted_iota(jnp.int32, (n_blocks, 1), 0)
    cur = tpos // SEL_BLOCK
    forced = (blk == 0) | (blk == cur) | (blk == cur - 1)
    score = jnp.where(blk <= cur, jnp.where(forced, FORCE_SCORE, imp), -1.0)
    rank = jnp.zeros(score.shape, F32)
    for j in range(n_blocks):
        row = score[j:j + 1, :]
        below = jnp.where(j < blk, 1.0, 0.0)
        rank = rank + jnp.where(row > score, 1.0, jnp.where(row == score, below, 0.0))
    return jnp.where(rank < float(min(N_SELECT, n_blocks)), 1.0, 0.0)


def _attn_prompt_kernel(tq, tk, qn_ref, qr_ref, kvc_ref, sk_ref, sv_ref, wk_ref, wv_ref, gate_ref, o_ref,
                        kb_ref, vt_ref, sel_ref, imp_ref):
    qi = pl.program_id(2)
    hd = NSA_HEAD_DIM
    t_len = sk_ref.shape[1]
    c_blocks = kvc_ref.shape[2]
    n_sel = t_len // SEL_BLOCK
    ratio = SEL_BLOCK // CMP_BLOCK
    rows_per_tile = tk // SEL_BLOCK

    @pl.when(qi == 0)
    def _():
        for br, (k_ref, v_ref) in enumerate(((sk_ref, sv_ref), (wk_ref, wv_ref))):
            for gl in range(2):
                kb_ref[br, gl] = k_ref[0, :, gl * hd:(gl + 1) * hd].astype(BF16)
            for kt in range(t_len // tk):
                vt_ref[br, kt] = v_ref[0, kt * tk:(kt + 1) * tk, :].T.astype(BF16)

    t0 = qi * tq
    tpos = t0 + lax.broadcasted_iota(jnp.int32, (1, tq), 1)
    gate_t = gate_ref[0].T
    qn_t = [qn_ref[0, :, c * LANES:(c + 1) * LANES].T for c in range(2 * NSA_GROUP * hd // LANES)]
    qr_t = [qr_ref[0, :, c * LANES:(c + 1) * LANES].T for c in range(2 * NSA_GROUP * hd // LANES)]
    cid = lax.broadcasted_iota(jnp.int32, (c_blocks, 1), 0)
    ok = jnp.where((cid + 1) * CMP_BLOCK - 1 <= tpos, 1.0, 0.0)
    ok4 = jnp.concatenate([ok] * NSA_GROUP, axis=1)
    qr4s, o_cmps = [], []
    for gl in range(2):
        per_head = lambda parts: [parts[2 * gl + hh // 2][(hh % 2) * hd:(hh % 2 + 1) * hd] for hh in range(NSA_GROUP)]
        qn4 = (jnp.concatenate(per_head(qn_t), axis=1) * NSA_SCALE).astype(BF16)
        qr4s.append((jnp.concatenate(per_head(qr_t), axis=1) * (NSA_SCALE * LOG2E)).astype(BF16))
        kc = kvc_ref[0, 0][:, gl * hd:(gl + 1) * hd]
        vc_t = kvc_ref[0, 1].T[gl * hd:(gl + 1) * hd]
        s = jnp.where(ok4 > 0.5, _dot(kc.astype(BF16), qn4), NEG_INF)
        e = jnp.exp(s - jnp.max(s, axis=0, keepdims=True))
        p = e / jnp.sum(e, axis=0, keepdims=True) * ok4
        o_cmps.append(_dot(vc_t.astype(BF16), p.astype(BF16)))
        imp = p[:, :tq]
        for hh in range(1, NSA_GROUP):
            imp = imp + p[:, hh * tq:(hh + 1) * tq]
        imp_ref[...] = imp
        imp_sel = imp_ref[pl.ds(0, n_sel, stride=ratio), :]
        for i in range(1, ratio):
            imp_sel = imp_sel + imp_ref[pl.ds(i, n_sel, stride=ratio), :]
        sel_ref[gl] = _select_blocks_t(imp_sel, tpos)

    def flash(br, lo, hi, valid_fn):
        def body(kt, carry):
            k0 = pl.multiple_of(kt * tk, tk)
            out = []
            for gl in range(2):
                m, l, acc = carry[gl]
                kb = kb_ref[br, gl, pl.ds(k0, tk), :]
                vt = vt_ref[br, kt, gl * hd:(gl + 1) * hd, :]
                bias = jnp.where(valid_fn(gl, kt, k0), 0.0, NEG_INF)
                s = _dot(kb, qr4s[gl]) + jnp.concatenate([bias] * NSA_GROUP, axis=1)
                m_new = jnp.maximum(m, jnp.max(s, axis=0, keepdims=True))
                alpha = jnp.exp2(m - m_new)
                p = jnp.exp2(s - m_new)
                out.append((m_new, alpha * l + jnp.sum(p, axis=0, keepdims=True),
                            alpha * acc + _dot(vt, p.astype(BF16))))
            return tuple(out)
        cols = NSA_GROUP * tq
        init = tuple((jnp.full((1, cols), M_INIT, F32), jnp.zeros((1, cols), F32), jnp.zeros((hd, cols), F32))
                     for _ in range(2))
        res = lax.fori_loop(lo, hi, body, init)
        return [acc / l for _, l, acc in res]

    def slc_valid(gl, kt, k0):
        picked = sel_ref[gl, pl.ds(pl.multiple_of(kt * rows_per_tile, rows_per_tile), rows_per_tile), :]
        picked = jnp.concatenate([jnp.broadcast_to(picked[i:i + 1], (SEL_BLOCK, tq))
                                  for i in range(rows_per_tile)], axis=0)
        kpos = k0 + lax.broadcasted_iota(jnp.int32, (tk, 1), 0)
        return (picked > 0.5) & (kpos <= tpos)

    def win_valid(gl, kt, k0):
        dist = tpos - (k0 + lax.broadcasted_iota(jnp.int32, (tk, 1), 0))
        return (dist >= 0) & (dist <= WINDOW)

    hi = (t0 + tq - 1) // tk + 1
    slc = flash(0, 0, hi, slc_valid)
    win = flash(1, jnp.maximum(t0 - WINDOW, 0) // tk, hi, win_valid)
    heads = []
    for gl in range(2):
        for hh in range(NSA_GROUP):
            c = (gl * NSA_GROUP + hh) * 3
            cols = slice(hh * tq, (hh + 1) * tq)
            heads.append(o_cmps[gl][:, cols] * gate_t[c:c + 1] + slc[gl][:, cols] * gate_t[c + 1:c + 2]
                         + win[gl][:, cols] * gate_t[c + 2:c + 3])
    for c in range(len(heads) // 2):
        o_ref[0, :, c * LANES:(c + 1) * LANES] = jnp.concatenate(heads[2 * c:2 * c + 2], axis=0).T


def _attn_prompt_call(qn, qr, kvc, slc, win, gate, tq=128, tk=512):
    b, t, nq = qn.shape
    pair_w = 2 * NSA_GROUP * NSA_HEAD_DIM
    c = kvc.shape[2]
    assert tk % (8 * SEL_BLOCK) == 0 and t % tk == 0 and c % LANES == 0 and tq == LANES
    qspec = pl.BlockSpec((1, tq, pair_w), lambda i, p, j: (i, j, p))
    kspec = pl.BlockSpec((1, t, LANES), lambda i, p, j: (i, 0, p))
    vspec = pl.BlockSpec((1, t, LANES), lambda i, p, j: (i, 0, 2 + p))
    return pl.pallas_call(
        functools.partial(_attn_prompt_kernel, tq, tk),
        grid=(b, 2, t // tq),
        in_specs=[qspec, qspec,
                  pl.BlockSpec((1, 2, c, LANES), lambda i, p, j: (i, 0, 0, p)),
                  kspec, vspec, kspec, vspec,
                  pl.BlockSpec((1, tq, LANES), lambda i, p, j: (i, j, p))],
        out_specs=qspec,
        out_shape=jax.ShapeDtypeStruct((b, t, nq), F32),
        scratch_shapes=[pltpu.VMEM((2, 2, t, NSA_HEAD_DIM), BF16),
                        pltpu.VMEM((2, t // tk, LANES, tk), BF16),
                        pltpu.VMEM((2, t // SEL_BLOCK, tq), F32),
                        pltpu.VMEM((c, tq), F32)],
        compiler_params=_cparams(3),
        name="nsa_attn_prompt",
    )(qn, qr, kvc, slc, slc, win, win, gate)


def _cmp_branch(q4, kc, vc, tpos, n_tok):
    c = kc.shape[0]
    s = _dot_nt((q4 * NSA_SCALE).astype(BF16), kc.astype(BF16))
    cidx = lax.broadcasted_iota(jnp.int32, (1, c), 1)
    ok = ((cidx + 1) * CMP_BLOCK - 1 <= tpos).astype(F32)
    ok4 = jnp.concatenate([ok] * NSA_GROUP, axis=0)
    s = jnp.where(ok4 > 0.5, s, NEG_INF)
    e = jnp.exp(s - jnp.max(s, axis=-1, keepdims=True))
    p = e / jnp.sum(e, axis=-1, keepdims=True) * ok4
    o = _dot(p.astype(BF16), vc.astype(BF16))
    imp = p[:n_tok]
    for hh in range(1, NSA_GROUP):
        imp = imp + p[hh * n_tok:(hh + 1) * n_tok]
    return o, imp


def _select_blocks(imp, tpos, n_blocks):
    c = imp.shape[1]
    lane = lax.broadcasted_iota(jnp.int32, (1, c), 1)

    def pair(x):
        lane1 = _lane_iota()
        return x + jnp.where((lane1 & 1) == 0, pltpu.roll(x, LANES - 1, axis=1), pltpu.roll(x, 1, axis=1))
    imp2 = _per_chunk(pair, imp)
    blk = lane // 2
    cur = tpos // SEL_BLOCK
    forced = (blk == 0) | (blk == cur) | (blk == cur - 1)
    score = jnp.where(blk <= cur, jnp.where(forced, FORCE_SCORE, imp2), -1.0)
    rank = jnp.zeros(score.shape, F32)
    for j in range(n_blocks):
        col = score[:, 2 * j:2 * j + 1]
        ahead = (col > score) | ((col == score) & (j < blk))
        rank = rank + ahead.astype(F32)
    return (rank < float(min(N_SELECT, n_blocks))).astype(F32)


def _flash_step(q4, k, v, valid, carry, transposed):
    m, l, acc = carry
    valid4 = jnp.concatenate([valid] * NSA_GROUP, axis=0)
    s = _dot(q4, k.astype(BF16)) if transposed else _dot_nt(q4, k.astype(BF16))
    s = jnp.where(valid4 > 0.5, s, NEG_INF)
    m_new = jnp.maximum(m, jnp.max(s, axis=-1, keepdims=True))
    alpha = jnp.exp(m - m_new)
    p = jnp.exp(s - m_new)
    l = alpha * l + jnp.sum(p, axis=-1, keepdims=True)
    pv = _dot_nt(p.astype(BF16), v.astype(BF16)) if transposed else _dot(p.astype(BF16), v.astype(BF16))
    return m_new, l, alpha * acc + pv


def _attn_sample_kernel(n_step, t_real, past_len, pt_ref, qn_ref, qr_ref, kvc_ref, gate_ref, snew_ref, wbuf_ref,
                        wnew_ref, expand_ref, *rest):
    pages = rest[:n_step]
    o_ref, sel_ref, ocmp_ref, m_ref, l_ref, acc_ref, qbd_ref = rest[n_step:]
    c = pl.program_id(1)
    hd = NSA_HEAD_DIM
    kvw = NSA_KV_WIDTH
    tq = qn_ref.shape[1]
    n_cmp = past_len // CMP_BLOCK
    tpos = past_len + lax.broadcasted_iota(jnp.int32, (tq, 1), 0) % t_real
    stack = lambda ref, g: jnp.concatenate(
        [ref[0, :, (g * NSA_GROUP + hh) * hd:(g * NSA_GROUP + hh + 1) * hd] for hh in range(NSA_GROUP)], axis=0)
    rows = NSA_GROUP * tq

    @pl.when(c == 0)
    def _():
        n_blocks = -(-(past_len + t_real) // SEL_BLOCK)
        for g in range(NSA_KV_HEADS):
            ksl = slice(g * hd, (g + 1) * hd)
            o_cmp, imp = _cmp_branch(stack(qn_ref, g), kvc_ref[0, 0][:, ksl], kvc_ref[0, 1][:, ksl], tpos, tq)
            imp = jnp.concatenate([imp, jnp.zeros((tq, LANES), F32)], axis=-1)
            sel_ref[g] = _select_blocks(imp, tpos, n_blocks)
            ocmp_ref[g] = o_cmp
            q4 = (stack(qr_ref, g) * NSA_SCALE).astype(BF16)
            zero = jnp.zeros((rows, hd), BF16)
            qbd_ref[g * rows:(g + 1) * rows, :] = jnp.concatenate(
                [q4 if gg == g else zero for gg in range(NSA_KV_HEADS)], axis=1)
        m_ref[...] = jnp.full(m_ref.shape, M_INIT, F32)
        l_ref[...] = jnp.zeros(l_ref.shape, F32)
        acc_ref[...] = jnp.zeros(acc_ref.shape, F32)

    tk = n_step * pages[0].shape[-1]
    k0 = c * tk
    kpos = k0 + lax.broadcasted_iota(jnp.int32, (1, tk), 1)
    kt_all = jnp.concatenate([pg[0, 0, 0].reshape(kvw, -1) for pg in pages], axis=1).astype(BF16)
    vt_all = jnp.concatenate([pg[0, 0, 1].reshape(kvw, -1) for pg in pages], axis=1).astype(BF16)
    sel_all = jnp.concatenate([sel_ref[g][:, :n_cmp] for g in range(NSA_KV_HEADS)], axis=0).astype(BF16)
    picked = _dot(sel_all, expand_ref[...])
    tpos_all = jnp.concatenate([tpos] * NSA_KV_HEADS, axis=0)
    bias = jnp.where((picked > 0.5) & (kpos <= tpos_all), 0.0, NEG_INF)
    bias = jnp.concatenate([bias[g * tq:(g + 1) * tq] for g in range(NSA_KV_HEADS) for _ in range(NSA_GROUP)],
                           axis=0)
    s = _dot(qbd_ref[...], kt_all) + bias
    m_old = m_ref[...]
    m_new = jnp.maximum(m_old, jnp.max(s, axis=-1, keepdims=True))
    alpha = jnp.exp(m_old - m_new)
    p = jnp.exp(s - m_new)
    m_ref[...] = m_new
    l_ref[...] = alpha * l_ref[...] + jnp.sum(p, axis=-1, keepdims=True)
    acc_ref[...] = alpha * acc_ref[...] + _dot_nt(p.astype(BF16), vt_all)

    @pl.when(c == pl.num_programs(1) - 1)
    def _():
        gate = gate_ref[0]
        rnew = lax.broadcasted_iota(jnp.int32, (1, tq), 1)
        newpos = past_len + rnew
        n_buf = wbuf_ref.shape[-1]
        bpos = past_len - n_buf + lax.broadcasted_iota(jnp.int32, (1, n_buf), 1)
        new_lane = 2 * (past_len // SEL_BLOCK)
        heads = []
        for g in range(NSA_KV_HEADS):
            q4 = (stack(qr_ref, g) * NSA_SCALE).astype(BF16)
            ks, vs = slice(g * hd, (g + 1) * hd), slice(kvw + g * hd, kvw + (g + 1) * hd)
            picked = sel_ref[g][:, new_lane:new_lane + 1] > 0.5
            valid = jnp.where(picked & (newpos <= tpos) & (rnew < t_real), 1.0, 0.0)
            gr = slice(g * rows, (g + 1) * rows)
            _, l_s, acc_s = _flash_step(q4, snew_ref[0, :, ks], snew_ref[0, :, vs], valid,
                                        (m_ref[gr, :], l_ref[gr, :], acc_ref[gr, ks]), False)
            dist = tpos - bpos
            valid = jnp.where((dist >= 0) & (dist <= WINDOW) & (bpos >= 0), 1.0, 0.0)
            init = (jnp.full((rows, 1), M_INIT, F32), jnp.zeros((rows, 1), F32), jnp.zeros((rows, hd), F32))
            carry = _flash_step(q4, wbuf_ref[0, 0, 0, g], wbuf_ref[0, 0, 1, g], valid, init, True)
            dist = tpos - newpos
            valid = jnp.where((dist >= 0) & (dist <= WINDOW) & (rnew < t_real), 1.0, 0.0)
            _, l_w, acc_w = _flash_step(q4, wnew_ref[0, :, ks], wnew_ref[0, :, vs], valid, carry, False)
            o_cmp = ocmp_ref[g]
            o_slc = acc_s / l_s
            o_win = acc_w / l_w
            for hh in range(NSA_GROUP):
                head = g * NSA_GROUP + hh
                hr = slice(hh * tq, (hh + 1) * tq)
                col = (head // (NSA_HEADS // 2)) * LANES + (head % (NSA_HEADS // 2)) * 3
                heads.append(o_cmp[hr] * gate[:, col:col + 1] + o_slc[hr] * gate[:, col + 1:col + 2]
                             + o_win[hr] * gate[:, col + 2:col + 3])
        o_ref[0] = jnp.concatenate(heads, axis=-1)


SAMPLE_PAGES_PER_STEP = 8


def _attn_sample_call(qn, qr, kvc, gate, slc_new, win_buf_t, win_new, pool_t, layer, page_table, t_real):
    b, tq, nq = qn.shape
    n_pages = page_table.shape[1]
    page = pool_t.shape[-1]
    past_len = n_pages * page
    assert past_len % SEL_BLOCK == 0 and t_real <= SEL_BLOCK and past_len % (CMP_BLOCK * LANES) == 0
    n = SAMPLE_PAGES_PER_STEP
    rows = NSA_GROUP * tq
    full = lambda a: pl.BlockSpec((1,) + a.shape[1:], lambda i, c, pt: (i,) + (0,) * (a.ndim - 1))
    wbuf_spec = pl.BlockSpec((1, 1) + win_buf_t.shape[2:], lambda i, c, pt: (layer, i, 0, 0, 0, 0))
    n_cmp = past_len // CMP_BLOCK
    sel_lanes = n_cmp + LANES
    expand = (jnp.arange(n_cmp)[:, None] == (jnp.arange(past_len) // CMP_BLOCK)[None, :]).astype(BF16)
    return pl.pallas_call(
        functools.partial(_attn_sample_kernel, n, t_real, past_len),
        grid_spec=pltpu.PrefetchScalarGridSpec(
            num_scalar_prefetch=1,
            grid=(b, n_pages // n),
            in_specs=[full(a) for a in (qn, qr, kvc, gate, slc_new)] + [wbuf_spec, full(win_new)]
            + [pl.BlockSpec((n_cmp, n * page), lambda i, c, pt: (0, c))]
            + _page_specs(n, layer, pool_t.shape[2:]),
            out_specs=pl.BlockSpec((1, tq, nq), lambda i, c, pt: (i, 0, 0)),
            scratch_shapes=[pltpu.VMEM((NSA_KV_HEADS, tq, sel_lanes), F32),
                            pltpu.VMEM((NSA_KV_HEADS, rows, NSA_HEAD_DIM), F32),
                            pltpu.VMEM((NSA_KV_HEADS * rows, 1), F32),
                            pltpu.VMEM((NSA_KV_HEADS * rows, 1), F32),
                            pltpu.VMEM((NSA_KV_HEADS * rows, NSA_KV_WIDTH), F32),
                            pltpu.VMEM((NSA_KV_HEADS * rows, NSA_KV_WIDTH), BF16)],
        ),
        out_shape=jax.ShapeDtypeStruct((b, tq, nq), F32),
        compiler_params=_cparams(2),
        name="nsa_attn_sample",
    )(page_table, qn, qr, kvc, gate, slc_new, win_buf_t, win_new, expand, *([pool_t] * n))


def _rope_tables(pos):
    half = NSA_HEAD_DIM // 2
    inv = ROPE_THETA ** (-jnp.arange(half, dtype=F32) / half)
    ang = pos.astype(F32)[:, None] * inv[None, :]
    cos, sin = jnp.cos(ang), jnp.sin(ang)
    reps = LANES // NSA_HEAD_DIM
    return jnp.tile(cos, (1, 2 * reps)), jnp.tile(jnp.concatenate([-sin, sin], axis=1), (1, reps))


def _nsa_params(w_in, g_q, g_ks, g_kw, g_kc, pe, w1, w2):
    d = w_in.shape[0]
    body = NSA_HEADS * NSA_HEAD_DIM + 6 * NSA_KV_WIDTH
    half = NSA_HEADS // 2 * 3
    zpad = jnp.zeros((d, LANES - half), w_in.dtype)
    w_pad = jnp.concatenate([w_in[:, :body], w_in[:, body:body + half], zpad, w_in[:, body + half:], zpad], axis=1)
    return dict(
        w_in=w_pad.astype(BF16),
        g_q=jnp.tile(g_q, NSA_HEADS)[None], g_ks=jnp.tile(g_ks, NSA_KV_HEADS)[None],
        g_kw=jnp.tile(g_kw, NSA_KV_HEADS)[None], g_kc=g_kc[None],
        pe=pe.reshape(2, CMP_BLOCK // 2, LANES), w1=w1.astype(BF16), w2=w2.astype(BF16))


def _kv_rows(x, b, t):
    return x.reshape(b, t, 2, NSA_KV_HEADS, NSA_HEAD_DIM)


def _nsa_prompt_layer(h, gmix, prm):
    b, t, d = h.shape
    cos, sin = _rope_tables(jnp.arange(t))
    qn, qr, cmp, slc, wrow, gate = _nsa_proj_call(
        h.reshape(b * t, d), gmix, prm["w_in"], prm["g_q"], prm["g_ks"], prm["g_kw"], cos, sin)
    r3 = lambda x: x.reshape(b, t, x.shape[-1])
    kvc = _compress_rows_call(r3(cmp), prm["pe"], prm["w1"], prm["w2"], prm["g_kc"])
    o = _attn_prompt_call(r3(qn), r3(qr), kvc, r3(slc), r3(wrow), r3(gate))
    keep = min(WINDOW, t)
    return (o.reshape(b * t, -1), _kv_rows(cmp, b, t), _kv_rows(slc, b, t), _kv_rows(wrow, b, t)[:, t - keep:])


def _nsa_sample_layer(h, cmp_pool_t, slc_pool_t, win_buf, win_buf_t, layer, page_table, gmix, prm):
    b, t, d = h.shape
    past_len = page_table.shape[1] * cmp_pool_t.shape[-1]
    cos, sin = _rope_tables(jnp.tile(past_len + jnp.arange(t), b))
    qn, qr, cmp, slc, wrow, gate = _nsa_proj_call(
        h.reshape(b * t, d), gmix, prm["w_in"], prm["g_q"], prm["g_ks"], prm["g_kw"], cos, sin)
    kvc = _compress_pages_call(cmp_pool_t, layer, page_table, prm["pe"], prm["w1"], prm["w2"], prm["g_kc"])
    tq = 8
    pad = lambda x: jnp.pad(x.reshape(b, t, x.shape[-1]), ((0, 0), (0, tq - t), (0, 0)))
    o = _attn_sample_call(pad(qn), pad(qr), kvc, pad(gate), pad(slc), win_buf_t, pad(wrow), slc_pool_t, layer,
                          page_table, t)
    new_win = jnp.concatenate([win_buf, _kv_rows(wrow, b, t)], axis=1)[:, t:]
    return o[:, :t].reshape(b * t, -1), _kv_rows(cmp, b, t), _kv_rows(slc, b, t), new_win


def _softplus(z):
    return jnp.maximum(z, 0.0) + jnp.log1p(jnp.exp(-jnp.abs(z)))


def _rwkv_proj_kernel(batch, h_ref, shift_ref, gmix_ref, mu_ref, wr_ref, wk_ref, wv_ref,
                      w0_ref, w1_ref, w2_ref, a0_ref, a1_ref, a2_ref, g1_ref, g2_ref,
                      r_ref, dec_ref, k_ref, v_ref, a_ref, g_ref, last_ref, carry_ref):
    i = pl.program_id(0)
    u = _rms(h_ref[...]) * gmix_ref[...]
    tm = u.shape[0]

    @pl.when(i == 0)
    def _():
        carry_ref[...] = shift_ref[...]
    prev = jnp.concatenate([carry_ref[...], u[:tm - batch]], axis=0)
    carry_ref[...] = u[tm - batch:]
    last_ref[...] = u[tm - batch:]
    xx = prev - u
    mix = lambda j: (u + xx * mu_ref[j:j + 1]).astype(BF16)
    r_ref[...] = _dot(mix(0), wr_ref[...])
    wl = w0_ref[...] + _dot(jnp.tanh(_dot(mix(1), w1_ref[...])).astype(BF16), w2_ref[...])
    dec_ref[...] = jnp.exp(-jnp.exp(-_softplus(-wl) - 0.5))
    k_ref[...] = _dot(mix(2), wk_ref[...])
    v_ref[...] = _dot(mix(3), wv_ref[...])
    a_ref[...] = jax.nn.sigmoid(a0_ref[...] + _dot(_dot(mix(4), a1_ref[...]).astype(BF16), a2_ref[...]))
    g_ref[...] = _dot(jax.nn.sigmoid(_dot(mix(5), g1_ref[...])).astype(BF16), g2_ref[...])


def _rwkv_proj_call(h, shift, gmix, prm):
    m, d = h.shape
    batch = shift.shape[0]
    tm = min(m, 256)
    assert tm % batch == 0 and tm > batch and batch % 8 == 0
    row = pl.BlockSpec((tm, d), lambda i: (i, 0))
    weights = [prm[n] for n in ("mu", "w_r", "w_k", "w_v", "w0", "w1", "w2", "a0", "a1", "a2", "g1", "g2")]
    outs = pl.pallas_call(
        functools.partial(_rwkv_proj_kernel, batch),
        grid=(m // tm,),
        in_specs=[row, _resident(shift.shape), _resident(gmix.shape)] + [_resident(w.shape) for w in weights],
        out_specs=[row] * 6 + [pl.BlockSpec((batch, d), lambda i: (0, 0))],
        out_shape=[jax.ShapeDtypeStruct((m, d), F32)] * 6 + [jax.ShapeDtypeStruct((batch, d), F32)],
        scratch_shapes=[pltpu.VMEM((batch, d), F32)],
        compiler_params=_cparams(1),
        name="rwkv_proj",
    )(h, shift, gmix, *weights)
    return outs[:6], outs[6]


def _rwkv_scan_kernel(tc, r_ref, dec_ref, k_ref, v_ref, a_ref, kk_ref, ka_ref, rk_ref, lw_ref, lb_ref, s0_ref,
                      y_ref, sout_ref, state_ref, vec_ref, stage_ref, ystage_ref):
    c = pl.program_id(1)
    n = RWKV_HEAD_DIM

    @pl.when(c == 0)
    def _():
        state_ref[...] = s0_ref[...]
        ystage_ref[...] = jnp.zeros_like(ystage_ref)

    def stage_in(t, slot):
        for i, ref in enumerate((r_ref, dec_ref, k_ref, v_ref, a_ref)):
            xa, xb = _rows_to_lanes(ref[t], ref[t + 1])
            stage_ref[i, 2 * slot] = xa
            stage_ref[i, 2 * slot + 1] = xb

    def step(t):
        r = stage_ref[0, t]
        k = stage_ref[2, t]
        v = stage_ref[3, t]
        a = stage_ref[4, t]
        kk = k * kk_ref[...]
        kk = kk / jnp.maximum(jnp.sqrt(jnp.sum(kk * kk, axis=0, keepdims=True)), 1e-12)
        k2 = k * (1.0 + (a - 1.0) * ka_ref[...])
        vec_ref[0] = -kk
        vec_ref[1] = kk * a
        vec_ref[2] = k2
        sa = jnp.zeros((n, LANES), F32)
        for j in range(n):
            sa = sa + state_ref[j] * vec_ref[0, j:j + 1, :]
        y = jnp.zeros((n, LANES), F32)
        for j in range(n):
            s_new = (state_ref[j] * stage_ref[1, t, j:j + 1, :] + sa * vec_ref[1, j:j + 1, :]
                     + v * vec_ref[2, j:j + 1, :])
            state_ref[j] = s_new
            y = y + s_new * stage_ref[0, t, j:j + 1, :]
        mean = jnp.mean(y, axis=0, keepdims=True)
        var = jnp.mean(jnp.square(y - mean), axis=0, keepdims=True)
        y = (y - mean) * lax.rsqrt(var + LNX_EPS) * lw_ref[...] + lb_ref[...]
        ystage_ref[t] = y + jnp.sum(r * k2 * rk_ref[...], axis=0, keepdims=True) * v

    def steps(slot):
        step(2 * slot)
        step(2 * slot + 1)

    def stage_out(slot, t):
        _lanes_to_rows(ystage_ref[2 * slot], ystage_ref[2 * slot + 1], y_ref, t)

    _pipelined_pairs(tc, stage_in, steps, stage_out)

    @pl.when(c == pl.num_programs(1) - 1)
    def _():
        sout_ref[...] = state_ref[...]


def _pipelined_pairs(tc, stage_in, steps, stage_out):
    stage_in(0, 0)

    def body(i, _):
        t = 4 * i
        stage_in(t + 2, 1)
        steps(0)
        stage_out(1, jnp.maximum(t - 2, 0))
        stage_in(jnp.minimum(t + 4, tc - 2), 0)
        steps(1)
        stage_out(0, t)
        return 0

    lax.fori_loop(0, tc // 4, body, 0)
    stage_out(1, tc - 2)


def _rows_to_lanes(xa, xb):
    chunks = xa.shape[1] // LANES
    stack = jnp.concatenate([x[:, c * LANES:(c + 1) * LANES] for x in (xa, xb) for c in range(chunks)], axis=0)
    tr = stack.T
    half = LANES // 2
    top, bot = tr[:half], tr[half:]
    low = _lane_iota() < half
    return (jnp.where(low, top, pltpu.roll(bot, half, axis=1)),
            jnp.where(low, pltpu.roll(top, half, axis=1), bot))


def _lanes_to_rows(ya, yb, out_ref, t):
    half = LANES // 2
    low = _lane_iota() < half
    top = jnp.where(low, ya, pltpu.roll(yb, half, axis=1))
    bot = jnp.where(low, pltpu.roll(ya, half, axis=1), yb)
    tr = jnp.concatenate([top, bot], axis=0).T
    chunks = out_ref.shape[2] // LANES
    for tok in range(2):
        for c in range(chunks):
            r0 = (tok * chunks + c) * 8
            out_ref[t + tok, :, c * LANES:(c + 1) * LANES] = tr[r0:r0 + 8]


def _lane_params(x, heads, width):
    per_chunk = LANES // width if width < LANES else 1
    x = x.reshape(heads // per_chunk, per_chunk, width)
    x = jnp.transpose(x, (2, 1, 0))
    return jnp.repeat(x[..., None], 8, axis=-1).reshape(width, LANES)


def _rwkv_scan_call(r, dec, k, v, a, kk, ka, rk, lw, lb, s0):
    t, b, d = r.shape
    n = RWKV_HEAD_DIM
    assert d == 8 * LANES and b % 8 == 0 and t % 4 == 0
    tc = math.gcd(t, 32)
    seq = pl.BlockSpec((tc, 8, d), lambda g, c: (c, g, 0))
    par = pl.BlockSpec((n, LANES), lambda g, c: (0, 0))
    st = pl.BlockSpec((n, n, LANES), lambda g, c: (0, 0, g))
    return pl.pallas_call(
        functools.partial(_rwkv_scan_kernel, tc),
        grid=(b // 8, t // tc),
        in_specs=[seq] * 5 + [par] * 5 + [st],
        out_specs=[seq, st],
        out_shape=[jax.ShapeDtypeStruct((t, b, d), F32), jax.ShapeDtypeStruct((n, n, b // 8 * LANES), F32)],
        scratch_shapes=[pltpu.VMEM((n, n, LANES), F32), pltpu.VMEM((3, n, LANES), F32),
                        pltpu.VMEM((5, 4, n, LANES), F32), pltpu.VMEM((4, n, LANES), F32)],
        compiler_params=_cparams(2),
        name="rwkv_scan",
    )(r, dec, k, v, a, kk, ka, rk, lw, lb, s0)


def _rwkv_params(mu, w_r, w_k, w_v, w0, w1, w2, a0, a1, a2, g1, g2):
    gpad = -g1.shape[1] % LANES
    return dict(mu=mu, w_r=w_r.astype(BF16), w_k=w_k.astype(BF16), w_v=w_v.astype(BF16), w0=w0[None],
                w1=w1.astype(BF16), w2=w2.astype(BF16), a0=a0[None], a1=a1.astype(BF16), a2=a2.astype(BF16),
                g1=jnp.pad(g1, ((0, 0), (0, gpad))).astype(BF16), g2=jnp.pad(g2, ((0, gpad), (0, 0))).astype(BF16))


def _rwkv_layer(h, batch, shift, s0, gmix, prm, k_k, k_a, r_k, lnx_w, lnx_b):
    m, d = h.shape
    t = m // batch
    n = RWKV_HEAD_DIM
    nh = d // n
    g8 = batch // 8
    (r, dec, k, v, a, g), new_shift = _rwkv_proj_call(h, shift, gmix, prm)
    rows = lambda x: x.reshape(t, batch, d)
    par = lambda x: _lane_params(x.reshape(-1), nh, n)
    st = s0.reshape(g8, 8, nh // 2, 2, n, n).transpose(5, 4, 0, 3, 2, 1).reshape(n, n, g8 * LANES)
    y, s = _rwkv_scan_call(rows(r), rows(dec), rows(k), rows(v), rows(a), par(k_k), par(k_a), par(r_k),
                           par(lnx_w), par(lnx_b), st)
    s = s.reshape(n, n, g8, 2, nh // 2, 8).transpose(2, 5, 4, 3, 1, 0).reshape(batch, nh, n, n)
    return y.reshape(m, d), g, new_shift, s


def _hgrn_proj_kernel(h_ref, gmix_ref, win_ref, lb_ref, omlb_ref, q_ref, f_ref, k_ref, i_ref, g_ref):
    d = h_ref.shape[1]
    u = (_rms(h_ref[...]) * gmix_ref[...]).astype(BF16)
    z = _dot(u, win_ref[...])
    q, f, g = z[:, :d], z[:, d:2 * d], z[:, 3 * d:]
    q_ref[...] = q * jax.nn.sigmoid(q)
    f_ref[...] = lb_ref[...] + omlb_ref[...] * jax.nn.sigmoid(f)
    k_ref[...] = omlb_ref[...] * jax.nn.sigmoid(-f)
    i_ref[...] = z[:, 2 * d:3 * d]
    g_ref[...] = g * jax.nn.sigmoid(g)


def _hgrn_proj_call(h, gmix, win, lb, omlb):
    m, d = h.shape
    tm = min(m, 256)
    row = pl.BlockSpec((tm, d), lambda i: (i, 0))
    return pl.pallas_call(
        _hgrn_proj_kernel,
        grid=(m // tm,),
        in_specs=[row, _resident(gmix.shape), _resident(win.shape), _resident(lb.shape), _resident(omlb.shape)],
        out_specs=[row] * 5,
        out_shape=[jax.ShapeDtypeStruct((m, d), F32)] * 5,
        compiler_params=_cparams(1),
        name="hgrn_proj",
    )(h, gmix, win, lb, omlb)


def _rows_to_lanes_dup(xa, xb):
    chunks = xa.shape[1] // LANES
    stack = jnp.concatenate([x[:, c * LANES:(c + 1) * LANES] for x in (xa, xb) for c in range(chunks)], axis=0)
    tr = stack.T
    half = LANES // 2
    low = _lane_iota() < half
    other = pltpu.roll(tr, half, axis=1)
    return jnp.where(low, tr, other), jnp.where(low, other, tr)


def _hgrn_scan_kernel(tc, q_ref, f_ref, k_ref, v_ref, s0_ref, o_ref, sout_ref, state_ref, keys_ref, vals_ref,
                      outs_ref):
    c = pl.program_id(1)
    dk = state_ref.shape[0]

    @pl.when(c == 0)
    def _():
        state_ref[...] = s0_ref[...]
        outs_ref[...] = jnp.zeros_like(outs_ref)

    def stage_in(t, slot):
        for i, ref in enumerate((q_ref, f_ref, k_ref)):
            xa, xb = _rows_to_lanes_dup(ref[t], ref[t + 1])
            keys_ref[i, 2 * slot] = xa
            keys_ref[i, 2 * slot + 1] = xb
        va, vb = _rows_to_lanes(v_ref[t], v_ref[t + 1])
        vals_ref[2 * slot] = va
        vals_ref[2 * slot + 1] = vb

    def step(t):
        v = vals_ref[t]
        o = jnp.zeros(v.shape, F32)
        for d in range(dk):
            s_new = state_ref[d] * keys_ref[1, t, d:d + 1, :] + keys_ref[2, t, d:d + 1, :] * v
            state_ref[d] = s_new
            o = o + s_new * keys_ref[0, t, d:d + 1, :]
        outs_ref[t] = o

    def steps(slot):
        step(2 * slot)
        step(2 * slot + 1)

    def stage_out(slot, t):
        _lanes_to_rows(outs_ref[2 * slot], outs_ref[2 * slot + 1], o_ref, t)

    _pipelined_pairs(tc, stage_in, steps, stage_out)

    @pl.when(c == pl.num_programs(1) - 1)
    def _():
        sout_ref[...] = state_ref[...]


def _hgrn_scan_call(q, f, k, v, s0):
    t, b, d = q.shape
    dk = HGRN_HEAD_DIM
    dv = dk // 2
    assert d == 8 * LANES and b % 8 == 0 and t % 4 == 0
    tc = math.gcd(t, 32)
    seq = pl.BlockSpec((tc, 8, d), lambda g, c: (c, g, 0))
    st = pl.BlockSpec((dk, dv, LANES), lambda g, c: (0, 0, g))
    return pl.pallas_call(
        functools.partial(_hgrn_scan_kernel, tc),
        grid=(b // 8, t // tc),
        in_specs=[seq] * 4 + [st],
        out_specs=[seq, st],
        out_shape=[jax.ShapeDtypeStruct((t, b, d), F32), jax.ShapeDtypeStruct((dk, dv, b // 8 * LANES), F32)],
        scratch_shapes=[pltpu.VMEM((dk, dv, LANES), F32), pltpu.VMEM((3, 4, dk, LANES), F32),
                        pltpu.VMEM((4, dv, LANES), F32), pltpu.VMEM((4, dv, LANES), F32)],
        compiler_params=_cparams(2),
        name="hgrn_scan",
    )(q, f, k, v, s0)


def _hgrn_layer(h, batch, s0, gmix, win, lb, omlb):
    m, d = h.shape
    t = m // batch
    n = HGRN_HEAD_DIM
    nh = d // n
    g8 = batch // 8
    q, f, k, i, g = _hgrn_proj_call(h, gmix, win, lb, omlb)
    rows = lambda x: x.reshape(t, batch, d)
    st = s0.reshape(g8, 8, nh, n, 2, n // 2).transpose(3, 5, 0, 4, 2, 1).reshape(n, n // 2, g8 * LANES)
    o, s = _hgrn_scan_call(rows(q), rows(f), rows(k), rows(i), st)
    s = s.reshape(n, n // 2, g8, 2, nh, 8).transpose(2, 5, 4, 0, 3, 1).reshape(batch, nh, n, n)
    return o.reshape(m, d), g, s


def kernel(x_prompt, x_sample, cache_cmp, cache_slc, cache_win, state_rwkv_shift, state_rwkv_wkv, state_hgrn,
           page_table, p_prompt, p_sample, norm_mix, norm_ffn, w_up, w_down, w_ple, w_ple_gate,
           nsa_w_in, nsa_g_q, nsa_g_ks, nsa_g_kw, nsa_g_kc, nsa_cmp_pe, nsa_cmp_w1, nsa_cmp_w2, nsa_w_out,
           rwkv_mu, rwkv_w_r, rwkv_w_k, rwkv_w_v, rwkv_w_o, rwkv_w0, rwkv_w1, rwkv_w2, rwkv_a0, rwkv_a1,
           rwkv_a2, rwkv_g1, rwkv_g2, rwkv_k_k, rwkv_k_a, rwkv_r_k, rwkv_lnx_w, rwkv_lnx_b,
           hgrn_w_in, hgrn_gn, hgrn_w_o, hgrn_lower_bounds):
    depth = norm_mix.shape[0]
    bp, tp, d = x_prompt.shape
    bs, ts, _ = x_sample.shape
    lb_soft = jax.nn.softmax(hgrn_lower_bounds.astype(F32), axis=0)
    lower_bound = jnp.cumsum(lb_soft, axis=0) - lb_soft[0]
    rows_minor = lambda x: jnp.transpose(x, (0, 1, 3, 4, 5, 2))
    cmp_pool_t, slc_pool_t, win_buf_t = rows_minor(cache_cmp), rows_minor(cache_slc), rows_minor(cache_win)
    hp, hs = x_prompt.reshape(bp * tp, d), x_sample.reshape(bs * ts, d)
    swap = lambda x, a, b: x.reshape(a, b, x.shape[-1]).transpose(1, 0, 2).reshape(a * b, x.shape[-1])
    time_major = False
    outs = [[] for _ in range(12)]
    for i in range(depth):
        kind, n = i % N_MIXERS, i // N_MIXERS
        gmix = norm_mix[i][None]
        ffn = (p_prompt[i].reshape(bp * tp, -1), p_sample[i].reshape(bs * ts, -1))
        if (kind != 0) != time_major:
            hp = swap(hp, tp, bp) if time_major else swap(hp, bp, tp)
            hs = swap(hs, ts, bs) if time_major else swap(hs, bs, ts)
            time_major = not time_major
        if time_major:
            ffn = (swap(ffn[0], bp, tp), swap(ffn[1], bs, ts))
        hp3, hs3 = hp.reshape(bp, tp, d), hs.reshape(bs, ts, d)
        tail = (norm_ffn[i][None], w_up[i].astype(BF16), w_down[i].astype(BF16), w_ple[i].astype(BF16),
                w_ple_gate[i].astype(BF16))
        if kind == 0:
            prm = _nsa_params(nsa_w_in[n], nsa_g_q[n], nsa_g_ks[n], nsa_g_kw[n], nsa_g_kc[n], nsa_cmp_pe[n],
                              nsa_cmp_w1[n], nsa_cmp_w2[n])
            op, rc, rs, rw = _nsa_prompt_layer(hp3, gmix, prm)
            os_, nc_rows, ns_rows, nw_buf = _nsa_sample_layer(hs3, cmp_pool_t, slc_pool_t, cache_win[n], win_buf_t,
                                                              n, page_table, gmix, prm)
            for lst, v in zip(outs[:6], (rc, nc_rows, rs, ns_rows, rw, nw_buf)):
                lst.append(v)
            wo = nsa_w_out[n].astype(BF16)
            hp = _ffn_call("nsa", hp, op, None, None, ffn[0], wo, *tail)
            hs = _ffn_call("nsa", hs, os_, None, None, ffn[1], wo, *tail)
        elif kind == 1:
            prm = _rwkv_params(rwkv_mu[n], rwkv_w_r[n], rwkv_w_k[n], rwkv_w_v[n], rwkv_w0[n], rwkv_w1[n],
                               rwkv_w2[n], rwkv_a0[n], rwkv_a1[n], rwkv_a2[n], rwkv_g1[n], rwkv_g2[n])
            vecs = (rwkv_k_k[n], rwkv_k_a[n], rwkv_r_k[n], rwkv_lnx_w[n], rwkv_lnx_b[n])
            nh = d // RWKV_HEAD_DIM
            zero_state = jnp.zeros((bp, nh, RWKV_HEAD_DIM, RWKV_HEAD_DIM), F32)
            yp, gp, shp, sp = _rwkv_layer(hp, bp, jnp.zeros((bp, d), F32), zero_state, gmix, prm, *vecs)
            ys, gs, shs, ss = _rwkv_layer(hs, bs, state_rwkv_shift[n], state_rwkv_wkv[n].astype(F32), gmix, prm,
                                          *vecs)
            for lst, v in zip(outs[6:10], (shp, shs, sp, ss)):
                lst.append(v)
            wo = rwkv_w_o[n].astype(BF16)
            hp = _ffn_call("rwkv", hp, yp, gp, None, ffn[0], wo, *tail)
            hs = _ffn_call("rwkv", hs, ys, gs, None, ffn[1], wo, *tail)
        else:
            lb = lower_bound[i][None]
            win = hgrn_w_in[n].astype(BF16)
            nh = d // HGRN_HEAD_DIM
            zero_state = jnp.zeros((bp, nh, HGRN_HEAD_DIM, HGRN_HEAD_DIM), F32)
            op, gp, sp = _hgrn_layer(hp, bp, zero_state, gmix, win, lb, 1.0 - lb)
            os_, gs, ss = _hgrn_layer(hs, bs, state_hgrn[n].astype(F32), gmix, win, lb, 1.0 - lb)
            outs[10].append(sp)
            outs[11].append(ss)
            wo, gn = hgrn_w_o[n].astype(BF16), hgrn_gn[n][None]
            hp = _ffn_call("hgrn", hp, op, gp, gn, ffn[0], wo, *tail)
            hs = _ffn_call("hgrn", hs, os_, gs, gn, ffn[1], wo, *tail)
    if time_major:
        hp, hs = swap(hp, tp, bp), swap(hs, ts, bs)
    return (hp.reshape(bp, tp, d), hs.reshape(bs, ts, d)) + tuple(jnp.stack(o) for o in outs)
```

```python
import functools
import math

import jax
import jax.numpy as jnp
from jax import lax
from jax.experimental import pallas as pl
from jax.experimental.pallas import tpu as pltpu

F32 = jnp.float32
BF16 = jnp.bfloat16

NORM_EPS = 1e-6
ROPE_THETA = 10000.0
NEG_INF = -1e30
M_INIT = -1e29
N_MIXERS = 3

NSA_HEADS = 16
NSA_KV_HEADS = 4
NSA_HEAD_DIM = 64
NSA_GROUP = NSA_HEADS // NSA_KV_HEADS
NSA_KV_WIDTH = NSA_KV_HEADS * NSA_HEAD_DIM
NSA_SCALE = NSA_HEAD_DIM ** -0.5
LOG2E = math.log2(math.e)
CMP_BLOCK = 32
SEL_BLOCK = 64
N_SELECT = 16
WINDOW = 512
FORCE_SCORE = 1e4

RWKV_HEAD_DIM = 64
LNX_EPS = 64e-5
HGRN_HEAD_DIM = 128

LANES = 128
VMEM_LIMIT = 56 * 1024 * 1024


def _cparams(n_axes):
    return pltpu.CompilerParams(dimension_semantics=("arbitrary",) * n_axes,
                                vmem_limit_bytes=VMEM_LIMIT)


def _resident(shape):
    zeros = (0,) * len(shape)
    return pl.BlockSpec(shape, lambda *_: zeros, pipeline_mode=pl.Buffered(1))


def _rms(x):
    return x * lax.rsqrt(jnp.mean(x * x, axis=-1, keepdims=True) + NORM_EPS)


def _dot(a, b):
    return jnp.dot(a, b, preferred_element_type=F32)


def _dot_nt(a, b):
    return lax.dot_general(a, b, (((1,), (1,)), ((), ())), preferred_element_type=F32)


def _per_chunk(fn, x, *rest):
    n = x.shape[-1] // LANES
    outs = [fn(*(a[:, c * LANES:(c + 1) * LANES] for a in (x,) + rest)) for c in range(n)]
    return outs[0] if n == 1 else jnp.concatenate(outs, axis=-1)


def _lane_iota():
    return lax.broadcasted_iota(jnp.int32, (1, LANES), 1)


def _group_ones(group):
    r = lax.broadcasted_iota(jnp.int32, (LANES, LANES), 0) // group
    c = lax.broadcasted_iota(jnp.int32, (LANES, LANES), 1) // group
    return jnp.where(r == c, 1.0, 0.0).astype(BF16)


def _head_rms(x, head_dim, ones):
    def one(c):
        sq = c * c
        hi = sq.astype(BF16)
        lo = (sq - hi.astype(F32)).astype(BF16)
        ss = _dot(hi, ones) + _dot(lo, ones)
        return c * lax.rsqrt(ss * (1.0 / head_dim) + NORM_EPS)
    return _per_chunk(one, x)


def _rope(x, cos, sin_signed):
    half = NSA_HEAD_DIM // 2
    lane = _lane_iota()

    def one(c):
        up = pltpu.roll(c, LANES - half, axis=1)
        dn = pltpu.roll(c, half, axis=1)
        rot = jnp.where((lane & half) == 0, up, dn)
        return c * cos + rot * sin_signed
    return _per_chunk(one, x)


def _ffn_kernel(mode, ff_chunk, *refs):
    if mode == "nsa":
        h_ref, o_ref, p_ref, wo_ref, gffn_ref, wup_ref, wdown_ref, wple_ref, wgate_ref, out_ref = refs
        o = o_ref[...]
    elif mode == "rwkv":
        h_ref, o_ref, aux_ref, p_ref, wo_ref, gffn_ref, wup_ref, wdown_ref, wple_ref, wgate_ref, out_ref = refs
        o = o_ref[...] * aux_ref[...]
    else:
        (h_ref, o_ref, aux_ref, gn_ref, p_ref, wo_ref, gffn_ref, wup_ref, wdown_ref, wple_ref, wgate_ref,
         out_ref) = refs
        gn = gn_ref[...]
        o = _per_chunk(lambda c: _rms(c) * gn, o_ref[...]) * aux_ref[...]
    h1 = h_ref[...] + _dot(o.astype(BF16), wo_ref[...])
    u = (_rms(h1) * gffn_ref[...]).astype(BF16)
    d_ff = wup_ref.shape[1]
    acc = jnp.zeros_like(h1)
    for j in range(d_ff // ff_chunk):
        a = _dot(u, wup_ref[:, j * ff_chunk:(j + 1) * ff_chunk])
        a = jnp.square(jnp.maximum(a, 0.0)).astype(BF16)
        acc = acc + _dot(a, wdown_ref[j * ff_chunk:(j + 1) * ff_chunk, :])
    h2 = h1 + acc
    gate = jax.nn.sigmoid(_dot(_rms(h2).astype(BF16), wgate_ref[...]))
    out_ref[...] = h2 + _dot(p_ref[...].astype(BF16), wple_ref[...]) * gate


def _ffn_call(mode, h, o, aux, gn, p, wo, gffn, wup, wdown, wple, wgate):
    m, d = h.shape
    tm = min(m, 512)
    row = lambda w: pl.BlockSpec((tm, w), lambda i: (i, 0))
    args, specs = [h, o], [row(d), row(d)]
    if mode != "nsa":
        args.append(aux)
        specs.append(row(d))
    if mode == "hgrn":
        args.append(gn)
        specs.append(_resident(gn.shape))
    args += [p, wo, gffn, wup, wdown, wple, wgate]
    specs += [row(p.shape[1])] + [_resident(a.shape) for a in (wo, gffn, wup, wdown, wple, wgate)]
    return pl.pallas_call(
        functools.partial(_ffn_kernel, mode, 1024),
        grid=(m // tm,),
        in_specs=specs,
        out_specs=row(d),
        out_shape=jax.ShapeDtypeStruct((m, d), F32),
        compiler_params=_cparams(1),
        name="ffn_" + mode,
    )(*args)


def _nsa_proj_kernel(h_ref, gmix_ref, win_ref, gq_ref, gks_ref, gkw_ref, cos_ref, sin_ref,
                     qn_ref, qr_ref, cmp_ref, slc_ref, wrow_ref, gate_ref):
    kvw = NSA_KV_WIDTH
    nq = NSA_HEADS * NSA_HEAD_DIM
    u = (_rms(h_ref[...]) * gmix_ref[...]).astype(BF16)
    z = _dot(u, win_ref[...])
    cos = cos_ref[...]
    sin = sin_ref[...]
    ones = _group_ones(NSA_HEAD_DIM)
    qn = _head_rms(z[:, :nq], NSA_HEAD_DIM, ones) * gq_ref[...]
    qn_ref[...] = qn
    qr_ref[...] = _rope(qn, cos, sin)
    cmp_ref[...] = z[:, nq:nq + 2 * kvw]
    o = nq + 2 * kvw
    slc_ref[:, :kvw] = _rope(_head_rms(z[:, o:o + kvw], NSA_HEAD_DIM, ones) * gks_ref[...], cos, sin)
    slc_ref[:, kvw:] = z[:, o + kvw:o + 2 * kvw]
    o += 2 * kvw
    wrow_ref[:, :kvw] = _rope(_head_rms(z[:, o:o + kvw], NSA_HEAD_DIM, ones) * gkw_ref[...], cos, sin)
    wrow_ref[:, kvw:] = z[:, o + kvw:o + 2 * kvw]
    o += 2 * kvw
    gate_ref[...] = jax.nn.sigmoid(z[:, o:])


def _nsa_proj_call(h, gmix, win, gq, gks, gkw, cos, sin):
    m, d = h.shape
    tm = min(m, 256)
    tab_tiles = cos.shape[0] // tm
    row = lambda w: pl.BlockSpec((tm, w), lambda i: (i, 0))
    tab = pl.BlockSpec((tm, LANES), lambda i: (i % tab_tiles, 0))
    nq = NSA_HEADS * NSA_HEAD_DIM
    widths = [nq, nq] + [2 * NSA_KV_WIDTH] * 3 + [2 * LANES]
    return pl.pallas_call(
        _nsa_proj_kernel,
        grid=(m // tm,),
        in_specs=[row(d), _resident(gmix.shape), _resident(win.shape), _resident(gq.shape),
                  _resident(gks.shape), _resident(gkw.shape), tab, tab],
        out_specs=[row(w) for w in widths],
        out_shape=[jax.ShapeDtypeStruct((m, w), F32) for w in widths],
        compiler_params=_cparams(1),
        name="nsa_proj",
    )(h, gmix, win, gq, gks, gkw, cos, sin)


def _compress_slot(load_pair, m, e, pe_ref, w1_ref, w2_ref, gkc_ref, flat_ref):
    lane = _lane_iota()
    low = lane < NSA_HEAD_DIM
    for s in range(CMP_BLOCK // 2):
        pe_row = pe_ref[e, s:s + 1, :]
        parts = []
        for kp in range(NSA_KV_HEADS // 2):
            a = load_pair(kp, 2 * s)
            b = load_pair(kp, 2 * s + 1)
            parts.append(jnp.where(low, a, pltpu.roll(b, NSA_HEAD_DIM, axis=1)))
            parts.append(jnp.where(low, pltpu.roll(a, NSA_HEAD_DIM, axis=1), b))
        flat_ref[:, s * LANES:(s + 1) * LANES] = (jnp.concatenate(parts, axis=0) + pe_row).astype(BF16)
    out = _dot(jax.nn.gelu(_dot(flat_ref[...], w1_ref[e])).astype(BF16), w2_ref[e])
    if e == 0:
        out = _rms(out) * gkc_ref[...]
    return jnp.concatenate([out[k * m:(k + 1) * m] for k in range(NSA_KV_HEADS)], axis=-1)


def _compress_rows_kernel(*refs):
    cols = refs[:NSA_KV_HEADS]
    pe_ref, w1_ref, w2_ref, gkc_ref, out_ref, flat_ref = refs[NSA_KV_HEADS:]
    m = cols[0].shape[1] // CMP_BLOCK
    pad = out_ref.shape[2] - m
    for e in range(2):
        load = lambda kp, r: cols[2 * e + kp][0, pl.ds(r, m, stride=CMP_BLOCK), :]
        res = _compress_slot(load, m, e, pe_ref, w1_ref, w2_ref, gkc_ref, flat_ref)
        out_ref[0, e] = jnp.concatenate([res, jnp.zeros((pad, res.shape[1]), F32)], axis=0)


def _compress_rows_call(cmp, pe, w1, w2, gkc):
    b, t, w = cmp.shape
    c = -(-(t // CMP_BLOCK) // LANES) * LANES
    return pl.pallas_call(
        _compress_rows_kernel,
        grid=(b,),
        in_specs=[pl.BlockSpec((1, t, LANES), functools.partial(lambda j, i: (i, 0, j), j)) for j in range(w // LANES)]
        + [_resident(a.shape) for a in (pe, w1, w2, gkc)],
        out_specs=pl.BlockSpec((1, 2, c, NSA_KV_WIDTH), lambda i: (i, 0, 0, 0)),
        out_shape=jax.ShapeDtypeStruct((b, 2, c, NSA_KV_WIDTH), F32),
        scratch_shapes=[pltpu.VMEM((NSA_KV_HEADS * (t // CMP_BLOCK), w1.shape[1]), BF16)],
        compiler_params=_cparams(1),
        name="nsa_compress_rows",
    )(*([cmp] * (w // LANES)), pe, w1, w2, gkc)


PAGES_PER_STEP = 16


def _page_specs(n, layer, tail):
    zeros = (0,) * len(tail)
    return [pl.BlockSpec((1, 1) + tail,
                         functools.partial(lambda i, b, c, pt: (layer, pt[b, c * n + i]) + zeros, i))
            for i in range(n)]


def _compress_pages_kernel(pt_ref, *refs):
    n = len(refs) - 7
    pages = refs[:n]
    pe_ref, w1_ref, w2_ref, gkc_ref, out_ref, rows_ref, flat_ref = refs[n:]
    page = pages[0].shape[-1]
    m = n * page // CMP_BLOCK
    for e in range(2):
        for i, pg in enumerate(pages):
            for kp in range(NSA_KV_HEADS // 2):
                tile = pg[0, 0, e, 2 * kp:2 * kp + 2].reshape(LANES, page)
                rows_ref[kp, i * page:(i + 1) * page, :] = tile.T
        load = lambda kp, r: rows_ref[kp, pl.ds(r, m, stride=CMP_BLOCK), :]
        out_ref[0, e] = _compress_slot(load, m, e, pe_ref, w1_ref, w2_ref, gkc_ref, flat_ref)


def _compress_pages_call(pool_t, layer, page_table, pe, w1, w2, gkc):
    b, n_pages = page_table.shape
    page = pool_t.shape[-1]
    n = PAGES_PER_STEP
    m = n * page // CMP_BLOCK
    return pl.pallas_call(
        _compress_pages_kernel,
        grid_spec=pltpu.PrefetchScalarGridSpec(
            num_scalar_prefetch=1,
            grid=(b, n_pages // n),
            in_specs=_page_specs(n, layer, pool_t.shape[2:])
            + [pl.BlockSpec(a.shape, functools.partial(lambda nd, i, c, pt: (0,) * nd, a.ndim),
                            pipeline_mode=pl.Buffered(1)) for a in (pe, w1, w2, gkc)],
            out_specs=pl.BlockSpec((1, 2, m, NSA_KV_WIDTH), lambda i, c, pt: (i, 0, c, 0)),
            scratch_shapes=[pltpu.VMEM((NSA_KV_HEADS // 2, n * page, LANES), F32),
                            pltpu.VMEM((NSA_KV_HEADS * m, w1.shape[1]), BF16)],
        ),
        out_shape=jax.ShapeDtypeStruct((b, 2, n_pages * page // CMP_BLOCK, NSA_KV_WIDTH), F32),
        compiler_params=_cparams(2),
        name="nsa_compress_pages",
    )(page_table, *([pool_t] * n), pe, w1, w2, gkc)


def _select_blocks_t(imp, tpos):
    n_blocks = imp.shape[0]
    blk = lax.broadcasted_iota(jnp.int32, (n_blocks, 1), 0)
    cur = tpos // SEL_BLOCK
    forced = (blk == 0) | (blk == cur) | (blk == cur - 1)
    score = jnp.where(blk <= cur, jnp.where(forced, FORCE_SCORE, imp), -1.0)
    rank = jnp.zeros(score.shape, F32)
    for j in range(n_blocks):
        row = score[j:j + 1, :]
        below = jnp.where(j < blk, 1.0, 0.0)
        rank = rank + jnp.where(row > score, 1.0, jnp.where(row == score, below, 0.0))
    return jnp.where(rank < float(min(N_SELECT, n_blocks)), 1.0, 0.0)


def _attn_prompt_kernel(tq, tk, qn_ref, qr_ref, kvc_ref, sk_ref, sv_ref, wk_ref, wv_ref, gate_ref, o_ref,
                        kb_ref, vt_ref, sel_ref, imp_ref):
    qi = pl.program_id(2)
    hd = NSA_HEAD_DIM
    t_len = sk_ref.shape[1]
    c_blocks = kvc_ref.shape[2]
    n_sel = t_len // SEL_BLOCK
    ratio = SEL_BLOCK // CMP_BLOCK
    rows_per_tile = tk // SEL_BLOCK

    @pl.when(qi == 0)
    def _():
        for br, (k_ref, v_ref) in enumerate(((sk_ref, sv_ref), (wk_ref, wv_ref))):
            for gl in range(2):
                kb_ref[br, gl] = k_ref[0, :, gl * hd:(gl + 1) * hd].astype(BF16)
            for kt in range(t_len // tk):
                vt_ref[br, kt] = v_ref[0, kt * tk:(kt + 1) * tk, :].T.astype(BF16)

    t0 = qi * tq
    tpos = t0 + lax.broadcasted_iota(jnp.int32, (1, tq), 1)
    gate_t = gate_ref[0].T
    qn_t = [qn_ref[0, :, c * LANES:(c + 1) * LANES].T for c in range(2 * NSA_GROUP * hd // LANES)]
    qr_t = [qr_ref[0, :, c * LANES:(c + 1) * LANES].T for c in range(2 * NSA_GROUP * hd // LANES)]
    cid = lax.broadcasted_iota(jnp.int32, (c_blocks, 1), 0)
    ok = jnp.where((cid + 1) * CMP_BLOCK - 1 <= tpos, 1.0, 0.0)
    ok4 = jnp.concatenate([ok] * NSA_GROUP, axis=1)
    qr4s, o_cmps = [], []
    for gl in range(2):
        per_head = lambda parts: [parts[2 * gl + hh // 2][(hh % 2) * hd:(hh % 2 + 1) * hd] for hh in range(NSA_GROUP)]
        qn4 = (jnp.concatenate(per_head(qn_t), axis=1) * NSA_SCALE).astype(BF16)
        qr4s.append((jnp.concatenate(per_head(qr_t), axis=1) * (NSA_SCALE * LOG2E)).astype(BF16))
        kc = kvc_ref[0, 0][:, gl * hd:(gl + 1) * hd]
        vc_t = kvc_ref[0, 1].T[gl * hd:(gl + 1) * hd]
        s = jnp.where(ok4 > 0.5, _dot(kc.astype(BF16), qn4), NEG_INF)
        e = jnp.exp(s - jnp.max(s, axis=0, keepdims=True))
        p = e / jnp.sum(e, axis=0, keepdims=True) * ok4
        o_cmps.append(_dot(vc_t.astype(BF16), p.astype(BF16)))
        imp = p[:, :tq]
        for hh in range(1, NSA_GROUP):
            imp = imp + p[:, hh * tq:(hh + 1) * tq]
        imp_ref[...] = imp
        imp_sel = imp_ref[pl.ds(0, n_sel, stride=ratio), :]
        for i in range(1, ratio):
            imp_sel = imp_sel + imp_ref[pl.ds(i, n_sel, stride=ratio), :]
        sel_ref[gl] = _select_blocks_t(imp_sel, tpos)

    def flash(br, lo, hi, valid_fn):
        def body(kt, carry):
            k0 = pl.multiple_of(kt * tk, tk)
            out = []
            for gl in range(2):
                m, l, acc = carry[gl]
                kb = kb_ref[br, gl, pl.ds(k0, tk), :]
                vt = vt_ref[br, kt, gl * hd:(gl + 1) * hd, :]
                bias = jnp.where(valid_fn(gl, kt, k0), 0.0, NEG_INF)
                s = _dot(kb, qr4s[gl]) + jnp.concatenate([bias] * NSA_GROUP, axis=1)
                m_new = jnp.maximum(m, jnp.max(s, axis=0, keepdims=True))
                alpha = jnp.exp2(m - m_new)
                p = jnp.exp2(s - m_new)
                out.append((m_new, alpha * l + jnp.sum(p, axis=0, keepdims=True),
                            alpha * acc + _dot(vt, p.astype(BF16))))
            return tuple(out)
        cols = NSA_GROUP * tq
        init = tuple((jnp.full((1, cols), M_INIT, F32), jnp.zeros((1, cols), F32), jnp.zeros((hd, cols), F32))
                     for _ in range(2))
        res = lax.fori_loop(lo, hi, body, init)
        return [acc / l for _, l, acc in res]

    def slc_valid(gl, kt, k0):
        picked = sel_ref[gl, pl.ds(pl.multiple_of(kt * rows_per_tile, rows_per_tile), rows_per_tile), :]
        picked = jnp.concatenate([jnp.broadcast_to(picked[i:i + 1], (SEL_BLOCK, tq))
                                  for i in range(rows_per_tile)], axis=0)
        kpos = k0 + lax.broadcasted_iota(jnp.int32, (tk, 1), 0)
        return (picked > 0.5) & (kpos <= tpos)

    def win_valid(gl, kt, k0):
        dist = tpos - (k0 + lax.broadcasted_iota(jnp.int32, (tk, 1), 0))
        return (dist >= 0) & (dist <= WINDOW)

    hi = (t0 + tq - 1) // tk + 1
    slc = flash(0, 0, hi, slc_valid)
    win = flash(1, jnp.maximum(t0 - WINDOW, 0) // tk, hi, win_valid)
    heads = []
    for gl in range(2):
        for hh in range(NSA_GROUP):
            c = (gl * NSA_GROUP + hh) * 3
            cols = slice(hh * tq, (hh + 1) * tq)
            heads.append(o_cmps[gl][:, cols] * gate_t[c:c + 1] + slc[gl][:, cols] * gate_t[c + 1:c + 2]
                         + win[gl][:, cols] * gate_t[c + 2:c + 3])
    for c in range(len(heads) // 2):
        o_ref[0, :, c * LANES:(c + 1) * LANES] = jnp.concatenate(heads[2 * c:2 * c + 2], axis=0).T


def _attn_prompt_call(qn, qr, kvc, slc, win, gate, tq=128, tk=512):
    b, t, nq = qn.shape
    pair_w = 2 * NSA_GROUP * NSA_HEAD_DIM
    c = kvc.shape[2]
    assert tk % (8 * SEL_BLOCK) == 0 and t % tk == 0 and c % LANES == 0 and tq == LANES
    qspec = pl.BlockSpec((1, tq, pair_w), lambda i, p, j: (i, j, p))
    kspec = pl.BlockSpec((1, t, LANES), lambda i, p, j: (i, 0, p))
    vspec = pl.BlockSpec((1, t, LANES), lambda i, p, j: (i, 0, 2 + p))
    return pl.pallas_call(
        functools.partial(_attn_prompt_kernel, tq, tk),
        grid=(b, 2, t // tq),
        in_specs=[qspec, qspec,
                  pl.BlockSpec((1, 2, c, LANES), lambda i, p, j: (i, 0, 0, p)),
                  kspec, vspec, kspec, vspec,
                  pl.BlockSpec((1, tq, LANES), lambda i, p, j: (i, j, p))],
        out_specs=qspec,
        out_shape=jax.ShapeDtypeStruct((b, t, nq), F32),
        scratch_shapes=[pltpu.VMEM((2, 2, t, NSA_HEAD_DIM), BF16),
                        pltpu.VMEM((2, t // tk, LANES, tk), BF16),
                        pltpu.VMEM((2, t // SEL_BLOCK, tq), F32),
                        pltpu.VMEM((c, tq), F32)],
        compiler_params=_cparams(3),
        name="nsa_attn_prompt",
    )(qn, qr, kvc, slc, slc, win, win, gate)


def _cmp_branch(q4, kc, vc, tpos, n_tok):
    c = kc.shape[0]
    s = _dot_nt((q4 * NSA_SCALE).astype(BF16), kc.astype(BF16))
    cidx = lax.broadcasted_iota(jnp.int32, (1, c), 1)
    ok = ((cidx + 1) * CMP_BLOCK - 1 <= tpos).astype(F32)
    ok4 = jnp.concatenate([ok] * NSA_GROUP, axis=0)
    s = jnp.where(ok4 > 0.5, s, NEG_INF)
    e = jnp.exp(s - jnp.max(s, axis=-1, keepdims=True))
    p = e / jnp.sum(e, axis=-1, keepdims=True) * ok4
    o = _dot(p.astype(BF16), vc.astype(BF16))
    imp = p[:n_tok]
    for hh in range(1, NSA_GROUP):
        imp = imp + p[hh * n_tok:(hh + 1) * n_tok]
    return o, imp


def _select_blocks(imp, tpos, n_blocks):
    c = imp.shape[1]
    lane = lax.broadcasted_iota(jnp.int32, (1, c), 1)

    def pair(x):
        lane1 = _lane_iota()
        return x + jnp.where((lane1 & 1) == 0, pltpu.roll(x, LANES - 1, axis=1), pltpu.roll(x, 1, axis=1))
    imp2 = _per_chunk(pair, imp)
    blk = lane // 2
    cur = tpos // SEL_BLOCK
    forced = (blk == 0) | (blk == cur) | (blk == cur - 1)
    score = jnp.where(blk <= cur, jnp.where(forced, FORCE_SCORE, imp2), -1.0)
    rank = jnp.zeros(score.shape, F32)
    for j in range(n_blocks):
        col = score[:, 2 * j:2 * j + 1]
        ahead = (col > score) | ((col == score) & (j < blk))
        rank = rank + ahead.astype(F32)
    return (rank < float(min(N_SELECT, n_blocks))).astype(F32)


def _flash_step(q4, k, v, valid, carry, transposed):
    m, l, acc = carry
    valid4 = jnp.concatenate([valid] * NSA_GROUP, axis=0)
    s = _dot(q4, k.astype(BF16)) if transposed else _dot_nt(q4, k.astype(BF16))
    s = jnp.where(valid4 > 0.5, s, NEG_INF)
    m_new = jnp.maximum(m, jnp.max(s, axis=-1, keepdims=True))
    alpha = jnp.exp(m - m_new)
    p = jnp.exp(s - m_new)
    l = alpha * l + jnp.sum(p, axis=-1, keepdims=True)
    pv = _dot_nt(p.astype(BF16), v.astype(BF16)) if transposed else _dot(p.astype(BF16), v.astype(BF16))
    return m_new, l, alpha * acc + pv


def _attn_sample_kernel(n_step, t_real, past_len, pt_ref, qn_ref, qr_ref, kvc_ref, gate_ref, snew_ref, wbuf_ref,
                        wnew_ref, expand_ref, *rest):
    pages = rest[:n_step]
    o_ref, sel_ref, ocmp_ref, m_ref, l_ref, acc_ref, qbd_ref = rest[n_step:]
    c = pl.program_id(1)
    hd = NSA_HEAD_DIM
    kvw = NSA_KV_WIDTH
    tq = qn_ref.shape[1]
    n_cmp = past_len // CMP_BLOCK
    tpos = past_len + lax.broadcasted_iota(jnp.int32, (tq, 1), 0) % t_real
    stack = lambda ref, g: jnp.concatenate(
        [ref[0, :, (g * NSA_GROUP + hh) * hd:(g * NSA_GROUP + hh + 1) * hd] for hh in range(NSA_GROUP)], axis=0)
    rows = NSA_GROUP * tq

    @pl.when(c == 0)
    def _():
        n_blocks = -(-(past_len + t_real) // SEL_BLOCK)
        for g in range(NSA_KV_HEADS):
            ksl = slice(g * hd, (g + 1) * hd)
            o_cmp, imp = _cmp_branch(stack(qn_ref, g), kvc_ref[0, 0][:, ksl], kvc_ref[0, 1][:, ksl], tpos, tq)
            imp = jnp.concatenate([imp, jnp.zeros((tq, LANES), F32)], axis=-1)
            sel_ref[g] = _select_blocks(imp, tpos, n_blocks)
            ocmp_ref[g] = o_cmp
            q4 = (stack(qr_ref, g) * NSA_SCALE).astype(BF16)
            zero = jnp.zeros((rows, hd), BF16)
            qbd_ref[g * rows:(g + 1) * rows, :] = jnp.concatenate(
                [q4 if gg == g else zero for gg in range(NSA_KV_HEADS)], axis=1)
        m_ref[...] = jnp.full(m_ref.shape, M_INIT, F32)
        l_ref[...] = jnp.zeros(l_ref.shape, F32)
        acc_ref[...] = jnp.zeros(acc_ref.shape, F32)

    tk = n_step * pages[0].shape[-1]
    k0 = c * tk
    kpos = k0 + lax.broadcasted_iota(jnp.int32, (1, tk), 1)
    kt_all = jnp.concatenate([pg[0, 0, 0].reshape(kvw, -1) for pg in pages], axis=1).astype(BF16)
    vt_all = jnp.concatenate([pg[0, 0, 1].reshape(kvw, -1) for pg in pages], axis=1).astype(BF16)
    sel_all = jnp.concatenate([sel_ref[g][:, :n_cmp] for g in range(NSA_KV_HEADS)], axis=0).astype(BF16)
    picked = _dot(sel_all, expand_ref[...])
    tpos_all = jnp.concatenate([tpos] * NSA_KV_HEADS, axis=0)
    bias = jnp.where((picked > 0.5) & (kpos <= tpos_all), 0.0, NEG_INF)
    bias = jnp.concatenate([bias[g * tq:(g + 1) * tq] for g in range(NSA_KV_HEADS) for _ in range(NSA_GROUP)],
                           axis=0)
    s = _dot(qbd_ref[...], kt_all) + bias
    m_old = m_ref[...]
    m_new = jnp.maximum(m_old, jnp.max(s, axis=-1, keepdims=True))
    alpha = jnp.exp(m_old - m_new)
    p = jnp.exp(s - m_new)
    m_ref[...] = m_new
    l_ref[...] = alpha * l_ref[...] + jnp.sum(p, axis=-1, keepdims=True)
    acc_ref[...] = alpha * acc_ref[...] + _dot_nt(p.astype(BF16), vt_all)

    @pl.when(c == pl.num_programs(1) - 1)
    def _():
        gate = gate_ref[0]
        rnew = lax.broadcasted_iota(jnp.int32, (1, tq), 1)
        newpos = past_len + rnew
        n_buf = wbuf_ref.shape[-1]
        bpos = past_len - n_buf + lax.broadcasted_iota(jnp.int32, (1, n_buf), 1)
        new_lane = 2 * (past_len // SEL_BLOCK)
        heads = []
        for g in range(NSA_KV_HEADS):
            q4 = (stack(qr_ref, g) * NSA_SCALE).astype(BF16)
            ks, vs = slice(g * hd, (g + 1) * hd), slice(kvw + g * hd, kvw + (g + 1) * hd)
            picked = sel_ref[g][:, new_lane:new_lane + 1] > 0.5
            valid = jnp.where(picked & (newpos <= tpos) & (rnew < t_real), 1.0, 0.0)
            gr = slice(g * rows, (g + 1) * rows)
            _, l_s, acc_s = _flash_step(q4, snew_ref[0, :, ks], snew_ref[0, :, vs], valid,
                                        (m_ref[gr, :], l_ref[gr, :], acc_ref[gr, ks]), False)
            dist = tpos - bpos
            valid = jnp.where((dist >= 0) & (dist <= WINDOW) & (bpos >= 0), 1.0, 0.0)
            init = (jnp.full((rows, 1), M_INIT, F32), jnp.zeros((rows, 1), F32), jnp.zeros((rows, hd), F32))
            carry = _flash_step(q4, wbuf_ref[0, 0, 0, g], wbuf_ref[0, 0, 1, g], valid, init, True)
            dist = tpos - newpos
            valid = jnp.where((dist >= 0) & (dist <= WINDOW) & (rnew < t_real), 1.0, 0.0)
            _, l_w, acc_w = _flash_step(q4, wnew_ref[0, :, ks], wnew_ref[0, :, vs], valid, carry, False)
            o_cmp = ocmp_ref[g]
            o_slc = acc_s / l_s
            o_win = acc_w / l_w
            for hh in range(NSA_GROUP):
                head = g * NSA_GROUP + hh
                hr = slice(hh * tq, (hh + 1) * tq)
                col = (head // (NSA_HEADS // 2)) * LANES + (head % (NSA_HEADS // 2)) * 3
                heads.append(o_cmp[hr] * gate[:, col:col + 1] + o_slc[hr] * gate[:, col + 1:col + 2]
                             + o_win[hr] * gate[:, col + 2:col + 3])
        o_ref[0] = jnp.concatenate(heads, axis=-1)


SAMPLE_PAGES_PER_STEP = 8


def _attn_sample_call(qn, qr, kvc, gate, slc_new, win_buf_t, win_new, pool_t, layer, page_table, t_real):
    b, tq, nq = qn.shape
    n_pages = page_table.shape[1]
    page = pool_t.shape[-1]
    past_len = n_pages * page
    assert past_len % SEL_BLOCK == 0 and t_real <= SEL_BLOCK and past_len % (CMP_BLOCK * LANES) == 0
    n = SAMPLE_PAGES_PER_STEP
    rows = NSA_GROUP * tq
    full = lambda a: pl.BlockSpec((1,) + a.shape[1:], lambda i, c, pt: (i,) + (0,) * (a.ndim - 1))
    wbuf_spec = pl.BlockSpec((1, 1) + win_buf_t.shape[2:], lambda i, c, pt: (layer, i, 0, 0, 0, 0))
    n_cmp = past_len // CMP_BLOCK
    sel_lanes = n_cmp + LANES
    expand = (jnp.arange(n_cmp)[:, None] == (jnp.arange(past_len) // CMP_BLOCK)[None, :]).astype(BF16)
    return pl.pallas_call(
        functools.partial(_attn_sample_kernel, n, t_real, past_len),
        grid_spec=pltpu.PrefetchScalarGridSpec(
            num_scalar_prefetch=1,
            grid=(b, n_pages // n),
            in_specs=[full(a) for a in (qn, qr, kvc, gate, slc_new)] + [wbuf_spec, full(win_new)]
            + [pl.BlockSpec((n_cmp, n * page), lambda i, c, pt: (0, c))]
            + _page_specs(n, layer, pool_t.shape[2:]),
            out_specs=pl.BlockSpec((1, tq, nq), lambda i, c, pt: (i, 0, 0)),
            scratch_shapes=[pltpu.VMEM((NSA_KV_HEADS, tq, sel_lanes), F32),
                            pltpu.VMEM((NSA_KV_HEADS, rows, NSA_HEAD_DIM), F32),
                            pltpu.VMEM((NSA_KV_HEADS * rows, 1), F32),
                            pltpu.VMEM((NSA_KV_HEADS * rows, 1), F32),
                            pltpu.VMEM((NSA_KV_HEADS * rows, NSA_KV_WIDTH), F32),
                            pltpu.VMEM((NSA_KV_HEADS * rows, NSA_KV_WIDTH), BF16)],
        ),
        out_shape=jax.ShapeDtypeStruct((b, tq, nq), F32),
        compiler_params=_cparams(2),
        name="nsa_attn_sample",
    )(page_table, qn, qr, kvc, gate, slc_new, win_buf_t, win_new, expand, *([pool_t] * n))


def _rope_tables(pos):
    half = NSA_HEAD_DIM // 2
    inv = ROPE_THETA ** (-jnp.arange(half, dtype=F32) / half)
    ang = pos.astype(F32)[:, None] * inv[None, :]
    cos, sin = jnp.cos(ang), jnp.sin(ang)
    reps = LANES // NSA_HEAD_DIM
    return jnp.tile(cos, (1, 2 * reps)), jnp.tile(jnp.concatenate([-sin, sin], axis=1), (1, reps))


def _nsa_params(w_in, g_q, g_ks, g_kw, g_kc, pe, w1, w2):
    d = w_in.shape[0]
    body = NSA_HEADS * NSA_HEAD_DIM + 6 * NSA_KV_WIDTH
    half = NSA_HEADS // 2 * 3
    zpad = jnp.zeros((d, LANES - half), w_in.dtype)
    w_pad = jnp.concatenate([w_in[:, :body], w_in[:, body:body + half], zpad, w_in[:, body + half:], zpad], axis=1)
    return dict(
        w_in=w_pad.astype(BF16),
        g_q=jnp.tile(g_q, NSA_HEADS)[None], g_ks=jnp.tile(g_ks, NSA_KV_HEADS)[None],
        g_kw=jnp.tile(g_kw, NSA_KV_HEADS)[None], g_kc=g_kc[None],
        pe=pe.reshape(2, CMP_BLOCK // 2, LANES), w1=w1.astype(BF16), w2=w2.astype(BF16))


def _kv_rows(x, b, t):
    return x.reshape(b, t, 2, NSA_KV_HEADS, NSA_HEAD_DIM)


def _nsa_prompt_layer(h, gmix, prm):
    b, t, d = h.shape
    cos, sin = _rope_tables(jnp.arange(t))
    qn, qr, cmp, slc, wrow, gate = _nsa_proj_call(
        h.reshape(b * t, d), gmix, prm["w_in"], prm["g_q"], prm["g_ks"], prm["g_kw"], cos, sin)
    r3 = lambda x: x.reshape(b, t, x.shape[-1])
    kvc = _compress_rows_call(r3(cmp), prm["pe"], prm["w1"], prm["w2"], prm["g_kc"])
    o = _attn_prompt_call(r3(qn), r3(qr), kvc, r3(slc), r3(wrow), r3(gate))
    keep = min(WINDOW, t)
    return (o.reshape(b * t, -1), _kv_rows(cmp, b, t), _kv_rows(slc, b, t), _kv_rows(wrow, b, t)[:, t - keep:])


def _nsa_sample_layer(h, cmp_pool_t, slc_pool_t, win_buf, win_buf_t, layer, page_table, gmix, prm):
    b, t, d = h.shape
    past_len = page_table.shape[1] * cmp_pool_t.shape[-1]
    cos, sin = _rope_tables(jnp.tile(past_len + jnp.arange(t), b))
    qn, qr, cmp, slc, wrow, gate = _nsa_proj_call(
        h.reshape(b * t, d), gmix, prm["w_in"], prm["g_q"], prm["g_ks"], prm["g_kw"], cos, sin)
    kvc = _compress_pages_call(cmp_pool_t, layer, page_table, prm["pe"], prm["w1"], prm["w2"], prm["g_kc"])
    tq = 8
    pad = lambda x: jnp.pad(x.reshape(b, t, x.shape[-1]), ((0, 0), (0, tq - t), (0, 0)))
    o = _attn_sample_call(pad(qn), pad(qr), kvc, pad(gate), pad(slc), win_buf_t, pad(wrow), slc_pool_t, layer,
                          page_table, t)
    new_win = jnp.concatenate([win_buf, _kv_rows(wrow, b, t)], axis=1)[:, t:]
    return o[:, :t].reshape(b * t, -1), _kv_rows(cmp, b, t), _kv_rows(slc, b, t), new_win


def _softplus(z):
    return jnp.maximum(z, 0.0) + jnp.log1p(jnp.exp(-jnp.abs(z)))


def _rwkv_proj_kernel(batch, h_ref, shift_ref, gmix_ref, mu_ref, wr_ref, wk_ref, wv_ref,
                      w0_ref, w1_ref, w2_ref, a0_ref, a1_ref, a2_ref, g1_ref, g2_ref,
                      r_ref, dec_ref, k_ref, v_ref, a_ref, g_ref, last_ref, carry_ref):
    i = pl.program_id(0)
    u = _rms(h_ref[...]) * gmix_ref[...]
    tm = u.shape[0]

    @pl.when(i == 0)
    def _():
        carry_ref[...] = shift_ref[...]
    prev = jnp.concatenate([carry_ref[...], u[:tm - batch]], axis=0)
    carry_ref[...] = u[tm - batch:]
    last_ref[...] = u[tm - batch:]
    xx = prev - u
    mix = lambda j: (u + xx * mu_ref[j:j + 1]).astype(BF16)
    r_ref[...] = _dot(mix(0), wr_ref[...])
    wl = w0_ref[...] + _dot(jnp.tanh(_dot(mix(1), w1_ref[...])).astype(BF16), w2_ref[...])
    dec_ref[...] = jnp.exp(-jnp.exp(-_softplus(-wl) - 0.5))
    k_ref[...] = _dot(mix(2), wk_ref[...])
    v_ref[...] = _dot(mix(3), wv_ref[...])
    a_ref[...] = jax.nn.sigmoid(a0_ref[...] + _dot(_dot(mix(4), a1_ref[...]).astype(BF16), a2_ref[...]))
    g_ref[...] = _dot(jax.nn.sigmoid(_dot(mix(5), g1_ref[...])).astype(BF16), g2_ref[...])


def _rwkv_proj_call(h, shift, gmix, prm):
    m, d = h.shape
    batch = shift.shape[0]
    tm = min(m, 256)
    assert tm % batch == 0 and tm > batch and batch % 8 == 0
    row = pl.BlockSpec((tm, d), lambda i: (i, 0))
    weights = [prm[n] for n in ("mu", "w_r", "w_k", "w_v", "w0", "w1", "w2", "a0", "a1", "a2", "g1", "g2")]
    outs = pl.pallas_call(
        functools.partial(_rwkv_proj_kernel, batch),
        grid=(m // tm,),
        in_specs=[row, _resident(shift.shape), _resident(gmix.shape)] + [_resident(w.shape) for w in weights],
        out_specs=[row] * 6 + [pl.BlockSpec((batch, d), lambda i: (0, 0))],
        out_shape=[jax.ShapeDtypeStruct((m, d), F32)] * 6 + [jax.ShapeDtypeStruct((batch, d), F32)],
        scratch_shapes=[pltpu.VMEM((batch, d), F32)],
        compiler_params=_cparams(1),
        name="rwkv_proj",
    )(h, shift, gmix, *weights)
    return outs[:6], outs[6]


def _rwkv_scan_kernel(tc, r_ref, dec_ref, k_ref, v_ref, a_ref, kk_ref, ka_ref, rk_ref, lw_ref, lb_ref, s0_ref,
                      y_ref, sout_ref, state_ref, vec_ref, stage_ref, ystage_ref):
    c = pl.program_id(1)
    n = RWKV_HEAD_DIM

    @pl.when(c == 0)
    def _():
        state_ref[...] = s0_ref[...]
        ystage_ref[...] = jnp.zeros_like(ystage_ref)

    def stage_in(t, slot):
        for i, ref in enumerate((r_ref, dec_ref, k_ref, v_ref, a_ref)):
            xa, xb = _rows_to_lanes(ref[t], ref[t + 1])
            stage_ref[i, 2 * slot] = xa
            stage_ref[i, 2 * slot + 1] = xb

    def step(t):
        r = stage_ref[0, t]
        k = stage_ref[2, t]
        v = stage_ref[3, t]
        a = stage_ref[4, t]
        kk = k * kk_ref[...]
        kk = kk / jnp.maximum(jnp.sqrt(jnp.sum(kk * kk, axis=0, keepdims=True)), 1e-12)
        k2 = k * (1.0 + (a - 1.0) * ka_ref[...])
        vec_ref[0] = -kk
        vec_ref[1] = kk * a
        vec_ref[2] = k2
        sa = jnp.zeros((n, LANES), F32)
        for j in range(n):
            sa = sa + state_ref[j] * vec_ref[0, j:j + 1, :]
        y = jnp.zeros((n, LANES), F32)
        for j in range(n):
            s_new = (state_ref[j] * stage_ref[1, t, j:j + 1, :] + sa * vec_ref[1, j:j + 1, :]
                     + v * vec_ref[2, j:j + 1, :])
            state_ref[j] = s_new
            y = y + s_new * stage_ref[0, t, j:j + 1, :]
        mean = jnp.mean(y, axis=0, keepdims=True)
        var = jnp.mean(jnp.square(y - mean), axis=0, keepdims=True)
        y = (y - mean) * lax.rsqrt(var + LNX_EPS) * lw_ref[...] + lb_ref[...]
        ystage_ref[t] = y + jnp.sum(r * k2 * rk_ref[...], axis=0, keepdims=True) * v

    def steps(slot):
        step(2 * slot)
        step(2 * slot + 1)

    def stage_out(slot, t):
        _lanes_to_rows(ystage_ref[2 * slot], ystage_ref[2 * slot + 1], y_ref, t)

    _pipelined_pairs(tc, stage_in, steps, stage_out)

    @pl.when(c == pl.num_programs(1) - 1)
    def _():
        sout_ref[...] = state_ref[...]


def _pipelined_pairs(tc, stage_in, steps, stage_out):
    stage_in(0, 0)

    def body(i, _):
        t = 4 * i
        stage_in(t + 2, 1)
        steps(0)
        stage_out(1, jnp.maximum(t - 2, 0))
        stage_in(jnp.minimum(t + 4, tc - 2), 0)
        steps(1)
        stage_out(0, t)
        return 0

    lax.fori_loop(0, tc // 4, body, 0)
    stage_out(1, tc - 2)


def _rows_to_lanes(xa, xb):
    chunks = xa.shape[1] // LANES
    stack = jnp.concatenate([x[:, c * LANES:(c + 1) * LANES] for x in (xa, xb) for c in range(chunks)], axis=0)
    tr = stack.T
    half = LANES // 2
    top, bot = tr[:half], tr[half:]
    low = _lane_iota() < half
    return (jnp.where(low, top, pltpu.roll(bot, half, axis=1)),
            jnp.where(low, pltpu.roll(top, half, axis=1), bot))


def _lanes_to_rows(ya, yb, out_ref, t):
    half = LANES // 2
    low = _lane_iota() < half
    top = jnp.where(low, ya, pltpu.roll(yb, half, axis=1))
    bot = jnp.where(low, pltpu.roll(ya, half, axis=1), yb)
    tr = jnp.concatenate([top, bot], axis=0).T
    chunks = out_ref.shape[2] // LANES
    for tok in range(2):
        for c in range(chunks):
            r0 = (tok * chunks + c) * 8
            out_ref[t + tok, :, c * LANES:(c + 1) * LANES] = tr[r0:r0 + 8]


def _lane_params(x, heads, width):
    per_chunk = LANES // width if width < LANES else 1
    x = x.reshape(heads // per_chunk, per_chunk, width)
    x = jnp.transpose(x, (2, 1, 0))
    return jnp.repeat(x[..., None], 8, axis=-1).reshape(width, LANES)


def _rwkv_scan_call(r, dec, k, v, a, kk, ka, rk, lw, lb, s0):
    t, b, d = r.shape
    n = RWKV_HEAD_DIM
    assert d == 8 * LANES and b % 8 == 0 and t % 4 == 0
    tc = math.gcd(t, 32)
    seq = pl.BlockSpec((tc, 8, d), lambda g, c: (c, g, 0))
    par = pl.BlockSpec((n, LANES), lambda g, c: (0, 0))
    st = pl.BlockSpec((n, n, LANES), lambda g, c: (0, 0, g))
    return pl.pallas_call(
        functools.partial(_rwkv_scan_kernel, tc),
        grid=(b // 8, t // tc),
        in_specs=[seq] * 5 + [par] * 5 + [st],
        out_specs=[seq, st],
        out_shape=[jax.ShapeDtypeStruct((t, b, d), F32), jax.ShapeDtypeStruct((n, n, b // 8 * LANES), F32)],
        scratch_shapes=[pltpu.VMEM((n, n, LANES), F32), pltpu.VMEM((3, n, LANES), F32),
                        pltpu.VMEM((5, 4, n, LANES), F32), pltpu.VMEM((4, n, LANES), F32)],
        compiler_params=_cparams(2),
        name="rwkv_scan",
    )(r, dec, k, v, a, kk, ka, rk, lw, lb, s0)


def _rwkv_params(mu, w_r, w_k, w_v, w0, w1, w2, a0, a1, a2, g1, g2):
    gpad = -g1.shape[1] % LANES
    return dict(mu=mu, w_r=w_r.astype(BF16), w_k=w_k.astype(BF16), w_v=w_v.astype(BF16), w0=w0[None],
                w1=w1.astype(BF16), w2=w2.astype(BF16), a0=a0[None], a1=a1.astype(BF16), a2=a2.astype(BF16),
                g1=jnp.pad(g1, ((0, 0), (0, gpad))).astype(BF16), g2=jnp.pad(g2, ((0, gpad), (0, 0))).astype(BF16))


def _rwkv_layer(h, batch, shift, s0, gmix, prm, k_k, k_a, r_k, lnx_w, lnx_b):
    m, d = h.shape
    t = m // batch
    n = RWKV_HEAD_DIM
    nh = d // n
    g8 = batch // 8
    (r, dec, k, v, a, g), new_shift = _rwkv_proj_call(h, shift, gmix, prm)
    rows = lambda x: x.reshape(t, batch, d)
    par = lambda x: _lane_params(x.reshape(-1), nh, n)
    st = s0.reshape(g8, 8, nh // 2, 2, n, n).transpose(5, 4, 0, 3, 2, 1).reshape(n, n, g8 * LANES)
    y, s = _rwkv_scan_call(rows(r), rows(dec), rows(k), rows(v), rows(a), par(k_k), par(k_a), par(r_k),
                           par(lnx_w), par(lnx_b), st)
    s = s.reshape(n, n, g8, 2, nh // 2, 8).transpose(2, 5, 4, 3, 1, 0).reshape(batch, nh, n, n)
    return y.reshape(m, d), g, new_shift, s


def _hgrn_proj_kernel(h_ref, gmix_ref, win_ref, lb_ref, omlb_ref, q_ref, f_ref, k_ref, i_ref, g_ref):
    d = h_ref.shape[1]
    u = (_rms(h_ref[...]) * gmix_ref[...]).astype(BF16)
    z = _dot(u, win_ref[...])
    q, f, g = z[:, :d], z[:, d:2 * d], z[:, 3 * d:]
    q_ref[...] = q * jax.nn.sigmoid(q)
    f_ref[...] = lb_ref[...] + omlb_ref[...] * jax.nn.sigmoid(f)
    k_ref[...] = omlb_ref[...] * jax.nn.sigmoid(-f)
    i_ref[...] = z[:, 2 * d:3 * d]
    g_ref[...] = g * jax.nn.sigmoid(g)


def _hgrn_proj_call(h, gmix, win, lb, omlb):
    m, d = h.shape
    tm = min(m, 256)
    row = pl.BlockSpec((tm, d), lambda i: (i, 0))
    return pl.pallas_call(
        _hgrn_proj_kernel,
        grid=(m // tm,),
        in_specs=[row, _resident(gmix.shape), _resident(win.shape), _resident(lb.shape), _resident(omlb.shape)],
        out_specs=[row] * 5,
        out_shape=[jax.ShapeDtypeStruct((m, d), F32)] * 5,
        compiler_params=_cparams(1),
        name="hgrn_proj",
    )(h, gmix, win, lb, omlb)


def _rows_to_lanes_dup(xa, xb):
    chunks = xa.shape[1] // LANES
    stack = jnp.concatenate([x[:, c * LANES:(c + 1) * LANES] for x in (xa, xb) for c in range(chunks)], axis=0)
    tr = stack.T
    half = LANES // 2
    low = _lane_iota() < half
    other = pltpu.roll(tr, half, axis=1)
    return jnp.where(low, tr, other), jnp.where(low, other, tr)


def _hgrn_scan_kernel(tc, q_ref, f_ref, k_ref, v_ref, s0_ref, o_ref, sout_ref, state_ref, keys_ref, vals_ref):
    c = pl.program_id(1)
    dk = state_ref.shape[0]

    @pl.when(c == 0)
    def _():
        state_ref[...] = s0_ref[...]

    def stage_in(tp, _):
        t = 2 * tp
        for i, ref in enumerate((q_ref, f_ref, k_ref)):
            xa, xb = _rows_to_lanes_dup(ref[t], ref[t + 1])
            keys_ref[i, t] = xa
            keys_ref[i, t + 1] = xb
        va, vb = _rows_to_lanes(v_ref[t], v_ref[t + 1])
        vals_ref[t] = va
        vals_ref[t + 1] = vb
        return 0

    lax.fori_loop(0, tc // 2, stage_in, 0)

    def step(t, _):
        v = vals_ref[t]
        o = jnp.zeros(v.shape, F32)
        for d in range(dk):
            s_new = state_ref[d] * keys_ref[1, t, d:d + 1, :] + keys_ref[2, t, d:d + 1, :] * v
            state_ref[d] = s_new
            o = o + s_new * keys_ref[0, t, d:d + 1, :]
        vals_ref[t] = o
        return 0

    lax.fori_loop(0, tc, step, 0)

    def stage_out(tq, _):
        for u in range(2):
            t = 4 * tq + 2 * u
            _lanes_to_rows(vals_ref[t], vals_ref[t + 1], o_ref, t)
        return 0

    lax.fori_loop(0, tc // 4, stage_out, 0)

    @pl.when(c == pl.num_programs(1) - 1)
    def _():
        sout_ref[...] = state_ref[...]


def _hgrn_scan_call(q, f, k, v, s0):
    t, b, d = q.shape
    dk = HGRN_HEAD_DIM
    dv = dk // 2
    assert d == 8 * LANES and b % 8 == 0 and t % 4 == 0
    tc = math.gcd(t, 32)
    seq = pl.BlockSpec((tc, 8, d), lambda g, c: (c, g, 0))
    st = pl.BlockSpec((dk, dv, LANES), lambda g, c: (0, 0, g))
    return pl.pallas_call(
        functools.partial(_hgrn_scan_kernel, tc),
        grid=(b // 8, t // tc),
        in_specs=[seq] * 4 + [st],
        out_specs=[seq, st],
        out_shape=[jax.ShapeDtypeStruct((t, b, d), F32), jax.ShapeDtypeStruct((dk, dv, b // 8 * LANES), F32)],
        scratch_shapes=[pltpu.VMEM((dk, dv, LANES), F32), pltpu.VMEM((3, tc, dk, LANES), F32),
                        pltpu.VMEM((tc, dv, LANES), F32)],
        compiler_params=_cparams(2),
        name="hgrn_scan",
    )(q, f, k, v, s0)


def _hgrn_layer(h, batch, s0, gmix, win, lb, omlb):
    m, d = h.shape
    t = m // batch
    n = HGRN_HEAD_DIM
    nh = d // n
    g8 = batch // 8
    q, f, k, i, g = _hgrn_proj_call(h, gmix, win, lb, omlb)
    rows = lambda x: x.reshape(t, batch, d)
    st = s0.reshape(g8, 8, nh, n, 2, n // 2).transpose(3, 5, 0, 4, 2, 1).reshape(n, n // 2, g8 * LANES)
    o, s = _hgrn_scan_call(rows(q), rows(f), rows(k), rows(i), st)
    s = s.reshape(n, n // 2, g8, 2, nh, 8).transpose(2, 5, 4, 0, 3, 1).reshape(batch, nh, n, n)
    return o.reshape(m, d), g, s


def kernel(x_prompt, x_sample, cache_cmp, cache_slc, cache_win, state_rwkv_shift, state_rwkv_wkv, state_hgrn,
           page_table, p_prompt, p_sample, norm_mix, norm_ffn, w_up, w_down, w_ple, w_ple_gate,
           nsa_w_in, nsa_g_q, nsa_g_ks, nsa_g_kw, nsa_g_kc, nsa_cmp_pe, nsa_cmp_w1, nsa_cmp_w2, nsa_w_out,
           rwkv_mu, rwkv_w_r, rwkv_w_k, rwkv_w_v, rwkv_w_o, rwkv_w0, rwkv_w1, rwkv_w2, rwkv_a0, rwkv_a1,
           rwkv_a2, rwkv_g1, rwkv_g2, rwkv_k_k, rwkv_k_a, rwkv_r_k, rwkv_lnx_w, rwkv_lnx_b,
           hgrn_w_in, hgrn_gn, hgrn_w_o, hgrn_lower_bounds):
    depth = norm_mix.shape[0]
    bp, tp, d = x_prompt.shape
    bs, ts, _ = x_sample.shape
    lb_soft = jax.nn.softmax(hgrn_lower_bounds.astype(F32), axis=0)
    lower_bound = jnp.cumsum(lb_soft, axis=0) - lb_soft[0]
    rows_minor = lambda x: jnp.transpose(x, (0, 1, 3, 4, 5, 2))
    cmp_pool_t, slc_pool_t, win_buf_t = rows_minor(cache_cmp), rows_minor(cache_slc), rows_minor(cache_win)
    hp, hs = x_prompt.reshape(bp * tp, d), x_sample.reshape(bs * ts, d)
    swap = lambda x, a, b: x.reshape(a, b, x.shape[-1]).transpose(1, 0, 2).reshape(a * b, x.shape[-1])
    time_major = False
    outs = [[] for _ in range(12)]
    for i in range(depth):
        kind, n = i % N_MIXERS, i // N_MIXERS
        gmix = norm_mix[i][None]
        ffn = (p_prompt[i].reshape(bp * tp, -1), p_sample[i].reshape(bs * ts, -1))
        if (kind != 0) != time_major:
            hp = swap(hp, tp, bp) if time_major else swap(hp, bp, tp)
            hs = swap(hs, ts, bs) if time_major else swap(hs, bs, ts)
            time_major = not time_major
        if time_major:
            ffn = (swap(ffn[0], bp, tp), swap(ffn[1], bs, ts))
        hp3, hs3 = hp.reshape(bp, tp, d), hs.reshape(bs, ts, d)
        tail = (norm_ffn[i][None], w_up[i].astype(BF16), w_down[i].astype(BF16), w_ple[i].astype(BF16),
                w_ple_gate[i].astype(BF16))
        if kind == 0:
            prm = _nsa_params(nsa_w_in[n], nsa_g_q[n], nsa_g_ks[n], nsa_g_kw[n], nsa_g_kc[n], nsa_cmp_pe[n],
                              nsa_cmp_w1[n], nsa_cmp_w2[n])
            op, rc, rs, rw = _nsa_prompt_layer(hp3, gmix, prm)
            os_, nc_rows, ns_rows, nw_buf = _nsa_sample_layer(hs3, cmp_pool_t, slc_pool_t, cache_win[n], win_buf_t,
                                                              n, page_table, gmix, prm)
            for lst, v in zip(outs[:6], (rc, nc_rows, rs, ns_rows, rw, nw_buf)):
                lst.append(v)
            wo = nsa_w_out[n].astype(BF16)
            hp = _ffn_call("nsa", hp, op, None, None, ffn[0], wo, *tail)
            hs = _ffn_call("nsa", hs, os_, None, None, ffn[1], wo, *tail)
        elif kind == 1:
            prm = _rwkv_params(rwkv_mu[n], rwkv_w_r[n], rwkv_w_k[n], rwkv_w_v[n], rwkv_w0[n], rwkv_w1[n],
                               rwkv_w2[n], rwkv_a0[n], rwkv_a1[n], rwkv_a2[n], rwkv_g1[n], rwkv_g2[n])
            vecs = (rwkv_k_k[n], rwkv_k_a[n], rwkv_r_k[n], rwkv_lnx_w[n], rwkv_lnx_b[n])
            nh = d // RWKV_HEAD_DIM
            zero_state = jnp.zeros((bp, nh, RWKV_HEAD_DIM, RWKV_HEAD_DIM), F32)
            yp, gp, shp, sp = _rwkv_layer(hp, bp, jnp.zeros((bp, d), F32), zero_state, gmix, prm, *vecs)
            ys, gs, shs, ss = _rwkv_layer(hs, bs, state_rwkv_shift[n], state_rwkv_wkv[n].astype(F32), gmix, prm,
                                          *vecs)
            for lst, v in zip(outs[6:10], (shp, shs, sp, ss)):
                lst.append(v)
            wo = rwkv_w_o[n].astype(BF16)
            hp = _ffn_call("rwkv", hp, yp, gp, None, ffn[0], wo, *tail)
            hs = _ffn_call("rwkv", hs, ys, gs, None, ffn[1], wo, *tail)
        else:
            lb = lower_bound[i][None]
            win = hgrn_w_in[n].astype(BF16)
            nh = d // HGRN_HEAD_DIM
            zero_state = jnp.zeros((bp, nh, HGRN_HEAD_DIM, HGRN_HEAD_DIM), F32)
            op, gp, sp = _hgrn_layer(hp, bp, zero_state, gmix, win, lb, 1.0 - lb)
            os_, gs, ss = _hgrn_layer(hs, bs, state_hgrn[n].astype(F32), gmix, win, lb, 1.0 - lb)
            outs[10].append(sp)
            outs[11].append(ss)
            wo, gn = hgrn_w_o[n].astype(BF16), hgrn_gn[n][None]
            hp = _ffn_call("hgrn", hp, op, gp, gn, ffn[0], wo, *tail)
            hs = _ffn_call("hgrn", hs, os_, gs, gn, ffn[1], wo, *tail)
    if time_major:
        hp, hs = swap(hp, tp, bp), swap(hs, ts, bs)
    return (hp.reshape(bp, tp, d), hs.reshape(bs, ts, d)) + tuple(jnp.stack(o) for o in outs)
```

```python
import functools
import math

import jax
import jax.numpy as jnp
from jax import lax
from jax.experimental import pallas as pl
from jax.experimental.pallas import tpu as pltpu

F32 = jnp.float32
BF16 = jnp.bfloat16

NORM_EPS = 1e-6
ROPE_THETA = 10000.0
NEG_INF = -1e30
M_INIT = -1e29
N_MIXERS = 3

NSA_HEADS = 16
NSA_KV_HEADS = 4
NSA_HEAD_DIM = 64
NSA_GROUP = NSA_HEADS // NSA_KV_HEADS
NSA_KV_WIDTH = NSA_KV_HEADS * NSA_HEAD_DIM
NSA_SCALE = NSA_HEAD_DIM ** -0.5
LOG2E = math.log2(math.e)
CMP_BLOCK = 32
SEL_BLOCK = 64
N_SELECT = 16
WINDOW = 512
FORCE_SCORE = 1e4

RWKV_HEAD_DIM = 64
LNX_EPS = 64e-5
HGRN_HEAD_DIM = 128

LANES = 128
VMEM_LIMIT = 56 * 1024 * 1024


def _cparams(n_axes):
    return pltpu.CompilerParams(dimension_semantics=("arbitrary",) * n_axes,
                                vmem_limit_bytes=VMEM_LIMIT)


def _resident(shape):
    zeros = (0,) * len(shape)
    return pl.BlockSpec(shape, lambda *_: zeros, pipeline_mode=pl.Buffered(1))


def _rms(x):
    return x * lax.rsqrt(jnp.mean(x * x, axis=-1, keepdims=True) + NORM_EPS)


def _dot(a, b):
    return jnp.dot(a, b, preferred_element_type=F32)


def _dot_nt(a, b):
    return lax.dot_general(a, b, (((1,), (1,)), ((), ())), preferred_element_type=F32)


def _per_chunk(fn, x, *rest):
    n = x.shape[-1] // LANES
    outs = [fn(*(a[:, c * LANES:(c + 1) * LANES] for a in (x,) + rest)) for c in range(n)]
    return outs[0] if n == 1 else jnp.concatenate(outs, axis=-1)


def _lane_iota():
    return lax.broadcasted_iota(jnp.int32, (1, LANES), 1)


def _group_ones(group):
    r = lax.broadcasted_iota(jnp.int32, (LANES, LANES), 0) // group
    c = lax.broadcasted_iota(jnp.int32, (LANES, LANES), 1) // group
    return jnp.where(r == c, 1.0, 0.0).astype(BF16)


def _head_rms(x, head_dim, ones):
    def one(c):
        sq = c * c
        hi = sq.astype(BF16)
        lo = (sq - hi.astype(F32)).astype(BF16)
        ss = _dot(hi, ones) + _dot(lo, ones)
        return c * lax.rsqrt(ss * (1.0 / head_dim) + NORM_EPS)
    return _per_chunk(one, x)


def _rope(x, cos, sin_signed):
    half = NSA_HEAD_DIM // 2
    lane = _lane_iota()

    def one(c):
        up = pltpu.roll(c, LANES - half, axis=1)
        dn = pltpu.roll(c, half, axis=1)
        rot = jnp.where((lane & half) == 0, up, dn)
        return c * cos + rot * sin_signed
    return _per_chunk(one, x)


def _ffn_kernel(mode, ff_chunk, *refs):
    if mode == "nsa":
        h_ref, o_ref, p_ref, wo_ref, gffn_ref, wup_ref, wdown_ref, wple_ref, wgate_ref, out_ref = refs
        o = o_ref[...]
    elif mode == "rwkv":
        h_ref, o_ref, aux_ref, p_ref, wo_ref, gffn_ref, wup_ref, wdown_ref, wple_ref, wgate_ref, out_ref = refs
        o = o_ref[...] * aux_ref[...]
    else:
        (h_ref, o_ref, aux_ref, gn_ref, p_ref, wo_ref, gffn_ref, wup_ref, wdown_ref, wple_ref, wgate_ref,
         out_ref) = refs
        gn = gn_ref[...]
        o = _per_chunk(lambda c: _rms(c) * gn, o_ref[...]) * aux_ref[...]
    h1 = h_ref[...] + _dot(o.astype(BF16), wo_ref[...])
    u = (_rms(h1) * gffn_ref[...]).astype(BF16)
    d_ff = wup_ref.shape[1]
    acc = jnp.zeros_like(h1)
    for j in range(d_ff // ff_chunk):
        a = _dot(u, wup_ref[:, j * ff_chunk:(j + 1) * ff_chunk])
        a = jnp.square(jnp.maximum(a, 0.0)).astype(BF16)
        acc = acc + _dot(a, wdown_ref[j * ff_chunk:(j + 1) * ff_chunk, :])
    h2 = h1 + acc
    gate = jax.nn.sigmoid(_dot(_rms(h2).astype(BF16), wgate_ref[...]))
    out_ref[...] = h2 + _dot(p_ref[...].astype(BF16), wple_ref[...]) * gate


def _ffn_call(mode, h, o, aux, gn, p, wo, gffn, wup, wdown, wple, wgate):
    m, d = h.shape
    tm = min(m, 512)
    row = lambda w: pl.BlockSpec((tm, w), lambda i: (i, 0))
    args, specs = [h, o], [row(d), row(d)]
    if mode != "nsa":
        args.append(aux)
        specs.append(row(d))
    if mode == "hgrn":
        args.append(gn)
        specs.append(_resident(gn.shape))
    args += [p, wo, gffn, wup, wdown, wple, wgate]
    specs += [row(p.shape[1])] + [_resident(a.shape) for a in (wo, gffn, wup, wdown, wple, wgate)]
    return pl.pallas_call(
        functools.partial(_ffn_kernel, mode, 1024),
        grid=(m // tm,),
        in_specs=specs,
        out_specs=row(d),
        out_shape=jax.ShapeDtypeStruct((m, d), F32),
        compiler_params=_cparams(1),
        name="ffn_" + mode,
    )(*args)


def _nsa_proj_kernel(h_ref, gmix_ref, win_ref, gq_ref, gks_ref, gkw_ref, cos_ref, sin_ref,
                     qn_ref, qr_ref, cmp_ref, slc_ref, wrow_ref, gate_ref):
    kvw = NSA_KV_WIDTH
    nq = NSA_HEADS * NSA_HEAD_DIM
    u = (_rms(h_ref[...]) * gmix_ref[...]).astype(BF16)
    z = _dot(u, win_ref[...])
    cos = cos_ref[...]
    sin = sin_ref[...]
    ones = _group_ones(NSA_HEAD_DIM)
    qn = _head_rms(z[:, :nq], NSA_HEAD_DIM, ones) * gq_ref[...]
    qn_ref[...] = qn
    qr_ref[...] = _rope(qn, cos, sin)
    cmp_ref[...] = z[:, nq:nq + 2 * kvw]
    o = nq + 2 * kvw
    slc_ref[:, :kvw] = _rope(_head_rms(z[:, o:o + kvw], NSA_HEAD_DIM, ones) * gks_ref[...], cos, sin)
    slc_ref[:, kvw:] = z[:, o + kvw:o + 2 * kvw]
    o += 2 * kvw
    wrow_ref[:, :kvw] = _rope(_head_rms(z[:, o:o + kvw], NSA_HEAD_DIM, ones) * gkw_ref[...], cos, sin)
    wrow_ref[:, kvw:] = z[:, o + kvw:o + 2 * kvw]
    o += 2 * kvw
    gate_ref[...] = jax.nn.sigmoid(z[:, o:])


def _nsa_proj_call(h, gmix, win, gq, gks, gkw, cos, sin):
    m, d = h.shape
    tm = min(m, 256)
    tab_tiles = cos.shape[0] // tm
    row = lambda w: pl.BlockSpec((tm, w), lambda i: (i, 0))
    tab = pl.BlockSpec((tm, LANES), lambda i: (i % tab_tiles, 0))
    nq = NSA_HEADS * NSA_HEAD_DIM
    widths = [nq, nq] + [2 * NSA_KV_WIDTH] * 3 + [2 * LANES]
    return pl.pallas_call(
        _nsa_proj_kernel,
        grid=(m // tm,),
        in_specs=[row(d), _resident(gmix.shape), _resident(win.shape), _resident(gq.shape),
                  _resident(gks.shape), _resident(gkw.shape), tab, tab],
        out_specs=[row(w) for w in widths],
        out_shape=[jax.ShapeDtypeStruct((m, w), F32) for w in widths],
        compiler_params=_cparams(1),
        name="nsa_proj",
    )(h, gmix, win, gq, gks, gkw, cos, sin)


def _compress_slot(load_pair, m, e, pe_ref, w1_ref, w2_ref, gkc_ref, flat_ref):
    lane = _lane_iota()
    low = lane < NSA_HEAD_DIM
    for s in range(CMP_BLOCK // 2):
        pe_row = pe_ref[e, s:s + 1, :]
        parts = []
        for kp in range(NSA_KV_HEADS // 2):
            a = load_pair(kp, 2 * s)
            b = load_pair(kp, 2 * s + 1)
            parts.append(jnp.where(low, a, pltpu.roll(b, NSA_HEAD_DIM, axis=1)))
            parts.append(jnp.where(low, pltpu.roll(a, NSA_HEAD_DIM, axis=1), b))
        flat_ref[:, s * LANES:(s + 1) * LANES] = (jnp.concatenate(parts, axis=0) + pe_row).astype(BF16)
    out = _dot(jax.nn.gelu(_dot(flat_ref[...], w1_ref[e])).astype(BF16), w2_ref[e])
    if e == 0:
        out = _rms(out) * gkc_ref[...]
    return jnp.concatenate([out[k * m:(k + 1) * m] for k in range(NSA_KV_HEADS)], axis=-1)


def _compress_rows_kernel(*refs):
    cols = refs[:NSA_KV_HEADS]
    pe_ref, w1_ref, w2_ref, gkc_ref, out_ref, flat_ref = refs[NSA_KV_HEADS:]
    m = cols[0].shape[1] // CMP_BLOCK
    pad = out_ref.shape[2] - m
    for e in range(2):
        load = lambda kp, r: cols[2 * e + kp][0, pl.ds(r, m, stride=CMP_BLOCK), :]
        res = _compress_slot(load, m, e, pe_ref, w1_ref, w2_ref, gkc_ref, flat_ref)
        out_ref[0, e] = jnp.concatenate([res, jnp.zeros((pad, res.shape[1]), F32)], axis=0)


def _compress_rows_call(cmp, pe, w1, w2, gkc):
    b, t, w = cmp.shape
    c = -(-(t // CMP_BLOCK) // LANES) * LANES
    return pl.pallas_call(
        _compress_rows_kernel,
        grid=(b,),
        in_specs=[pl.BlockSpec((1, t, LANES), functools.partial(lambda j, i: (i, 0, j), j)) for j in range(w // LANES)]
        + [_resident(a.shape) for a in (pe, w1, w2, gkc)],
        out_specs=pl.BlockSpec((1, 2, c, NSA_KV_WIDTH), lambda i: (i, 0, 0, 0)),
        out_shape=jax.ShapeDtypeStruct((b, 2, c, NSA_KV_WIDTH), F32),
        scratch_shapes=[pltpu.VMEM((NSA_KV_HEADS * (t // CMP_BLOCK), w1.shape[1]), BF16)],
        compiler_params=_cparams(1),
        name="nsa_compress_rows",
    )(*([cmp] * (w // LANES)), pe, w1, w2, gkc)


PAGES_PER_STEP = 16


def _page_specs(n, layer, tail):
    zeros = (0,) * len(tail)
    return [pl.BlockSpec((1, 1) + tail,
                         functools.partial(lambda i, b, c, pt: (layer, pt[b, c * n + i]) + zeros, i))
            for i in range(n)]


def _compress_pages_kernel(pt_ref, *refs):
    n = len(refs) - 7
    pages = refs[:n]
    pe_ref, w1_ref, w2_ref, gkc_ref, out_ref, rows_ref, flat_ref = refs[n:]
    page = pages[0].shape[-1]
    m = n * page // CMP_BLOCK
    for e in range(2):
        for i, pg in enumerate(pages):
            for kp in range(NSA_KV_HEADS // 2):
                tile = pg[0, 0, e, 2 * kp:2 * kp + 2].reshape(LANES, page)
                rows_ref[e, kp, i * page:(i + 1) * page, :] = tile.T
    for e in range(2):
        load = lambda kp, r: rows_ref[e, kp, pl.ds(r, m, stride=CMP_BLOCK), :]
        out_ref[0, e] = _compress_slot(load, m, e, pe_ref, w1_ref, w2_ref, gkc_ref, flat_ref.at[e])


def _compress_pages_call(pool_t, layer, page_table, pe, w1, w2, gkc):
    b, n_pages = page_table.shape
    page = pool_t.shape[-1]
    n = PAGES_PER_STEP
    m = n * page // CMP_BLOCK
    return pl.pallas_call(
        _compress_pages_kernel,
        grid_spec=pltpu.PrefetchScalarGridSpec(
            num_scalar_prefetch=1,
            grid=(b, n_pages // n),
            in_specs=_page_specs(n, layer, pool_t.shape[2:])
            + [pl.BlockSpec(a.shape, functools.partial(lambda nd, i, c, pt: (0,) * nd, a.ndim),
                            pipeline_mode=pl.Buffered(1)) for a in (pe, w1, w2, gkc)],
            out_specs=pl.BlockSpec((1, 2, m, NSA_KV_WIDTH), lambda i, c, pt: (i, 0, c, 0)),
            scratch_shapes=[pltpu.VMEM((2, NSA_KV_HEADS // 2, n * page, LANES), F32),
                            pltpu.VMEM((2, NSA_KV_HEADS * m, w1.shape[1]), BF16)],
        ),
        out_shape=jax.ShapeDtypeStruct((b, 2, n_pages * page // CMP_BLOCK, NSA_KV_WIDTH), F32),
        compiler_params=_cparams(2),
        name="nsa_compress_pages",
    )(page_table, *([pool_t] * n), pe, w1, w2, gkc)


def _select_blocks_t(imp, tpos):
    n_blocks = imp.shape[0]
    blk = lax.broadcasted_iota(jnp.int32, (n_blocks, 1), 0)
    cur = tpos // SEL_BLOCK
    forced = (blk == 0) | (blk == cur) | (blk == cur - 1)
    score = jnp.where(blk <= cur, jnp.where(forced, FORCE_SCORE, imp), -1.0)
    rank = jnp.zeros(score.shape, F32)
    for j in range(n_blocks):
        row = score[j:j + 1, :]
        below = jnp.where(j < blk, 1.0, 0.0)
        rank = rank + jnp.where(row > score, 1.0, jnp.where(row == score, below, 0.0))
    return jnp.where(rank < float(min(N_SELECT, n_blocks)), 1.0, 0.0)


def _attn_prompt_kernel(tq, tk, qn_ref, qr_ref, kvc_ref, sk_ref, sv_ref, wk_ref, wv_ref, gate_ref, o_ref,
                        kb_ref, vt_ref, sel_ref, imp_ref):
    qi = pl.program_id(2)
    hd = NSA_HEAD_DIM
    t_len = sk_ref.shape[1]
    c_blocks = kvc_ref.shape[2]
    n_sel = t_len // SEL_BLOCK
    ratio = SEL_BLOCK // CMP_BLOCK
    rows_per_tile = tk // SEL_BLOCK

    @pl.when(qi == 0)
    def _():
        for br, (k_ref, v_ref) in enumerate(((sk_ref, sv_ref), (wk_ref, wv_ref))):
            for gl in range(2):
                kb_ref[br, gl] = k_ref[0, :, gl * hd:(gl + 1) * hd].astype(BF16)
            for kt in range(t_len // tk):
                vt_ref[br, kt] = v_ref[0, kt * tk:(kt + 1) * tk, :].T.astype(BF16)

    t0 = qi * tq
    tpos = t0 + lax.broadcasted_iota(jnp.int32, (1, tq), 1)
    gate_t = gate_ref[0].T
    qn_t = [qn_ref[0, :, c * LANES:(c + 1) * LANES].T for c in range(2 * NSA_GROUP * hd // LANES)]
    qr_t = [qr_ref[0, :, c * LANES:(c + 1) * LANES].T for c in range(2 * NSA_GROUP * hd // LANES)]
    cid = lax.broadcasted_iota(jnp.int32, (c_blocks, 1), 0)
    ok = jnp.where((cid + 1) * CMP_BLOCK - 1 <= tpos, 1.0, 0.0)
    ok4 = jnp.concatenate([ok] * NSA_GROUP, axis=1)
    qr4s, o_cmps = [], []
    for gl in range(2):
        per_head = lambda parts: [parts[2 * gl + hh // 2][(hh % 2) * hd:(hh % 2 + 1) * hd] for hh in range(NSA_GROUP)]
        qn4 = (jnp.concatenate(per_head(qn_t), axis=1) * NSA_SCALE).astype(BF16)
        qr4s.append((jnp.concatenate(per_head(qr_t), axis=1) * (NSA_SCALE * LOG2E)).astype(BF16))
        kc = kvc_ref[0, 0][:, gl * hd:(gl + 1) * hd]
        vc_t = kvc_ref[0, 1].T[gl * hd:(gl + 1) * hd]
        s = jnp.where(ok4 > 0.5, _dot(kc.astype(BF16), qn4), NEG_INF)
        e = jnp.exp(s - jnp.max(s, axis=0, keepdims=True))
        p = e / jnp.sum(e, axis=0, keepdims=True) * ok4
        o_cmps.append(_dot(vc_t.astype(BF16), p.astype(BF16)))
        imp = p[:, :tq]
        for hh in range(1, NSA_GROUP):
            imp = imp + p[:, hh * tq:(hh + 1) * tq]
        parts = []
        for j in range(tq // LANES):
            imp_ref[j] = imp[:, j * LANES:(j + 1) * LANES]
            part = imp_ref[j, pl.ds(0, n_sel, stride=ratio), :]
            for i in range(1, ratio):
                part = part + imp_ref[j, pl.ds(i, n_sel, stride=ratio), :]
            parts.append(part)
        imp_sel = parts[0] if len(parts) == 1 else jnp.concatenate(parts, axis=1)
        sel_ref[gl] = _select_blocks_t(imp_sel, tpos)

    def flash(br, lo, hi, valid_fn):
        def body(kt, carry):
            k0 = pl.multiple_of(kt * tk, tk)
            out = []
            for gl in range(2):
                m, l, acc = carry[gl]
                kb = kb_ref[br, gl, pl.ds(k0, tk), :]
                vt = vt_ref[br, kt, gl * hd:(gl + 1) * hd, :]
                bias = jnp.where(valid_fn(gl, kt, k0), 0.0, NEG_INF)
                s = _dot(kb, qr4s[gl]) + jnp.concatenate([bias] * NSA_GROUP, axis=1)
                m_new = jnp.maximum(m, jnp.max(s, axis=0, keepdims=True))
                alpha = jnp.exp2(m - m_new)
                p = jnp.exp2(s - m_new)
                out.append((m_new, alpha * l + jnp.sum(p, axis=0, keepdims=True),
                            alpha * acc + _dot(vt, p.astype(BF16))))
            return tuple(out)
        cols = NSA_GROUP * tq
        init = tuple((jnp.full((1, cols), M_INIT, F32), jnp.zeros((1, cols), F32), jnp.zeros((hd, cols), F32))
                     for _ in range(2))
        res = lax.fori_loop(lo, hi, body, init)
        return [acc / l for _, l, acc in res]

    def slc_valid(gl, kt, k0):
        picked = sel_ref[gl, pl.ds(pl.multiple_of(kt * rows_per_tile, rows_per_tile), rows_per_tile), :]
        picked = jnp.concatenate([jnp.broadcast_to(picked[i:i + 1], (SEL_BLOCK, tq))
                                  for i in range(rows_per_tile)], axis=0)
        kpos = k0 + lax.broadcasted_iota(jnp.int32, (tk, 1), 0)
        return (picked > 0.5) & (kpos <= tpos)

    def win_valid(gl, kt, k0):
        dist = tpos - (k0 + lax.broadcasted_iota(jnp.int32, (tk, 1), 0))
        return (dist >= 0) & (dist <= WINDOW)

    hi = (t0 + tq - 1) // tk + 1
    slc = flash(0, 0, hi, slc_valid)
    win = flash(1, jnp.maximum(t0 - WINDOW, 0) // tk, hi, win_valid)
    heads = []
    for gl in range(2):
        for hh in range(NSA_GROUP):
            c = (gl * NSA_GROUP + hh) * 3
            cols = slice(hh * tq, (hh + 1) * tq)
            heads.append(o_cmps[gl][:, cols] * gate_t[c:c + 1] + slc[gl][:, cols] * gate_t[c + 1:c + 2]
                         + win[gl][:, cols] * gate_t[c + 2:c + 3])
    for c in range(len(heads) // 2):
        o_ref[0, :, c * LANES:(c + 1) * LANES] = jnp.concatenate(heads[2 * c:2 * c + 2], axis=0).T


def _attn_prompt_call(qn, qr, kvc, slc, win, gate, tq=256, tk=512):
    b, t, nq = qn.shape
    pair_w = 2 * NSA_GROUP * NSA_HEAD_DIM
    c = kvc.shape[2]
    assert tk % (8 * SEL_BLOCK) == 0 and t % tk == 0 and c % LANES == 0 and tq % LANES == 0 and t % tq == 0
    qspec = pl.BlockSpec((1, tq, pair_w), lambda i, p, j: (i, j, p))
    kspec = pl.BlockSpec((1, t, LANES), lambda i, p, j: (i, 0, p))
    vspec = pl.BlockSpec((1, t, LANES), lambda i, p, j: (i, 0, 2 + p))
    return pl.pallas_call(
        functools.partial(_attn_prompt_kernel, tq, tk),
        grid=(b, 2, t // tq),
        in_specs=[qspec, qspec,
                  pl.BlockSpec((1, 2, c, LANES), lambda i, p, j: (i, 0, 0, p)),
                  kspec, vspec, kspec, vspec,
                  pl.BlockSpec((1, tq, LANES), lambda i, p, j: (i, j, p))],
        out_specs=qspec,
        out_shape=jax.ShapeDtypeStruct((b, t, nq), F32),
        scratch_shapes=[pltpu.VMEM((2, 2, t, NSA_HEAD_DIM), BF16),
                        pltpu.VMEM((2, t // tk, LANES, tk), BF16),
                        pltpu.VMEM((2, t // SEL_BLOCK, tq), F32),
                        pltpu.VMEM((tq // LANES, c, LANES), F32)],
        compiler_params=_cparams(3),
        name="nsa_attn_prompt",
    )(qn, qr, kvc, slc, slc, win, win, gate)


def _cmp_branch(q4, kc, vc, tpos, n_tok):
    c = kc.shape[0]
    s = _dot_nt((q4 * NSA_SCALE).astype(BF16), kc.astype(BF16))
    cidx = lax.broadcasted_iota(jnp.int32, (1, c), 1)
    ok = ((cidx + 1) * CMP_BLOCK - 1 <= tpos).astype(F32)
    ok4 = jnp.concatenate([ok] * NSA_GROUP, axis=0)
    s = jnp.where(ok4 > 0.5, s, NEG_INF)
    e = jnp.exp(s - jnp.max(s, axis=-1, keepdims=True))
    p = e / jnp.sum(e, axis=-1, keepdims=True) * ok4
    o = _dot(p.astype(BF16), vc.astype(BF16))
    imp = p[:n_tok]
    for hh in range(1, NSA_GROUP):
        imp = imp + p[hh * n_tok:(hh + 1) * n_tok]
    return o, imp


def _select_blocks(imp, tpos, n_blocks):
    c = imp.shape[1]
    lane = lax.broadcasted_iota(jnp.int32, (1, c), 1)

    def pair(x):
        lane1 = _lane_iota()
        return x + jnp.where((lane1 & 1) == 0, pltpu.roll(x, LANES - 1, axis=1), pltpu.roll(x, 1, axis=1))
    imp2 = _per_chunk(pair, imp)
    blk = lane // 2
    cur = tpos // SEL_BLOCK
    forced = (blk == 0) | (blk == cur) | (blk == cur - 1)
    score = jnp.where(blk <= cur, jnp.where(forced, FORCE_SCORE, imp2), -1.0)
    rank = jnp.zeros(score.shape, F32)
    for j in range(n_blocks):
        col = score[:, 2 * j:2 * j + 1]
        ahead = (col > score) | ((col == score) & (j < blk))
        rank = rank + ahead.astype(F32)
    return (rank < float(min(N_SELECT, n_blocks))).astype(F32)


def _flash_step(q4, k, v, valid, carry, transposed):
    m, l, acc = carry
    valid4 = jnp.concatenate([valid] * NSA_GROUP, axis=0)
    s = _dot(q4, k.astype(BF16)) if transposed else _dot_nt(q4, k.astype(BF16))
    s = jnp.where(valid4 > 0.5, s, NEG_INF)
    m_new = jnp.maximum(m, jnp.max(s, axis=-1, keepdims=True))
    alpha = jnp.exp(m - m_new)
    p = jnp.exp(s - m_new)
    l = alpha * l + jnp.sum(p, axis=-1, keepdims=True)
    pv = _dot_nt(p.astype(BF16), v.astype(BF16)) if transposed else _dot(p.astype(BF16), v.astype(BF16))
    return m_new, l, alpha * acc + pv


def _attn_sample_kernel(n_step, t_real, past_len, pt_ref, qn_ref, qr_ref, kvc_ref, gate_ref, snew_ref, wbuf_ref,
                        wnew_ref, expand_ref, *rest):
    pages = rest[:n_step]
    o_ref, sel_ref, ocmp_ref, m_ref, l_ref, acc_ref, qbd_ref = rest[n_step:]
    c = pl.program_id(1)
    hd = NSA_HEAD_DIM
    kvw = NSA_KV_WIDTH
    tq = qn_ref.shape[1]
    n_cmp = past_len // CMP_BLOCK
    tpos = past_len + lax.broadcasted_iota(jnp.int32, (tq, 1), 0) % t_real
    stack = lambda ref, g: jnp.concatenate(
        [ref[0, :, (g * NSA_GROUP + hh) * hd:(g * NSA_GROUP + hh + 1) * hd] for hh in range(NSA_GROUP)], axis=0)
    rows = NSA_GROUP * tq

    @pl.when(c == 0)
    def _():
        n_blocks = -(-(past_len + t_real) // SEL_BLOCK)
        for g in range(NSA_KV_HEADS):
            ksl = slice(g * hd, (g + 1) * hd)
            o_cmp, imp = _cmp_branch(stack(qn_ref, g), kvc_ref[0, 0][:, ksl], kvc_ref[0, 1][:, ksl], tpos, tq)
            imp = jnp.concatenate([imp, jnp.zeros((tq, LANES), F32)], axis=-1)
            sel_ref[g] = _select_blocks(imp, tpos, n_blocks)
            ocmp_ref[g] = o_cmp
            q4 = (stack(qr_ref, g) * NSA_SCALE).astype(BF16)
            zero = jnp.zeros((rows, hd), BF16)
            qbd_ref[g * rows:(g + 1) * rows, :] = jnp.concatenate(
                [q4 if gg == g else zero for gg in range(NSA_KV_HEADS)], axis=1)
        m_ref[...] = jnp.full(m_ref.shape, M_INIT, F32)
        l_ref[...] = jnp.zeros(l_ref.shape, F32)
        acc_ref[...] = jnp.zeros(acc_ref.shape, F32)

    tk = n_step * pages[0].shape[-1]
    k0 = c * tk
    kpos = k0 + lax.broadcasted_iota(jnp.int32, (1, tk), 1)
    kt_all = jnp.concatenate([pg[0, 0, 0].reshape(kvw, -1) for pg in pages], axis=1).astype(BF16)
    vt_all = jnp.concatenate([pg[0, 0, 1].reshape(kvw, -1) for pg in pages], axis=1).astype(BF16)
    sel_all = jnp.concatenate([sel_ref[g][:, :n_cmp] for g in range(NSA_KV_HEADS)], axis=0).astype(BF16)
    picked = _dot(sel_all, expand_ref[...])
    tpos_all = jnp.concatenate([tpos] * NSA_KV_HEADS, axis=0)
    bias = jnp.where((picked > 0.5) & (kpos <= tpos_all), 0.0, NEG_INF)
    bias = jnp.concatenate([bias[g * tq:(g + 1) * tq] for g in range(NSA_KV_HEADS) for _ in range(NSA_GROUP)],
                           axis=0)
    s = _dot(qbd_ref[...], kt_all) + bias
    m_old = m_ref[...]
    m_new = jnp.maximum(m_old, jnp.max(s, axis=-1, keepdims=True))
    alpha = jnp.exp(m_old - m_new)
    p = jnp.exp(s - m_new)
    m_ref[...] = m_new
    l_ref[...] = alpha * l_ref[...] + jnp.sum(p, axis=-1, keepdims=True)
    acc_ref[...] = alpha * acc_ref[...] + _dot_nt(p.astype(BF16), vt_all)

    @pl.when(c == pl.num_programs(1) - 1)
    def _():
        gate = gate_ref[0]
        rnew = lax.broadcasted_iota(jnp.int32, (1, tq), 1)
        newpos = past_len + rnew
        n_buf = wbuf_ref.shape[-1]
        bpos = past_len - n_buf + lax.broadcasted_iota(jnp.int32, (1, n_buf), 1)
        new_lane = 2 * (past_len // SEL_BLOCK)
        heads = []
        for g in range(NSA_KV_HEADS):
            q4 = (stack(qr_ref, g) * NSA_SCALE).astype(BF16)
            ks, vs = slice(g * hd, (g + 1) * hd), slice(kvw + g * hd, kvw + (g + 1) * hd)
            picked = sel_ref[g][:, new_lane:new_lane + 1] > 0.5
            valid = jnp.where(picked & (newpos <= tpos) & (rnew < t_real), 1.0, 0.0)
            gr = slice(g * rows, (g + 1) * rows)
            _, l_s, acc_s = _flash_step(q4, snew_ref[0, :, ks], snew_ref[0, :, vs], valid,
                                        (m_ref[gr, :], l_ref[gr, :], acc_ref[gr, ks]), False)
            dist = tpos - bpos
            valid = jnp.where((dist >= 0) & (dist <= WINDOW) & (bpos >= 0), 1.0, 0.0)
            init = (jnp.full((rows, 1), M_INIT, F32), jnp.zeros((rows, 1), F32), jnp.zeros((rows, hd), F32))
            carry = _flash_step(q4, wbuf_ref[0, 0, 0, g], wbuf_ref[0, 0, 1, g], valid, init, True)
            dist = tpos - newpos
            valid = jnp.where((dist >= 0) & (dist <= WINDOW) & (rnew < t_real), 1.0, 0.0)
            _, l_w, acc_w = _flash_step(q4, wnew_ref[0, :, ks], wnew_ref[0, :, vs], valid, carry, False)
            o_cmp = ocmp_ref[g]
            o_slc = acc_s / l_s
            o_win = acc_w / l_w
            for hh in range(NSA_GROUP):
                head = g * NSA_GROUP + hh
                hr = slice(hh * tq, (hh + 1) * tq)
                col = (head // (NSA_HEADS // 2)) * LANES + (head % (NSA_HEADS // 2)) * 3
                heads.append(o_cmp[hr] * gate[:, col:col + 1] + o_slc[hr] * gate[:, col + 1:col + 2]
                             + o_win[hr] * gate[:, col + 2:col + 3])
        o_ref[0] = jnp.concatenate(heads, axis=-1)


SAMPLE_PAGES_PER_STEP = 8


def _attn_sample_call(qn, qr, kvc, gate, slc_new, win_buf_t, win_new, pool_t, layer, page_table, t_real):
    b, tq, nq = qn.shape
    n_pages = page_table.shape[1]
    page = pool_t.shape[-1]
    past_len = n_pages * page
    assert past_len % SEL_BLOCK == 0 and t_real <= SEL_BLOCK and past_len % (CMP_BLOCK * LANES) == 0
    n = SAMPLE_PAGES_PER_STEP
    rows = NSA_GROUP * tq
    full = lambda a: pl.BlockSpec((1,) + a.shape[1:], lambda i, c, pt: (i,) + (0,) * (a.ndim - 1))
    wbuf_spec = pl.BlockSpec((1, 1) + win_buf_t.shape[2:], lambda i, c, pt: (layer, i, 0, 0, 0, 0))
    n_cmp = past_len // CMP_BLOCK
    sel_lanes = n_cmp + LANES
    expand = (jnp.arange(n_cmp)[:, None] == (jnp.arange(past_len) // CMP_BLOCK)[None, :]).astype(BF16)
    return pl.pallas_call(
        functools.partial(_attn_sample_kernel, n, t_real, past_len),
        grid_spec=pltpu.PrefetchScalarGridSpec(
            num_scalar_prefetch=1,
            grid=(b, n_pages // n),
            in_specs=[full(a) for a in (qn, qr, kvc, gate, slc_new)] + [wbuf_spec, full(win_new)]
            + [pl.BlockSpec((n_cmp, n * page), lambda i, c, pt: (0, c))]
            + _page_specs(n, layer, pool_t.shape[2:]),
            out_specs=pl.BlockSpec((1, tq, nq), lambda i, c, pt: (i, 0, 0)),
            scratch_shapes=[pltpu.VMEM((NSA_KV_HEADS, tq, sel_lanes), F32),
                            pltpu.VMEM((NSA_KV_HEADS, rows, NSA_HEAD_DIM), F32),
                            pltpu.VMEM((NSA_KV_HEADS * rows, 1), F32),
                            pltpu.VMEM((NSA_KV_HEADS * rows, 1), F32),
                            pltpu.VMEM((NSA_KV_HEADS * rows, NSA_KV_WIDTH), F32),
                            pltpu.VMEM((NSA_KV_HEADS * rows, NSA_KV_WIDTH), BF16)],
        ),
        out_shape=jax.ShapeDtypeStruct((b, tq, nq), F32),
        compiler_params=_cparams(2),
        name="nsa_attn_sample",
    )(page_table, qn, qr, kvc, gate, slc_new, win_buf_t, win_new, expand, *([pool_t] * n))


def _rope_tables(pos):
    half = NSA_HEAD_DIM // 2
    inv = ROPE_THETA ** (-jnp.arange(half, dtype=F32) / half)
    ang = pos.astype(F32)[:, None] * inv[None, :]
    cos, sin = jnp.cos(ang), jnp.sin(ang)
    reps = LANES // NSA_HEAD_DIM
    return jnp.tile(cos, (1, 2 * reps)), jnp.tile(jnp.concatenate([-sin, sin], axis=1), (1, reps))


def _nsa_params(w_in, g_q, g_ks, g_kw, g_kc, pe, w1, w2):
    d = w_in.shape[0]
    body = NSA_HEADS * NSA_HEAD_DIM + 6 * NSA_KV_WIDTH
    half = NSA_HEADS // 2 * 3
    zpad = jnp.zeros((d, LANES - half), w_in.dtype)
    w_pad = jnp.concatenate([w_in[:, :body], w_in[:, body:body + half], zpad, w_in[:, body + half:], zpad], axis=1)
    return dict(
        w_in=w_pad.astype(BF16),
        g_q=jnp.tile(g_q, NSA_HEADS)[None], g_ks=jnp.tile(g_ks, NSA_KV_HEADS)[None],
        g_kw=jnp.tile(g_kw, NSA_KV_HEADS)[None], g_kc=g_kc[None],
        pe=pe.reshape(2, CMP_BLOCK // 2, LANES), w1=w1.astype(BF16), w2=w2.astype(BF16))


def _kv_rows(x, b, t):
    return x.reshape(b, t, 2, NSA_KV_HEADS, NSA_HEAD_DIM)


def _nsa_prompt_layer(h, gmix, prm):
    b, t, d = h.shape
    cos, sin = _rope_tables(jnp.arange(t))
    qn, qr, cmp, slc, wrow, gate = _nsa_proj_call(
        h.reshape(b * t, d), gmix, prm["w_in"], prm["g_q"], prm["g_ks"], prm["g_kw"], cos, sin)
    r3 = lambda x: x.reshape(b, t, x.shape[-1])
    kvc = _compress_rows_call(r3(cmp), prm["pe"], prm["w1"], prm["w2"], prm["g_kc"])
    o = _attn_prompt_call(r3(qn), r3(qr), kvc, r3(slc), r3(wrow), r3(gate))
    keep = min(WINDOW, t)
    return (o.reshape(b * t, -1), _kv_rows(cmp, b, t), _kv_rows(slc, b, t), _kv_rows(wrow, b, t)[:, t - keep:])


def _nsa_sample_layer(h, cmp_pool_t, slc_pool_t, win_buf, win_buf_t, layer, page_table, gmix, prm):
    b, t, d = h.shape
    past_len = page_table.shape[1] * cmp_pool_t.shape[-1]
    cos, sin = _rope_tables(jnp.tile(past_len + jnp.arange(t), b))
    qn, qr, cmp, slc, wrow, gate = _nsa_proj_call(
        h.reshape(b * t, d), gmix, prm["w_in"], prm["g_q"], prm["g_ks"], prm["g_kw"], cos, sin)
    kvc = _compress_pages_call(cmp_pool_t, layer, page_table, prm["pe"], prm["w1"], prm["w2"], prm["g_kc"])
    tq = 8
    pad = lambda x: jnp.pad(x.reshape(b, t, x.shape[-1]), ((0, 0), (0, tq - t), (0, 0)))
    o = _attn_sample_call(pad(qn), pad(qr), kvc, pad(gate), pad(slc), win_buf_t, pad(wrow), slc_pool_t, layer,
                          page_table, t)
    new_win = jnp.concatenate([win_buf, _kv_rows(wrow, b, t)], axis=1)[:, t:]
    return o[:, :t].reshape(b * t, -1), _kv_rows(cmp, b, t), _kv_rows(slc, b, t), new_win


def _softplus(z):
    return jnp.maximum(z, 0.0) + jnp.log1p(jnp.exp(-jnp.abs(z)))


def _rwkv_proj_kernel(batch, h_ref, shift_ref, gmix_ref, mu_ref, wr_ref, wk_ref, wv_ref,
                      w0_ref, w1_ref, w2_ref, a0_ref, a1_ref, a2_ref, g1_ref, g2_ref,
                      r_ref, dec_ref, k_ref, v_ref, a_ref, g_ref, last_ref, carry_ref):
    i = pl.program_id(0)
    u = _rms(h_ref[...]) * gmix_ref[...]
    tm = u.shape[0]

    @pl.when(i == 0)
    def _():
        carry_ref[...] = shift_ref[...]
    prev = jnp.concatenate([carry_ref[...], u[:tm - batch]], axis=0)
    carry_ref[...] = u[tm - batch:]
    last_ref[...] = u[tm - batch:]
    xx = prev - u
    mix = lambda j: (u + xx * mu_ref[j:j + 1]).astype(BF16)
    r_ref[...] = _dot(mix(0), wr_ref[...])
    wl = w0_ref[...] + _dot(jnp.tanh(_dot(mix(1), w1_ref[...])).astype(BF16), w2_ref[...])
    dec_ref[...] = jnp.exp(-jnp.exp(-_softplus(-wl) - 0.5))
    k_ref[...] = _dot(mix(2), wk_ref[...])
    v_ref[...] = _dot(mix(3), wv_ref[...])
    a_ref[...] = jax.nn.sigmoid(a0_ref[...] + _dot(_dot(mix(4), a1_ref[...]).astype(BF16), a2_ref[...]))
    g_ref[...] = _dot(jax.nn.sigmoid(_dot(mix(5), g1_ref[...])).astype(BF16), g2_ref[...])


def _rwkv_proj_call(h, shift, gmix, prm):
    m, d = h.shape
    batch = shift.shape[0]
    tm = min(m, 256)
    assert tm % batch == 0 and tm > batch and batch % 8 == 0
    row = pl.BlockSpec((tm, d), lambda i: (i, 0))
    weights = [prm[n] for n in ("mu", "w_r", "w_k", "w_v", "w0", "w1", "w2", "a0", "a1", "a2", "g1", "g2")]
    outs = pl.pallas_call(
        functools.partial(_rwkv_proj_kernel, batch),
        grid=(m // tm,),
        in_specs=[row, _resident(shift.shape), _resident(gmix.shape)] + [_resident(w.shape) for w in weights],
        out_specs=[row] * 6 + [pl.BlockSpec((batch, d), lambda i: (0, 0))],
        out_shape=[jax.ShapeDtypeStruct((m, d), F32)] * 6 + [jax.ShapeDtypeStruct((batch, d), F32)],
        scratch_shapes=[pltpu.VMEM((batch, d), F32)],
        compiler_params=_cparams(1),
        name="rwkv_proj",
    )(h, shift, gmix, *weights)
    return outs[:6], outs[6]


def _rwkv_scan_kernel(tc, r_ref, dec_ref, k_ref, v_ref, a_ref, kk_ref, ka_ref, rk_ref, lw_ref, lb_ref, s0_ref,
                      y_ref, sout_ref, state_ref, vec_ref, stage_ref, ystage_ref):
    c = pl.program_id(1)
    n = RWKV_HEAD_DIM

    @pl.when(c == 0)
    def _():
        state_ref[...] = s0_ref[...]
        ystage_ref[...] = jnp.zeros_like(ystage_ref)

    def stage_in(t, slot):
        for i, ref in enumerate((r_ref, dec_ref, k_ref, v_ref, a_ref)):
            xa, xb = _rows_to_lanes(ref[t], ref[t + 1])
            stage_ref[i, 2 * slot] = xa
            stage_ref[i, 2 * slot + 1] = xb

    def step(t):
        r = stage_ref[0, t]
        k = stage_ref[2, t]
        v = stage_ref[3, t]
        a = stage_ref[4, t]
        kk = k * kk_ref[...]
        kk = kk / jnp.maximum(jnp.sqrt(jnp.sum(kk * kk, axis=0, keepdims=True)), 1e-12)
        k2 = k * (1.0 + (a - 1.0) * ka_ref[...])
        vec_ref[0] = -kk
        vec_ref[1] = kk * a
        vec_ref[2] = k2
        sa = jnp.zeros((n, LANES), F32)
        for j in range(n):
            sa = sa + state_ref[j] * vec_ref[0, j:j + 1, :]
        y = jnp.zeros((n, LANES), F32)
        for j in range(n):
            s_new = (state_ref[j] * stage_ref[1, t, j:j + 1, :] + sa * vec_ref[1, j:j + 1, :]
                     + v * vec_ref[2, j:j + 1, :])
            state_ref[j] = s_new
            y = y + s_new * stage_ref[0, t, j:j + 1, :]
        mean = jnp.mean(y, axis=0, keepdims=True)
        var = jnp.mean(jnp.square(y - mean), axis=0, keepdims=True)
        y = (y - mean) * lax.rsqrt(var + LNX_EPS) * lw_ref[...] + lb_ref[...]
        ystage_ref[t] = y + jnp.sum(r * k2 * rk_ref[...], axis=0, keepdims=True) * v

    def steps(slot):
        step(2 * slot)
        step(2 * slot + 1)

    def stage_out(slot, t):
        _lanes_to_rows(ystage_ref[2 * slot], ystage_ref[2 * slot + 1], y_ref, t)

    _pipelined_pairs(tc, stage_in, steps, stage_out)

    @pl.when(c == pl.num_programs(1) - 1)
    def _():
        sout_ref[...] = state_ref[...]


def _pipelined_pairs(tc, stage_in, steps, stage_out):
    stage_in(0, 0)

    def body(i, _):
        t = 4 * i
        stage_in(t + 2, 1)
        steps(0)
        stage_out(1, jnp.maximum(t - 2, 0))
        stage_in(jnp.minimum(t + 4, tc - 2), 0)
        steps(1)
        stage_out(0, t)
        return 0

    lax.fori_loop(0, tc // 4, body, 0)
    stage_out(1, tc - 2)


def _rows_to_lanes(xa, xb):
    chunks = xa.shape[1] // LANES
    stack = jnp.concatenate([x[:, c * LANES:(c + 1) * LANES] for x in (xa, xb) for c in range(chunks)], axis=0)
    tr = stack.T
    half = LANES // 2
    top, bot = tr[:half], tr[half:]
    low = _lane_iota() < half
    return (jnp.where(low, top, pltpu.roll(bot, half, axis=1)),
            jnp.where(low, pltpu.roll(top, half, axis=1), bot))


def _lanes_to_rows(ya, yb, out_ref, t):
    half = LANES // 2
    low = _lane_iota() < half
    top = jnp.where(low, ya, pltpu.roll(yb, half, axis=1))
    bot = jnp.where(low, pltpu.roll(ya, half, axis=1), yb)
    tr = jnp.concatenate([top, bot], axis=0).T
    chunks = out_ref.shape[2] // LANES
    for tok in range(2):
        for c in range(chunks):
            r0 = (tok * chunks + c) * 8
            out_ref[t + tok, :, c * LANES:(c + 1) * LANES] = tr[r0:r0 + 8]


def _lane_params(x, heads, width):
    per_chunk = LANES // width if width < LANES else 1
    x = x.reshape(heads // per_chunk, per_chunk, width)
    x = jnp.transpose(x, (2, 1, 0))
    return jnp.repeat(x[..., None], 8, axis=-1).reshape(width, LANES)


def _rwkv_scan_call(r, dec, k, v, a, kk, ka, rk, lw, lb, s0):
    t, b, d = r.shape
    n = RWKV_HEAD_DIM
    assert d == 8 * LANES and b % 8 == 0 and t % 4 == 0
    tc = math.gcd(t, 32)
    seq = pl.BlockSpec((tc, 8, d), lambda g, c: (c, g, 0))
    par = pl.BlockSpec((n, LANES), lambda g, c: (0, 0))
    st = pl.BlockSpec((n, n, LANES), lambda g, c: (0, 0, g))
    return pl.pallas_call(
        functools.partial(_rwkv_scan_kernel, tc),
        grid=(b // 8, t // tc),
        in_specs=[seq] * 5 + [par] * 5 + [st],
        out_specs=[seq, st],
        out_shape=[jax.ShapeDtypeStruct((t, b, d), F32), jax.ShapeDtypeStruct((n, n, b // 8 * LANES), F32)],
        scratch_shapes=[pltpu.VMEM((n, n, LANES), F32), pltpu.VMEM((3, n, LANES), F32),
                        pltpu.VMEM((5, 4, n, LANES), F32), pltpu.VMEM((4, n, LANES), F32)],
        compiler_params=_cparams(2),
        name="rwkv_scan",
    )(r, dec, k, v, a, kk, ka, rk, lw, lb, s0)


def _rwkv_params(mu, w_r, w_k, w_v, w0, w1, w2, a0, a1, a2, g1, g2):
    gpad = -g1.shape[1] % LANES
    return dict(mu=mu, w_r=w_r.astype(BF16), w_k=w_k.astype(BF16), w_v=w_v.astype(BF16), w0=w0[None],
                w1=w1.astype(BF16), w2=w2.astype(BF16), a0=a0[None], a1=a1.astype(BF16), a2=a2.astype(BF16),
                g1=jnp.pad(g1, ((0, 0), (0, gpad))).astype(BF16), g2=jnp.pad(g2, ((0, gpad), (0, 0))).astype(BF16))


def _rwkv_layer(h, batch, shift, s0, gmix, prm, k_k, k_a, r_k, lnx_w, lnx_b):
    m, d = h.shape
    t = m // batch
    n = RWKV_HEAD_DIM
    nh = d // n
    g8 = batch // 8
    (r, dec, k, v, a, g), new_shift = _rwkv_proj_call(h, shift, gmix, prm)
    rows = lambda x: x.reshape(t, batch, d)
    par = lambda x: _lane_params(x.reshape(-1), nh, n)
    st = s0.reshape(g8, 8, nh // 2, 2, n, n).transpose(5, 4, 0, 3, 2, 1).reshape(n, n, g8 * LANES)
    y, s = _rwkv_scan_call(rows(r), rows(dec), rows(k), rows(v), rows(a), par(k_k), par(k_a), par(r_k),
                           par(lnx_w), par(lnx_b), st)
    s = s.reshape(n, n, g8, 2, nh // 2, 8).transpose(2, 5, 4, 3, 1, 0).reshape(batch, nh, n, n)
    return y.reshape(m, d), g, new_shift, s


def _hgrn_proj_kernel(h_ref, gmix_ref, win_ref, lb_ref, omlb_ref, q_ref, f_ref, k_ref, i_ref, g_ref):
    d = h_ref.shape[1]
    u = (_rms(h_ref[...]) * gmix_ref[...]).astype(BF16)
    z = _dot(u, win_ref[...])
    q, f, g = z[:, :d], z[:, d:2 * d], z[:, 3 * d:]
    q_ref[...] = q * jax.nn.sigmoid(q)
    f_ref[...] = lb_ref[...] + omlb_ref[...] * jax.nn.sigmoid(f)
    k_ref[...] = omlb_ref[...] * jax.nn.sigmoid(-f)
    i_ref[...] = z[:, 2 * d:3 * d]
    g_ref[...] = g * jax.nn.sigmoid(g)


def _hgrn_proj_call(h, gmix, win, lb, omlb):
    m, d = h.shape
    tm = min(m, 256)
    row = pl.BlockSpec((tm, d), lambda i: (i, 0))
    return pl.pallas_call(
        _hgrn_proj_kernel,
        grid=(m // tm,),
        in_specs=[row, _resident(gmix.shape), _resident(win.shape), _resident(lb.shape), _resident(omlb.shape)],
        out_specs=[row] * 5,
        out_shape=[jax.ShapeDtypeStruct((m, d), F32)] * 5,
        compiler_params=_cparams(1),
        name="hgrn_proj",
    )(h, gmix, win, lb, omlb)


def _rows_to_lanes_dup(xa, xb):
    chunks = xa.shape[1] // LANES
    stack = jnp.concatenate([x[:, c * LANES:(c + 1) * LANES] for x in (xa, xb) for c in range(chunks)], axis=0)
    tr = stack.T
    half = LANES // 2
    low = _lane_iota() < half
    other = pltpu.roll(tr, half, axis=1)
    return jnp.where(low, tr, other), jnp.where(low, other, tr)


def _hgrn_scan_kernel(tc, q_ref, f_ref, k_ref, v_ref, s0_ref, o_ref, sout_ref, state_ref, keys_ref, vals_ref):
    c = pl.program_id(1)
    dk = state_ref.shape[0]

    @pl.when(c == 0)
    def _():
        state_ref[...] = s0_ref[...]

    def stage_in(tp, _):
        t = 2 * tp
        for i, ref in enumerate((q_ref, f_ref, k_ref)):
            xa, xb = _rows_to_lanes_dup(ref[t], ref[t + 1])
            keys_ref[i, t] = xa
            keys_ref[i, t + 1] = xb
        va, vb = _rows_to_lanes(v_ref[t], v_ref[t + 1])
        vals_ref[t] = va
        vals_ref[t + 1] = vb
        return 0

    lax.fori_loop(0, tc // 2, stage_in, 0)

    def step(t, _):
        v = vals_ref[t]
        o = jnp.zeros(v.shape, F32)
        for d in range(dk):
            s_new = state_ref[d] * keys_ref[1, t, d:d + 1, :] + keys_ref[2, t, d:d + 1, :] * v
            state_ref[d] = s_new
            o = o + s_new * keys_ref[0, t, d:d + 1, :]
        vals_ref[t] = o
        return 0

    lax.fori_loop(0, tc, step, 0)

    def stage_out(tq, _):
        for u in range(2):
            t = 4 * tq + 2 * u
            _lanes_to_rows(vals_ref[t], vals_ref[t + 1], o_ref, t)
        return 0

    lax.fori_loop(0, tc // 4, stage_out, 0)

    @pl.when(c == pl.num_programs(1) - 1)
    def _():
        sout_ref[...] = state_ref[...]


def _hgrn_scan_call(q, f, k, v, s0):
    t, b, d = q.shape
    dk = HGRN_HEAD_DIM
    dv = dk // 2
    assert d == 8 * LANES and b % 8 == 0 and t % 4 == 0
    tc = math.gcd(t, 32)
    seq = pl.BlockSpec((tc, 8, d), lambda g, c: (c, g, 0))
    st = pl.BlockSpec((dk, dv, LANES), lambda g, c: (0, 0, g))
    return pl.pallas_call(
        functools.partial(_hgrn_scan_kernel, tc),
        grid=(b // 8, t // tc),
        in_specs=[seq] * 4 + [st],
        out_specs=[seq, st],
        out_shape=[jax.ShapeDtypeStruct((t, b, d), F32), jax.ShapeDtypeStruct((dk, dv, b // 8 * LANES), F32)],
        scratch_shapes=[pltpu.VMEM((dk, dv, LANES), F32), pltpu.VMEM((3, tc, dk, LANES), F32),
                        pltpu.VMEM((tc, dv, LANES), F32)],
        compiler_params=_cparams(2),
        name="hgrn_scan",
    )(q, f, k, v, s0)


def _hgrn_layer(h, batch, s0, gmix, win, lb, omlb):
    m, d = h.shape
    t = m // batch
    n = HGRN_HEAD_DIM
    nh = d // n
    g8 = batch // 8
    q, f, k, i, g = _hgrn_proj_call(h, gmix, win, lb, omlb)
    rows = lambda x: x.reshape(t, batch, d)
    st = s0.reshape(g8, 8, nh, n, 2, n // 2).transpose(3, 5, 0, 4, 2, 1).reshape(n, n // 2, g8 * LANES)
    o, s = _hgrn_scan_call(rows(q), rows(f), rows(k), rows(i), st)
    s = s.reshape(n, n // 2, g8, 2, nh, 8).transpose(2, 5, 4, 0, 3, 1).reshape(batch, nh, n, n)
    return o.reshape(m, d), g, s


def kernel(x_prompt, x_sample, cache_cmp, cache_slc, cache_win, state_rwkv_shift, state_rwkv_wkv, state_hgrn,
           page_table, p_prompt, p_sample, norm_mix, norm_ffn, w_up, w_down, w_ple, w_ple_gate,
           nsa_w_in, nsa_g_q, nsa_g_ks, nsa_g_kw, nsa_g_kc, nsa_cmp_pe, nsa_cmp_w1, nsa_cmp_w2, nsa_w_out,
           rwkv_mu, rwkv_w_r, rwkv_w_k, rwkv_w_v, rwkv_w_o, rwkv_w0, rwkv_w1, rwkv_w2, rwkv_a0, rwkv_a1,
           rwkv_a2, rwkv_g1, rwkv_g2, rwkv_k_k, rwkv_k_a, rwkv_r_k, rwkv_lnx_w, rwkv_lnx_b,
           hgrn_w_in, hgrn_gn, hgrn_w_o, hgrn_lower_bounds):
    depth = norm_mix.shape[0]
    bp, tp, d = x_prompt.shape
    bs, ts, _ = x_sample.shape
    lb_soft = jax.nn.softmax(hgrn_lower_bounds.astype(F32), axis=0)
    lower_bound = jnp.cumsum(lb_soft, axis=0) - lb_soft[0]
    rows_minor = lambda x: jnp.transpose(x, (0, 1, 3, 4, 5, 2))
    cmp_pool_t, slc_pool_t, win_buf_t = rows_minor(cache_cmp), rows_minor(cache_slc), rows_minor(cache_win)
    hp, hs = x_prompt.reshape(bp * tp, d), x_sample.reshape(bs * ts, d)
    swap = lambda x, a, b: x.reshape(a, b, x.shape[-1]).transpose(1, 0, 2).reshape(a * b, x.shape[-1])
    time_major = False
    outs = [[] for _ in range(12)]
    for i in range(depth):
        kind, n = i % N_MIXERS, i // N_MIXERS
        gmix = norm_mix[i][None]
        ffn = (p_prompt[i].reshape(bp * tp, -1), p_sample[i].reshape(bs * ts, -1))
        if (kind != 0) != time_major:
            hp = swap(hp, tp, bp) if time_major else swap(hp, bp, tp)
            hs = swap(hs, ts, bs) if time_major else swap(hs, bs, ts)
            time_major = not time_major
        if time_major:
            ffn = (swap(ffn[0], bp, tp), swap(ffn[1], bs, ts))
        hp3, hs3 = hp.reshape(bp, tp, d), hs.reshape(bs, ts, d)
        tail = (norm_ffn[i][None], w_up[i].astype(BF16), w_down[i].astype(BF16), w_ple[i].astype(BF16),
                w_ple_gate[i].astype(BF16))
        if kind == 0:
            prm = _nsa_params(nsa_w_in[n], nsa_g_q[n], nsa_g_ks[n], nsa_g_kw[n], nsa_g_kc[n], nsa_cmp_pe[n],
                              nsa_cmp_w1[n], nsa_cmp_w2[n])
            op, rc, rs, rw = _nsa_prompt_layer(hp3, gmix, prm)
            os_, nc_rows, ns_rows, nw_buf = _nsa_sample_layer(hs3, cmp_pool_t, slc_pool_t, cache_win[n], win_buf_t,
                                                              n, page_table, gmix, prm)
            for lst, v in zip(outs[:6], (rc, nc_rows, rs, ns_rows, rw, nw_buf)):
                lst.append(v)
            wo = nsa_w_out[n].astype(BF16)
            hp = _ffn_call("nsa", hp, op, None, None, ffn[0], wo, *tail)
            hs = _ffn_call("nsa", hs, os_, None, None, ffn[1], wo, *tail)
        elif kind == 1:
            prm = _rwkv_params(rwkv_mu[n], rwkv_w_r[n], rwkv_w_k[n], rwkv_w_v[n], rwkv_w0[n], rwkv_w1[n],
                               rwkv_w2[n], rwkv_a0[n], rwkv_a1[n], rwkv_a2[n], rwkv_g1[n], rwkv_g2[n])
            vecs = (rwkv_k_k[n], rwkv_k_a[n], rwkv_r_k[n], rwkv_lnx_w[n], rwkv_lnx_b[n])
            nh = d // RWKV_HEAD_DIM
            zero_state = jnp.zeros((bp, nh, RWKV_HEAD_DIM, RWKV_HEAD_DIM), F32)
            yp, gp, shp, sp = _rwkv_layer(hp, bp, jnp.zeros((bp, d), F32), zero_state, gmix, prm, *vecs)
            ys, gs, shs, ss = _rwkv_layer(hs, bs, state_rwkv_shift[n], state_rwkv_wkv[n].astype(F32), gmix, prm,
                                          *vecs)
            for lst, v in zip(outs[6:10], (shp, shs, sp, ss)):
                lst.append(v)
            wo = rwkv_w_o[n].astype(BF16)
            hp = _ffn_call("rwkv", hp, yp, gp, None, ffn[0], wo, *tail)
            hs = _ffn_call("rwkv", hs, ys, gs, None, ffn[1], wo, *tail)
        else:
            lb = lower_bound[i][None]
            win = hgrn_w_in[n].astype(BF16)
            nh = d // HGRN_HEAD_DIM
            zero_state = jnp.zeros((bp, nh, HGRN_HEAD_DIM, HGRN_HEAD_DIM), F32)
            op, gp, sp = _hgrn_layer(hp, bp, zero_state, gmix, win, lb, 1.0 - lb)
            os_, gs, ss = _hgrn_layer(hs, bs, state_hgrn[n].astype(F32), gmix, win, lb, 1.0 - lb)
            outs[10].append(sp)
            outs[11].append(ss)
            wo, gn = hgrn_w_o[n].astype(BF16), hgrn_gn[n][None]
            hp = _ffn_call("hgrn", hp, op, gp, gn, ffn[0], wo, *tail)
            hs = _ffn_call("hgrn", hs, os_, gs, gn, ffn[1], wo, *tail)
    if time_major:
        hp, hs = swap(hp, tp, bp), swap(hs, ts, bs)
    return (hp.reshape(bp, tp, d), hs.reshape(bs, ts, d)) + tuple(jnp.stack(o) for o in outs)
```

```python
import functools
import math

import jax
import jax.numpy as jnp
from jax import lax
from jax.experimental import pallas as pl
from jax.experimental.pallas import tpu as pltpu

F32 = jnp.float32
BF16 = jnp.bfloat16

NORM_EPS = 1e-6
ROPE_THETA = 10000.0
NEG_INF = -1e30
M_INIT = -1e29
N_MIXERS = 3

NSA_HEADS = 16
NSA_KV_HEADS = 4
NSA_HEAD_DIM = 64
NSA_GROUP = NSA_HEADS // NSA_KV_HEADS
NSA_KV_WIDTH = NSA_KV_HEADS * NSA_HEAD_DIM
NSA_SCALE = NSA_HEAD_DIM ** -0.5
LOG2E = math.log2(math.e)
CMP_BLOCK = 32
SEL_BLOCK = 64
N_SELECT = 16
WINDOW = 512
FORCE_SCORE = 1e4

RWKV_HEAD_DIM = 64
LNX_EPS = 64e-5
HGRN_HEAD_DIM = 128

LANES = 128
VMEM_LIMIT = 56 * 1024 * 1024


def _cparams(n_axes):
    return pltpu.CompilerParams(dimension_semantics=("arbitrary",) * n_axes,
                                vmem_limit_bytes=VMEM_LIMIT)


def _resident(shape):
    zeros = (0,) * len(shape)
    return pl.BlockSpec(shape, lambda *_: zeros, pipeline_mode=pl.Buffered(1))


def _rms(x):
    return x * lax.rsqrt(jnp.mean(x * x, axis=-1, keepdims=True) + NORM_EPS)


def _dot(a, b):
    return jnp.dot(a, b, preferred_element_type=F32)


def _dot_nt(a, b):
    return lax.dot_general(a, b, (((1,), (1,)), ((), ())), preferred_element_type=F32)


def _per_chunk(fn, x, *rest):
    n = x.shape[-1] // LANES
    outs = [fn(*(a[:, c * LANES:(c + 1) * LANES] for a in (x,) + rest)) for c in range(n)]
    return outs[0] if n == 1 else jnp.concatenate(outs, axis=-1)


def _lane_iota():
    return lax.broadcasted_iota(jnp.int32, (1, LANES), 1)


def _group_ones(group):
    r = lax.broadcasted_iota(jnp.int32, (LANES, LANES), 0) // group
    c = lax.broadcasted_iota(jnp.int32, (LANES, LANES), 1) // group
    return jnp.where(r == c, 1.0, 0.0).astype(BF16)


def _head_rms(x, head_dim, ones):
    def one(c):
        sq = c * c
        hi = sq.astype(BF16)
        lo = (sq - hi.astype(F32)).astype(BF16)
        ss = _dot(hi, ones) + _dot(lo, ones)
        return c * lax.rsqrt(ss * (1.0 / head_dim) + NORM_EPS)
    return _per_chunk(one, x)


def _rope(x, cos, sin_signed):
    half = NSA_HEAD_DIM // 2
    lane = _lane_iota()

    def one(c):
        up = pltpu.roll(c, LANES - half, axis=1)
        dn = pltpu.roll(c, half, axis=1)
        rot = jnp.where((lane & half) == 0, up, dn)
        return c * cos + rot * sin_signed
    return _per_chunk(one, x)


def _ffn_kernel(mode, ff_chunk, *refs):
    if mode == "nsa":
        h_ref, o_ref, p_ref, wo_ref, gffn_ref, wup_ref, wdown_ref, wple_ref, wgate_ref, out_ref = refs
        o = o_ref[...]
    elif mode == "rwkv":
        h_ref, o_ref, aux_ref, p_ref, wo_ref, gffn_ref, wup_ref, wdown_ref, wple_ref, wgate_ref, out_ref = refs
        o = o_ref[...] * aux_ref[...]
    else:
        (h_ref, o_ref, aux_ref, gn_ref, p_ref, wo_ref, gffn_ref, wup_ref, wdown_ref, wple_ref, wgate_ref,
         out_ref) = refs
        gn = gn_ref[...]
        o = _per_chunk(lambda c: _rms(c) * gn, o_ref[...]) * aux_ref[...]
    h1 = h_ref[...] + _dot(o.astype(BF16), wo_ref[...])
    u = (_rms(h1) * gffn_ref[...]).astype(BF16)
    d_ff = wup_ref.shape[1]
    acc = jnp.zeros_like(h1)
    for j in range(d_ff // ff_chunk):
        a = _dot(u, wup_ref[:, j * ff_chunk:(j + 1) * ff_chunk])
        a = jnp.square(jnp.maximum(a, 0.0)).astype(BF16)
        acc = acc + _dot(a, wdown_ref[j * ff_chunk:(j + 1) * ff_chunk, :])
    h2 = h1 + acc
    gate = jax.nn.sigmoid(_dot(_rms(h2).astype(BF16), wgate_ref[...]))
    out_ref[...] = h2 + _dot(p_ref[...].astype(BF16), wple_ref[...]) * gate


def _ffn_call(mode, h, o, aux, gn, p, wo, gffn, wup, wdown, wple, wgate):
    m, d = h.shape
    tm = min(m, 512)
    row = lambda w: pl.BlockSpec((tm, w), lambda i: (i, 0))
    args, specs = [h, o], [row(d), row(d)]
    if mode != "nsa":
        args.append(aux)
        specs.append(row(d))
    if mode == "hgrn":
        args.append(gn)
        specs.append(_resident(gn.shape))
    args += [p, wo, gffn, wup, wdown, wple, wgate]
    specs += [row(p.shape[1])] + [_resident(a.shape) for a in (wo, gffn, wup, wdown, wple, wgate)]
    return pl.pallas_call(
        functools.partial(_ffn_kernel, mode, 1024),
        grid=(m // tm,),
        in_specs=specs,
        out_specs=row(d),
        out_shape=jax.ShapeDtypeStruct((m, d), F32),
        compiler_params=_cparams(1),
        name="ffn_" + mode,
    )(*args)


def _nsa_proj_kernel(h_ref, gmix_ref, win_ref, gq_ref, gks_ref, gkw_ref, cos_ref, sin_ref,
                     qn_ref, qr_ref, cmp_ref, slc_ref, wrow_ref, gate_ref):
    kvw = NSA_KV_WIDTH
    nq = NSA_HEADS * NSA_HEAD_DIM
    u = (_rms(h_ref[...]) * gmix_ref[...]).astype(BF16)
    z = _dot(u, win_ref[...])
    cos = cos_ref[...]
    sin = sin_ref[...]
    ones = _group_ones(NSA_HEAD_DIM)
    qn = _head_rms(z[:, :nq], NSA_HEAD_DIM, ones) * gq_ref[...]
    qn_ref[...] = qn
    qr_ref[...] = _rope(qn, cos, sin)
    cmp_ref[...] = z[:, nq:nq + 2 * kvw]
    o = nq + 2 * kvw
    slc_ref[:, :kvw] = _rope(_head_rms(z[:, o:o + kvw], NSA_HEAD_DIM, ones) * gks_ref[...], cos, sin)
    slc_ref[:, kvw:] = z[:, o + kvw:o + 2 * kvw]
    o += 2 * kvw
    wrow_ref[:, :kvw] = _rope(_head_rms(z[:, o:o + kvw], NSA_HEAD_DIM, ones) * gkw_ref[...], cos, sin)
    wrow_ref[:, kvw:] = z[:, o + kvw:o + 2 * kvw]
    o += 2 * kvw
    gate_ref[...] = jax.nn.sigmoid(z[:, o:])


def _nsa_proj_call(h, gmix, win, gq, gks, gkw, cos, sin):
    m, d = h.shape
    tm = min(m, 256)
    tab_tiles = cos.shape[0] // tm
    row = lambda w: pl.BlockSpec((tm, w), lambda i: (i, 0))
    tab = pl.BlockSpec((tm, LANES), lambda i: (i % tab_tiles, 0))
    nq = NSA_HEADS * NSA_HEAD_DIM
    widths = [nq, nq] + [2 * NSA_KV_WIDTH] * 3 + [2 * LANES]
    return pl.pallas_call(
        _nsa_proj_kernel,
        grid=(m // tm,),
        in_specs=[row(d), _resident(gmix.shape), _resident(win.shape), _resident(gq.shape),
                  _resident(gks.shape), _resident(gkw.shape), tab, tab],
        out_specs=[row(w) for w in widths],
        out_shape=[jax.ShapeDtypeStruct((m, w), F32) for w in widths],
        compiler_params=_cparams(1),
        name="nsa_proj",
    )(h, gmix, win, gq, gks, gkw, cos, sin)


def _compress_slot(load_pair, m, e, pe_ref, w1_ref, w2_ref, gkc_ref, flat_ref):
    lane = _lane_iota()
    low = lane < NSA_HEAD_DIM
    for s in range(CMP_BLOCK // 2):
        pe_row = pe_ref[e, s:s + 1, :]
        parts = []
        for kp in range(NSA_KV_HEADS // 2):
            a = load_pair(kp, 2 * s)
            b = load_pair(kp, 2 * s + 1)
            parts.append(jnp.where(low, a, pltpu.roll(b, NSA_HEAD_DIM, axis=1)))
            parts.append(jnp.where(low, pltpu.roll(a, NSA_HEAD_DIM, axis=1), b))
        flat_ref[:, s * LANES:(s + 1) * LANES] = (jnp.concatenate(parts, axis=0) + pe_row).astype(BF16)
    out = _dot(jax.nn.gelu(_dot(flat_ref[...], w1_ref[e])).astype(BF16), w2_ref[e])
    if e == 0:
        out = _rms(out) * gkc_ref[...]
    return jnp.concatenate([out[k * m:(k + 1) * m] for k in range(NSA_KV_HEADS)], axis=-1)


def _compress_rows_kernel(*refs):
    cols = refs[:NSA_KV_HEADS]
    pe_ref, w1_ref, w2_ref, gkc_ref, out_ref, flat_ref = refs[NSA_KV_HEADS:]
    m = cols[0].shape[1] // CMP_BLOCK
    pad = out_ref.shape[2] - m
    for e in range(2):
        load = lambda kp, r: cols[2 * e + kp][0, pl.ds(r, m, stride=CMP_BLOCK), :]
        res = _compress_slot(load, m, e, pe_ref, w1_ref, w2_ref, gkc_ref, flat_ref)
        out_ref[0, e] = jnp.concatenate([res, jnp.zeros((pad, res.shape[1]), F32)], axis=0)


def _compress_rows_call(cmp, pe, w1, w2, gkc):
    b, t, w = cmp.shape
    c = -(-(t // CMP_BLOCK) // LANES) * LANES
    return pl.pallas_call(
        _compress_rows_kernel,
        grid=(b,),
        in_specs=[pl.BlockSpec((1, t, LANES), functools.partial(lambda j, i: (i, 0, j), j)) for j in range(w // LANES)]
        + [_resident(a.shape) for a in (pe, w1, w2, gkc)],
        out_specs=pl.BlockSpec((1, 2, c, NSA_KV_WIDTH), lambda i: (i, 0, 0, 0)),
        out_shape=jax.ShapeDtypeStruct((b, 2, c, NSA_KV_WIDTH), F32),
        scratch_shapes=[pltpu.VMEM((NSA_KV_HEADS * (t // CMP_BLOCK), w1.shape[1]), BF16)],
        compiler_params=_cparams(1),
        name="nsa_compress_rows",
    )(*([cmp] * (w // LANES)), pe, w1, w2, gkc)


PAGES_PER_STEP = 32


def _page_specs(n, layer, tail):
    zeros = (0,) * len(tail)
    return [pl.BlockSpec((1, 1) + tail,
                         functools.partial(lambda i, b, c, pt: (layer, pt[b, c * n + i]) + zeros, i))
            for i in range(n)]


def _compress_pages_kernel(pt_ref, *refs):
    n = len(refs) - 7
    pages = refs[:n]
    pe_ref, w1_ref, w2_ref, gkc_ref, out_ref, rows_ref, flat_ref = refs[n:]
    page = pages[0].shape[-1]
    m = n * page // CMP_BLOCK
    for e in range(2):
        for i, pg in enumerate(pages):
            for kp in range(NSA_KV_HEADS // 2):
                tile = pg[0, 0, e, 2 * kp:2 * kp + 2].reshape(LANES, page)
                rows_ref[e, kp, i * page:(i + 1) * page, :] = tile.T
    for e in range(2):
        load = lambda kp, r: rows_ref[e, kp, pl.ds(r, m, stride=CMP_BLOCK), :]
        out_ref[0, e] = _compress_slot(load, m, e, pe_ref, w1_ref, w2_ref, gkc_ref, flat_ref.at[e])


def _compress_pages_call(pool_t, layer, page_table, pe, w1, w2, gkc):
    b, n_pages = page_table.shape
    page = pool_t.shape[-1]
    n = PAGES_PER_STEP
    m = n * page // CMP_BLOCK
    return pl.pallas_call(
        _compress_pages_kernel,
        grid_spec=pltpu.PrefetchScalarGridSpec(
            num_scalar_prefetch=1,
            grid=(b, n_pages // n),
            in_specs=_page_specs(n, layer, pool_t.shape[2:])
            + [pl.BlockSpec(a.shape, functools.partial(lambda nd, i, c, pt: (0,) * nd, a.ndim),
                            pipeline_mode=pl.Buffered(1)) for a in (pe, w1, w2, gkc)],
            out_specs=pl.BlockSpec((1, 2, m, NSA_KV_WIDTH), lambda i, c, pt: (i, 0, c, 0)),
            scratch_shapes=[pltpu.VMEM((2, NSA_KV_HEADS // 2, n * page, LANES), F32),
                            pltpu.VMEM((2, NSA_KV_HEADS * m, w1.shape[1]), BF16)],
        ),
        out_shape=jax.ShapeDtypeStruct((b, 2, n_pages * page // CMP_BLOCK, NSA_KV_WIDTH), F32),
        compiler_params=_cparams(2),
        name="nsa_compress_pages",
    )(page_table, *([pool_t] * n), pe, w1, w2, gkc)


def _select_blocks_t(imp, tpos):
    n_blocks = imp.shape[0]
    blk = lax.broadcasted_iota(jnp.int32, (n_blocks, 1), 0)
    cur = tpos // SEL_BLOCK
    forced = (blk == 0) | (blk == cur) | (blk == cur - 1)
    score = jnp.where(blk <= cur, jnp.where(forced, FORCE_SCORE, imp), -1.0)
    rank = jnp.zeros(score.shape, F32)
    for j in range(n_blocks):
        row = score[j:j + 1, :]
        below = jnp.where(j < blk, 1.0, 0.0)
        rank = rank + jnp.where(row > score, 1.0, jnp.where(row == score, below, 0.0))
    return jnp.where(rank < float(min(N_SELECT, n_blocks)), 1.0, 0.0)


def _attn_prompt_kernel(tq, tk, qn_ref, qr_ref, kvc_ref, sk_ref, sv_ref, wk_ref, wv_ref, gate_ref, o_ref,
                        kb_ref, vt_ref, sel_ref, imp_ref):
    qi = pl.program_id(2)
    hd = NSA_HEAD_DIM
    t_len = sk_ref.shape[1]
    c_blocks = kvc_ref.shape[2]
    n_sel = t_len // SEL_BLOCK
    ratio = SEL_BLOCK // CMP_BLOCK
    rows_per_tile = tk // SEL_BLOCK

    @pl.when(qi == 0)
    def _():
        for br, (k_ref, v_ref) in enumerate(((sk_ref, sv_ref), (wk_ref, wv_ref))):
            for gl in range(2):
                kb_ref[br, gl] = k_ref[0, :, gl * hd:(gl + 1) * hd].astype(BF16)
            for kt in range(t_len // tk):
                vt_ref[br, kt] = v_ref[0, kt * tk:(kt + 1) * tk, :].T.astype(BF16)

    t0 = qi * tq
    tpos = t0 + lax.broadcasted_iota(jnp.int32, (1, tq), 1)
    gate_t = gate_ref[0].T
    qn_t = [qn_ref[0, :, c * LANES:(c + 1) * LANES].T for c in range(2 * NSA_GROUP * hd // LANES)]
    qr_t = [qr_ref[0, :, c * LANES:(c + 1) * LANES].T for c in range(2 * NSA_GROUP * hd // LANES)]
    cid = lax.broadcasted_iota(jnp.int32, (c_blocks, 1), 0)
    ok = jnp.where((cid + 1) * CMP_BLOCK - 1 <= tpos, 1.0, 0.0)
    ok4 = jnp.concatenate([ok] * NSA_GROUP, axis=1)
    qr4s, o_cmps = [], []
    for gl in range(2):
        per_head = lambda parts: [parts[2 * gl + hh // 2][(hh % 2) * hd:(hh % 2 + 1) * hd] for hh in range(NSA_GROUP)]
        qn4 = (jnp.concatenate(per_head(qn_t), axis=1) * NSA_SCALE).astype(BF16)
        qr4s.append((jnp.concatenate(per_head(qr_t), axis=1) * (NSA_SCALE * LOG2E)).astype(BF16))
        kc = kvc_ref[0, 0][:, gl * hd:(gl + 1) * hd]
        vc_t = kvc_ref[0, 1].T[gl * hd:(gl + 1) * hd]
        s = jnp.where(ok4 > 0.5, _dot(kc.astype(BF16), qn4), NEG_INF)
        e = jnp.exp(s - jnp.max(s, axis=0, keepdims=True))
        p = e / jnp.sum(e, axis=0, keepdims=True) * ok4
        o_cmps.append(_dot(vc_t.astype(BF16), p.astype(BF16)))
        imp = p[:, :tq]
        for hh in range(1, NSA_GROUP):
            imp = imp + p[:, hh * tq:(hh + 1) * tq]
        parts = []
        for j in range(tq // LANES):
            imp_ref[j] = imp[:, j * LANES:(j + 1) * LANES]
            part = imp_ref[j, pl.ds(0, n_sel, stride=ratio), :]
            for i in range(1, ratio):
                part = part + imp_ref[j, pl.ds(i, n_sel, stride=ratio), :]
            parts.append(part)
        imp_sel = parts[0] if len(parts) == 1 else jnp.concatenate(parts, axis=1)
        sel_ref[gl] = _select_blocks_t(imp_sel, tpos)

    def flash(br, lo, hi, valid_fn):
        def body(kt, carry):
            k0 = pl.multiple_of(kt * tk, tk)
            out = []
            for gl in range(2):
                m, l, acc = carry[gl]
                kb = kb_ref[br, gl, pl.ds(k0, tk), :]
                vt = vt_ref[br, kt, gl * hd:(gl + 1) * hd, :]
                bias = jnp.where(valid_fn(gl, kt, k0), 0.0, NEG_INF)
                s = _dot(kb, qr4s[gl]) + jnp.concatenate([bias] * NSA_GROUP, axis=1)
                m_new = jnp.maximum(m, jnp.max(s, axis=0, keepdims=True))
                alpha = jnp.exp2(m - m_new)
                p = jnp.exp2(s - m_new)
                out.append((m_new, alpha * l + jnp.sum(p, axis=0, keepdims=True),
                            alpha * acc + _dot(vt, p.astype(BF16))))
            return tuple(out)
        cols = NSA_GROUP * tq
        init = tuple((jnp.full((1, cols), M_INIT, F32), jnp.zeros((1, cols), F32), jnp.zeros((hd, cols), F32))
                     for _ in range(2))
        res = lax.fori_loop(lo, hi, body, init)
        return [acc / l for _, l, acc in res]

    def slc_valid(gl, kt, k0):
        picked = sel_ref[gl, pl.ds(pl.multiple_of(kt * rows_per_tile, rows_per_tile), rows_per_tile), :]
        picked = jnp.concatenate([jnp.broadcast_to(picked[i:i + 1], (SEL_BLOCK, tq))
                                  for i in range(rows_per_tile)], axis=0)
        kpos = k0 + lax.broadcasted_iota(jnp.int32, (tk, 1), 0)
        return (picked > 0.5) & (kpos <= tpos)

    def win_valid(gl, kt, k0):
        dist = tpos - (k0 + lax.broadcasted_iota(jnp.int32, (tk, 1), 0))
        return (dist >= 0) & (dist <= WINDOW)

    hi = (t0 + tq - 1) // tk + 1
    slc = flash(0, 0, hi, slc_valid)
    win = flash(1, jnp.maximum(t0 - WINDOW, 0) // tk, hi, win_valid)
    heads = []
    for gl in range(2):
        for hh in range(NSA_GROUP):
            c = (gl * NSA_GROUP + hh) * 3
            cols = slice(hh * tq, (hh + 1) * tq)
            heads.append(o_cmps[gl][:, cols] * gate_t[c:c + 1] + slc[gl][:, cols] * gate_t[c + 1:c + 2]
                         + win[gl][:, cols] * gate_t[c + 2:c + 3])
    for c in range(len(heads) // 2):
        o_ref[0, :, c * LANES:(c + 1) * LANES] = jnp.concatenate(heads[2 * c:2 * c + 2], axis=0).T


def _attn_prompt_call(qn, qr, kvc, slc, win, gate, tq=512, tk=512):
    b, t, nq = qn.shape
    pair_w = 2 * NSA_GROUP * NSA_HEAD_DIM
    c = kvc.shape[2]
    assert tk % (8 * SEL_BLOCK) == 0 and t % tk == 0 and c % LANES == 0 and tq % LANES == 0 and t % tq == 0
    qspec = pl.BlockSpec((1, tq, pair_w), lambda i, p, j: (i, j, p))
    kspec = pl.BlockSpec((1, t, LANES), lambda i, p, j: (i, 0, p))
    vspec = pl.BlockSpec((1, t, LANES), lambda i, p, j: (i, 0, 2 + p))
    return pl.pallas_call(
        functools.partial(_attn_prompt_kernel, tq, tk),
        grid=(b, 2, t // tq),
        in_specs=[qspec, qspec,
                  pl.BlockSpec((1, 2, c, LANES), lambda i, p, j: (i, 0, 0, p)),
                  kspec, vspec, kspec, vspec,
                  pl.BlockSpec((1, tq, LANES), lambda i, p, j: (i, j, p))],
        out_specs=qspec,
        out_shape=jax.ShapeDtypeStruct((b, t, nq), F32),
        scratch_shapes=[pltpu.VMEM((2, 2, t, NSA_HEAD_DIM), BF16),
                        pltpu.VMEM((2, t // tk, LANES, tk), BF16),
                        pltpu.VMEM((2, t // SEL_BLOCK, tq), F32),
                        pltpu.VMEM((tq // LANES, c, LANES), F32)],
        compiler_params=_cparams(3),
        name="nsa_attn_prompt",
    )(qn, qr, kvc, slc, slc, win, win, gate)


def _cmp_branch(q4, kc, vc, tpos, n_tok):
    c = kc.shape[0]
    s = _dot_nt((q4 * NSA_SCALE).astype(BF16), kc.astype(BF16))
    cidx = lax.broadcasted_iota(jnp.int32, (1, c), 1)
    ok = ((cidx + 1) * CMP_BLOCK - 1 <= tpos).astype(F32)
    ok4 = jnp.concatenate([ok] * NSA_GROUP, axis=0)
    s = jnp.where(ok4 > 0.5, s, NEG_INF)
    e = jnp.exp(s - jnp.max(s, axis=-1, keepdims=True))
    p = e / jnp.sum(e, axis=-1, keepdims=True) * ok4
    o = _dot(p.astype(BF16), vc.astype(BF16))
    imp = p[:n_tok]
    for hh in range(1, NSA_GROUP):
        imp = imp + p[hh * n_tok:(hh + 1) * n_tok]
    return o, imp


def _select_blocks(imp, tpos, n_blocks):
    c = imp.shape[1]
    lane = lax.broadcasted_iota(jnp.int32, (1, c), 1)

    def pair(x):
        lane1 = _lane_iota()
        return x + jnp.where((lane1 & 1) == 0, pltpu.roll(x, LANES - 1, axis=1), pltpu.roll(x, 1, axis=1))
    imp2 = _per_chunk(pair, imp)
    blk = lane // 2
    cur = tpos // SEL_BLOCK
    forced = (blk == 0) | (blk == cur) | (blk == cur - 1)
    score = jnp.where(blk <= cur, jnp.where(forced, FORCE_SCORE, imp2), -1.0)
    rank = jnp.zeros(score.shape, F32)
    for j in range(n_blocks):
        col = score[:, 2 * j:2 * j + 1]
        ahead = (col > score) | ((col == score) & (j < blk))
        rank = rank + ahead.astype(F32)
    return (rank < float(min(N_SELECT, n_blocks))).astype(F32)


def _flash_step(q4, k, v, valid, carry, transposed):
    m, l, acc = carry
    valid4 = jnp.concatenate([valid] * NSA_GROUP, axis=0)
    s = _dot(q4, k.astype(BF16)) if transposed else _dot_nt(q4, k.astype(BF16))
    s = jnp.where(valid4 > 0.5, s, NEG_INF)
    m_new = jnp.maximum(m, jnp.max(s, axis=-1, keepdims=True))
    alpha = jnp.exp(m - m_new)
    p = jnp.exp(s - m_new)
    l = alpha * l + jnp.sum(p, axis=-1, keepdims=True)
    pv = _dot_nt(p.astype(BF16), v.astype(BF16)) if transposed else _dot(p.astype(BF16), v.astype(BF16))
    return m_new, l, alpha * acc + pv


def _attn_sample_kernel(n_step, t_real, past_len, pt_ref, qn_ref, qr_ref, kvc_ref, gate_ref, snew_ref, wbuf_ref,
                        wnew_ref, expand_ref, *rest):
    pages = rest[:n_step]
    o_ref, sel_ref, ocmp_ref, m_ref, l_ref, acc_ref, qbd_ref = rest[n_step:]
    c = pl.program_id(1)
    hd = NSA_HEAD_DIM
    kvw = NSA_KV_WIDTH
    tq = qn_ref.shape[1]
    n_cmp = past_len // CMP_BLOCK
    tpos = past_len + lax.broadcasted_iota(jnp.int32, (tq, 1), 0) % t_real
    stack = lambda ref, g: jnp.concatenate(
        [ref[0, :, (g * NSA_GROUP + hh) * hd:(g * NSA_GROUP + hh + 1) * hd] for hh in range(NSA_GROUP)], axis=0)
    rows = NSA_GROUP * tq

    @pl.when(c == 0)
    def _():
        n_blocks = -(-(past_len + t_real) // SEL_BLOCK)
        for g in range(NSA_KV_HEADS):
            ksl = slice(g * hd, (g + 1) * hd)
            o_cmp, imp = _cmp_branch(stack(qn_ref, g), kvc_ref[0, 0][:, ksl], kvc_ref[0, 1][:, ksl], tpos, tq)
            imp = jnp.concatenate([imp, jnp.zeros((tq, LANES), F32)], axis=-1)
            sel_ref[g] = _select_blocks(imp, tpos, n_blocks)
            ocmp_ref[g] = o_cmp
            q4 = (stack(qr_ref, g) * NSA_SCALE).astype(BF16)
            zero = jnp.zeros((rows, hd), BF16)
            qbd_ref[g * rows:(g + 1) * rows, :] = jnp.concatenate(
                [q4 if gg == g else zero for gg in range(NSA_KV_HEADS)], axis=1)
        m_ref[...] = jnp.full(m_ref.shape, M_INIT, F32)
        l_ref[...] = jnp.zeros(l_ref.shape, F32)
        acc_ref[...] = jnp.zeros(acc_ref.shape, F32)

    tk = n_step * pages[0].shape[-1]
    k0 = c * tk
    kpos = k0 + lax.broadcasted_iota(jnp.int32, (1, tk), 1)
    kt_all = jnp.concatenate([pg[0, 0, 0].reshape(kvw, -1) for pg in pages], axis=1).astype(BF16)
    vt_all = jnp.concatenate([pg[0, 0, 1].reshape(kvw, -1) for pg in pages], axis=1).astype(BF16)
    sel_all = jnp.concatenate([sel_ref[g][:, :n_cmp] for g in range(NSA_KV_HEADS)], axis=0).astype(BF16)
    picked = _dot(sel_all, expand_ref[...])
    tpos_all = jnp.concatenate([tpos] * NSA_KV_HEADS, axis=0)
    bias = jnp.where((picked > 0.5) & (kpos <= tpos_all), 0.0, NEG_INF)
    bias = jnp.concatenate([bias[g * tq:(g + 1) * tq] for g in range(NSA_KV_HEADS) for _ in range(NSA_GROUP)],
                           axis=0)
    s = _dot(qbd_ref[...], kt_all) + bias
    m_old = m_ref[...]
    m_new = jnp.maximum(m_old, jnp.max(s, axis=-1, keepdims=True))
    alpha = jnp.exp(m_old - m_new)
    p = jnp.exp(s - m_new)
    m_ref[...] = m_new
    l_ref[...] = alpha * l_ref[...] + jnp.sum(p, axis=-1, keepdims=True)
    acc_ref[...] = alpha * acc_ref[...] + _dot_nt(p.astype(BF16), vt_all)

    @pl.when(c == pl.num_programs(1) - 1)
    def _():
        gate = gate_ref[0]
        rnew = lax.broadcasted_iota(jnp.int32, (1, tq), 1)
        newpos = past_len + rnew
        n_buf = wbuf_ref.shape[-1]
        bpos = past_len - n_buf + lax.broadcasted_iota(jnp.int32, (1, n_buf), 1)
        new_lane = 2 * (past_len // SEL_BLOCK)
        heads = []
        for g in range(NSA_KV_HEADS):
            q4 = (stack(qr_ref, g) * NSA_SCALE).astype(BF16)
            ks, vs = slice(g * hd, (g + 1) * hd), slice(kvw + g * hd, kvw + (g + 1) * hd)
            picked = sel_ref[g][:, new_lane:new_lane + 1] > 0.5
            valid = jnp.where(picked & (newpos <= tpos) & (rnew < t_real), 1.0, 0.0)
            gr = slice(g * rows, (g + 1) * rows)
            _, l_s, acc_s = _flash_step(q4, snew_ref[0, :, ks], snew_ref[0, :, vs], valid,
                                        (m_ref[gr, :], l_ref[gr, :], acc_ref[gr, ks]), False)
            dist = tpos - bpos
            valid = jnp.where((dist >= 0) & (dist <= WINDOW) & (bpos >= 0), 1.0, 0.0)
            init = (jnp.full((rows, 1), M_INIT, F32), jnp.zeros((rows, 1), F32), jnp.zeros((rows, hd), F32))
            carry = _flash_step(q4, wbuf_ref[0, 0, 0, g], wbuf_ref[0, 0, 1, g], valid, init, True)
            dist = tpos - newpos
            valid = jnp.where((dist >= 0) & (dist <= WINDOW) & (rnew < t_real), 1.0, 0.0)
            _, l_w, acc_w = _flash_step(q4, wnew_ref[0, :, ks], wnew_ref[0, :, vs], valid, carry, False)
            o_cmp = ocmp_ref[g]
            o_slc = acc_s / l_s
            o_win = acc_w / l_w
            for hh in range(NSA_GROUP):
                head = g * NSA_GROUP + hh
                hr = slice(hh * tq, (hh + 1) * tq)
                col = (head // (NSA_HEADS // 2)) * LANES + (head % (NSA_HEADS // 2)) * 3
                heads.append(o_cmp[hr] * gate[:, col:col + 1] + o_slc[hr] * gate[:, col + 1:col + 2]
                             + o_win[hr] * gate[:, col + 2:col + 3])
        o_ref[0] = jnp.concatenate(heads, axis=-1)


SAMPLE_PAGES_PER_STEP = 16


def _attn_sample_call(qn, qr, kvc, gate, slc_new, win_buf_t, win_new, pool_t, layer, page_table, t_real):
    b, tq, nq = qn.shape
    n_pages = page_table.shape[1]
    page = pool_t.shape[-1]
    past_len = n_pages * page
    assert past_len % SEL_BLOCK == 0 and t_real <= SEL_BLOCK and past_len % (CMP_BLOCK * LANES) == 0
    n = SAMPLE_PAGES_PER_STEP
    rows = NSA_GROUP * tq
    full = lambda a: pl.BlockSpec((1,) + a.shape[1:], lambda i, c, pt: (i,) + (0,) * (a.ndim - 1))
    wbuf_spec = pl.BlockSpec((1, 1) + win_buf_t.shape[2:], lambda i, c, pt: (layer, i, 0, 0, 0, 0))
    n_cmp = past_len // CMP_BLOCK
    sel_lanes = n_cmp + LANES
    expand = (jnp.arange(n_cmp)[:, None] == (jnp.arange(past_len) // CMP_BLOCK)[None, :]).astype(BF16)
    return pl.pallas_call(
        functools.partial(_attn_sample_kernel, n, t_real, past_len),
        grid_spec=pltpu.PrefetchScalarGridSpec(
            num_scalar_prefetch=1,
            grid=(b, n_pages // n),
            in_specs=[full(a) for a in (qn, qr, kvc, gate, slc_new)] + [wbuf_spec, full(win_new)]
            + [pl.BlockSpec((n_cmp, n * page), lambda i, c, pt: (0, c))]
            + _page_specs(n, layer, pool_t.shape[2:]),
            out_specs=pl.BlockSpec((1, tq, nq), lambda i, c, pt: (i, 0, 0)),
            scratch_shapes=[pltpu.VMEM((NSA_KV_HEADS, tq, sel_lanes), F32),
                            pltpu.VMEM((NSA_KV_HEADS, rows, NSA_HEAD_DIM), F32),
                            pltpu.VMEM((NSA_KV_HEADS * rows, 1), F32),
                            pltpu.VMEM((NSA_KV_HEADS * rows, 1), F32),
                            pltpu.VMEM((NSA_KV_HEADS * rows, NSA_KV_WIDTH), F32),
                            pltpu.VMEM((NSA_KV_HEADS * rows, NSA_KV_WIDTH), BF16)],
        ),
        out_shape=jax.ShapeDtypeStruct((b, tq, nq), F32),
        compiler_params=_cparams(2),
        name="nsa_attn_sample",
    )(page_table, qn, qr, kvc, gate, slc_new, win_buf_t, win_new, expand, *([pool_t] * n))


def _rope_tables(pos):
    half = NSA_HEAD_DIM // 2
    inv = ROPE_THETA ** (-jnp.arange(half, dtype=F32) / half)
    ang = pos.astype(F32)[:, None] * inv[None, :]
    cos, sin = jnp.cos(ang), jnp.sin(ang)
    reps = LANES // NSA_HEAD_DIM
    return jnp.tile(cos, (1, 2 * reps)), jnp.tile(jnp.concatenate([-sin, sin], axis=1), (1, reps))


def _nsa_params(w_in, g_q, g_ks, g_kw, g_kc, pe, w1, w2):
    d = w_in.shape[0]
    body = NSA_HEADS * NSA_HEAD_DIM + 6 * NSA_KV_WIDTH
    half = NSA_HEADS // 2 * 3
    zpad = jnp.zeros((d, LANES - half), w_in.dtype)
    w_pad = jnp.concatenate([w_in[:, :body], w_in[:, body:body + half], zpad, w_in[:, body + half:], zpad], axis=1)
    return dict(
        w_in=w_pad.astype(BF16),
        g_q=jnp.tile(g_q, NSA_HEADS)[None], g_ks=jnp.tile(g_ks, NSA_KV_HEADS)[None],
        g_kw=jnp.tile(g_kw, NSA_KV_HEADS)[None], g_kc=g_kc[None],
        pe=pe.reshape(2, CMP_BLOCK // 2, LANES), w1=w1.astype(BF16), w2=w2.astype(BF16))


def _kv_rows(x, b, t):
    return x.reshape(b, t, 2, NSA_KV_HEADS, NSA_HEAD_DIM)


def _nsa_prompt_layer(h, gmix, prm):
    b, t, d = h.shape
    cos, sin = _rope_tables(jnp.arange(t))
    qn, qr, cmp, slc, wrow, gate = _nsa_proj_call(
        h.reshape(b * t, d), gmix, prm["w_in"], prm["g_q"], prm["g_ks"], prm["g_kw"], cos, sin)
    r3 = lambda x: x.reshape(b, t, x.shape[-1])
    kvc = _compress_rows_call(r3(cmp), prm["pe"], prm["w1"], prm["w2"], prm["g_kc"])
    o = _attn_prompt_call(r3(qn), r3(qr), kvc, r3(slc), r3(wrow), r3(gate))
    keep = min(WINDOW, t)
    return (o.reshape(b * t, -1), _kv_rows(cmp, b, t), _kv_rows(slc, b, t), _kv_rows(wrow, b, t)[:, t - keep:])


def _nsa_sample_layer(h, cmp_pool_t, slc_pool_t, win_buf, win_buf_t, layer, page_table, gmix, prm):
    b, t, d = h.shape
    past_len = page_table.shape[1] * cmp_pool_t.shape[-1]
    cos, sin = _rope_tables(jnp.tile(past_len + jnp.arange(t), b))
    qn, qr, cmp, slc, wrow, gate = _nsa_proj_call(
        h.reshape(b * t, d), gmix, prm["w_in"], prm["g_q"], prm["g_ks"], prm["g_kw"], cos, sin)
    kvc = _compress_pages_call(cmp_pool_t, layer, page_table, prm["pe"], prm["w1"], prm["w2"], prm["g_kc"])
    tq = 8
    pad = lambda x: jnp.pad(x.reshape(b, t, x.shape[-1]), ((0, 0), (0, tq - t), (0, 0)))
    o = _attn_sample_call(pad(qn), pad(qr), kvc, pad(gate), pad(slc), win_buf_t, pad(wrow), slc_pool_t, layer,
                          page_table, t)
    new_win = jnp.concatenate([win_buf, _kv_rows(wrow, b, t)], axis=1)[:, t:]
    return o[:, :t].reshape(b * t, -1), _kv_rows(cmp, b, t), _kv_rows(slc, b, t), new_win


def _softplus(z):
    return jnp.maximum(z, 0.0) + jnp.log1p(jnp.exp(-jnp.abs(z)))


def _rwkv_proj_kernel(batch, h_ref, shift_ref, gmix_ref, mu_ref, wr_ref, wk_ref, wv_ref,
                      w0_ref, w1_ref, w2_ref, a0_ref, a1_ref, a2_ref, g1_ref, g2_ref,
                      r_ref, dec_ref, k_ref, v_ref, a_ref, g_ref, last_ref, carry_ref):
    i = pl.program_id(0)
    u = _rms(h_ref[...]) * gmix_ref[...]
    tm = u.shape[0]

    @pl.when(i == 0)
    def _():
        carry_ref[...] = shift_ref[...]
    prev = jnp.concatenate([carry_ref[...], u[:tm - batch]], axis=0)
    carry_ref[...] = u[tm - batch:]
    last_ref[...] = u[tm - batch:]
    xx = prev - u
    mix = lambda j: (u + xx * mu_ref[j:j + 1]).astype(BF16)
    r_ref[...] = _dot(mix(0), wr_ref[...])
    wl = w0_ref[...] + _dot(jnp.tanh(_dot(mix(1), w1_ref[...])).astype(BF16), w2_ref[...])
    dec_ref[...] = jnp.exp(-jnp.exp(-_softplus(-wl) - 0.5))
    k_ref[...] = _dot(mix(2), wk_ref[...])
    v_ref[...] = _dot(mix(3), wv_ref[...])
    a_ref[...] = jax.nn.sigmoid(a0_ref[...] + _dot(_dot(mix(4), a1_ref[...]).astype(BF16), a2_ref[...]))
    g_ref[...] = _dot(jax.nn.sigmoid(_dot(mix(5), g1_ref[...])).astype(BF16), g2_ref[...])


def _rwkv_proj_call(h, shift, gmix, prm):
    m, d = h.shape
    batch = shift.shape[0]
    tm = min(m, 256)
    assert tm % batch == 0 and tm > batch and batch % 8 == 0
    row = pl.BlockSpec((tm, d), lambda i: (i, 0))
    weights = [prm[n] for n in ("mu", "w_r", "w_k", "w_v", "w0", "w1", "w2", "a0", "a1", "a2", "g1", "g2")]
    outs = pl.pallas_call(
        functools.partial(_rwkv_proj_kernel, batch),
        grid=(m // tm,),
        in_specs=[row, _resident(shift.shape), _resident(gmix.shape)] + [_resident(w.shape) for w in weights],
        out_specs=[row] * 6 + [pl.BlockSpec((batch, d), lambda i: (0, 0))],
        out_shape=[jax.ShapeDtypeStruct((m, d), F32)] * 6 + [jax.ShapeDtypeStruct((batch, d), F32)],
        scratch_shapes=[pltpu.VMEM((batch, d), F32)],
        compiler_params=_cparams(1),
        name="rwkv_proj",
    )(h, shift, gmix, *weights)
    return outs[:6], outs[6]


def _rwkv_scan_kernel(tc, r_ref, dec_ref, k_ref, v_ref, a_ref, kk_ref, ka_ref, rk_ref, lw_ref, lb_ref, s0_ref,
                      y_ref, sout_ref, state_ref, vec_ref, stage_ref, ystage_ref):
    c = pl.program_id(1)
    n = RWKV_HEAD_DIM

    @pl.when(c == 0)
    def _():
        state_ref[...] = s0_ref[...]
        ystage_ref[...] = jnp.zeros_like(ystage_ref)

    def stage_in(t, slot):
        for i, ref in enumerate((r_ref, dec_ref, k_ref, v_ref, a_ref)):
            xa, xb = _rows_to_lanes(ref[t], ref[t + 1])
            stage_ref[i, 2 * slot] = xa
            stage_ref[i, 2 * slot + 1] = xb

    def step(t):
        r = stage_ref[0, t]
        k = stage_ref[2, t]
        v = stage_ref[3, t]
        a = stage_ref[4, t]
        kk = k * kk_ref[...]
        kk = kk / jnp.maximum(jnp.sqrt(jnp.sum(kk * kk, axis=0, keepdims=True)), 1e-12)
        k2 = k * (1.0 + (a - 1.0) * ka_ref[...])
        vec_ref[0] = -kk
        vec_ref[1] = kk * a
        vec_ref[2] = k2
        sa = jnp.zeros((n, LANES), F32)
        for j in range(n):
            sa = sa + state_ref[j] * vec_ref[0, j:j + 1, :]
        y = jnp.zeros((n, LANES), F32)
        for j in range(n):
            s_new = (state_ref[j] * stage_ref[1, t, j:j + 1, :] + sa * vec_ref[1, j:j + 1, :]
                     + v * vec_ref[2, j:j + 1, :])
            state_ref[j] = s_new
            y = y + s_new * stage_ref[0, t, j:j + 1, :]
        mean = jnp.mean(y, axis=0, keepdims=True)
        var = jnp.mean(jnp.square(y - mean), axis=0, keepdims=True)
        y = (y - mean) * lax.rsqrt(var + LNX_EPS) * lw_ref[...] + lb_ref[...]
        ystage_ref[t] = y + jnp.sum(r * k2 * rk_ref[...], axis=0, keepdims=True) * v

    def steps(slot):
        step(2 * slot)
        step(2 * slot + 1)

    def stage_out(slot, t):
        _lanes_to_rows(ystage_ref[2 * slot], ystage_ref[2 * slot + 1], y_ref, t)

    _pipelined_pairs(tc, stage_in, steps, stage_out)

    @pl.when(c == pl.num_programs(1) - 1)
    def _():
        sout_ref[...] = state_ref[...]


def _pipelined_pairs(tc, stage_in, steps, stage_out):
    stage_in(0, 0)

    def body(i, _):
        t = 4 * i
        stage_in(t + 2, 1)
        steps(0)
        stage_out(1, jnp.maximum(t - 2, 0))
        stage_in(jnp.minimum(t + 4, tc - 2), 0)
        steps(1)
        stage_out(0, t)
        return 0

    lax.fori_loop(0, tc // 4, body, 0)
    stage_out(1, tc - 2)


def _rows_to_lanes(xa, xb):
    chunks = xa.shape[1] // LANES
    stack = jnp.concatenate([x[:, c * LANES:(c + 1) * LANES] for x in (xa, xb) for c in range(chunks)], axis=0)
    tr = stack.T
    half = LANES // 2
    top, bot = tr[:half], tr[half:]
    low = _lane_iota() < half
    return (jnp.where(low, top, pltpu.roll(bot, half, axis=1)),
            jnp.where(low, pltpu.roll(top, half, axis=1), bot))


def _lanes_to_rows(ya, yb, out_ref, t):
    half = LANES // 2
    low = _lane_iota() < half
    top = jnp.where(low, ya, pltpu.roll(yb, half, axis=1))
    bot = jnp.where(low, pltpu.roll(ya, half, axis=1), yb)
    tr = jnp.concatenate([top, bot], axis=0).T
    chunks = out_ref.shape[2] // LANES
    for tok in range(2):
        for c in range(chunks):
            r0 = (tok * chunks + c) * 8
            out_ref[t + tok, :, c * LANES:(c + 1) * LANES] = tr[r0:r0 + 8]


def _lane_params(x, heads, width):
    per_chunk = LANES // width if width < LANES else 1
    x = x.reshape(heads // per_chunk, per_chunk, width)
    x = jnp.transpose(x, (2, 1, 0))
    return jnp.repeat(x[..., None], 8, axis=-1).reshape(width, LANES)


def _rwkv_scan_call(r, dec, k, v, a, kk, ka, rk, lw, lb, s0):
    t, b, d = r.shape
    n = RWKV_HEAD_DIM
    assert d == 8 * LANES and b % 8 == 0 and t % 4 == 0
    tc = math.gcd(t, 32)
    seq = pl.BlockSpec((tc, 8, d), lambda g, c: (c, g, 0))
    par = pl.BlockSpec((n, LANES), lambda g, c: (0, 0))
    st = pl.BlockSpec((n, n, LANES), lambda g, c: (0, 0, g))
    return pl.pallas_call(
        functools.partial(_rwkv_scan_kernel, tc),
        grid=(b // 8, t // tc),
        in_specs=[seq] * 5 + [par] * 5 + [st],
        out_specs=[seq, st],
        out_shape=[jax.ShapeDtypeStruct((t, b, d), F32), jax.ShapeDtypeStruct((n, n, b // 8 * LANES), F32)],
        scratch_shapes=[pltpu.VMEM((n, n, LANES), F32), pltpu.VMEM((3, n, LANES), F32),
                        pltpu.VMEM((5, 4, n, LANES), F32), pltpu.VMEM((4, n, LANES), F32)],
        compiler_params=_cparams(2),
        name="rwkv_scan",
    )(r, dec, k, v, a, kk, ka, rk, lw, lb, s0)


def _rwkv_params(mu, w_r, w_k, w_v, w0, w1, w2, a0, a1, a2, g1, g2):
    gpad = -g1.shape[1] % LANES
    return dict(mu=mu, w_r=w_r.astype(BF16), w_k=w_k.astype(BF16), w_v=w_v.astype(BF16), w0=w0[None],
                w1=w1.astype(BF16), w2=w2.astype(BF16), a0=a0[None], a1=a1.astype(BF16), a2=a2.astype(BF16),
                g1=jnp.pad(g1, ((0, 0), (0, gpad))).astype(BF16), g2=jnp.pad(g2, ((0, gpad), (0, 0))).astype(BF16))


def _rwkv_layer(h, batch, shift, s0, gmix, prm, k_k, k_a, r_k, lnx_w, lnx_b):
    m, d = h.shape
    t = m // batch
    n = RWKV_HEAD_DIM
    nh = d // n
    g8 = batch // 8
    (r, dec, k, v, a, g), new_shift = _rwkv_proj_call(h, shift, gmix, prm)
    rows = lambda x: x.reshape(t, batch, d)
    par = lambda x: _lane_params(x.reshape(-1), nh, n)
    st = s0.reshape(g8, 8, nh // 2, 2, n, n).transpose(5, 4, 0, 3, 2, 1).reshape(n, n, g8 * LANES)
    y, s = _rwkv_scan_call(rows(r), rows(dec), rows(k), rows(v), rows(a), par(k_k), par(k_a), par(r_k),
                           par(lnx_w), par(lnx_b), st)
    s = s.reshape(n, n, g8, 2, nh // 2, 8).transpose(2, 5, 4, 3, 1, 0).reshape(batch, nh, n, n)
    return y.reshape(m, d), g, new_shift, s


def _hgrn_proj_kernel(h_ref, gmix_ref, win_ref, lb_ref, omlb_ref, q_ref, f_ref, k_ref, i_ref, g_ref):
    d = h_ref.shape[1]
    u = (_rms(h_ref[...]) * gmix_ref[...]).astype(BF16)
    z = _dot(u, win_ref[...])
    q, f, g = z[:, :d], z[:, d:2 * d], z[:, 3 * d:]
    q_ref[...] = q * jax.nn.sigmoid(q)
    f_ref[...] = lb_ref[...] + omlb_ref[...] * jax.nn.sigmoid(f)
    k_ref[...] = omlb_ref[...] * jax.nn.sigmoid(-f)
    i_ref[...] = z[:, 2 * d:3 * d]
    g_ref[...] = g * jax.nn.sigmoid(g)


def _hgrn_proj_call(h, gmix, win, lb, omlb):
    m, d = h.shape
    tm = min(m, 256)
    row = pl.BlockSpec((tm, d), lambda i: (i, 0))
    return pl.pallas_call(
        _hgrn_proj_kernel,
        grid=(m // tm,),
        in_specs=[row, _resident(gmix.shape), _resident(win.shape), _resident(lb.shape), _resident(omlb.shape)],
        out_specs=[row] * 5,
        out_shape=[jax.ShapeDtypeStruct((m, d), F32)] * 5,
        compiler_params=_cparams(1),
        name="hgrn_proj",
    )(h, gmix, win, lb, omlb)


def _rows_to_lanes_dup(xa, xb):
    chunks = xa.shape[1] // LANES
    stack = jnp.concatenate([x[:, c * LANES:(c + 1) * LANES] for x in (xa, xb) for c in range(chunks)], axis=0)
    tr = stack.T
    half = LANES // 2
    low = _lane_iota() < half
    other = pltpu.roll(tr, half, axis=1)
    return jnp.where(low, tr, other), jnp.where(low, other, tr)


def _hgrn_scan_kernel(tc, q_ref, f_ref, k_ref, v_ref, s0_ref, o_ref, sout_ref, state_ref, keys_ref, vals_ref):
    c = pl.program_id(1)
    dk = state_ref.shape[0]

    @pl.when(c == 0)
    def _():
        state_ref[...] = s0_ref[...]

    def stage_in(tp, _):
        t = 2 * tp
        for i, ref in enumerate((q_ref, f_ref, k_ref)):
            xa, xb = _rows_to_lanes_dup(ref[t], ref[t + 1])
            keys_ref[i, t] = xa
            keys_ref[i, t + 1] = xb
        va, vb = _rows_to_lanes(v_ref[t], v_ref[t + 1])
        vals_ref[t] = va
        vals_ref[t + 1] = vb
        return 0

    lax.fori_loop(0, tc // 2, stage_in, 0)

    def step(t, _):
        v = vals_ref[t]
        o = jnp.zeros(v.shape, F32)
        for d in range(dk):
            s_new = state_ref[d] * keys_ref[1, t, d:d + 1, :] + keys_ref[2, t, d:d + 1, :] * v
            state_ref[d] = s_new
            o = o + s_new * keys_ref[0, t, d:d + 1, :]
        vals_ref[t] = o
        return 0

    lax.fori_loop(0, tc, step, 0)

    def stage_out(tq, _):
        for u in range(2):
            t = 4 * tq + 2 * u
            _lanes_to_rows(vals_ref[t], vals_ref[t + 1], o_ref, t)
        return 0

    lax.fori_loop(0, tc // 4, stage_out, 0)

    @pl.when(c == pl.num_programs(1) - 1)
    def _():
        sout_ref[...] = state_ref[...]


def _hgrn_scan_call(q, f, k, v, s0):
    t, b, d = q.shape
    dk = HGRN_HEAD_DIM
    dv = dk // 2
    assert d == 8 * LANES and b % 8 == 0 and t % 4 == 0
    tc = math.gcd(t, 32)
    seq = pl.BlockSpec((tc, 8, d), lambda g, c: (c, g, 0))
    st = pl.BlockSpec((dk, dv, LANES), lambda g, c: (0, 0, g))
    return pl.pallas_call(
        functools.partial(_hgrn_scan_kernel, tc),
        grid=(b // 8, t // tc),
        in_specs=[seq] * 4 + [st],
        out_specs=[seq, st],
        out_shape=[jax.ShapeDtypeStruct((t, b, d), F32), jax.ShapeDtypeStruct((dk, dv, b // 8 * LANES), F32)],
        scratch_shapes=[pltpu.VMEM((dk, dv, LANES), F32), pltpu.VMEM((3, tc, dk, LANES), F32),
                        pltpu.VMEM((tc, dv, LANES), F32)],
        compiler_params=_cparams(2),
        name="hgrn_scan",
    )(q, f, k, v, s0)


def _hgrn_layer(h, batch, s0, gmix, win, lb, omlb):
    m, d = h.shape
    t = m // batch
    n = HGRN_HEAD_DIM
    nh = d // n
    g8 = batch // 8
    q, f, k, i, g = _hgrn_proj_call(h, gmix, win, lb, omlb)
    rows = lambda x: x.reshape(t, batch, d)
    st = s0.reshape(g8, 8, nh, n, 2, n // 2).transpose(3, 5, 0, 4, 2, 1).reshape(n, n // 2, g8 * LANES)
    o, s = _hgrn_scan_call(rows(q), rows(f), rows(k), rows(i), st)
    s = s.reshape(n, n // 2, g8, 2, nh, 8).transpose(2, 5, 4, 0, 3, 1).reshape(batch, nh, n, n)
    return o.reshape(m, d), g, s


def kernel(x_prompt, x_sample, cache_cmp, cache_slc, cache_win, state_rwkv_shift, state_rwkv_wkv, state_hgrn,
           page_table, p_prompt, p_sample, norm_mix, norm_ffn, w_up, w_down, w_ple, w_ple_gate,
           nsa_w_in, nsa_g_q, nsa_g_ks, nsa_g_kw, nsa_g_kc, nsa_cmp_pe, nsa_cmp_w1, nsa_cmp_w2, nsa_w_out,
           rwkv_mu, rwkv_w_r, rwkv_w_k, rwkv_w_v, rwkv_w_o, rwkv_w0, rwkv_w1, rwkv_w2, rwkv_a0, rwkv_a1,
           rwkv_a2, rwkv_g1, rwkv_g2, rwkv_k_k, rwkv_k_a, rwkv_r_k, rwkv_lnx_w, rwkv_lnx_b,
           hgrn_w_in, hgrn_gn, hgrn_w_o, hgrn_lower_bounds):
    depth = norm_mix.shape[0]
    bp, tp, d = x_prompt.shape
    bs, ts, _ = x_sample.shape
    lb_soft = jax.nn.softmax(hgrn_lower_bounds.astype(F32), axis=0)
    lower_bound = jnp.cumsum(lb_soft, axis=0) - lb_soft[0]
    rows_minor = lambda x: jnp.transpose(x, (0, 1, 3, 4, 5, 2))
    cmp_pool_t, slc_pool_t, win_buf_t = rows_minor(cache_cmp), rows_minor(cache_slc), rows_minor(cache_win)
    hp, hs = x_prompt.reshape(bp * tp, d), x_sample.reshape(bs * ts, d)
    swap = lambda x, a, b: x.reshape(a, b, x.shape[-1]).transpose(1, 0, 2).reshape(a * b, x.shape[-1])
    time_major = False
    outs = [[] for _ in range(12)]
    for i in range(depth):
        kind, n = i % N_MIXERS, i // N_MIXERS
        gmix = norm_mix[i][None]
        ffn = (p_prompt[i].reshape(bp * tp, -1), p_sample[i].reshape(bs * ts, -1))
        if (kind != 0) != time_major:
            hp = swap(hp, tp, bp) if time_major else swap(hp, bp, tp)
            hs = swap(hs, ts, bs) if time_major else swap(hs, bs, ts)
            time_major = not time_major
        if time_major:
            ffn = (swap(ffn[0], bp, tp), swap(ffn[1], bs, ts))
        hp3, hs3 = hp.reshape(bp, tp, d), hs.reshape(bs, ts, d)
        tail = (norm_ffn[i][None], w_up[i].astype(BF16), w_down[i].astype(BF16), w_ple[i].astype(BF16),
                w_ple_gate[i].astype(BF16))
        if kind == 0:
            prm = _nsa_params(nsa_w_in[n], nsa_g_q[n], nsa_g_ks[n], nsa_g_kw[n], nsa_g_kc[n], nsa_cmp_pe[n],
                              nsa_cmp_w1[n], nsa_cmp_w2[n])
            op, rc, rs, rw = _nsa_prompt_layer(hp3, gmix, prm)
            os_, nc_rows, ns_rows, nw_buf = _nsa_sample_layer(hs3, cmp_pool_t, slc_pool_t, cache_win[n], win_buf_t,
                                                              n, page_table, gmix, prm)
            for lst, v in zip(outs[:6], (rc, nc_rows, rs, ns_rows, rw, nw_buf)):
                lst.append(v)
            wo = nsa_w_out[n].astype(BF16)
            hp = _ffn_call("nsa", hp, op, None, None, ffn[0], wo, *tail)
            hs = _ffn_call("nsa", hs, os_, None, None, ffn[1], wo, *tail)
        elif kind == 1:
            prm = _rwkv_params(rwkv_mu[n], rwkv_w_r[n], rwkv_w_k[n], rwkv_w_v[n], rwkv_w0[n], rwkv_w1[n],
                               rwkv_w2[n], rwkv_a0[n], rwkv_a1[n], rwkv_a2[n], rwkv_g1[n], rwkv_g2[n])
            vecs = (rwkv_k_k[n], rwkv_k_a[n], rwkv_r_k[n], rwkv_lnx_w[n], rwkv_lnx_b[n])
            nh = d // RWKV_HEAD_DIM
            zero_state = jnp.zeros((bp, nh, RWKV_HEAD_DIM, RWKV_HEAD_DIM), F32)
            yp, gp, shp, sp = _rwkv_layer(hp, bp, jnp.zeros((bp, d), F32), zero_state, gmix, prm, *vecs)
            ys, gs, shs, ss = _rwkv_layer(hs, bs, state_rwkv_shift[n], state_rwkv_wkv[n].astype(F32), gmix, prm,
                                          *vecs)
            for lst, v in zip(outs[6:10], (shp, shs, sp, ss)):
                lst.append(v)
            wo = rwkv_w_o[n].astype(BF16)
            hp = _ffn_call("rwkv", hp, yp, gp, None, ffn[0], wo, *tail)
            hs = _ffn_call("rwkv", hs, ys, gs, None, ffn[1], wo, *tail)
        else:
            lb = lower_bound[i][None]
            win = hgrn_w_in[n].astype(BF16)
            nh = d // HGRN_HEAD_DIM
            zero_state = jnp.zeros((bp, nh, HGRN_HEAD_DIM, HGRN_HEAD_DIM), F32)
            op, gp, sp = _hgrn_layer(hp, bp, zero_state, gmix, win, lb, 1.0 - lb)
            os_, gs, ss = _hgrn_layer(hs, bs, state_hgrn[n].astype(F32), gmix, win, lb, 1.0 - lb)
            outs[10].append(sp)
            outs[11].append(ss)
            wo, gn = hgrn_w_o[n].astype(BF16), hgrn_gn[n][None]
            hp = _ffn_call("hgrn", hp, op, gp, gn, ffn[0], wo, *tail)
            hs = _ffn_call("hgrn", hs, os_, gs, gn, ffn[1], wo, *tail)
    if time_major:
        hp, hs = swap(hp, tp, bp), swap(hs, ts, bs)
    return (hp.reshape(bp, tp, d), hs.reshape(bs, ts, d)) + tuple(jnp.stack(o) for o in outs)
```

```python
import functools
import math

import jax
import jax.numpy as jnp
from jax import lax
from jax.experimental import pallas as pl
from jax.experimental.pallas import tpu as pltpu

F32 = jnp.float32
BF16 = jnp.bfloat16

NORM_EPS = 1e-6
ROPE_THETA = 10000.0
NEG_INF = -1e30
M_INIT = -1e29
N_MIXERS = 3

NSA_HEADS = 16
NSA_KV_HEADS = 4
NSA_HEAD_DIM = 64
NSA_GROUP = NSA_HEADS // NSA_KV_HEADS
NSA_KV_WIDTH = NSA_KV_HEADS * NSA_HEAD_DIM
NSA_SCALE = NSA_HEAD_DIM ** -0.5
LOG2E = math.log2(math.e)
CMP_BLOCK = 32
SEL_BLOCK = 64
N_SELECT = 16
WINDOW = 512
FORCE_SCORE = 1e4

RWKV_HEAD_DIM = 64
LNX_EPS = 64e-5
HGRN_HEAD_DIM = 128

LANES = 128
VMEM_LIMIT = 56 * 1024 * 1024


def _cparams(n_axes):
    return pltpu.CompilerParams(dimension_semantics=("arbitrary",) * n_axes,
                                vmem_limit_bytes=VMEM_LIMIT)


def _resident(shape):
    zeros = (0,) * len(shape)
    return pl.BlockSpec(shape, lambda *_: zeros, pipeline_mode=pl.Buffered(1))


def _rms(x):
    return x * lax.rsqrt(jnp.mean(x * x, axis=-1, keepdims=True) + NORM_EPS)


def _dot(a, b):
    return jnp.dot(a, b, preferred_element_type=F32)


def _dot_nt(a, b):
    return lax.dot_general(a, b, (((1,), (1,)), ((), ())), preferred_element_type=F32)


def _per_chunk(fn, x, *rest):
    n = x.shape[-1] // LANES
    outs = [fn(*(a[:, c * LANES:(c + 1) * LANES] for a in (x,) + rest)) for c in range(n)]
    return outs[0] if n == 1 else jnp.concatenate(outs, axis=-1)


def _lane_iota():
    return lax.broadcasted_iota(jnp.int32, (1, LANES), 1)


def _group_ones(group):
    r = lax.broadcasted_iota(jnp.int32, (LANES, LANES), 0) // group
    c = lax.broadcasted_iota(jnp.int32, (LANES, LANES), 1) // group
    return jnp.where(r == c, 1.0, 0.0).astype(BF16)


def _head_rms(x, head_dim, ones):
    def one(c):
        sq = c * c
        hi = sq.astype(BF16)
        lo = (sq - hi.astype(F32)).astype(BF16)
        ss = _dot(hi, ones) + _dot(lo, ones)
        return c * lax.rsqrt(ss * (1.0 / head_dim) + NORM_EPS)
    return _per_chunk(one, x)


def _rope(x, cos, sin_signed):
    half = NSA_HEAD_DIM // 2
    lane = _lane_iota()

    def one(c):
        up = pltpu.roll(c, LANES - half, axis=1)
        dn = pltpu.roll(c, half, axis=1)
        rot = jnp.where((lane & half) == 0, up, dn)
        return c * cos + rot * sin_signed
    return _per_chunk(one, x)


def _ffn_kernel(mode, ff_chunk, *refs):
    if mode == "nsa":
        h_ref, o_ref, p_ref, wo_ref, gffn_ref, wup_ref, wdown_ref, wple_ref, wgate_ref, out_ref = refs
        o = o_ref[...]
    elif mode == "rwkv":
        h_ref, o_ref, aux_ref, p_ref, wo_ref, gffn_ref, wup_ref, wdown_ref, wple_ref, wgate_ref, out_ref = refs
        o = o_ref[...] * aux_ref[...]
    else:
        (h_ref, o_ref, aux_ref, gn_ref, p_ref, wo_ref, gffn_ref, wup_ref, wdown_ref, wple_ref, wgate_ref,
         out_ref) = refs
        gn = gn_ref[...]
        o = _per_chunk(lambda c: _rms(c) * gn, o_ref[...]) * aux_ref[...]
    h1 = h_ref[...] + _dot(o.astype(BF16), wo_ref[...])
    u = (_rms(h1) * gffn_ref[...]).astype(BF16)
    d_ff = wup_ref.shape[1]
    acc = jnp.zeros_like(h1)
    for j in range(d_ff // ff_chunk):
        a = _dot(u, wup_ref[:, j * ff_chunk:(j + 1) * ff_chunk])
        a = jnp.square(jnp.maximum(a, 0.0)).astype(BF16)
        acc = acc + _dot(a, wdown_ref[j * ff_chunk:(j + 1) * ff_chunk, :])
    h2 = h1 + acc
    gate = jax.nn.sigmoid(_dot(_rms(h2).astype(BF16), wgate_ref[...]))
    out_ref[...] = h2 + _dot(p_ref[...].astype(BF16), wple_ref[...]) * gate


def _ffn_call(mode, h, o, aux, gn, p, wo, gffn, wup, wdown, wple, wgate):
    m, d = h.shape
    tm = min(m, 512)
    row = lambda w: pl.BlockSpec((tm, w), lambda i: (i, 0))
    args, specs = [h, o], [row(d), row(d)]
    if mode != "nsa":
        args.append(aux)
        specs.append(row(d))
    if mode == "hgrn":
        args.append(gn)
        specs.append(_resident(gn.shape))
    args += [p, wo, gffn, wup, wdown, wple, wgate]
    specs += [row(p.shape[1])] + [_resident(a.shape) for a in (wo, gffn, wup, wdown, wple, wgate)]
    return pl.pallas_call(
        functools.partial(_ffn_kernel, mode, 1024),
        grid=(m // tm,),
        in_specs=specs,
        out_specs=row(d),
        out_shape=jax.ShapeDtypeStruct((m, d), F32),
        compiler_params=_cparams(1),
        name="ffn_" + mode,
    )(*args)


def _nsa_proj_kernel(h_ref, gmix_ref, win_ref, gq_ref, gks_ref, gkw_ref, cos_ref, sin_ref,
                     qn_ref, qr_ref, cmp_ref, slc_ref, wrow_ref, gate_ref):
    kvw = NSA_KV_WIDTH
    nq = NSA_HEADS * NSA_HEAD_DIM
    u = (_rms(h_ref[...]) * gmix_ref[...]).astype(BF16)
    z = _dot(u, win_ref[...])
    cos = cos_ref[...]
    sin = sin_ref[...]
    ones = _group_ones(NSA_HEAD_DIM)
    qn = _head_rms(z[:, :nq], NSA_HEAD_DIM, ones) * gq_ref[...]
    qn_ref[...] = qn
    qr_ref[...] = _rope(qn, cos, sin)
    cmp_ref[...] = z[:, nq:nq + 2 * kvw]
    o = nq + 2 * kvw
    slc_ref[:, :kvw] = _rope(_head_rms(z[:, o:o + kvw], NSA_HEAD_DIM, ones) * gks_ref[...], cos, sin)
    slc_ref[:, kvw:] = z[:, o + kvw:o + 2 * kvw]
    o += 2 * kvw
    wrow_ref[:, :kvw] = _rope(_head_rms(z[:, o:o + kvw], NSA_HEAD_DIM, ones) * gkw_ref[...], cos, sin)
    wrow_ref[:, kvw:] = z[:, o + kvw:o + 2 * kvw]
    o += 2 * kvw
    gate_ref[...] = jax.nn.sigmoid(z[:, o:])


def _nsa_proj_call(h, gmix, win, gq, gks, gkw, cos, sin):
    m, d = h.shape
    tm = min(m, 256)
    tab_tiles = cos.shape[0] // tm
    row = lambda w: pl.BlockSpec((tm, w), lambda i: (i, 0))
    tab = pl.BlockSpec((tm, LANES), lambda i: (i % tab_tiles, 0))
    nq = NSA_HEADS * NSA_HEAD_DIM
    widths = [nq, nq] + [2 * NSA_KV_WIDTH] * 3 + [2 * LANES]
    return pl.pallas_call(
        _nsa_proj_kernel,
        grid=(m // tm,),
        in_specs=[row(d), _resident(gmix.shape), _resident(win.shape), _resident(gq.shape),
                  _resident(gks.shape), _resident(gkw.shape), tab, tab],
        out_specs=[row(w) for w in widths],
        out_shape=[jax.ShapeDtypeStruct((m, w), F32) for w in widths],
        compiler_params=_cparams(1),
        name="nsa_proj",
    )(h, gmix, win, gq, gks, gkw, cos, sin)


def _compress_slot(load_pair, m, e, pe_ref, w1_ref, w2_ref, gkc_ref, flat_ref):
    lane = _lane_iota()
    low = lane < NSA_HEAD_DIM
    for s in range(CMP_BLOCK // 2):
        pe_row = pe_ref[e, s:s + 1, :]
        parts = []
        for kp in range(NSA_KV_HEADS // 2):
            a = load_pair(kp, 2 * s)
            b = load_pair(kp, 2 * s + 1)
            parts.append(jnp.where(low, a, pltpu.roll(b, NSA_HEAD_DIM, axis=1)))
            parts.append(jnp.where(low, pltpu.roll(a, NSA_HEAD_DIM, axis=1), b))
        flat_ref[:, s * LANES:(s + 1) * LANES] = (jnp.concatenate(parts, axis=0) + pe_row).astype(BF16)
    out = _dot(jax.nn.gelu(_dot(flat_ref[...], w1_ref[e])).astype(BF16), w2_ref[e])
    if e == 0:
        out = _rms(out) * gkc_ref[...]
    return jnp.concatenate([out[k * m:(k + 1) * m] for k in range(NSA_KV_HEADS)], axis=-1)


def _compress_rows_kernel(*refs):
    cols = refs[:NSA_KV_HEADS]
    pe_ref, w1_ref, w2_ref, gkc_ref, out_ref, flat_ref = refs[NSA_KV_HEADS:]
    m = cols[0].shape[1] // CMP_BLOCK
    pad = out_ref.shape[2] - m
    for e in range(2):
        load = lambda kp, r: cols[2 * e + kp][0, pl.ds(r, m, stride=CMP_BLOCK), :]
        res = _compress_slot(load, m, e, pe_ref, w1_ref, w2_ref, gkc_ref, flat_ref)
        out_ref[0, e] = jnp.concatenate([res, jnp.zeros((pad, res.shape[1]), F32)], axis=0)


def _compress_rows_call(cmp, pe, w1, w2, gkc):
    b, t, w = cmp.shape
    c = -(-(t // CMP_BLOCK) // LANES) * LANES
    return pl.pallas_call(
        _compress_rows_kernel,
        grid=(b,),
        in_specs=[pl.BlockSpec((1, t, LANES), functools.partial(lambda j, i: (i, 0, j), j)) for j in range(w // LANES)]
        + [_resident(a.shape) for a in (pe, w1, w2, gkc)],
        out_specs=pl.BlockSpec((1, 2, c, NSA_KV_WIDTH), lambda i: (i, 0, 0, 0)),
        out_shape=jax.ShapeDtypeStruct((b, 2, c, NSA_KV_WIDTH), F32),
        scratch_shapes=[pltpu.VMEM((NSA_KV_HEADS * (t // CMP_BLOCK), w1.shape[1]), BF16)],
        compiler_params=_cparams(1),
        name="nsa_compress_rows",
    )(*([cmp] * (w // LANES)), pe, w1, w2, gkc)


PAGES_PER_STEP = 32


def _page_specs(n, layer, tail):
    zeros = (0,) * len(tail)
    return [pl.BlockSpec((1, 1) + tail,
                         functools.partial(lambda i, b, c, pt: (layer, pt[b, c * n + i]) + zeros, i))
            for i in range(n)]


def _compress_pages_kernel(pt_ref, *refs):
    n = len(refs) - 7
    pages = refs[:n]
    pe_ref, w1_ref, w2_ref, gkc_ref, out_ref, rows_ref, flat_ref = refs[n:]
    page = pages[0].shape[-1]
    m = n * page // CMP_BLOCK
    for e in range(2):
        for i, pg in enumerate(pages):
            for kp in range(NSA_KV_HEADS // 2):
                tile = pg[0, 0, e, 2 * kp:2 * kp + 2].reshape(LANES, page)
                rows_ref[e, kp, i * page:(i + 1) * page, :] = tile.T
    for e in range(2):
        load = lambda kp, r: rows_ref[e, kp, pl.ds(r, m, stride=CMP_BLOCK), :]
        out_ref[0, e] = _compress_slot(load, m, e, pe_ref, w1_ref, w2_ref, gkc_ref, flat_ref.at[e])


def _compress_pages_call(pool_t, layer, page_table, pe, w1, w2, gkc):
    b, n_pages = page_table.shape
    page = pool_t.shape[-1]
    n = PAGES_PER_STEP
    m = n * page // CMP_BLOCK
    return pl.pallas_call(
        _compress_pages_kernel,
        grid_spec=pltpu.PrefetchScalarGridSpec(
            num_scalar_prefetch=1,
            grid=(b, n_pages // n),
            in_specs=_page_specs(n, layer, pool_t.shape[2:])
            + [pl.BlockSpec(a.shape, functools.partial(lambda nd, i, c, pt: (0,) * nd, a.ndim),
                            pipeline_mode=pl.Buffered(1)) for a in (pe, w1, w2, gkc)],
            out_specs=pl.BlockSpec((1, 2, m, NSA_KV_WIDTH), lambda i, c, pt: (i, 0, c, 0)),
            scratch_shapes=[pltpu.VMEM((2, NSA_KV_HEADS // 2, n * page, LANES), F32),
                            pltpu.VMEM((2, NSA_KV_HEADS * m, w1.shape[1]), BF16)],
        ),
        out_shape=jax.ShapeDtypeStruct((b, 2, n_pages * page // CMP_BLOCK, NSA_KV_WIDTH), F32),
        compiler_params=_cparams(2),
        name="nsa_compress_pages",
    )(page_table, *([pool_t] * n), pe, w1, w2, gkc)


def _select_blocks_t(imp, tpos):
    n_blocks = imp.shape[0]
    blk = lax.broadcasted_iota(jnp.int32, (n_blocks, 1), 0)
    cur = tpos // SEL_BLOCK
    forced = (blk == 0) | (blk == cur) | (blk == cur - 1)
    score = jnp.where(blk <= cur, jnp.where(forced, FORCE_SCORE, imp), -1.0)
    rank = jnp.zeros(score.shape, F32)
    for j in range(n_blocks):
        row = score[j:j + 1, :]
        below = jnp.where(j < blk, 1.0, 0.0)
        rank = rank + jnp.where(row > score, 1.0, jnp.where(row == score, below, 0.0))
    return jnp.where(rank < float(min(N_SELECT, n_blocks)), 1.0, 0.0)


def _attn_prompt_kernel(tq, tk, qn_ref, qr_ref, kvc_ref, sk_ref, sv_ref, wk_ref, wv_ref, gate_ref, o_ref,
                        kb_ref, vt_ref, sel_ref, imp_ref):
    qi = pl.program_id(2)
    hd = NSA_HEAD_DIM
    t_len = sk_ref.shape[1]
    c_blocks = kvc_ref.shape[2]
    n_sel = t_len // SEL_BLOCK
    ratio = SEL_BLOCK // CMP_BLOCK
    rows_per_tile = tk // SEL_BLOCK

    @pl.when(qi == 0)
    def _():
        for br, (k_ref, v_ref) in enumerate(((sk_ref, sv_ref), (wk_ref, wv_ref))):
            for gl in range(2):
                kb_ref[br, gl] = k_ref[0, :, gl * hd:(gl + 1) * hd].astype(BF16)
            for kt in range(t_len // tk):
                vt_ref[br, kt] = v_ref[0, kt * tk:(kt + 1) * tk, :].T.astype(BF16)

    t0 = qi * tq
    tpos = t0 + lax.broadcasted_iota(jnp.int32, (1, tq), 1)
    gate_t = gate_ref[0].T
    qn_t = [qn_ref[0, :, c * LANES:(c + 1) * LANES].T for c in range(2 * NSA_GROUP * hd // LANES)]
    qr_t = [qr_ref[0, :, c * LANES:(c + 1) * LANES].T for c in range(2 * NSA_GROUP * hd // LANES)]
    cid = lax.broadcasted_iota(jnp.int32, (c_blocks, 1), 0)
    ok = jnp.where((cid + 1) * CMP_BLOCK - 1 <= tpos, 1.0, 0.0)
    ok4 = jnp.concatenate([ok] * NSA_GROUP, axis=1)
    qr4s, o_cmps = [], []
    for gl in range(2):
        per_head = lambda parts: [parts[2 * gl + hh // 2][(hh % 2) * hd:(hh % 2 + 1) * hd] for hh in range(NSA_GROUP)]
        qn4 = (jnp.concatenate(per_head(qn_t), axis=1) * NSA_SCALE).astype(BF16)
        qr4s.append((jnp.concatenate(per_head(qr_t), axis=1) * (NSA_SCALE * LOG2E)).astype(BF16))
        kc = kvc_ref[0, 0][:, gl * hd:(gl + 1) * hd]
        vc_t = kvc_ref[0, 1].T[gl * hd:(gl + 1) * hd]
        s = jnp.where(ok4 > 0.5, _dot(kc.astype(BF16), qn4), NEG_INF)
        e = jnp.exp(s - jnp.max(s, axis=0, keepdims=True))
        p = e / jnp.sum(e, axis=0, keepdims=True) * ok4
        o_cmps.append(_dot(vc_t.astype(BF16), p.astype(BF16)))
        imp = p[:, :tq]
        for hh in range(1, NSA_GROUP):
            imp = imp + p[:, hh * tq:(hh + 1) * tq]
        parts = []
        for j in range(tq // LANES):
            imp_ref[j] = imp[:, j * LANES:(j + 1) * LANES]
            part = imp_ref[j, pl.ds(0, n_sel, stride=ratio), :]
            for i in range(1, ratio):
                part = part + imp_ref[j, pl.ds(i, n_sel, stride=ratio), :]
            parts.append(part)
        imp_sel = parts[0] if len(parts) == 1 else jnp.concatenate(parts, axis=1)
        sel_ref[gl] = _select_blocks_t(imp_sel, tpos)

    def flash(br, lo, hi, valid_fn):
        def body(kt, carry):
            k0 = pl.multiple_of(kt * tk, tk)
            out = []
            for gl in range(2):
                m, l, acc = carry[gl]
                kb = kb_ref[br, gl, pl.ds(k0, tk), :]
                vt = vt_ref[br, kt, gl * hd:(gl + 1) * hd, :]
                bias = jnp.where(valid_fn(gl, kt, k0), 0.0, NEG_INF)
                s = _dot(kb, qr4s[gl]) + jnp.concatenate([bias] * NSA_GROUP, axis=1)
                m_new = jnp.maximum(m, jnp.max(s, axis=0, keepdims=True))
                alpha = jnp.exp2(m - m_new)
                p = jnp.exp2(s - m_new)
                out.append((m_new, alpha * l + jnp.sum(p, axis=0, keepdims=True),
                            alpha * acc + _dot(vt, p.astype(BF16))))
            return tuple(out)
        cols = NSA_GROUP * tq
        init = tuple((jnp.full((1, cols), M_INIT, F32), jnp.zeros((1, cols), F32), jnp.zeros((hd, cols), F32))
                     for _ in range(2))
        res = lax.fori_loop(lo, hi, body, init)
        return [acc / l for _, l, acc in res]

    def slc_valid(gl, kt, k0):
        picked = sel_ref[gl, pl.ds(pl.multiple_of(kt * rows_per_tile, rows_per_tile), rows_per_tile), :]
        picked = jnp.concatenate([jnp.broadcast_to(picked[i:i + 1], (SEL_BLOCK, tq))
                                  for i in range(rows_per_tile)], axis=0)
        kpos = k0 + lax.broadcasted_iota(jnp.int32, (tk, 1), 0)
        return (picked > 0.5) & (kpos <= tpos)

    def win_valid(gl, kt, k0):
        dist = tpos - (k0 + lax.broadcasted_iota(jnp.int32, (tk, 1), 0))
        return (dist >= 0) & (dist <= WINDOW)

    hi = (t0 + tq - 1) // tk + 1
    slc = flash(0, 0, hi, slc_valid)
    win = flash(1, jnp.maximum(t0 - WINDOW, 0) // tk, hi, win_valid)
    heads = []
    for gl in range(2):
        for hh in range(NSA_GROUP):
            c = (gl * NSA_GROUP + hh) * 3
            cols = slice(hh * tq, (hh + 1) * tq)
            heads.append(o_cmps[gl][:, cols] * gate_t[c:c + 1] + slc[gl][:, cols] * gate_t[c + 1:c + 2]
                         + win[gl][:, cols] * gate_t[c + 2:c + 3])
    for c in range(len(heads) // 2):
        o_ref[0, :, c * LANES:(c + 1) * LANES] = jnp.concatenate(heads[2 * c:2 * c + 2], axis=0).T


def _attn_prompt_call(qn, qr, kvc, slc, win, gate, tq=512, tk=512):
    b, t, nq = qn.shape
    pair_w = 2 * NSA_GROUP * NSA_HEAD_DIM
    c = kvc.shape[2]
    assert tk % (8 * SEL_BLOCK) == 0 and t % tk == 0 and c % LANES == 0 and tq % LANES == 0 and t % tq == 0
    qspec = pl.BlockSpec((1, tq, pair_w), lambda i, p, j: (i, j, p))
    kspec = pl.BlockSpec((1, t, LANES), lambda i, p, j: (i, 0, p))
    vspec = pl.BlockSpec((1, t, LANES), lambda i, p, j: (i, 0, 2 + p))
    return pl.pallas_call(
        functools.partial(_attn_prompt_kernel, tq, tk),
        grid=(b, 2, t // tq),
        in_specs=[qspec, qspec,
                  pl.BlockSpec((1, 2, c, LANES), lambda i, p, j: (i, 0, 0, p)),
                  kspec, vspec, kspec, vspec,
                  pl.BlockSpec((1, tq, LANES), lambda i, p, j: (i, j, p))],
        out_specs=qspec,
        out_shape=jax.ShapeDtypeStruct((b, t, nq), F32),
        scratch_shapes=[pltpu.VMEM((2, 2, t, NSA_HEAD_DIM), BF16),
                        pltpu.VMEM((2, t // tk, LANES, tk), BF16),
                        pltpu.VMEM((2, t // SEL_BLOCK, tq), F32),
                        pltpu.VMEM((tq // LANES, c, LANES), F32)],
        compiler_params=_cparams(3),
        name="nsa_attn_prompt",
    )(qn, qr, kvc, slc, slc, win, win, gate)


def _select_blocks(imp, tpos, n_blocks):
    c = imp.shape[1]
    lane = lax.broadcasted_iota(jnp.int32, (1, c), 1)

    def pair(x):
        lane1 = _lane_iota()
        return x + jnp.where((lane1 & 1) == 0, pltpu.roll(x, LANES - 1, axis=1), pltpu.roll(x, 1, axis=1))
    imp2 = _per_chunk(pair, imp)
    blk = lane // 2
    cur = tpos // SEL_BLOCK
    forced = (blk == 0) | (blk == cur) | (blk == cur - 1)
    score = jnp.where(blk <= cur, jnp.where(forced, FORCE_SCORE, imp2), -1.0)
    rank = jnp.zeros(score.shape, F32)
    for j in range(n_blocks):
        col = score[:, 2 * j:2 * j + 1]
        ahead = (col > score) | ((col == score) & (j < blk))
        rank = rank + ahead.astype(F32)
    return (rank < float(min(N_SELECT, n_blocks))).astype(F32)


def _attn_sample_kernel(n_step, t_real, past_len, pt_ref, qn_ref, qr_ref, kvc_ref, gate_ref, snew_ref, wbuf_ref,
                        wnew_ref, expand_ref, *rest):
    pages = rest[:n_step]
    o_ref, sel_ref, ocmp_ref, m_ref, l_ref, acc_ref, qbd_ref, qnbd_ref = rest[n_step:]
    c = pl.program_id(1)
    hd = NSA_HEAD_DIM
    kvw = NSA_KV_WIDTH
    tq = qn_ref.shape[1]
    n_cmp = past_len // CMP_BLOCK
    tpos = past_len + lax.broadcasted_iota(jnp.int32, (tq, 1), 0) % t_real
    stack = lambda ref, g: jnp.concatenate(
        [ref[0, :, (g * NSA_GROUP + hh) * hd:(g * NSA_GROUP + hh + 1) * hd] for hh in range(NSA_GROUP)], axis=0)
    rows = NSA_GROUP * tq

    tpos_all = jnp.concatenate([tpos] * NSA_KV_HEADS, axis=0)
    per_head = lambda x: jnp.concatenate(
        [x[g * tq:(g + 1) * tq] for g in range(NSA_KV_HEADS) for _ in range(NSA_GROUP)], axis=0)

    @pl.when(c == 0)
    def _():
        n_blocks = -(-(past_len + t_real) // SEL_BLOCK)
        zero = jnp.zeros((rows, hd), BF16)
        for g in range(NSA_KV_HEADS):
            for src, dst in ((qn_ref, qnbd_ref), (qr_ref, qbd_ref)):
                q4 = (stack(src, g) * NSA_SCALE).astype(BF16)
                dst[g * rows:(g + 1) * rows, :] = jnp.concatenate(
                    [q4 if gg == g else zero for gg in range(NSA_KV_HEADS)], axis=1)
        cidx = lax.broadcasted_iota(jnp.int32, (1, n_cmp), 1)
        ok = per_head(jnp.where((cidx + 1) * CMP_BLOCK - 1 <= tpos_all, 1.0, 0.0))
        s = jnp.where(ok > 0.5, _dot_nt(qnbd_ref[...], kvc_ref[0, 0].astype(BF16)), NEG_INF)
        e = jnp.exp(s - jnp.max(s, axis=-1, keepdims=True))
        p = e / jnp.sum(e, axis=-1, keepdims=True) * ok
        o_all = _dot(p.astype(BF16), kvc_ref[0, 1].astype(BF16))
        imps = []
        for g in range(NSA_KV_HEADS):
            ocmp_ref[g] = o_all[g * rows:(g + 1) * rows, g * hd:(g + 1) * hd]
            imp = p[g * rows:g * rows + tq]
            for hh in range(1, NSA_GROUP):
                imp = imp + p[g * rows + hh * tq:g * rows + (hh + 1) * tq]
            imps.append(imp)
        imp_all = jnp.concatenate(imps, axis=0)
        imp_all = jnp.concatenate([imp_all, jnp.zeros((imp_all.shape[0], LANES), F32)], axis=-1)
        sel_all = _select_blocks(imp_all, tpos_all, n_blocks)
        for g in range(NSA_KV_HEADS):
            sel_ref[g] = sel_all[g * tq:(g + 1) * tq]
        m_ref[...] = jnp.full(m_ref.shape, M_INIT, F32)
        l_ref[...] = jnp.zeros(l_ref.shape, F32)
        acc_ref[...] = jnp.zeros(acc_ref.shape, F32)

    tk = n_step * pages[0].shape[-1]
    k0 = c * tk
    kpos = k0 + lax.broadcasted_iota(jnp.int32, (1, tk), 1)
    kt_all = jnp.concatenate([pg[0, 0, 0].reshape(kvw, -1) for pg in pages], axis=1).astype(BF16)
    vt_all = jnp.concatenate([pg[0, 0, 1].reshape(kvw, -1) for pg in pages], axis=1).astype(BF16)
    sel_all = jnp.concatenate([sel_ref[g][:, :n_cmp] for g in range(NSA_KV_HEADS)], axis=0).astype(BF16)
    picked = _dot(sel_all, expand_ref[...])
    bias = per_head(jnp.where((picked > 0.5) & (kpos <= tpos_all), 0.0, NEG_INF))
    s = _dot(qbd_ref[...], kt_all) + bias
    m_old = m_ref[...]
    m_new = jnp.maximum(m_old, jnp.max(s, axis=-1, keepdims=True))
    alpha = jnp.exp(m_old - m_new)
    p = jnp.exp(s - m_new)
    m_ref[...] = m_new
    l_ref[...] = alpha * l_ref[...] + jnp.sum(p, axis=-1, keepdims=True)
    acc_ref[...] = alpha * acc_ref[...] + _dot_nt(p.astype(BF16), vt_all)

    @pl.when(c == pl.num_programs(1) - 1)
    def _():
        gate = gate_ref[0]
        rnew = lax.broadcasted_iota(jnp.int32, (1, tq), 1)
        newpos = past_len + rnew
        n_buf = wbuf_ref.shape[-1]
        bpos = past_len - n_buf + lax.broadcasted_iota(jnp.int32, (1, n_buf), 1)
        new_lane = 2 * (past_len // SEL_BLOCK)
        qbd = qbd_ref[...]
        picked = jnp.concatenate([sel_ref[g][:, new_lane:new_lane + 1] for g in range(NSA_KV_HEADS)], axis=0) > 0.5
        bias = per_head(jnp.where(picked & (newpos <= tpos_all) & (rnew < t_real), 0.0, NEG_INF))
        snew = snew_ref[0]
        s = _dot_nt(qbd, snew[:, :kvw].astype(BF16)) + bias
        m_old = m_ref[...]
        m_new = jnp.maximum(m_old, jnp.max(s, axis=-1, keepdims=True))
        alpha = jnp.exp(m_old - m_new)
        p = jnp.exp(s - m_new)
        o_slc = ((alpha * acc_ref[...] + _dot(p.astype(BF16), snew[:, kvw:].astype(BF16)))
                 / (alpha * l_ref[...] + jnp.sum(p, axis=-1, keepdims=True)))
        dist = tpos_all - bpos
        bias_b = per_head(jnp.where((dist >= 0) & (dist <= WINDOW) & (bpos >= 0), 0.0, NEG_INF))
        dist = tpos_all - newpos
        bias_n = per_head(jnp.where((dist >= 0) & (dist <= WINDOW) & (rnew < t_real), 0.0, NEG_INF))
        wnew = wnew_ref[0]
        sb = _dot(qbd, wbuf_ref[0, 0, 0].reshape(kvw, n_buf).astype(BF16)) + bias_b
        sn = _dot_nt(qbd, wnew[:, :kvw].astype(BF16)) + bias_n
        m_w = jnp.maximum(jnp.max(sb, axis=-1, keepdims=True), jnp.max(sn, axis=-1, keepdims=True))
        pb = jnp.exp(sb - m_w)
        pn = jnp.exp(sn - m_w)
        o_win = ((_dot_nt(pb.astype(BF16), wbuf_ref[0, 0, 1].reshape(kvw, n_buf).astype(BF16))
                  + _dot(pn.astype(BF16), wnew[:, kvw:].astype(BF16)))
                 / (jnp.sum(pb, axis=-1, keepdims=True) + jnp.sum(pn, axis=-1, keepdims=True)))
        heads = []
        for g in range(NSA_KV_HEADS):
            o_cmp = ocmp_ref[g]
            for hh in range(NSA_GROUP):
                head = g * NSA_GROUP + hh
                hr = slice(hh * tq, (hh + 1) * tq)
                ar = slice(g * rows + hh * tq, g * rows + (hh + 1) * tq)
                cs = slice(g * hd, (g + 1) * hd)
                col = (head // (NSA_HEADS // 2)) * LANES + (head % (NSA_HEADS // 2)) * 3
                heads.append(o_cmp[hr] * gate[:, col:col + 1] + o_slc[ar, cs] * gate[:, col + 1:col + 2]
                             + o_win[ar, cs] * gate[:, col + 2:col + 3])
        o_ref[0] = jnp.concatenate(heads, axis=-1)


SAMPLE_PAGES_PER_STEP = 16


def _attn_sample_call(qn, qr, kvc, gate, slc_new, win_buf_t, win_new, pool_t, layer, page_table, t_real):
    b, tq, nq = qn.shape
    n_pages = page_table.shape[1]
    page = pool_t.shape[-1]
    past_len = n_pages * page
    assert past_len % SEL_BLOCK == 0 and t_real <= SEL_BLOCK and past_len % (CMP_BLOCK * LANES) == 0
    n = SAMPLE_PAGES_PER_STEP
    rows = NSA_GROUP * tq
    full = lambda a: pl.BlockSpec((1,) + a.shape[1:], lambda i, c, pt: (i,) + (0,) * (a.ndim - 1))
    wbuf_spec = pl.BlockSpec((1, 1) + win_buf_t.shape[2:], lambda i, c, pt: (layer, i, 0, 0, 0, 0))
    n_cmp = past_len // CMP_BLOCK
    sel_lanes = n_cmp + LANES
    expand = (jnp.arange(n_cmp)[:, None] == (jnp.arange(past_len) // CMP_BLOCK)[None, :]).astype(BF16)
    return pl.pallas_call(
        functools.partial(_attn_sample_kernel, n, t_real, past_len),
        grid_spec=pltpu.PrefetchScalarGridSpec(
            num_scalar_prefetch=1,
            grid=(b, n_pages // n),
            in_specs=[full(a) for a in (qn, qr, kvc, gate, slc_new)] + [wbuf_spec, full(win_new)]
            + [pl.BlockSpec((n_cmp, n * page), lambda i, c, pt: (0, c))]
            + _page_specs(n, layer, pool_t.shape[2:]),
            out_specs=pl.BlockSpec((1, tq, nq), lambda i, c, pt: (i, 0, 0)),
            scratch_shapes=[pltpu.VMEM((NSA_KV_HEADS, tq, sel_lanes), F32),
                            pltpu.VMEM((NSA_KV_HEADS, rows, NSA_HEAD_DIM), F32),
                            pltpu.VMEM((NSA_KV_HEADS * rows, 1), F32),
                            pltpu.VMEM((NSA_KV_HEADS * rows, 1), F32),
                            pltpu.VMEM((NSA_KV_HEADS * rows, NSA_KV_WIDTH), F32),
                            pltpu.VMEM((NSA_KV_HEADS * rows, NSA_KV_WIDTH), BF16),
                            pltpu.VMEM((NSA_KV_HEADS * rows, NSA_KV_WIDTH), BF16)],
        ),
        out_shape=jax.ShapeDtypeStruct((b, tq, nq), F32),
        compiler_params=_cparams(2),
        name="nsa_attn_sample",
    )(page_table, qn, qr, kvc, gate, slc_new, win_buf_t, win_new, expand, *([pool_t] * n))


def _rope_tables(pos):
    half = NSA_HEAD_DIM // 2
    inv = ROPE_THETA ** (-jnp.arange(half, dtype=F32) / half)
    ang = pos.astype(F32)[:, None] * inv[None, :]
    cos, sin = jnp.cos(ang), jnp.sin(ang)
    reps = LANES // NSA_HEAD_DIM
    return jnp.tile(cos, (1, 2 * reps)), jnp.tile(jnp.concatenate([-sin, sin], axis=1), (1, reps))


def _nsa_params(w_in, g_q, g_ks, g_kw, g_kc, pe, w1, w2):
    d = w_in.shape[0]
    body = NSA_HEADS * NSA_HEAD_DIM + 6 * NSA_KV_WIDTH
    half = NSA_HEADS // 2 * 3
    zpad = jnp.zeros((d, LANES - half), w_in.dtype)
    w_pad = jnp.concatenate([w_in[:, :body], w_in[:, body:body + half], zpad, w_in[:, body + half:], zpad], axis=1)
    return dict(
        w_in=w_pad.astype(BF16),
        g_q=jnp.tile(g_q, NSA_HEADS)[None], g_ks=jnp.tile(g_ks, NSA_KV_HEADS)[None],
        g_kw=jnp.tile(g_kw, NSA_KV_HEADS)[None], g_kc=g_kc[None],
        pe=pe.reshape(2, CMP_BLOCK // 2, LANES), w1=w1.astype(BF16), w2=w2.astype(BF16))


def _kv_rows(x, b, t):
    return x.reshape(b, t, 2, NSA_KV_HEADS, NSA_HEAD_DIM)


def _nsa_prompt_layer(h, gmix, prm):
    b, t, d = h.shape
    cos, sin = _rope_tables(jnp.arange(t))
    qn, qr, cmp, slc, wrow, gate = _nsa_proj_call(
        h.reshape(b * t, d), gmix, prm["w_in"], prm["g_q"], prm["g_ks"], prm["g_kw"], cos, sin)
    r3 = lambda x: x.reshape(b, t, x.shape[-1])
    kvc = _compress_rows_call(r3(cmp), prm["pe"], prm["w1"], prm["w2"], prm["g_kc"])
    o = _attn_prompt_call(r3(qn), r3(qr), kvc, r3(slc), r3(wrow), r3(gate))
    keep = min(WINDOW, t)
    return (o.reshape(b * t, -1), _kv_rows(cmp, b, t), _kv_rows(slc, b, t), _kv_rows(wrow, b, t)[:, t - keep:])


def _nsa_sample_layer(h, cmp_pool_t, slc_pool_t, win_buf, win_buf_t, layer, page_table, gmix, prm):
    b, t, d = h.shape
    past_len = page_table.shape[1] * cmp_pool_t.shape[-1]
    cos, sin = _rope_tables(jnp.tile(past_len + jnp.arange(t), b))
    qn, qr, cmp, slc, wrow, gate = _nsa_proj_call(
        h.reshape(b * t, d), gmix, prm["w_in"], prm["g_q"], prm["g_ks"], prm["g_kw"], cos, sin)
    kvc = _compress_pages_call(cmp_pool_t, layer, page_table, prm["pe"], prm["w1"], prm["w2"], prm["g_kc"])
    tq = 8
    pad = lambda x: jnp.pad(x.reshape(b, t, x.shape[-1]), ((0, 0), (0, tq - t), (0, 0)))
    o = _attn_sample_call(pad(qn), pad(qr), kvc, pad(gate), pad(slc), win_buf_t, pad(wrow), slc_pool_t, layer,
                          page_table, t)
    new_win = jnp.concatenate([win_buf, _kv_rows(wrow, b, t)], axis=1)[:, t:]
    return o[:, :t].reshape(b * t, -1), _kv_rows(cmp, b, t), _kv_rows(slc, b, t), new_win


def _softplus(z):
    return jnp.maximum(z, 0.0) + jnp.log1p(jnp.exp(-jnp.abs(z)))


def _rwkv_proj_kernel(batch, h_ref, shift_ref, gmix_ref, mu_ref, wr_ref, wk_ref, wv_ref,
                      w0_ref, w1_ref, w2_ref, a0_ref, a1_ref, a2_ref, g1_ref, g2_ref,
                      r_ref, dec_ref, k_ref, v_ref, a_ref, g_ref, last_ref, carry_ref):
    i = pl.program_id(0)
    u = _rms(h_ref[...]) * gmix_ref[...]
    tm = u.shape[0]

    @pl.when(i == 0)
    def _():
        carry_ref[...] = shift_ref[...]
    prev = jnp.concatenate([carry_ref[...], u[:tm - batch]], axis=0)
    carry_ref[...] = u[tm - batch:]
    last_ref[...] = u[tm - batch:]
    xx = prev - u
    mix = lambda j: (u + xx * mu_ref[j:j + 1]).astype(BF16)
    r_ref[...] = _dot(mix(0), wr_ref[...])
    wl = w0_ref[...] + _dot(jnp.tanh(_dot(mix(1), w1_ref[...])).astype(BF16), w2_ref[...])
    dec_ref[...] = jnp.exp(-jnp.exp(-_softplus(-wl) - 0.5))
    k_ref[...] = _dot(mix(2), wk_ref[...])
    v_ref[...] = _dot(mix(3), wv_ref[...])
    a_ref[...] = jax.nn.sigmoid(a0_ref[...] + _dot(_dot(mix(4), a1_ref[...]).astype(BF16), a2_ref[...]))
    g_ref[...] = _dot(jax.nn.sigmoid(_dot(mix(5), g1_ref[...])).astype(BF16), g2_ref[...])


def _rwkv_proj_call(h, shift, gmix, prm):
    m, d = h.shape
    batch = shift.shape[0]
    tm = min(m, 256)
    assert tm % batch == 0 and tm > batch and batch % 8 == 0
    row = pl.BlockSpec((tm, d), lambda i: (i, 0))
    weights = [prm[n] for n in ("mu", "w_r", "w_k", "w_v", "w0", "w1", "w2", "a0", "a1", "a2", "g1", "g2")]
    outs = pl.pallas_call(
        functools.partial(_rwkv_proj_kernel, batch),
        grid=(m // tm,),
        in_specs=[row, _resident(shift.shape), _resident(gmix.shape)] + [_resident(w.shape) for w in weights],
        out_specs=[row] * 6 + [pl.BlockSpec((batch, d), lambda i: (0, 0))],
        out_shape=[jax.ShapeDtypeStruct((m, d), F32)] * 6 + [jax.ShapeDtypeStruct((batch, d), F32)],
        scratch_shapes=[pltpu.VMEM((batch, d), F32)],
        compiler_params=_cparams(1),
        name="rwkv_proj",
    )(h, shift, gmix, *weights)
    return outs[:6], outs[6]


def _rwkv_scan_kernel(tc, r_ref, dec_ref, k_ref, v_ref, a_ref, kk_ref, ka_ref, rk_ref, lw_ref, lb_ref, s0_ref,
                      y_ref, sout_ref, state_ref, vec_ref, stage_ref, ystage_ref):
    c = pl.program_id(1)
    n = RWKV_HEAD_DIM

    @pl.when(c == 0)
    def _():
        state_ref[...] = s0_ref[...]
        ystage_ref[...] = jnp.zeros_like(ystage_ref)

    def stage_in(t, slot):
        for i, ref in enumerate((r_ref, dec_ref, k_ref, v_ref, a_ref)):
            xa, xb = _rows_to_lanes(ref[t], ref[t + 1])
            stage_ref[i, 2 * slot] = xa
            stage_ref[i, 2 * slot + 1] = xb

    def step(t):
        r = stage_ref[0, t]
        k = stage_ref[2, t]
        v = stage_ref[3, t]
        a = stage_ref[4, t]
        kk = k * kk_ref[...]
        kk = kk / jnp.maximum(jnp.sqrt(jnp.sum(kk * kk, axis=0, keepdims=True)), 1e-12)
        k2 = k * (1.0 + (a - 1.0) * ka_ref[...])
        vec_ref[0] = -kk
        vec_ref[1] = kk * a
        vec_ref[2] = k2
        sa = jnp.zeros((n, LANES), F32)
        for j in range(n):
            sa = sa + state_ref[j] * vec_ref[0, j:j + 1, :]
        y = jnp.zeros((n, LANES), F32)
        for j in range(n):
            s_new = (state_ref[j] * stage_ref[1, t, j:j + 1, :] + sa * vec_ref[1, j:j + 1, :]
                     + v * vec_ref[2, j:j + 1, :])
            state_ref[j] = s_new
            y = y + s_new * stage_ref[0, t, j:j + 1, :]
        mean = jnp.mean(y, axis=0, keepdims=True)
        var = jnp.mean(jnp.square(y - mean), axis=0, keepdims=True)
        y = (y - mean) * lax.rsqrt(var + LNX_EPS) * lw_ref[...] + lb_ref[...]
        ystage_ref[t] = y + jnp.sum(r * k2 * rk_ref[...], axis=0, keepdims=True) * v

    def steps(slot):
        step(2 * slot)
        step(2 * slot + 1)

    def stage_out(slot, t):
        _lanes_to_rows(ystage_ref[2 * slot], ystage_ref[2 * slot + 1], y_ref, t)

    _pipelined_pairs(tc, stage_in, steps, stage_out)

    @pl.when(c == pl.num_programs(1) - 1)
    def _():
        sout_ref[...] = state_ref[...]


def _pipelined_pairs(tc, stage_in, steps, stage_out):
    stage_in(0, 0)

    def body(i, _):
        t = 4 * i
        stage_in(t + 2, 1)
        steps(0)
        stage_out(1, jnp.maximum(t - 2, 0))
        stage_in(jnp.minimum(t + 4, tc - 2), 0)
        steps(1)
        stage_out(0, t)
        return 0

    lax.fori_loop(0, tc // 4, body, 0)
    stage_out(1, tc - 2)


def _rows_to_lanes(xa, xb):
    chunks = xa.shape[1] // LANES
    stack = jnp.concatenate([x[:, c * LANES:(c + 1) * LANES] for x in (xa, xb) for c in range(chunks)], axis=0)
    tr = stack.T
    half = LANES // 2
    top, bot = tr[:half], tr[half:]
    low = _lane_iota() < half
    return (jnp.where(low, top, pltpu.roll(bot, half, axis=1)),
            jnp.where(low, pltpu.roll(top, half, axis=1), bot))


def _lanes_to_rows(ya, yb, out_ref, t):
    half = LANES // 2
    low = _lane_iota() < half
    top = jnp.where(low, ya, pltpu.roll(yb, half, axis=1))
    bot = jnp.where(low, pltpu.roll(ya, half, axis=1), yb)
    tr = jnp.concatenate([top, bot], axis=0).T
    chunks = out_ref.shape[2] // LANES
    for tok in range(2):
        for c in range(chunks):
            r0 = (tok * chunks + c) * 8
            out_ref[t + tok, :, c * LANES:(c + 1) * LANES] = tr[r0:r0 + 8]


def _lane_params(x, heads, width):
    per_chunk = LANES // width if width < LANES else 1
    x = x.reshape(heads // per_chunk, per_chunk, width)
    x = jnp.transpose(x, (2, 1, 0))
    return jnp.repeat(x[..., None], 8, axis=-1).reshape(width, LANES)


def _rwkv_scan_call(r, dec, k, v, a, kk, ka, rk, lw, lb, s0):
    t, b, d = r.shape
    n = RWKV_HEAD_DIM
    assert d == 8 * LANES and b % 8 == 0 and t % 4 == 0
    tc = math.gcd(t, 32)
    seq = pl.BlockSpec((tc, 8, d), lambda g, c: (c, g, 0))
    par = pl.BlockSpec((n, LANES), lambda g, c: (0, 0))
    st = pl.BlockSpec((n, n, LANES), lambda g, c: (0, 0, g))
    return pl.pallas_call(
        functools.partial(_rwkv_scan_kernel, tc),
        grid=(b // 8, t // tc),
        in_specs=[seq] * 5 + [par] * 5 + [st],
        out_specs=[seq, st],
        out_shape=[jax.ShapeDtypeStruct((t, b, d), F32), jax.ShapeDtypeStruct((n, n, b // 8 * LANES), F32)],
        scratch_shapes=[pltpu.VMEM((n, n, LANES), F32), pltpu.VMEM((3, n, LANES), F32),
                        pltpu.VMEM((5, 4, n, LANES), F32), pltpu.VMEM((4, n, LANES), F32)],
        compiler_params=_cparams(2),
        name="rwkv_scan",
    )(r, dec, k, v, a, kk, ka, rk, lw, lb, s0)


def _rwkv_params(mu, w_r, w_k, w_v, w0, w1, w2, a0, a1, a2, g1, g2):
    gpad = -g1.shape[1] % LANES
    return dict(mu=mu, w_r=w_r.astype(BF16), w_k=w_k.astype(BF16), w_v=w_v.astype(BF16), w0=w0[None],
                w1=w1.astype(BF16), w2=w2.astype(BF16), a0=a0[None], a1=a1.astype(BF16), a2=a2.astype(BF16),
                g1=jnp.pad(g1, ((0, 0), (0, gpad))).astype(BF16), g2=jnp.pad(g2, ((0, gpad), (0, 0))).astype(BF16))


def _rwkv_layer(h, batch, shift, s0, gmix, prm, k_k, k_a, r_k, lnx_w, lnx_b):
    m, d = h.shape
    t = m // batch
    n = RWKV_HEAD_DIM
    nh = d // n
    g8 = batch // 8
    (r, dec, k, v, a, g), new_shift = _rwkv_proj_call(h, shift, gmix, prm)
    rows = lambda x: x.reshape(t, batch, d)
    par = lambda x: _lane_params(x.reshape(-1), nh, n)
    st = s0.reshape(g8, 8, nh // 2, 2, n, n).transpose(5, 4, 0, 3, 2, 1).reshape(n, n, g8 * LANES)
    y, s = _rwkv_scan_call(rows(r), rows(dec), rows(k), rows(v), rows(a), par(k_k), par(k_a), par(r_k),
                           par(lnx_w), par(lnx_b), st)
    s = s.reshape(n, n, g8, 2, nh // 2, 8).transpose(2, 5, 4, 3, 1, 0).reshape(batch, nh, n, n)
    return y.reshape(m, d), g, new_shift, s


def _hgrn_proj_kernel(h_ref, gmix_ref, win_ref, lb_ref, omlb_ref, q_ref, f_ref, i_ref, g_ref):
    d = h_ref.shape[1]
    u = (_rms(h_ref[...]) * gmix_ref[...]).astype(BF16)
    z = _dot(u, win_ref[...])
    q, f, g = z[:, :d], z[:, d:2 * d], z[:, 3 * d:]
    q_ref[...] = q * jax.nn.sigmoid(q)
    f_ref[...] = lb_ref[...] + omlb_ref[...] * jax.nn.sigmoid(f)
    i_ref[...] = z[:, 2 * d:3 * d]
    g_ref[...] = g * jax.nn.sigmoid(g)


def _hgrn_proj_call(h, gmix, win, lb, omlb):
    m, d = h.shape
    tm = min(m, 256)
    row = pl.BlockSpec((tm, d), lambda i: (i, 0))
    return pl.pallas_call(
        _hgrn_proj_kernel,
        grid=(m // tm,),
        in_specs=[row, _resident(gmix.shape), _resident(win.shape), _resident(lb.shape), _resident(omlb.shape)],
        out_specs=[row] * 4,
        out_shape=[jax.ShapeDtypeStruct((m, d), F32)] * 4,
        compiler_params=_cparams(1),
        name="hgrn_proj",
    )(h, gmix, win, lb, omlb)


def _rows_to_lanes_dup(xa, xb):
    chunks = xa.shape[1] // LANES
    stack = jnp.concatenate([x[:, c * LANES:(c + 1) * LANES] for x in (xa, xb) for c in range(chunks)], axis=0)
    tr = stack.T
    half = LANES // 2
    low = _lane_iota() < half
    other = pltpu.roll(tr, half, axis=1)
    return jnp.where(low, tr, other), jnp.where(low, other, tr)


def _hgrn_scan_kernel(tc, q_ref, f_ref, v_ref, s0_ref, o_ref, sout_ref, state_ref, keys_ref, vals_ref):
    c = pl.program_id(1)
    dk = state_ref.shape[0]

    @pl.when(c == 0)
    def _():
        state_ref[...] = s0_ref[...]

    def stage_in(tp, _):
        t = 2 * tp
        for i, ref in enumerate((q_ref, f_ref)):
            xa, xb = _rows_to_lanes_dup(ref[t], ref[t + 1])
            keys_ref[i, t] = xa
            keys_ref[i, t + 1] = xb
            if i == 1:
                keys_ref[2, t] = 1.0 - xa
                keys_ref[2, t + 1] = 1.0 - xb
        va, vb = _rows_to_lanes(v_ref[t], v_ref[t + 1])
        vals_ref[t] = va
        vals_ref[t + 1] = vb
        return 0

    lax.fori_loop(0, tc // 2, stage_in, 0)

    def step(t, _):
        v = vals_ref[t]
        o = jnp.zeros(v.shape, F32)
        for d in range(dk):
            s_new = state_ref[d] * keys_ref[1, t, d:d + 1, :] + keys_ref[2, t, d:d + 1, :] * v
            state_ref[d] = s_new
            o = o + s_new * keys_ref[0, t, d:d + 1, :]
        vals_ref[t] = o
        return 0

    lax.fori_loop(0, tc, step, 0)

    def stage_out(tq, _):
        for u in range(2):
            t = 4 * tq + 2 * u
            _lanes_to_rows(vals_ref[t], vals_ref[t + 1], o_ref, t)
        return 0

    lax.fori_loop(0, tc // 4, stage_out, 0)

    @pl.when(c == pl.num_programs(1) - 1)
    def _():
        sout_ref[...] = state_ref[...]


def _hgrn_scan_call(q, f, v, s0):
    t, b, d = q.shape
    dk = HGRN_HEAD_DIM
    dv = dk // 2
    assert d == 8 * LANES and b % 8 == 0 and t % 4 == 0
    tc = math.gcd(t, 32)
    seq = pl.BlockSpec((tc, 8, d), lambda g, c: (c, g, 0))
    st = pl.BlockSpec((dk, dv, LANES), lambda g, c: (0, 0, g))
    return pl.pallas_call(
        functools.partial(_hgrn_scan_kernel, tc),
        grid=(b // 8, t // tc),
        in_specs=[seq] * 3 + [st],
        out_specs=[seq, st],
        out_shape=[jax.ShapeDtypeStruct((t, b, d), F32), jax.ShapeDtypeStruct((dk, dv, b // 8 * LANES), F32)],
        scratch_shapes=[pltpu.VMEM((dk, dv, LANES), F32), pltpu.VMEM((3, tc, dk, LANES), F32),
                        pltpu.VMEM((tc, dv, LANES), F32)],
        compiler_params=_cparams(2),
        name="hgrn_scan",
    )(q, f, v, s0)


def _hgrn_layer(h, batch, s0, gmix, win, lb, omlb):
    m, d = h.shape
    t = m // batch
    n = HGRN_HEAD_DIM
    nh = d // n
    g8 = batch // 8
    q, f, i, g = _hgrn_proj_call(h, gmix, win, lb, omlb)
    rows = lambda x: x.reshape(t, batch, d)
    st = s0.reshape(g8, 8, nh, n, 2, n // 2).transpose(3, 5, 0, 4, 2, 1).reshape(n, n // 2, g8 * LANES)
    o, s = _hgrn_scan_call(rows(q), rows(f), rows(i), st)
    s = s.reshape(n, n // 2, g8, 2, nh, 8).transpose(2, 5, 4, 0, 3, 1).reshape(batch, nh, n, n)
    return o.reshape(m, d), g, s


def kernel(x_prompt, x_sample, cache_cmp, cache_slc, cache_win, state_rwkv_shift, state_rwkv_wkv, state_hgrn,
           page_table, p_prompt, p_sample, norm_mix, norm_ffn, w_up, w_down, w_ple, w_ple_gate,
           nsa_w_in, nsa_g_q, nsa_g_ks, nsa_g_kw, nsa_g_kc, nsa_cmp_pe, nsa_cmp_w1, nsa_cmp_w2, nsa_w_out,
           rwkv_mu, rwkv_w_r, rwkv_w_k, rwkv_w_v, rwkv_w_o, rwkv_w0, rwkv_w1, rwkv_w2, rwkv_a0, rwkv_a1,
           rwkv_a2, rwkv_g1, rwkv_g2, rwkv_k_k, rwkv_k_a, rwkv_r_k, rwkv_lnx_w, rwkv_lnx_b,
           hgrn_w_in, hgrn_gn, hgrn_w_o, hgrn_lower_bounds):
    depth = norm_mix.shape[0]
    bp, tp, d = x_prompt.shape
    bs, ts, _ = x_sample.shape
    lb_soft = jax.nn.softmax(hgrn_lower_bounds.astype(F32), axis=0)
    lower_bound = jnp.cumsum(lb_soft, axis=0) - lb_soft[0]
    rows_minor = lambda x: jnp.transpose(x, (0, 1, 3, 4, 5, 2))
    cmp_pool_t, slc_pool_t, win_buf_t = rows_minor(cache_cmp), rows_minor(cache_slc), rows_minor(cache_win)
    hp, hs = x_prompt.reshape(bp * tp, d), x_sample.reshape(bs * ts, d)
    swap = lambda x, a, b: x.reshape(a, b, x.shape[-1]).transpose(1, 0, 2).reshape(a * b, x.shape[-1])
    time_major = False
    outs = [[] for _ in range(12)]
    for i in range(depth):
        kind, n = i % N_MIXERS, i // N_MIXERS
        gmix = norm_mix[i][None]
        ffn = (p_prompt[i].reshape(bp * tp, -1), p_sample[i].reshape(bs * ts, -1))
        if (kind != 0) != time_major:
            hp = swap(hp, tp, bp) if time_major else swap(hp, bp, tp)
            hs = swap(hs, ts, bs) if time_major else swap(hs, bs, ts)
            time_major = not time_major
        if time_major:
            ffn = (swap(ffn[0], bp, tp), swap(ffn[1], bs, ts))
        hp3, hs3 = hp.reshape(bp, tp, d), hs.reshape(bs, ts, d)
        tail = (norm_ffn[i][None], w_up[i].astype(BF16), w_down[i].astype(BF16), w_ple[i].astype(BF16),
                w_ple_gate[i].astype(BF16))
        if kind == 0:
            prm = _nsa_params(nsa_w_in[n], nsa_g_q[n], nsa_g_ks[n], nsa_g_kw[n], nsa_g_kc[n], nsa_cmp_pe[n],
                              nsa_cmp_w1[n], nsa_cmp_w2[n])
            op, rc, rs, rw = _nsa_prompt_layer(hp3, gmix, prm)
            os_, nc_rows, ns_rows, nw_buf = _nsa_sample_layer(hs3, cmp_pool_t, slc_pool_t, cache_win[n], win_buf_t,
                                                              n, page_table, gmix, prm)
            for lst, v in zip(outs[:6], (rc, nc_rows, rs, ns_rows, rw, nw_buf)):
                lst.append(v)
            wo = nsa_w_out[n].astype(BF16)
            hp = _ffn_call("nsa", hp, op, None, None, ffn[0], wo, *tail)
            hs = _ffn_call("nsa", hs, os_, None, None, ffn[1], wo, *tail)
        elif kind == 1:
            prm = _rwkv_params(rwkv_mu[n], rwkv_w_r[n], rwkv_w_k[n], rwkv_w_v[n], rwkv_w0[n], rwkv_w1[n],
                               rwkv_w2[n], rwkv_a0[n], rwkv_a1[n], rwkv_a2[n], rwkv_g1[n], rwkv_g2[n])
            vecs = (rwkv_k_k[n], rwkv_k_a[n], rwkv_r_k[n], rwkv_lnx_w[n], rwkv_lnx_b[n])
            nh = d // RWKV_HEAD_DIM
            zero_state = jnp.zeros((bp, nh, RWKV_HEAD_DIM, RWKV_HEAD_DIM), F32)
            yp, gp, shp, sp = _rwkv_layer(hp, bp, jnp.zeros((bp, d), F32), zero_state, gmix, prm, *vecs)
            ys, gs, shs, ss = _rwkv_layer(hs, bs, state_rwkv_shift[n], state_rwkv_wkv[n].astype(F32), gmix, prm,
                                          *vecs)
            for lst, v in zip(outs[6:10], (shp, shs, sp, ss)):
                lst.append(v)
            wo = rwkv_w_o[n].astype(BF16)
            hp = _ffn_call("rwkv", hp, yp, gp, None, ffn[0], wo, *tail)
            hs = _ffn_call("rwkv", hs, ys, gs, None, ffn[1], wo, *tail)
        else:
            lb = lower_bound[i][None]
            win = hgrn_w_in[n].astype(BF16)
            nh = d // HGRN_HEAD_DIM
            zero_state = jnp.zeros((bp, nh, HGRN_HEAD_DIM, HGRN_HEAD_DIM), F32)
            op, gp, sp = _hgrn_layer(hp, bp, zero_state, gmix, win, lb, 1.0 - lb)
            os_, gs, ss = _hgrn_layer(hs, bs, state_hgrn[n].astype(F32), gmix, win, lb, 1.0 - lb)
            outs[10].append(sp)
            outs[11].append(ss)
            wo, gn = hgrn_w_o[n].astype(BF16), hgrn_gn[n][None]
            hp = _ffn_call("hgrn", hp, op, gp, gn, ffn[0], wo, *tail)
            hs = _ffn_call("hgrn", hs, os_, gs, gn, ffn[1], wo, *tail)
    if time_major:
        hp, hs = swap(hp, tp, bp), swap(hs, ts, bs)
    return (hp.reshape(bp, tp, d), hs.reshape(bs, ts, d)) + tuple(jnp.stack(o) for o in outs)
```

```python
import functools
import math

import jax
import jax.numpy as jnp
from jax import lax
from jax.experimental import pallas as pl
from jax.experimental.pallas import tpu as pltpu

F32 = jnp.float32
BF16 = jnp.bfloat16

NORM_EPS = 1e-6
ROPE_THETA = 10000.0
NEG_INF = -1e30
M_INIT = -1e29
N_MIXERS = 3

NSA_HEADS = 16
NSA_KV_HEADS = 4
NSA_HEAD_DIM = 64
NSA_GROUP = NSA_HEADS // NSA_KV_HEADS
NSA_KV_WIDTH = NSA_KV_HEADS * NSA_HEAD_DIM
NSA_SCALE = NSA_HEAD_DIM ** -0.5
LOG2E = math.log2(math.e)
CMP_BLOCK = 32
SEL_BLOCK = 64
N_SELECT = 16
WINDOW = 512
FORCE_SCORE = 1e4

RWKV_HEAD_DIM = 64
LNX_EPS = 64e-5
HGRN_HEAD_DIM = 128

LANES = 128
VMEM_LIMIT = 56 * 1024 * 1024


def _cparams(n_axes):
    return pltpu.CompilerParams(dimension_semantics=("arbitrary",) * n_axes,
                                vmem_limit_bytes=VMEM_LIMIT)


def _resident(shape):
    zeros = (0,) * len(shape)
    return pl.BlockSpec(shape, lambda *_: zeros, pipeline_mode=pl.Buffered(1))


def _rms(x):
    return x * lax.rsqrt(jnp.mean(x * x, axis=-1, keepdims=True) + NORM_EPS)


def _dot(a, b):
    return jnp.dot(a, b, preferred_element_type=F32)


def _dot_nt(a, b):
    return lax.dot_general(a, b, (((1,), (1,)), ((), ())), preferred_element_type=F32)


def _per_chunk(fn, x, *rest):
    n = x.shape[-1] // LANES
    outs = [fn(*(a[:, c * LANES:(c + 1) * LANES] for a in (x,) + rest)) for c in range(n)]
    return outs[0] if n == 1 else jnp.concatenate(outs, axis=-1)


def _lane_iota():
    return lax.broadcasted_iota(jnp.int32, (1, LANES), 1)


def _group_ones(group):
    r = lax.broadcasted_iota(jnp.int32, (LANES, LANES), 0) // group
    c = lax.broadcasted_iota(jnp.int32, (LANES, LANES), 1) // group
    return jnp.where(r == c, 1.0, 0.0).astype(BF16)


def _head_rms(x, head_dim, ones):
    def one(c):
        sq = c * c
        hi = sq.astype(BF16)
        lo = (sq - hi.astype(F32)).astype(BF16)
        ss = _dot(hi, ones) + _dot(lo, ones)
        return c * lax.rsqrt(ss * (1.0 / head_dim) + NORM_EPS)
    return _per_chunk(one, x)


def _rope(x, cos, sin_signed):
    half = NSA_HEAD_DIM // 2
    lane = _lane_iota()

    def one(c):
        up = pltpu.roll(c, LANES - half, axis=1)
        dn = pltpu.roll(c, half, axis=1)
        rot = jnp.where((lane & half) == 0, up, dn)
        return c * cos + rot * sin_signed
    return _per_chunk(one, x)


def _ffn_kernel(mode, ff_chunk, *refs):
    if mode == "nsa":
        h_ref, o_ref, p_ref, wo_ref, gffn_ref, wup_ref, wdown_ref, wple_ref, wgate_ref, out_ref = refs
        o = o_ref[...]
    elif mode == "rwkv":
        h_ref, o_ref, aux_ref, p_ref, wo_ref, gffn_ref, wup_ref, wdown_ref, wple_ref, wgate_ref, out_ref = refs
        o = o_ref[...] * aux_ref[...]
    else:
        (h_ref, o_ref, aux_ref, gn_ref, p_ref, wo_ref, gffn_ref, wup_ref, wdown_ref, wple_ref, wgate_ref,
         out_ref) = refs
        gn = gn_ref[...]
        o = _per_chunk(lambda c: _rms(c) * gn, o_ref[...]) * aux_ref[...]
    h1 = h_ref[...] + _dot(o.astype(BF16), wo_ref[...])
    u = (_rms(h1) * gffn_ref[...]).astype(BF16)
    d_ff = wup_ref.shape[1]
    acc = jnp.zeros_like(h1)
    for j in range(d_ff // ff_chunk):
        a = _dot(u, wup_ref[:, j * ff_chunk:(j + 1) * ff_chunk])
        a = jnp.square(jnp.maximum(a, 0.0)).astype(BF16)
        acc = acc + _dot(a, wdown_ref[j * ff_chunk:(j + 1) * ff_chunk, :])
    h2 = h1 + acc
    gate = jax.nn.sigmoid(_dot(_rms(h2).astype(BF16), wgate_ref[...]))
    out_ref[...] = h2 + _dot(p_ref[...].astype(BF16), wple_ref[...]) * gate


def _ffn_call(mode, h, o, aux, gn, p, wo, gffn, wup, wdown, wple, wgate):
    m, d = h.shape
    tm = min(m, 512)
    row = lambda w: pl.BlockSpec((tm, w), lambda i: (i, 0))
    args, specs = [h, o], [row(d), row(d)]
    if mode != "nsa":
        args.append(aux)
        specs.append(row(d))
    if mode == "hgrn":
        args.append(gn)
        specs.append(_resident(gn.shape))
    args += [p, wo, gffn, wup, wdown, wple, wgate]
    specs += [row(p.shape[1])] + [_resident(a.shape) for a in (wo, gffn, wup, wdown, wple, wgate)]
    return pl.pallas_call(
        functools.partial(_ffn_kernel, mode, 1024),
        grid=(m // tm,),
        in_specs=specs,
        out_specs=row(d),
        out_shape=jax.ShapeDtypeStruct((m, d), F32),
        compiler_params=_cparams(1),
        name="ffn_" + mode,
    )(*args)


def _nsa_proj_kernel(h_ref, gmix_ref, win_ref, gq_ref, gks_ref, gkw_ref, cos_ref, sin_ref,
                     qn_ref, qr_ref, cmp_ref, slc_ref, wrow_ref, gate_ref):
    kvw = NSA_KV_WIDTH
    nq = NSA_HEADS * NSA_HEAD_DIM
    u = (_rms(h_ref[...]) * gmix_ref[...]).astype(BF16)
    z = _dot(u, win_ref[...])
    cos = cos_ref[...]
    sin = sin_ref[...]
    ones = _group_ones(NSA_HEAD_DIM)
    qn = _head_rms(z[:, :nq], NSA_HEAD_DIM, ones) * gq_ref[...]
    qn_ref[...] = qn
    qr_ref[...] = _rope(qn, cos, sin)
    cmp_ref[...] = z[:, nq:nq + 2 * kvw]
    o = nq + 2 * kvw
    slc_ref[:, :kvw] = _rope(_head_rms(z[:, o:o + kvw], NSA_HEAD_DIM, ones) * gks_ref[...], cos, sin)
    slc_ref[:, kvw:] = z[:, o + kvw:o + 2 * kvw]
    o += 2 * kvw
    wrow_ref[:, :kvw] = _rope(_head_rms(z[:, o:o + kvw], NSA_HEAD_DIM, ones) * gkw_ref[...], cos, sin)
    wrow_ref[:, kvw:] = z[:, o + kvw:o + 2 * kvw]
    o += 2 * kvw
    gate_ref[...] = jax.nn.sigmoid(z[:, o:])


def _nsa_proj_call(h, gmix, win, gq, gks, gkw, cos, sin):
    m, d = h.shape
    tm = min(m, 512)
    tab_tiles = cos.shape[0] // tm
    row = lambda w: pl.BlockSpec((tm, w), lambda i: (i, 0))
    tab = pl.BlockSpec((tm, LANES), lambda i: (i % tab_tiles, 0))
    nq = NSA_HEADS * NSA_HEAD_DIM
    widths = [nq, nq] + [2 * NSA_KV_WIDTH] * 3 + [2 * LANES]
    return pl.pallas_call(
        _nsa_proj_kernel,
        grid=(m // tm,),
        in_specs=[row(d), _resident(gmix.shape), _resident(win.shape), _resident(gq.shape),
                  _resident(gks.shape), _resident(gkw.shape), tab, tab],
        out_specs=[row(w) for w in widths],
        out_shape=[jax.ShapeDtypeStruct((m, w), F32) for w in widths],
        compiler_params=_cparams(1),
        name="nsa_proj",
    )(h, gmix, win, gq, gks, gkw, cos, sin)


def _compress_slot(load_pair, m, e, pe_ref, w1_ref, w2_ref, gkc_ref, flat_ref):
    lane = _lane_iota()
    low = lane < NSA_HEAD_DIM
    for s in range(CMP_BLOCK // 2):
        pe_row = pe_ref[e, s:s + 1, :]
        parts = []
        for kp in range(NSA_KV_HEADS // 2):
            a = load_pair(kp, 2 * s)
            b = load_pair(kp, 2 * s + 1)
            parts.append(jnp.where(low, a, pltpu.roll(b, NSA_HEAD_DIM, axis=1)))
            parts.append(jnp.where(low, pltpu.roll(a, NSA_HEAD_DIM, axis=1), b))
        flat_ref[:, s * LANES:(s + 1) * LANES] = (jnp.concatenate(parts, axis=0) + pe_row).astype(BF16)
    out = _dot(jax.nn.gelu(_dot(flat_ref[...], w1_ref[e])).astype(BF16), w2_ref[e])
    if e == 0:
        out = _rms(out) * gkc_ref[...]
    return jnp.concatenate([out[k * m:(k + 1) * m] for k in range(NSA_KV_HEADS)], axis=-1)


def _compress_rows_kernel(*refs):
    cols = refs[:NSA_KV_HEADS]
    pe_ref, w1_ref, w2_ref, gkc_ref, out_ref, flat_ref = refs[NSA_KV_HEADS:]
    m = cols[0].shape[1] // CMP_BLOCK
    pad = out_ref.shape[2] - m
    for e in range(2):
        load = lambda kp, r: cols[2 * e + kp][0, pl.ds(r, m, stride=CMP_BLOCK), :]
        res = _compress_slot(load, m, e, pe_ref, w1_ref, w2_ref, gkc_ref, flat_ref)
        out_ref[0, e] = jnp.concatenate([res, jnp.zeros((pad, res.shape[1]), F32)], axis=0)


def _compress_rows_call(cmp, pe, w1, w2, gkc):
    b, t, w = cmp.shape
    c = -(-(t // CMP_BLOCK) // LANES) * LANES
    return pl.pallas_call(
        _compress_rows_kernel,
        grid=(b,),
        in_specs=[pl.BlockSpec((1, t, LANES), functools.partial(lambda j, i: (i, 0, j), j)) for j in range(w // LANES)]
        + [_resident(a.shape) for a in (pe, w1, w2, gkc)],
        out_specs=pl.BlockSpec((1, 2, c, NSA_KV_WIDTH), lambda i: (i, 0, 0, 0)),
        out_shape=jax.ShapeDtypeStruct((b, 2, c, NSA_KV_WIDTH), F32),
        scratch_shapes=[pltpu.VMEM((NSA_KV_HEADS * (t // CMP_BLOCK), w1.shape[1]), BF16)],
        compiler_params=_cparams(1),
        name="nsa_compress_rows",
    )(*([cmp] * (w // LANES)), pe, w1, w2, gkc)


PAGES_PER_STEP = 32


def _page_specs(n, layer, tail):
    zeros = (0,) * len(tail)
    return [pl.BlockSpec((1, 1) + tail,
                         functools.partial(lambda i, b, c, pt: (layer, pt[b, c * n + i]) + zeros, i))
            for i in range(n)]


def _compress_pages_kernel(pt_ref, *refs):
    n = len(refs) - 7
    pages = refs[:n]
    pe_ref, w1_ref, w2_ref, gkc_ref, out_ref, rows_ref, flat_ref = refs[n:]
    page = pages[0].shape[-1]
    m = n * page // CMP_BLOCK
    for e in range(2):
        for i, pg in enumerate(pages):
            for kp in range(NSA_KV_HEADS // 2):
                tile = pg[0, 0, e, 2 * kp:2 * kp + 2].reshape(LANES, page)
                rows_ref[e, kp, i * page:(i + 1) * page, :] = tile.T
    for e in range(2):
        load = lambda kp, r: rows_ref[e, kp, pl.ds(r, m, stride=CMP_BLOCK), :]
        out_ref[0, e] = _compress_slot(load, m, e, pe_ref, w1_ref, w2_ref, gkc_ref, flat_ref.at[e])


def _compress_pages_call(pool_t, layer, page_table, pe, w1, w2, gkc):
    b, n_pages = page_table.shape
    page = pool_t.shape[-1]
    n = PAGES_PER_STEP
    m = n * page // CMP_BLOCK
    return pl.pallas_call(
        _compress_pages_kernel,
        grid_spec=pltpu.PrefetchScalarGridSpec(
            num_scalar_prefetch=1,
            grid=(b, n_pages // n),
            in_specs=_page_specs(n, layer, pool_t.shape[2:])
            + [pl.BlockSpec(a.shape, functools.partial(lambda nd, i, c, pt: (0,) * nd, a.ndim),
                            pipeline_mode=pl.Buffered(1)) for a in (pe, w1, w2, gkc)],
            out_specs=pl.BlockSpec((1, 2, m, NSA_KV_WIDTH), lambda i, c, pt: (i, 0, c, 0)),
            scratch_shapes=[pltpu.VMEM((2, NSA_KV_HEADS // 2, n * page, LANES), F32),
                            pltpu.VMEM((2, NSA_KV_HEADS * m, w1.shape[1]), BF16)],
        ),
        out_shape=jax.ShapeDtypeStruct((b, 2, n_pages * page // CMP_BLOCK, NSA_KV_WIDTH), F32),
        compiler_params=_cparams(2),
        name="nsa_compress_pages",
    )(page_table, *([pool_t] * n), pe, w1, w2, gkc)


def _select_blocks_t(imp, tpos):
    n_blocks = imp.shape[0]
    blk = lax.broadcasted_iota(jnp.int32, (n_blocks, 1), 0)
    cur = tpos // SEL_BLOCK
    forced = (blk == 0) | (blk == cur) | (blk == cur - 1)
    score = jnp.where(blk <= cur, jnp.where(forced, FORCE_SCORE, imp), -1.0)
    rank = jnp.zeros(score.shape, F32)
    for j in range(n_blocks):
        row = score[j:j + 1, :]
        below = jnp.where(j < blk, 1.0, 0.0)
        rank = rank + jnp.where(row > score, 1.0, jnp.where(row == score, below, 0.0))
    return jnp.where(rank < float(min(N_SELECT, n_blocks)), 1.0, 0.0)


def _attn_prompt_kernel(tq, tk, qn_ref, qr_ref, kvc_ref, sk_ref, sv_ref, wk_ref, wv_ref, gate_ref, o_ref,
                        kb_ref, vt_ref, sel_ref, imp_ref):
    qi = pl.program_id(2)
    hd = NSA_HEAD_DIM
    t_len = sk_ref.shape[1]
    c_blocks = kvc_ref.shape[2]
    n_sel = t_len // SEL_BLOCK
    ratio = SEL_BLOCK // CMP_BLOCK
    rows_per_tile = tk // SEL_BLOCK

    @pl.when(qi == 0)
    def _():
        for br, (k_ref, v_ref) in enumerate(((sk_ref, sv_ref), (wk_ref, wv_ref))):
            for gl in range(2):
                kb_ref[br, gl] = k_ref[0, :, gl * hd:(gl + 1) * hd].astype(BF16)
            for kt in range(t_len // tk):
                vt_ref[br, kt] = v_ref[0, kt * tk:(kt + 1) * tk, :].T.astype(BF16)

    t0 = qi * tq
    tpos = t0 + lax.broadcasted_iota(jnp.int32, (1, tq), 1)
    gate_t = gate_ref[0].T
    qn_t = [qn_ref[0, :, c * LANES:(c + 1) * LANES].T for c in range(2 * NSA_GROUP * hd // LANES)]
    qr_t = [qr_ref[0, :, c * LANES:(c + 1) * LANES].T for c in range(2 * NSA_GROUP * hd // LANES)]
    cid = lax.broadcasted_iota(jnp.int32, (c_blocks, 1), 0)
    ok = jnp.where((cid + 1) * CMP_BLOCK - 1 <= tpos, 1.0, 0.0)
    ok4 = jnp.concatenate([ok] * NSA_GROUP, axis=1)
    qr4s, o_cmps = [], []
    for gl in range(2):
        per_head = lambda parts: [parts[2 * gl + hh // 2][(hh % 2) * hd:(hh % 2 + 1) * hd] for hh in range(NSA_GROUP)]
        qn4 = (jnp.concatenate(per_head(qn_t), axis=1) * NSA_SCALE).astype(BF16)
        qr4s.append((jnp.concatenate(per_head(qr_t), axis=1) * (NSA_SCALE * LOG2E)).astype(BF16))
        kc = kvc_ref[0, 0][:, gl * hd:(gl + 1) * hd]
        vc_t = kvc_ref[0, 1].T[gl * hd:(gl + 1) * hd]
        s = jnp.where(ok4 > 0.5, _dot(kc.astype(BF16), qn4), NEG_INF)
        e = jnp.exp(s - jnp.max(s, axis=0, keepdims=True))
        p = e / jnp.sum(e, axis=0, keepdims=True) * ok4
        o_cmps.append(_dot(vc_t.astype(BF16), p.astype(BF16)))
        imp = p[:, :tq]
        for hh in range(1, NSA_GROUP):
            imp = imp + p[:, hh * tq:(hh + 1) * tq]
        parts = []
        for j in range(tq // LANES):
            imp_ref[j] = imp[:, j * LANES:(j + 1) * LANES]
            part = imp_ref[j, pl.ds(0, n_sel, stride=ratio), :]
            for i in range(1, ratio):
                part = part + imp_ref[j, pl.ds(i, n_sel, stride=ratio), :]
            parts.append(part)
        imp_sel = parts[0] if len(parts) == 1 else jnp.concatenate(parts, axis=1)
        sel_ref[gl] = _select_blocks_t(imp_sel, tpos)

    def flash(br, lo, hi, valid_fn):
        def body(kt, carry):
            k0 = pl.multiple_of(kt * tk, tk)
            out = []
            for gl in range(2):
                m, l, acc = carry[gl]
                kb = kb_ref[br, gl, pl.ds(k0, tk), :]
                vt = vt_ref[br, kt, gl * hd:(gl + 1) * hd, :]
                bias = jnp.where(valid_fn(gl, kt, k0), 0.0, NEG_INF)
                s = _dot(kb, qr4s[gl]) + jnp.concatenate([bias] * NSA_GROUP, axis=1)
                m_new = jnp.maximum(m, jnp.max(s, axis=0, keepdims=True))
                alpha = jnp.exp2(m - m_new)
                p = jnp.exp2(s - m_new)
                out.append((m_new, alpha * l + jnp.sum(p, axis=0, keepdims=True),
                            alpha * acc + _dot(vt, p.astype(BF16))))
            return tuple(out)
        cols = NSA_GROUP * tq
        init = tuple((jnp.full((1, cols), M_INIT, F32), jnp.zeros((1, cols), F32), jnp.zeros((hd, cols), F32))
                     for _ in range(2))
        res = lax.fori_loop(lo, hi, body, init)
        return [acc / l for _, l, acc in res]

    def slc_valid(gl, kt, k0):
        picked = sel_ref[gl, pl.ds(pl.multiple_of(kt * rows_per_tile, rows_per_tile), rows_per_tile), :]
        picked = jnp.concatenate([jnp.broadcast_to(picked[i:i + 1], (SEL_BLOCK, tq))
                                  for i in range(rows_per_tile)], axis=0)
        kpos = k0 + lax.broadcasted_iota(jnp.int32, (tk, 1), 0)
        return (picked > 0.5) & (kpos <= tpos)

    def win_valid(gl, kt, k0):
        dist = tpos - (k0 + lax.broadcasted_iota(jnp.int32, (tk, 1), 0))
        return (dist >= 0) & (dist <= WINDOW)

    hi = (t0 + tq - 1) // tk + 1
    slc = flash(0, 0, hi, slc_valid)
    win = flash(1, jnp.maximum(t0 - WINDOW, 0) // tk, hi, win_valid)
    heads = []
    for gl in range(2):
        for hh in range(NSA_GROUP):
            c = (gl * NSA_GROUP + hh) * 3
            cols = slice(hh * tq, (hh + 1) * tq)
            heads.append(o_cmps[gl][:, cols] * gate_t[c:c + 1] + slc[gl][:, cols] * gate_t[c + 1:c + 2]
                         + win[gl][:, cols] * gate_t[c + 2:c + 3])
    for c in range(len(heads) // 2):
        o_ref[0, :, c * LANES:(c + 1) * LANES] = jnp.concatenate(heads[2 * c:2 * c + 2], axis=0).T


def _attn_prompt_call(qn, qr, kvc, slc, win, gate, tq=512, tk=512):
    b, t, nq = qn.shape
    pair_w = 2 * NSA_GROUP * NSA_HEAD_DIM
    c = kvc.shape[2]
    assert tk % (8 * SEL_BLOCK) == 0 and t % tk == 0 and c % LANES == 0 and tq % LANES == 0 and t % tq == 0
    qspec = pl.BlockSpec((1, tq, pair_w), lambda i, p, j: (i, j, p))
    kspec = pl.BlockSpec((1, t, LANES), lambda i, p, j: (i, 0, p))
    vspec = pl.BlockSpec((1, t, LANES), lambda i, p, j: (i, 0, 2 + p))
    return pl.pallas_call(
        functools.partial(_attn_prompt_kernel, tq, tk),
        grid=(b, 2, t // tq),
        in_specs=[qspec, qspec,
                  pl.BlockSpec((1, 2, c, LANES), lambda i, p, j: (i, 0, 0, p)),
                  kspec, vspec, kspec, vspec,
                  pl.BlockSpec((1, tq, LANES), lambda i, p, j: (i, j, p))],
        out_specs=qspec,
        out_shape=jax.ShapeDtypeStruct((b, t, nq), F32),
        scratch_shapes=[pltpu.VMEM((2, 2, t, NSA_HEAD_DIM), BF16),
                        pltpu.VMEM((2, t // tk, LANES, tk), BF16),
                        pltpu.VMEM((2, t // SEL_BLOCK, tq), F32),
                        pltpu.VMEM((tq // LANES, c, LANES), F32)],
        compiler_params=_cparams(3),
        name="nsa_attn_prompt",
    )(qn, qr, kvc, slc, slc, win, win, gate)


def _select_blocks(imp, tpos, n_blocks):
    c = imp.shape[1]
    lane = lax.broadcasted_iota(jnp.int32, (1, c), 1)

    def pair(x):
        lane1 = _lane_iota()
        return x + jnp.where((lane1 & 1) == 0, pltpu.roll(x, LANES - 1, axis=1), pltpu.roll(x, 1, axis=1))
    imp2 = _per_chunk(pair, imp)
    blk = lane // 2
    cur = tpos // SEL_BLOCK
    forced = (blk == 0) | (blk == cur) | (blk == cur - 1)
    score = jnp.where(blk <= cur, jnp.where(forced, FORCE_SCORE, imp2), -1.0)
    rank = jnp.zeros(score.shape, F32)
    for j in range(n_blocks):
        col = score[:, 2 * j:2 * j + 1]
        ahead = (col > score) | ((col == score) & (j < blk))
        rank = rank + ahead.astype(F32)
    return (rank < float(min(N_SELECT, n_blocks))).astype(F32)


def _attn_sample_kernel(n_step, t_real, past_len, pt_ref, qn_ref, qr_ref, kvc_ref, gate_ref, snew_ref, wbuf_ref,
                        wnew_ref, expand_ref, *rest):
    pages = rest[:n_step]
    o_ref, sel_ref, ocmp_ref, m_ref, l_ref, acc_ref, qbd_ref, qnbd_ref = rest[n_step:]
    c = pl.program_id(1)
    hd = NSA_HEAD_DIM
    kvw = NSA_KV_WIDTH
    tq = qn_ref.shape[1]
    n_cmp = past_len // CMP_BLOCK
    tpos = past_len + lax.broadcasted_iota(jnp.int32, (tq, 1), 0) % t_real
    stack = lambda ref, g: jnp.concatenate(
        [ref[0, :, (g * NSA_GROUP + hh) * hd:(g * NSA_GROUP + hh + 1) * hd] for hh in range(NSA_GROUP)], axis=0)
    rows = NSA_GROUP * tq

    tpos_all = jnp.concatenate([tpos] * NSA_KV_HEADS, axis=0)
    per_head = lambda x: jnp.concatenate(
        [x[g * tq:(g + 1) * tq] for g in range(NSA_KV_HEADS) for _ in range(NSA_GROUP)], axis=0)

    @pl.when(c == 0)
    def _():
        n_blocks = -(-(past_len + t_real) // SEL_BLOCK)
        zero = jnp.zeros((rows, hd), BF16)
        for g in range(NSA_KV_HEADS):
            for src, dst in ((qn_ref, qnbd_ref), (qr_ref, qbd_ref)):
                q4 = (stack(src, g) * NSA_SCALE).astype(BF16)
                dst[g * rows:(g + 1) * rows, :] = jnp.concatenate(
                    [q4 if gg == g else zero for gg in range(NSA_KV_HEADS)], axis=1)
        cidx = lax.broadcasted_iota(jnp.int32, (1, n_cmp), 1)
        ok = per_head(jnp.where((cidx + 1) * CMP_BLOCK - 1 <= tpos_all, 1.0, 0.0))
        s = jnp.where(ok > 0.5, _dot_nt(qnbd_ref[...], kvc_ref[0, 0].astype(BF16)), NEG_INF)
        e = jnp.exp(s - jnp.max(s, axis=-1, keepdims=True))
        p = e / jnp.sum(e, axis=-1, keepdims=True) * ok
        o_all = _dot(p.astype(BF16), kvc_ref[0, 1].astype(BF16))
        imps = []
        for g in range(NSA_KV_HEADS):
            ocmp_ref[g] = o_all[g * rows:(g + 1) * rows, g * hd:(g + 1) * hd]
            imp = p[g * rows:g * rows + tq]
            for hh in range(1, NSA_GROUP):
                imp = imp + p[g * rows + hh * tq:g * rows + (hh + 1) * tq]
            imps.append(imp)
        imp_all = jnp.concatenate(imps, axis=0)
        imp_all = jnp.concatenate([imp_all, jnp.zeros((imp_all.shape[0], LANES), F32)], axis=-1)
        sel_all = _select_blocks(imp_all, tpos_all, n_blocks)
        for g in range(NSA_KV_HEADS):
            sel_ref[g] = sel_all[g * tq:(g + 1) * tq]
        m_ref[...] = jnp.full(m_ref.shape, M_INIT, F32)
        l_ref[...] = jnp.zeros(l_ref.shape, F32)
        acc_ref[...] = jnp.zeros(acc_ref.shape, F32)

    tk = n_step * pages[0].shape[-1]
    k0 = c * tk
    kpos = k0 + lax.broadcasted_iota(jnp.int32, (1, tk), 1)
    kt_all = jnp.concatenate([pg[0, 0, 0].reshape(kvw, -1) for pg in pages], axis=1).astype(BF16)
    vt_all = jnp.concatenate([pg[0, 0, 1].reshape(kvw, -1) for pg in pages], axis=1).astype(BF16)
    sel_all = jnp.concatenate([sel_ref[g][:, :n_cmp] for g in range(NSA_KV_HEADS)], axis=0).astype(BF16)
    picked = _dot(sel_all, expand_ref[...])
    bias = per_head(jnp.where((picked > 0.5) & (kpos <= tpos_all), 0.0, NEG_INF))
    s = _dot(qbd_ref[...], kt_all) + bias
    m_old = m_ref[...]
    m_new = jnp.maximum(m_old, jnp.max(s, axis=-1, keepdims=True))
    alpha = jnp.exp(m_old - m_new)
    p = jnp.exp(s - m_new)
    m_ref[...] = m_new
    l_ref[...] = alpha * l_ref[...] + jnp.sum(p, axis=-1, keepdims=True)
    acc_ref[...] = alpha * acc_ref[...] + _dot_nt(p.astype(BF16), vt_all)

    @pl.when(c == pl.num_programs(1) - 1)
    def _():
        gate = gate_ref[0]
        rnew = lax.broadcasted_iota(jnp.int32, (1, tq), 1)
        newpos = past_len + rnew
        n_buf = wbuf_ref.shape[-1]
        bpos = past_len - n_buf + lax.broadcasted_iota(jnp.int32, (1, n_buf), 1)
        new_lane = 2 * (past_len // SEL_BLOCK)
        qbd = qbd_ref[...]
        picked = jnp.concatenate([sel_ref[g][:, new_lane:new_lane + 1] for g in range(NSA_KV_HEADS)], axis=0) > 0.5
        bias = per_head(jnp.where(picked & (newpos <= tpos_all) & (rnew < t_real), 0.0, NEG_INF))
        snew = snew_ref[0]
        s = _dot_nt(qbd, snew[:, :kvw].astype(BF16)) + bias
        m_old = m_ref[...]
        m_new = jnp.maximum(m_old, jnp.max(s, axis=-1, keepdims=True))
        alpha = jnp.exp(m_old - m_new)
        p = jnp.exp(s - m_new)
        o_slc = ((alpha * acc_ref[...] + _dot(p.astype(BF16), snew[:, kvw:].astype(BF16)))
                 / (alpha * l_ref[...] + jnp.sum(p, axis=-1, keepdims=True)))
        dist = tpos_all - bpos
        bias_b = per_head(jnp.where((dist >= 0) & (dist <= WINDOW) & (bpos >= 0), 0.0, NEG_INF))
        dist = tpos_all - newpos
        bias_n = per_head(jnp.where((dist >= 0) & (dist <= WINDOW) & (rnew < t_real), 0.0, NEG_INF))
        wnew = wnew_ref[0]
        sb = _dot(qbd, wbuf_ref[0, 0, 0].reshape(kvw, n_buf).astype(BF16)) + bias_b
        sn = _dot_nt(qbd, wnew[:, :kvw].astype(BF16)) + bias_n
        m_w = jnp.maximum(jnp.max(sb, axis=-1, keepdims=True), jnp.max(sn, axis=-1, keepdims=True))
        pb = jnp.exp(sb - m_w)
        pn = jnp.exp(sn - m_w)
        o_win = ((_dot_nt(pb.astype(BF16), wbuf_ref[0, 0, 1].reshape(kvw, n_buf).astype(BF16))
                  + _dot(pn.astype(BF16), wnew[:, kvw:].astype(BF16)))
                 / (jnp.sum(pb, axis=-1, keepdims=True) + jnp.sum(pn, axis=-1, keepdims=True)))
        heads = []
        for g in range(NSA_KV_HEADS):
            o_cmp = ocmp_ref[g]
            for hh in range(NSA_GROUP):
                head = g * NSA_GROUP + hh
                hr = slice(hh * tq, (hh + 1) * tq)
                ar = slice(g * rows + hh * tq, g * rows + (hh + 1) * tq)
                cs = slice(g * hd, (g + 1) * hd)
                col = (head // (NSA_HEADS // 2)) * LANES + (head % (NSA_HEADS // 2)) * 3
                heads.append(o_cmp[hr] * gate[:, col:col + 1] + o_slc[ar, cs] * gate[:, col + 1:col + 2]
                             + o_win[ar, cs] * gate[:, col + 2:col + 3])
        o_ref[0] = jnp.concatenate(heads, axis=-1)


SAMPLE_PAGES_PER_STEP = 16


def _attn_sample_call(qn, qr, kvc, gate, slc_new, win_buf_t, win_new, pool_t, layer, page_table, t_real):
    b, tq, nq = qn.shape
    n_pages = page_table.shape[1]
    page = pool_t.shape[-1]
    past_len = n_pages * page
    assert past_len % SEL_BLOCK == 0 and t_real <= SEL_BLOCK and past_len % (CMP_BLOCK * LANES) == 0
    n = SAMPLE_PAGES_PER_STEP
    rows = NSA_GROUP * tq
    full = lambda a: pl.BlockSpec((1,) + a.shape[1:], lambda i, c, pt: (i,) + (0,) * (a.ndim - 1))
    wbuf_spec = pl.BlockSpec((1, 1) + win_buf_t.shape[2:], lambda i, c, pt: (layer, i, 0, 0, 0, 0))
    n_cmp = past_len // CMP_BLOCK
    sel_lanes = n_cmp + LANES
    expand = (jnp.arange(n_cmp)[:, None] == (jnp.arange(past_len) // CMP_BLOCK)[None, :]).astype(BF16)
    return pl.pallas_call(
        functools.partial(_attn_sample_kernel, n, t_real, past_len),
        grid_spec=pltpu.PrefetchScalarGridSpec(
            num_scalar_prefetch=1,
            grid=(b, n_pages // n),
            in_specs=[full(a) for a in (qn, qr, kvc, gate, slc_new)] + [wbuf_spec, full(win_new)]
            + [pl.BlockSpec((n_cmp, n * page), lambda i, c, pt: (0, c))]
            + _page_specs(n, layer, pool_t.shape[2:]),
            out_specs=pl.BlockSpec((1, tq, nq), lambda i, c, pt: (i, 0, 0)),
            scratch_shapes=[pltpu.VMEM((NSA_KV_HEADS, tq, sel_lanes), F32),
                            pltpu.VMEM((NSA_KV_HEADS, rows, NSA_HEAD_DIM), F32),
                            pltpu.VMEM((NSA_KV_HEADS * rows, 1), F32),
                            pltpu.VMEM((NSA_KV_HEADS * rows, 1), F32),
                            pltpu.VMEM((NSA_KV_HEADS * rows, NSA_KV_WIDTH), F32),
                            pltpu.VMEM((NSA_KV_HEADS * rows, NSA_KV_WIDTH), BF16),
                            pltpu.VMEM((NSA_KV_HEADS * rows, NSA_KV_WIDTH), BF16)],
        ),
        out_shape=jax.ShapeDtypeStruct((b, tq, nq), F32),
        compiler_params=_cparams(2),
        name="nsa_attn_sample",
    )(page_table, qn, qr, kvc, gate, slc_new, win_buf_t, win_new, expand, *([pool_t] * n))


def _rope_tables(pos):
    half = NSA_HEAD_DIM // 2
    inv = ROPE_THETA ** (-jnp.arange(half, dtype=F32) / half)
    ang = pos.astype(F32)[:, None] * inv[None, :]
    cos, sin = jnp.cos(ang), jnp.sin(ang)
    reps = LANES // NSA_HEAD_DIM
    return jnp.tile(cos, (1, 2 * reps)), jnp.tile(jnp.concatenate([-sin, sin], axis=1), (1, reps))


def _nsa_params(w_in, g_q, g_ks, g_kw, g_kc, pe, w1, w2):
    d = w_in.shape[0]
    body = NSA_HEADS * NSA_HEAD_DIM + 6 * NSA_KV_WIDTH
    half = NSA_HEADS // 2 * 3
    zpad = jnp.zeros((d, LANES - half), w_in.dtype)
    w_pad = jnp.concatenate([w_in[:, :body], w_in[:, body:body + half], zpad, w_in[:, body + half:], zpad], axis=1)
    return dict(
        w_in=w_pad.astype(BF16),
        g_q=jnp.tile(g_q, NSA_HEADS)[None], g_ks=jnp.tile(g_ks, NSA_KV_HEADS)[None],
        g_kw=jnp.tile(g_kw, NSA_KV_HEADS)[None], g_kc=g_kc[None],
        pe=pe.reshape(2, CMP_BLOCK // 2, LANES), w1=w1.astype(BF16), w2=w2.astype(BF16))


def _kv_rows(x, b, t):
    return x.reshape(b, t, 2, NSA_KV_HEADS, NSA_HEAD_DIM)


def _nsa_prompt_layer(h, gmix, prm):
    b, t, d = h.shape
    cos, sin = _rope_tables(jnp.arange(t))
    qn, qr, cmp, slc, wrow, gate = _nsa_proj_call(
        h.reshape(b * t, d), gmix, prm["w_in"], prm["g_q"], prm["g_ks"], prm["g_kw"], cos, sin)
    r3 = lambda x: x.reshape(b, t, x.shape[-1])
    kvc = _compress_rows_call(r3(cmp), prm["pe"], prm["w1"], prm["w2"], prm["g_kc"])
    o = _attn_prompt_call(r3(qn), r3(qr), kvc, r3(slc), r3(wrow), r3(gate))
    keep = min(WINDOW, t)
    return (o.reshape(b * t, -1), _kv_rows(cmp, b, t), _kv_rows(slc, b, t), _kv_rows(wrow, b, t)[:, t - keep:])


def _nsa_sample_layer(h, cmp_pool_t, slc_pool_t, win_buf, win_buf_t, layer, page_table, gmix, prm):
    b, t, d = h.shape
    past_len = page_table.shape[1] * cmp_pool_t.shape[-1]
    cos, sin = _rope_tables(jnp.tile(past_len + jnp.arange(t), b))
    qn, qr, cmp, slc, wrow, gate = _nsa_proj_call(
        h.reshape(b * t, d), gmix, prm["w_in"], prm["g_q"], prm["g_ks"], prm["g_kw"], cos, sin)
    kvc = _compress_pages_call(cmp_pool_t, layer, page_table, prm["pe"], prm["w1"], prm["w2"], prm["g_kc"])
    tq = 8
    pad = lambda x: jnp.pad(x.reshape(b, t, x.shape[-1]), ((0, 0), (0, tq - t), (0, 0)))
    o = _attn_sample_call(pad(qn), pad(qr), kvc, pad(gate), pad(slc), win_buf_t, pad(wrow), slc_pool_t, layer,
                          page_table, t)
    new_win = jnp.concatenate([win_buf, _kv_rows(wrow, b, t)], axis=1)[:, t:]
    return o[:, :t].reshape(b * t, -1), _kv_rows(cmp, b, t), _kv_rows(slc, b, t), new_win


def _softplus(z):
    return jnp.maximum(z, 0.0) + jnp.log1p(jnp.exp(-jnp.abs(z)))


def _rwkv_proj_kernel(batch, h_ref, shift_ref, gmix_ref, mu_ref, wr_ref, wk_ref, wv_ref,
                      w0_ref, w1_ref, w2_ref, a0_ref, a1_ref, a2_ref, g1_ref, g2_ref,
                      r_ref, dec_ref, k_ref, v_ref, a_ref, g_ref, last_ref, carry_ref):
    i = pl.program_id(0)
    u = _rms(h_ref[...]) * gmix_ref[...]
    tm = u.shape[0]

    @pl.when(i == 0)
    def _():
        carry_ref[...] = shift_ref[...]
    prev = jnp.concatenate([carry_ref[...], u[:tm - batch]], axis=0)
    carry_ref[...] = u[tm - batch:]
    last_ref[...] = u[tm - batch:]
    xx = prev - u
    mix = lambda j: (u + xx * mu_ref[j:j + 1]).astype(BF16)
    r_ref[...] = _dot(mix(0), wr_ref[...])
    wl = w0_ref[...] + _dot(jnp.tanh(_dot(mix(1), w1_ref[...])).astype(BF16), w2_ref[...])
    dec_ref[...] = jnp.exp(-jnp.exp(-_softplus(-wl) - 0.5))
    k_ref[...] = _dot(mix(2), wk_ref[...])
    v_ref[...] = _dot(mix(3), wv_ref[...])
    a_ref[...] = jax.nn.sigmoid(a0_ref[...] + _dot(_dot(mix(4), a1_ref[...]).astype(BF16), a2_ref[...]))
    g_ref[...] = _dot(jax.nn.sigmoid(_dot(mix(5), g1_ref[...])).astype(BF16), g2_ref[...])


def _rwkv_proj_call(h, shift, gmix, prm):
    m, d = h.shape
    batch = shift.shape[0]
    tm = min(m, 512)
    assert tm % batch == 0 and tm > batch and batch % 8 == 0
    row = pl.BlockSpec((tm, d), lambda i: (i, 0))
    weights = [prm[n] for n in ("mu", "w_r", "w_k", "w_v", "w0", "w1", "w2", "a0", "a1", "a2", "g1", "g2")]
    outs = pl.pallas_call(
        functools.partial(_rwkv_proj_kernel, batch),
        grid=(m // tm,),
        in_specs=[row, _resident(shift.shape), _resident(gmix.shape)] + [_resident(w.shape) for w in weights],
        out_specs=[row] * 6 + [pl.BlockSpec((batch, d), lambda i: (0, 0))],
        out_shape=[jax.ShapeDtypeStruct((m, d), F32)] * 6 + [jax.ShapeDtypeStruct((batch, d), F32)],
        scratch_shapes=[pltpu.VMEM((batch, d), F32)],
        compiler_params=_cparams(1),
        name="rwkv_proj",
    )(h, shift, gmix, *weights)
    return outs[:6], outs[6]


def _rwkv_scan_kernel(tc, r_ref, dec_ref, k_ref, v_ref, a_ref, kk_ref, ka_ref, rk_ref, lw_ref, lb_ref, s0_ref,
                      y_ref, sout_ref, state_ref, vec_ref, stage_ref, ystage_ref):
    c = pl.program_id(1)
    n = RWKV_HEAD_DIM

    @pl.when(c == 0)
    def _():
        state_ref[...] = s0_ref[...]
        ystage_ref[...] = jnp.zeros_like(ystage_ref)

    def stage_in(t, slot):
        for i, ref in enumerate((r_ref, dec_ref, k_ref, v_ref, a_ref)):
            xa, xb = _rows_to_lanes(ref[t], ref[t + 1])
            stage_ref[i, 2 * slot] = xa
            stage_ref[i, 2 * slot + 1] = xb

    def step(t):
        r = stage_ref[0, t]
        k = stage_ref[2, t]
        v = stage_ref[3, t]
        a = stage_ref[4, t]
        kk = k * kk_ref[...]
        kk = kk / jnp.maximum(jnp.sqrt(jnp.sum(kk * kk, axis=0, keepdims=True)), 1e-12)
        k2 = k * (1.0 + (a - 1.0) * ka_ref[...])
        vec_ref[0] = -kk
        vec_ref[1] = kk * a
        vec_ref[2] = k2
        sa = jnp.zeros((n, LANES), F32)
        for j in range(n):
            sa = sa + state_ref[j] * vec_ref[0, j:j + 1, :]
        y = jnp.zeros((n, LANES), F32)
        for j in range(n):
            s_new = (state_ref[j] * stage_ref[1, t, j:j + 1, :] + sa * vec_ref[1, j:j + 1, :]
                     + v * vec_ref[2, j:j + 1, :])
            state_ref[j] = s_new
            y = y + s_new * stage_ref[0, t, j:j + 1, :]
        mean = jnp.mean(y, axis=0, keepdims=True)
        var = jnp.mean(jnp.square(y - mean), axis=0, keepdims=True)
        y = (y - mean) * lax.rsqrt(var + LNX_EPS) * lw_ref[...] + lb_ref[...]
        ystage_ref[t] = y + jnp.sum(r * k2 * rk_ref[...], axis=0, keepdims=True) * v

    def steps(slot):
        step(2 * slot)
        step(2 * slot + 1)

    def stage_out(slot, t):
        _lanes_to_rows(ystage_ref[2 * slot], ystage_ref[2 * slot + 1], y_ref, t)

    _pipelined_pairs(tc, stage_in, steps, stage_out)

    @pl.when(c == pl.num_programs(1) - 1)
    def _():
        sout_ref[...] = state_ref[...]


def _pipelined_pairs(tc, stage_in, steps, stage_out):
    stage_in(0, 0)

    def body(i, _):
        t = 4 * i
        stage_in(t + 2, 1)
        steps(0)
        stage_out(1, jnp.maximum(t - 2, 0))
        stage_in(jnp.minimum(t + 4, tc - 2), 0)
        steps(1)
        stage_out(0, t)
        return 0

    lax.fori_loop(0, tc // 4, body, 0)
    stage_out(1, tc - 2)


def _rows_to_lanes(xa, xb):
    chunks = xa.shape[1] // LANES
    stack = jnp.concatenate([x[:, c * LANES:(c + 1) * LANES] for x in (xa, xb) for c in range(chunks)], axis=0)
    tr = stack.T
    half = LANES // 2
    top, bot = tr[:half], tr[half:]
    low = _lane_iota() < half
    return (jnp.where(low, top, pltpu.roll(bot, half, axis=1)),
            jnp.where(low, pltpu.roll(top, half, axis=1), bot))


def _lanes_to_rows(ya, yb, out_ref, t):
    half = LANES // 2
    low = _lane_iota() < half
    top = jnp.where(low, ya, pltpu.roll(yb, half, axis=1))
    bot = jnp.where(low, pltpu.roll(ya, half, axis=1), yb)
    tr = jnp.concatenate([top, bot], axis=0).T
    chunks = out_ref.shape[2] // LANES
    for tok in range(2):
        for c in range(chunks):
            r0 = (tok * chunks + c) * 8
            out_ref[t + tok, :, c * LANES:(c + 1) * LANES] = tr[r0:r0 + 8]


def _lane_params(x, heads, width):
    per_chunk = LANES // width if width < LANES else 1
    x = x.reshape(heads // per_chunk, per_chunk, width)
    x = jnp.transpose(x, (2, 1, 0))
    return jnp.repeat(x[..., None], 8, axis=-1).reshape(width, LANES)


def _rwkv_scan_call(r, dec, k, v, a, kk, ka, rk, lw, lb, s0):
    t, b, d = r.shape
    n = RWKV_HEAD_DIM
    assert d == 8 * LANES and b % 8 == 0 and t % 4 == 0
    tc = math.gcd(t, 32)
    seq = pl.BlockSpec((tc, 8, d), lambda g, c: (c, g, 0))
    par = pl.BlockSpec((n, LANES), lambda g, c: (0, 0))
    st = pl.BlockSpec((n, n, LANES), lambda g, c: (0, 0, g))
    return pl.pallas_call(
        functools.partial(_rwkv_scan_kernel, tc),
        grid=(b // 8, t // tc),
        in_specs=[seq] * 5 + [par] * 5 + [st],
        out_specs=[seq, st],
        out_shape=[jax.ShapeDtypeStruct((t, b, d), F32), jax.ShapeDtypeStruct((n, n, b // 8 * LANES), F32)],
        scratch_shapes=[pltpu.VMEM((n, n, LANES), F32), pltpu.VMEM((3, n, LANES), F32),
                        pltpu.VMEM((5, 4, n, LANES), F32), pltpu.VMEM((4, n, LANES), F32)],
        compiler_params=_cparams(2),
        name="rwkv_scan",
    )(r, dec, k, v, a, kk, ka, rk, lw, lb, s0)


def _rwkv_params(mu, w_r, w_k, w_v, w0, w1, w2, a0, a1, a2, g1, g2):
    gpad = -g1.shape[1] % LANES
    return dict(mu=mu, w_r=w_r.astype(BF16), w_k=w_k.astype(BF16), w_v=w_v.astype(BF16), w0=w0[None],
                w1=w1.astype(BF16), w2=w2.astype(BF16), a0=a0[None], a1=a1.astype(BF16), a2=a2.astype(BF16),
                g1=jnp.pad(g1, ((0, 0), (0, gpad))).astype(BF16), g2=jnp.pad(g2, ((0, gpad), (0, 0))).astype(BF16))


def _rwkv_layer(h, batch, shift, s0, gmix, prm, k_k, k_a, r_k, lnx_w, lnx_b):
    m, d = h.shape
    t = m // batch
    n = RWKV_HEAD_DIM
    nh = d // n
    g8 = batch // 8
    (r, dec, k, v, a, g), new_shift = _rwkv_proj_call(h, shift, gmix, prm)
    rows = lambda x: x.reshape(t, batch, d)
    par = lambda x: _lane_params(x.reshape(-1), nh, n)
    st = s0.reshape(g8, 8, nh // 2, 2, n, n).transpose(5, 4, 0, 3, 2, 1).reshape(n, n, g8 * LANES)
    y, s = _rwkv_scan_call(rows(r), rows(dec), rows(k), rows(v), rows(a), par(k_k), par(k_a), par(r_k),
                           par(lnx_w), par(lnx_b), st)
    s = s.reshape(n, n, g8, 2, nh // 2, 8).transpose(2, 5, 4, 3, 1, 0).reshape(batch, nh, n, n)
    return y.reshape(m, d), g, new_shift, s


def _hgrn_proj_kernel(h_ref, gmix_ref, win_ref, lb_ref, omlb_ref, q_ref, f_ref, i_ref, g_ref):
    d = h_ref.shape[1]
    u = (_rms(h_ref[...]) * gmix_ref[...]).astype(BF16)
    z = _dot(u, win_ref[...])
    q, f, g = z[:, :d], z[:, d:2 * d], z[:, 3 * d:]
    q_ref[...] = q * jax.nn.sigmoid(q)
    f_ref[...] = lb_ref[...] + omlb_ref[...] * jax.nn.sigmoid(f)
    i_ref[...] = z[:, 2 * d:3 * d]
    g_ref[...] = g * jax.nn.sigmoid(g)


def _hgrn_proj_call(h, gmix, win, lb, omlb):
    m, d = h.shape
    tm = min(m, 512)
    row = pl.BlockSpec((tm, d), lambda i: (i, 0))
    return pl.pallas_call(
        _hgrn_proj_kernel,
        grid=(m // tm,),
        in_specs=[row, _resident(gmix.shape), _resident(win.shape), _resident(lb.shape), _resident(omlb.shape)],
        out_specs=[row] * 4,
        out_shape=[jax.ShapeDtypeStruct((m, d), F32)] * 4,
        compiler_params=_cparams(1),
        name="hgrn_proj",
    )(h, gmix, win, lb, omlb)


def _rows_to_lanes_dup(xa, xb):
    chunks = xa.shape[1] // LANES
    stack = jnp.concatenate([x[:, c * LANES:(c + 1) * LANES] for x in (xa, xb) for c in range(chunks)], axis=0)
    tr = stack.T
    half = LANES // 2
    low = _lane_iota() < half
    other = pltpu.roll(tr, half, axis=1)
    return jnp.where(low, tr, other), jnp.where(low, other, tr)


def _hgrn_scan_kernel(tc, q_ref, f_ref, v_ref, s0_ref, o_ref, sout_ref, state_ref, keys_ref, vals_ref):
    c = pl.program_id(1)
    dk = state_ref.shape[0]

    @pl.when(c == 0)
    def _():
        state_ref[...] = s0_ref[...]

    def stage_in(tp, _):
        t = 2 * tp
        for i, ref in enumerate((q_ref, f_ref)):
            xa, xb = _rows_to_lanes_dup(ref[t], ref[t + 1])
            keys_ref[i, t] = xa
            keys_ref[i, t + 1] = xb
            if i == 1:
                keys_ref[2, t] = 1.0 - xa
                keys_ref[2, t + 1] = 1.0 - xb
        va, vb = _rows_to_lanes(v_ref[t], v_ref[t + 1])
        vals_ref[t] = va
        vals_ref[t + 1] = vb
        return 0

    lax.fori_loop(0, tc // 2, stage_in, 0)

    def step(t, _):
        v = vals_ref[t]
        o = jnp.zeros(v.shape, F32)
        for d in range(dk):
            s_new = state_ref[d] * keys_ref[1, t, d:d + 1, :] + keys_ref[2, t, d:d + 1, :] * v
            state_ref[d] = s_new
            o = o + s_new * keys_ref[0, t, d:d + 1, :]
        vals_ref[t] = o
        return 0

    lax.fori_loop(0, tc, step, 0)

    def stage_out(tq, _):
        for u in range(2):
            t = 4 * tq + 2 * u
            _lanes_to_rows(vals_ref[t], vals_ref[t + 1], o_ref, t)
        return 0

    lax.fori_loop(0, tc // 4, stage_out, 0)

    @pl.when(c == pl.num_programs(1) - 1)
    def _():
        sout_ref[...] = state_ref[...]


def _hgrn_scan_call(q, f, v, s0):
    t, b, d = q.shape
    dk = HGRN_HEAD_DIM
    dv = dk // 2
    assert d == 8 * LANES and b % 8 == 0 and t % 4 == 0
    tc = math.gcd(t, 32)
    seq = pl.BlockSpec((tc, 8, d), lambda g, c: (c, g, 0))
    st = pl.BlockSpec((dk, dv, LANES), lambda g, c: (0, 0, g))
    return pl.pallas_call(
        functools.partial(_hgrn_scan_kernel, tc),
        grid=(b // 8, t // tc),
        in_specs=[seq] * 3 + [st],
        out_specs=[seq, st],
        out_shape=[jax.ShapeDtypeStruct((t, b, d), F32), jax.ShapeDtypeStruct((dk, dv, b // 8 * LANES), F32)],
        scratch_shapes=[pltpu.VMEM((dk, dv, LANES), F32), pltpu.VMEM((3, tc, dk, LANES), F32),
                        pltpu.VMEM((tc, dv, LANES), F32)],
        compiler_params=_cparams(2),
        name="hgrn_scan",
    )(q, f, v, s0)


def _hgrn_layer(h, batch, s0, gmix, win, lb, omlb):
    m, d = h.shape
    t = m // batch
    n = HGRN_HEAD_DIM
    nh = d // n
    g8 = batch // 8
    q, f, i, g = _hgrn_proj_call(h, gmix, win, lb, omlb)
    rows = lambda x: x.reshape(t, batch, d)
    st = s0.reshape(g8, 8, nh, n, 2, n // 2).transpose(3, 5, 0, 4, 2, 1).reshape(n, n // 2, g8 * LANES)
    o, s = _hgrn_scan_call(rows(q), rows(f), rows(i), st)
    s = s.reshape(n, n // 2, g8, 2, nh, 8).transpose(2, 5, 4, 0, 3, 1).reshape(batch, nh, n, n)
    return o.reshape(m, d), g, s


def kernel(x_prompt, x_sample, cache_cmp, cache_slc, cache_win, state_rwkv_shift, state_rwkv_wkv, state_hgrn,
           page_table, p_prompt, p_sample, norm_mix, norm_ffn, w_up, w_down, w_ple, w_ple_gate,
           nsa_w_in, nsa_g_q, nsa_g_ks, nsa_g_kw, nsa_g_kc, nsa_cmp_pe, nsa_cmp_w1, nsa_cmp_w2, nsa_w_out,
           rwkv_mu, rwkv_w_r, rwkv_w_k, rwkv_w_v, rwkv_w_o, rwkv_w0, rwkv_w1, rwkv_w2, rwkv_a0, rwkv_a1,
           rwkv_a2, rwkv_g1, rwkv_g2, rwkv_k_k, rwkv_k_a, rwkv_r_k, rwkv_lnx_w, rwkv_lnx_b,
           hgrn_w_in, hgrn_gn, hgrn_w_o, hgrn_lower_bounds):
    depth = norm_mix.shape[0]
    bp, tp, d = x_prompt.shape
    bs, ts, _ = x_sample.shape
    lb_soft = jax.nn.softmax(hgrn_lower_bounds.astype(F32), axis=0)
    lower_bound = jnp.cumsum(lb_soft, axis=0) - lb_soft[0]
    rows_minor = lambda x: jnp.transpose(x, (0, 1, 3, 4, 5, 2))
    cmp_pool_t, slc_pool_t, win_buf_t = rows_minor(cache_cmp), rows_minor(cache_slc), rows_minor(cache_win)
    hp, hs = x_prompt.reshape(bp * tp, d), x_sample.reshape(bs * ts, d)
    swap = lambda x, a, b: x.reshape(a, b, x.shape[-1]).transpose(1, 0, 2).reshape(a * b, x.shape[-1])
    time_major = False
    outs = [[] for _ in range(12)]
    for i in range(depth):
        kind, n = i % N_MIXERS, i // N_MIXERS
        gmix = norm_mix[i][None]
        ffn = (p_prompt[i].reshape(bp * tp, -1), p_sample[i].reshape(bs * ts, -1))
        if (kind != 0) != time_major:
            hp = swap(hp, tp, bp) if time_major else swap(hp, bp, tp)
            hs = swap(hs, ts, bs) if time_major else swap(hs, bs, ts)
            time_major = not time_major
        if time_major:
            ffn = (swap(ffn[0], bp, tp), swap(ffn[1], bs, ts))
        hp3, hs3 = hp.reshape(bp, tp, d), hs.reshape(bs, ts, d)
        tail = (norm_ffn[i][None], w_up[i].astype(BF16), w_down[i].astype(BF16), w_ple[i].astype(BF16),
                w_ple_gate[i].astype(BF16))
        if kind == 0:
            prm = _nsa_params(nsa_w_in[n], nsa_g_q[n], nsa_g_ks[n], nsa_g_kw[n], nsa_g_kc[n], nsa_cmp_pe[n],
                              nsa_cmp_w1[n], nsa_cmp_w2[n])
            op, rc, rs, rw = _nsa_prompt_layer(hp3, gmix, prm)
            os_, nc_rows, ns_rows, nw_buf = _nsa_sample_layer(hs3, cmp_pool_t, slc_pool_t, cache_win[n], win_buf_t,
                                                              n, page_table, gmix, prm)
            for lst, v in zip(outs[:6], (rc, nc_rows, rs, ns_rows, rw, nw_buf)):
                lst.append(v)
            wo = nsa_w_out[n].astype(BF16)
            hp = _ffn_call("nsa", hp, op, None, None, ffn[0], wo, *tail)
            hs = _ffn_call("nsa", hs, os_, None, None, ffn[1], wo, *tail)
        elif kind == 1:
            prm = _rwkv_params(rwkv_mu[n], rwkv_w_r[n], rwkv_w_k[n], rwkv_w_v[n], rwkv_w0[n], rwkv_w1[n],
                               rwkv_w2[n], rwkv_a0[n], rwkv_a1[n], rwkv_a2[n], rwkv_g1[n], rwkv_g2[n])
            vecs = (rwkv_k_k[n], rwkv_k_a[n], rwkv_r_k[n], rwkv_lnx_w[n], rwkv_lnx_b[n])
            nh = d // RWKV_HEAD_DIM
            zero_state = jnp.zeros((bp, nh, RWKV_HEAD_DIM, RWKV_HEAD_DIM), F32)
            yp, gp, shp, sp = _rwkv_layer(hp, bp, jnp.zeros((bp, d), F32), zero_state, gmix, prm, *vecs)
            ys, gs, shs, ss = _rwkv_layer(hs, bs, state_rwkv_shift[n], state_rwkv_wkv[n].astype(F32), gmix, prm,
                                          *vecs)
            for lst, v in zip(outs[6:10], (shp, shs, sp, ss)):
                lst.append(v)
            wo = rwkv_w_o[n].astype(BF16)
            hp = _ffn_call("rwkv", hp, yp, gp, None, ffn[0], wo, *tail)
            hs = _ffn_call("rwkv", hs, ys, gs, None, ffn[1], wo, *tail)
        else:
            lb = lower_bound[i][None]
            win = hgrn_w_in[n].astype(BF16)
            nh = d // HGRN_HEAD_DIM
            zero_state = jnp.zeros((bp, nh, HGRN_HEAD_DIM, HGRN_HEAD_DIM), F32)
            op, gp, sp = _hgrn_layer(hp, bp, zero_state, gmix, win, lb, 1.0 - lb)
            os_, gs, ss = _hgrn_layer(hs, bs, state_hgrn[n].astype(F32), gmix, win, lb, 1.0 - lb)
            outs[10].append(sp)
            outs[11].append(ss)
            wo, gn = hgrn_w_o[n].astype(BF16), hgrn_gn[n][None]
            hp = _ffn_call("hgrn", hp, op, gp, gn, ffn[0], wo, *tail)
            hs = _ffn_call("hgrn", hs, os_, gs, gn, ffn[1], wo, *tail)
    if time_major:
        hp, hs = swap(hp, tp, bp), swap(hs, ts, bs)
    return (hp.reshape(bp, tp, d), hs.reshape(bs, ts, d)) + tuple(jnp.stack(o) for o in outs)
```

```python
import functools
import math

import jax
import jax.numpy as jnp
from jax import lax
from jax.experimental import pallas as pl
from jax.experimental.pallas import tpu as pltpu

F32 = jnp.float32
BF16 = jnp.bfloat16

NORM_EPS = 1e-6
ROPE_THETA = 10000.0
NEG_INF = -1e30
M_INIT = -1e29
N_MIXERS = 3

NSA_HEADS = 16
NSA_KV_HEADS = 4
NSA_HEAD_DIM = 64
NSA_GROUP = NSA_HEADS // NSA_KV_HEADS
NSA_KV_WIDTH = NSA_KV_HEADS * NSA_HEAD_DIM
NSA_SCALE = NSA_HEAD_DIM ** -0.5
LOG2E = math.log2(math.e)
CMP_BLOCK = 32
SEL_BLOCK = 64
N_SELECT = 16
WINDOW = 512
FORCE_SCORE = 1e4

RWKV_HEAD_DIM = 64
LNX_EPS = 64e-5
HGRN_HEAD_DIM = 128

LANES = 128
VMEM_LIMIT = 56 * 1024 * 1024


def _cparams(n_axes):
    return pltpu.CompilerParams(dimension_semantics=("arbitrary",) * n_axes,
                                vmem_limit_bytes=VMEM_LIMIT)


def _resident(shape):
    zeros = (0,) * len(shape)
    return pl.BlockSpec(shape, lambda *_: zeros, pipeline_mode=pl.Buffered(1))


def _rms(x):
    return x * lax.rsqrt(jnp.mean(x * x, axis=-1, keepdims=True) + NORM_EPS)


def _dot(a, b):
    return jnp.dot(a, b, preferred_element_type=F32)


def _dot_nt(a, b):
    return lax.dot_general(a, b, (((1,), (1,)), ((), ())), preferred_element_type=F32)


def _per_chunk(fn, x, *rest):
    n = x.shape[-1] // LANES
    outs = [fn(*(a[:, c * LANES:(c + 1) * LANES] for a in (x,) + rest)) for c in range(n)]
    return outs[0] if n == 1 else jnp.concatenate(outs, axis=-1)


def _lane_iota():
    return lax.broadcasted_iota(jnp.int32, (1, LANES), 1)


def _group_ones(group):
    r = lax.broadcasted_iota(jnp.int32, (LANES, LANES), 0) // group
    c = lax.broadcasted_iota(jnp.int32, (LANES, LANES), 1) // group
    return jnp.where(r == c, 1.0, 0.0).astype(BF16)


def _head_rms(x, head_dim, ones):
    def one(c):
        sq = c * c
        hi = sq.astype(BF16)
        lo = (sq - hi.astype(F32)).astype(BF16)
        ss = _dot(hi, ones) + _dot(lo, ones)
        return c * lax.rsqrt(ss * (1.0 / head_dim) + NORM_EPS)
    return _per_chunk(one, x)


def _rope(x, cos, sin_signed):
    half = NSA_HEAD_DIM // 2
    lane = _lane_iota()

    def one(c):
        up = pltpu.roll(c, LANES - half, axis=1)
        dn = pltpu.roll(c, half, axis=1)
        rot = jnp.where((lane & half) == 0, up, dn)
        return c * cos + rot * sin_signed
    return _per_chunk(one, x)


def _ffn_kernel(mode, ff_chunk, *refs):
    if mode == "nsa":
        h_ref, o_ref, p_ref, wo_ref, gffn_ref, wup_ref, wdown_ref, wple_ref, wgate_ref, out_ref = refs
        o = o_ref[...]
    elif mode == "rwkv":
        h_ref, o_ref, aux_ref, p_ref, wo_ref, gffn_ref, wup_ref, wdown_ref, wple_ref, wgate_ref, out_ref = refs
        o = o_ref[...] * aux_ref[...]
    else:
        (h_ref, o_ref, aux_ref, gn_ref, p_ref, wo_ref, gffn_ref, wup_ref, wdown_ref, wple_ref, wgate_ref,
         out_ref) = refs
        gn = gn_ref[...]
        o = _per_chunk(lambda c: _rms(c) * gn, o_ref[...]) * aux_ref[...]
    h1 = h_ref[...] + _dot(o.astype(BF16), wo_ref[...])
    u = (_rms(h1) * gffn_ref[...]).astype(BF16)
    d_ff = wup_ref.shape[1]
    acc = jnp.zeros_like(h1)
    for j in range(d_ff // ff_chunk):
        a = _dot(u, wup_ref[:, j * ff_chunk:(j + 1) * ff_chunk])
        a = jnp.square(jnp.maximum(a, 0.0)).astype(BF16)
        acc = acc + _dot(a, wdown_ref[j * ff_chunk:(j + 1) * ff_chunk, :])
    h2 = h1 + acc
    gate = jax.nn.sigmoid(_dot(_rms(h2).astype(BF16), wgate_ref[...]))
    out_ref[...] = h2 + _dot(p_ref[...].astype(BF16), wple_ref[...]) * gate


def _ffn_call(mode, h, o, aux, gn, p, wo, gffn, wup, wdown, wple, wgate):
    m, d = h.shape
    tm = min(m, 512)
    row = lambda w: pl.BlockSpec((tm, w), lambda i: (i, 0))
    args, specs = [h, o], [row(d), row(d)]
    if mode != "nsa":
        args.append(aux)
        specs.append(row(d))
    if mode == "hgrn":
        args.append(gn)
        specs.append(_resident(gn.shape))
    args += [p, wo, gffn, wup, wdown, wple, wgate]
    specs += [row(p.shape[1])] + [_resident(a.shape) for a in (wo, gffn, wup, wdown, wple, wgate)]
    return pl.pallas_call(
        functools.partial(_ffn_kernel, mode, 1024),
        grid=(m // tm,),
        in_specs=specs,
        out_specs=row(d),
        out_shape=jax.ShapeDtypeStruct((m, d), F32),
        compiler_params=_cparams(1),
        name="ffn_" + mode,
    )(*args)


def _nsa_proj_kernel(h_ref, gmix_ref, win_ref, gq_ref, gks_ref, gkw_ref, cos_ref, sin_ref,
                     qn_ref, qr_ref, cmp_ref, slc_ref, wrow_ref, gate_ref):
    kvw = NSA_KV_WIDTH
    nq = NSA_HEADS * NSA_HEAD_DIM
    u = (_rms(h_ref[...]) * gmix_ref[...]).astype(BF16)
    z = _dot(u, win_ref[...])
    cos = cos_ref[...]
    sin = sin_ref[...]
    ones = _group_ones(NSA_HEAD_DIM)
    qn = _head_rms(z[:, :nq], NSA_HEAD_DIM, ones) * gq_ref[...]
    qn_ref[...] = qn
    qr_ref[...] = _rope(qn, cos, sin)
    cmp_ref[...] = z[:, nq:nq + 2 * kvw]
    o = nq + 2 * kvw
    slc_ref[:, :kvw] = _rope(_head_rms(z[:, o:o + kvw], NSA_HEAD_DIM, ones) * gks_ref[...], cos, sin)
    slc_ref[:, kvw:] = z[:, o + kvw:o + 2 * kvw]
    o += 2 * kvw
    wrow_ref[:, :kvw] = _rope(_head_rms(z[:, o:o + kvw], NSA_HEAD_DIM, ones) * gkw_ref[...], cos, sin)
    wrow_ref[:, kvw:] = z[:, o + kvw:o + 2 * kvw]
    o += 2 * kvw
    gate_ref[...] = jax.nn.sigmoid(z[:, o:])


def _nsa_proj_call(h, gmix, win, gq, gks, gkw, cos, sin):
    m, d = h.shape
    tm = min(m, 512)
    tab_tiles = cos.shape[0] // tm
    row = lambda w: pl.BlockSpec((tm, w), lambda i: (i, 0))
    tab = pl.BlockSpec((tm, LANES), lambda i: (i % tab_tiles, 0))
    nq = NSA_HEADS * NSA_HEAD_DIM
    widths = [nq, nq] + [2 * NSA_KV_WIDTH] * 3 + [2 * LANES]
    return pl.pallas_call(
        _nsa_proj_kernel,
        grid=(m // tm,),
        in_specs=[row(d), _resident(gmix.shape), _resident(win.shape), _resident(gq.shape),
                  _resident(gks.shape), _resident(gkw.shape), tab, tab],
        out_specs=[row(w) for w in widths],
        out_shape=[jax.ShapeDtypeStruct((m, w), F32) for w in widths],
        compiler_params=_cparams(1),
        name="nsa_proj",
    )(h, gmix, win, gq, gks, gkw, cos, sin)


def _compress_slot(load_pair, m, e, pe_ref, w1_ref, w2_ref, gkc_ref, flat_ref):
    lane = _lane_iota()
    low = lane < NSA_HEAD_DIM
    for s in range(CMP_BLOCK // 2):
        pe_row = pe_ref[e, s:s + 1, :]
        parts = []
        for kp in range(NSA_KV_HEADS // 2):
            a = load_pair(kp, 2 * s)
            b = load_pair(kp, 2 * s + 1)
            parts.append(jnp.where(low, a, pltpu.roll(b, NSA_HEAD_DIM, axis=1)))
            parts.append(jnp.where(low, pltpu.roll(a, NSA_HEAD_DIM, axis=1), b))
        flat_ref[:, s * LANES:(s + 1) * LANES] = (jnp.concatenate(parts, axis=0) + pe_row).astype(BF16)
    out = _dot(jax.nn.gelu(_dot(flat_ref[...], w1_ref[e])).astype(BF16), w2_ref[e])
    if e == 0:
        out = _rms(out) * gkc_ref[...]
    return jnp.concatenate([out[k * m:(k + 1) * m] for k in range(NSA_KV_HEADS)], axis=-1)


def _compress_rows_kernel(*refs):
    cols = refs[:NSA_KV_HEADS]
    pe_ref, w1_ref, w2_ref, gkc_ref, out_ref, flat_ref = refs[NSA_KV_HEADS:]
    m = cols[0].shape[1] // CMP_BLOCK
    pad = out_ref.shape[2] - m
    for e in range(2):
        load = lambda kp, r: cols[2 * e + kp][0, pl.ds(r, m, stride=CMP_BLOCK), :]
        res = _compress_slot(load, m, e, pe_ref, w1_ref, w2_ref, gkc_ref, flat_ref)
        out_ref[0, e] = jnp.concatenate([res, jnp.zeros((pad, res.shape[1]), F32)], axis=0)


def _compress_rows_call(cmp, pe, w1, w2, gkc):
    b, t, w = cmp.shape
    c = -(-(t // CMP_BLOCK) // LANES) * LANES
    return pl.pallas_call(
        _compress_rows_kernel,
        grid=(b,),
        in_specs=[pl.BlockSpec((1, t, LANES), functools.partial(lambda j, i: (i, 0, j), j)) for j in range(w // LANES)]
        + [_resident(a.shape) for a in (pe, w1, w2, gkc)],
        out_specs=pl.BlockSpec((1, 2, c, NSA_KV_WIDTH), lambda i: (i, 0, 0, 0)),
        out_shape=jax.ShapeDtypeStruct((b, 2, c, NSA_KV_WIDTH), F32),
        scratch_shapes=[pltpu.VMEM((NSA_KV_HEADS * (t // CMP_BLOCK), w1.shape[1]), BF16)],
        compiler_params=_cparams(1),
        name="nsa_compress_rows",
    )(*([cmp] * (w // LANES)), pe, w1, w2, gkc)


PAGES_PER_STEP = 32


def _page_specs(n, layer, tail):
    zeros = (0,) * len(tail)
    return [pl.BlockSpec((1, 1) + tail,
                         functools.partial(lambda i, b, c, pt: (layer, pt[b, c * n + i]) + zeros, i))
            for i in range(n)]


def _compress_pages_kernel(pt_ref, *refs):
    n = len(refs) - 7
    pages = refs[:n]
    pe_ref, w1_ref, w2_ref, gkc_ref, out_ref, rows_ref, flat_ref = refs[n:]
    page = pages[0].shape[-1]
    m = n * page // CMP_BLOCK
    for e in range(2):
        for i, pg in enumerate(pages):
            for kp in range(NSA_KV_HEADS // 2):
                tile = pg[0, 0, e, 2 * kp:2 * kp + 2].reshape(LANES, page)
                rows_ref[e, kp, i * page:(i + 1) * page, :] = tile.T
    for e in range(2):
        load = lambda kp, r: rows_ref[e, kp, pl.ds(r, m, stride=CMP_BLOCK), :]
        out_ref[0, e] = _compress_slot(load, m, e, pe_ref, w1_ref, w2_ref, gkc_ref, flat_ref.at[e])


def _compress_pages_call(pool_t, layer, page_table, pe, w1, w2, gkc):
    b, n_pages = page_table.shape
    page = pool_t.shape[-1]
    n = PAGES_PER_STEP
    m = n * page // CMP_BLOCK
    return pl.pallas_call(
        _compress_pages_kernel,
        grid_spec=pltpu.PrefetchScalarGridSpec(
            num_scalar_prefetch=1,
            grid=(b, n_pages // n),
            in_specs=_page_specs(n, layer, pool_t.shape[2:])
            + [pl.BlockSpec(a.shape, functools.partial(lambda nd, i, c, pt: (0,) * nd, a.ndim),
                            pipeline_mode=pl.Buffered(1)) for a in (pe, w1, w2, gkc)],
            out_specs=pl.BlockSpec((1, 2, m, NSA_KV_WIDTH), lambda i, c, pt: (i, 0, c, 0)),
            scratch_shapes=[pltpu.VMEM((2, NSA_KV_HEADS // 2, n * page, LANES), F32),
                            pltpu.VMEM((2, NSA_KV_HEADS * m, w1.shape[1]), BF16)],
        ),
        out_shape=jax.ShapeDtypeStruct((b, 2, n_pages * page // CMP_BLOCK, NSA_KV_WIDTH), F32),
        compiler_params=_cparams(2),
        name="nsa_compress_pages",
    )(page_table, *([pool_t] * n), pe, w1, w2, gkc)


def _select_blocks_t(imp, tpos):
    n_blocks = imp.shape[0]
    blk = lax.broadcasted_iota(jnp.int32, (n_blocks, 1), 0)
    cur = tpos // SEL_BLOCK
    forced = (blk == 0) | (blk == cur) | (blk == cur - 1)
    score = jnp.where(blk <= cur, jnp.where(forced, FORCE_SCORE, imp), -1.0)
    rank = jnp.zeros(score.shape, F32)
    for j in range(n_blocks):
        row = score[j:j + 1, :]
        below = jnp.where(j < blk, 1.0, 0.0)
        rank = rank + jnp.where(row > score, 1.0, jnp.where(row == score, below, 0.0))
    return jnp.where(rank < float(min(N_SELECT, n_blocks)), 1.0, 0.0)


def _attn_prompt_kernel(tq, tk, qn_ref, qr_ref, kvc_ref, sk_ref, sv_ref, wk_ref, wv_ref, gate_ref, o_ref,
                        kb_ref, vt_ref, sel_ref, imp_ref):
    qi = pl.program_id(2)
    hd = NSA_HEAD_DIM
    t_len = sk_ref.shape[1]
    c_blocks = kvc_ref.shape[2]
    n_sel = t_len // SEL_BLOCK
    ratio = SEL_BLOCK // CMP_BLOCK
    rows_per_tile = tk // SEL_BLOCK

    @pl.when(qi == 0)
    def _():
        for br, (k_ref, v_ref) in enumerate(((sk_ref, sv_ref), (wk_ref, wv_ref))):
            for gl in range(2):
                kb_ref[br, gl] = k_ref[0, :, gl * hd:(gl + 1) * hd].astype(BF16)
            for kt in range(t_len // tk):
                vt_ref[br, kt] = v_ref[0, kt * tk:(kt + 1) * tk, :].T.astype(BF16)

    t0 = qi * tq
    tpos = t0 + lax.broadcasted_iota(jnp.int32, (1, tq), 1)
    gate_t = gate_ref[0].T
    qn_t = [qn_ref[0, :, c * LANES:(c + 1) * LANES].T for c in range(2 * NSA_GROUP * hd // LANES)]
    qr_t = [qr_ref[0, :, c * LANES:(c + 1) * LANES].T for c in range(2 * NSA_GROUP * hd // LANES)]
    cid = lax.broadcasted_iota(jnp.int32, (c_blocks, 1), 0)
    ok = jnp.where((cid + 1) * CMP_BLOCK - 1 <= tpos, 1.0, 0.0)
    ok4 = jnp.concatenate([ok] * NSA_GROUP, axis=1)
    qr4s, o_cmps = [], []
    for gl in range(2):
        per_head = lambda parts: [parts[2 * gl + hh // 2][(hh % 2) * hd:(hh % 2 + 1) * hd] for hh in range(NSA_GROUP)]
        qn4 = (jnp.concatenate(per_head(qn_t), axis=1) * NSA_SCALE).astype(BF16)
        qr4s.append((jnp.concatenate(per_head(qr_t), axis=1) * (NSA_SCALE * LOG2E)).astype(BF16))
        kc = kvc_ref[0, 0][:, gl * hd:(gl + 1) * hd]
        vc_t = kvc_ref[0, 1].T[gl * hd:(gl + 1) * hd]
        s = jnp.where(ok4 > 0.5, _dot(kc.astype(BF16), qn4), NEG_INF)
        e = jnp.exp(s - jnp.max(s, axis=0, keepdims=True))
        p = e / jnp.sum(e, axis=0, keepdims=True) * ok4
        o_cmps.append(_dot(vc_t.astype(BF16), p.astype(BF16)))
        imp = p[:, :tq]
        for hh in range(1, NSA_GROUP):
            imp = imp + p[:, hh * tq:(hh + 1) * tq]
        parts = []
        for j in range(tq // LANES):
            imp_ref[j] = imp[:, j * LANES:(j + 1) * LANES]
            part = imp_ref[j, pl.ds(0, n_sel, stride=ratio), :]
            for i in range(1, ratio):
                part = part + imp_ref[j, pl.ds(i, n_sel, stride=ratio), :]
            parts.append(part)
        imp_sel = parts[0] if len(parts) == 1 else jnp.concatenate(parts, axis=1)
        sel_ref[gl] = _select_blocks_t(imp_sel, tpos)

    def flash(br, lo, hi, valid_fn):
        def body(kt, carry):
            k0 = pl.multiple_of(kt * tk, tk)
            out = []
            for gl in range(2):
                m, l, acc = carry[gl]
                kb = kb_ref[br, gl, pl.ds(k0, tk), :]
                vt = vt_ref[br, kt, gl * hd:(gl + 1) * hd, :]
                bias = jnp.where(valid_fn(gl, kt, k0), 0.0, NEG_INF)
                s = _dot(kb, qr4s[gl]) + jnp.concatenate([bias] * NSA_GROUP, axis=1)
                m_new = jnp.maximum(m, jnp.max(s, axis=0, keepdims=True))
                alpha = jnp.exp2(m - m_new)
                p = jnp.exp2(s - m_new)
                out.append((m_new, alpha * l + jnp.sum(p, axis=0, keepdims=True),
                            alpha * acc + _dot(vt, p.astype(BF16))))
            return tuple(out)
        cols = NSA_GROUP * tq
        init = tuple((jnp.full((1, cols), M_INIT, F32), jnp.zeros((1, cols), F32), jnp.zeros((hd, cols), F32))
                     for _ in range(2))
        res = lax.fori_loop(lo, hi, body, init)
        return [acc / l for _, l, acc in res]

    def slc_valid(gl, kt, k0):
        picked = sel_ref[gl, pl.ds(pl.multiple_of(kt * rows_per_tile, rows_per_tile), rows_per_tile), :]
        picked = jnp.concatenate([jnp.broadcast_to(picked[i:i + 1], (SEL_BLOCK, tq))
                                  for i in range(rows_per_tile)], axis=0)
        kpos = k0 + lax.broadcasted_iota(jnp.int32, (tk, 1), 0)
        return (picked > 0.5) & (kpos <= tpos)

    def win_valid(gl, kt, k0):
        dist = tpos - (k0 + lax.broadcasted_iota(jnp.int32, (tk, 1), 0))
        return (dist >= 0) & (dist <= WINDOW)

    hi = (t0 + tq - 1) // tk + 1
    slc = flash(0, 0, hi, slc_valid)
    win = flash(1, jnp.maximum(t0 - WINDOW, 0) // tk, hi, win_valid)
    heads = []
    for gl in range(2):
        for hh in range(NSA_GROUP):
            c = (gl * NSA_GROUP + hh) * 3
            cols = slice(hh * tq, (hh + 1) * tq)
            heads.append(o_cmps[gl][:, cols] * gate_t[c:c + 1] + slc[gl][:, cols] * gate_t[c + 1:c + 2]
                         + win[gl][:, cols] * gate_t[c + 2:c + 3])
    for c in range(len(heads) // 2):
        o_ref[0, :, c * LANES:(c + 1) * LANES] = jnp.concatenate(heads[2 * c:2 * c + 2], axis=0).T


def _attn_prompt_call(qn, qr, kvc, slc, win, gate, tq=512, tk=512):
    b, t, nq = qn.shape
    pair_w = 2 * NSA_GROUP * NSA_HEAD_DIM
    c = kvc.shape[2]
    assert tk % (8 * SEL_BLOCK) == 0 and t % tk == 0 and c % LANES == 0 and tq % LANES == 0 and t % tq == 0
    qspec = pl.BlockSpec((1, tq, pair_w), lambda i, p, j: (i, j, p))
    kspec = pl.BlockSpec((1, t, LANES), lambda i, p, j: (i, 0, p))
    vspec = pl.BlockSpec((1, t, LANES), lambda i, p, j: (i, 0, 2 + p))
    return pl.pallas_call(
        functools.partial(_attn_prompt_kernel, tq, tk),
        grid=(b, 2, t // tq),
        in_specs=[qspec, qspec,
                  pl.BlockSpec((1, 2, c, LANES), lambda i, p, j: (i, 0, 0, p)),
                  kspec, vspec, kspec, vspec,
                  pl.BlockSpec((1, tq, LANES), lambda i, p, j: (i, j, p))],
        out_specs=qspec,
        out_shape=jax.ShapeDtypeStruct((b, t, nq), F32),
        scratch_shapes=[pltpu.VMEM((2, 2, t, NSA_HEAD_DIM), BF16),
                        pltpu.VMEM((2, t // tk, LANES, tk), BF16),
                        pltpu.VMEM((2, t // SEL_BLOCK, tq), F32),
                        pltpu.VMEM((tq // LANES, c, LANES), F32)],
        compiler_params=_cparams(3),
        name="nsa_attn_prompt",
    )(qn, qr, kvc, slc, slc, win, win, gate)


def _select_blocks(imp, tpos, n_blocks):
    c = imp.shape[1]
    lane = lax.broadcasted_iota(jnp.int32, (1, c), 1)

    def pair(x):
        lane1 = _lane_iota()
        return x + jnp.where((lane1 & 1) == 0, pltpu.roll(x, LANES - 1, axis=1), pltpu.roll(x, 1, axis=1))
    imp2 = _per_chunk(pair, imp)
    blk = lane // 2
    cur = tpos // SEL_BLOCK
    forced = (blk == 0) | (blk == cur) | (blk == cur - 1)
    score = jnp.where(blk <= cur, jnp.where(forced, FORCE_SCORE, imp2), -1.0)
    rank = jnp.zeros(score.shape, F32)
    for j in range(n_blocks):
        col = score[:, 2 * j:2 * j + 1]
        ahead = (col > score) | ((col == score) & (j < blk))
        rank = rank + ahead.astype(F32)
    return (rank < float(min(N_SELECT, n_blocks))).astype(F32)


def _attn_sample_kernel(n_step, t_real, past_len, pt_ref, qn_ref, qr_ref, kvc_ref, gate_ref, snew_ref, wbuf_ref,
                        wnew_ref, expand_ref, *rest):
    pages = rest[:n_step]
    o_ref, sel_ref, ocmp_ref, m_ref, l_ref, acc_ref, qbd_ref, qnbd_ref = rest[n_step:]
    c = pl.program_id(1)
    hd = NSA_HEAD_DIM
    kvw = NSA_KV_WIDTH
    tq = qn_ref.shape[1]
    n_cmp = past_len // CMP_BLOCK
    tpos = past_len + lax.broadcasted_iota(jnp.int32, (tq, 1), 0) % t_real
    stack = lambda ref, g: jnp.concatenate(
        [ref[0, :, (g * NSA_GROUP + hh) * hd:(g * NSA_GROUP + hh + 1) * hd] for hh in range(NSA_GROUP)], axis=0)
    rows = NSA_GROUP * tq

    tpos_all = jnp.concatenate([tpos] * NSA_KV_HEADS, axis=0)
    per_head = lambda x: jnp.concatenate(
        [x[g * tq:(g + 1) * tq] for g in range(NSA_KV_HEADS) for _ in range(NSA_GROUP)], axis=0)

    @pl.when(c == 0)
    def _():
        n_blocks = -(-(past_len + t_real) // SEL_BLOCK)
        zero = jnp.zeros((rows, hd), BF16)
        for g in range(NSA_KV_HEADS):
            for src, dst in ((qn_ref, qnbd_ref), (qr_ref, qbd_ref)):
                q4 = (stack(src, g) * NSA_SCALE).astype(BF16)
                dst[g * rows:(g + 1) * rows, :] = jnp.concatenate(
                    [q4 if gg == g else zero for gg in range(NSA_KV_HEADS)], axis=1)
        cidx = lax.broadcasted_iota(jnp.int32, (1, n_cmp), 1)
        ok = per_head(jnp.where((cidx + 1) * CMP_BLOCK - 1 <= tpos_all, 1.0, 0.0))
        s = jnp.where(ok > 0.5, _dot_nt(qnbd_ref[...], kvc_ref[0, 0].astype(BF16)), NEG_INF)
        e = jnp.exp(s - jnp.max(s, axis=-1, keepdims=True))
        p = e / jnp.sum(e, axis=-1, keepdims=True) * ok
        o_all = _dot(p.astype(BF16), kvc_ref[0, 1].astype(BF16))
        imps = []
        for g in range(NSA_KV_HEADS):
            ocmp_ref[g] = o_all[g * rows:(g + 1) * rows, g * hd:(g + 1) * hd]
            imp = p[g * rows:g * rows + tq]
            for hh in range(1, NSA_GROUP):
                imp = imp + p[g * rows + hh * tq:g * rows + (hh + 1) * tq]
            imps.append(imp)
        imp_all = jnp.concatenate(imps, axis=0)
        imp_all = jnp.concatenate([imp_all, jnp.zeros((imp_all.shape[0], LANES), F32)], axis=-1)
        sel_all = _select_blocks(imp_all, tpos_all, n_blocks)
        for g in range(NSA_KV_HEADS):
            sel_ref[g] = sel_all[g * tq:(g + 1) * tq]
        m_ref[...] = jnp.full(m_ref.shape, M_INIT, F32)
        l_ref[...] = jnp.zeros(l_ref.shape, F32)
        acc_ref[...] = jnp.zeros(acc_ref.shape, F32)

    tk = n_step * pages[0].shape[-1]
    k0 = c * tk
    kpos = k0 + lax.broadcasted_iota(jnp.int32, (1, tk), 1)
    kt_all = jnp.concatenate([pg[0, 0, 0].reshape(kvw, -1) for pg in pages], axis=1).astype(BF16)
    vt_all = jnp.concatenate([pg[0, 0, 1].reshape(kvw, -1) for pg in pages], axis=1).astype(BF16)
    sel_all = jnp.concatenate([sel_ref[g][:, :n_cmp] for g in range(NSA_KV_HEADS)], axis=0).astype(BF16)
    picked = _dot(sel_all, expand_ref[...])
    bias = per_head(jnp.where((picked > 0.5) & (kpos <= tpos_all), 0.0, NEG_INF))
    s = _dot(qbd_ref[...], kt_all) + bias
    m_old = m_ref[...]
    m_new = jnp.maximum(m_old, jnp.max(s, axis=-1, keepdims=True))
    alpha = jnp.exp(m_old - m_new)
    p = jnp.exp(s - m_new)
    m_ref[...] = m_new
    l_ref[...] = alpha * l_ref[...] + jnp.sum(p, axis=-1, keepdims=True)
    acc_ref[...] = alpha * acc_ref[...] + _dot_nt(p.astype(BF16), vt_all)

    @pl.when(c == pl.num_programs(1) - 1)
    def _():
        gate = gate_ref[0]
        rnew = lax.broadcasted_iota(jnp.int32, (1, tq), 1)
        newpos = past_len + rnew
        n_buf = wbuf_ref.shape[-1]
        bpos = past_len - n_buf + lax.broadcasted_iota(jnp.int32, (1, n_buf), 1)
        new_lane = 2 * (past_len // SEL_BLOCK)
        qbd = qbd_ref[...]
        picked = jnp.concatenate([sel_ref[g][:, new_lane:new_lane + 1] for g in range(NSA_KV_HEADS)], axis=0) > 0.5
        bias = per_head(jnp.where(picked & (newpos <= tpos_all) & (rnew < t_real), 0.0, NEG_INF))
        snew = snew_ref[0]
        s = _dot_nt(qbd, snew[:, :kvw].astype(BF16)) + bias
        m_old = m_ref[...]
        m_new = jnp.maximum(m_old, jnp.max(s, axis=-1, keepdims=True))
        alpha = jnp.exp(m_old - m_new)
        p = jnp.exp(s - m_new)
        o_slc = ((alpha * acc_ref[...] + _dot(p.astype(BF16), snew[:, kvw:].astype(BF16)))
                 / (alpha * l_ref[...] + jnp.sum(p, axis=-1, keepdims=True)))
        dist = tpos_all - bpos
        bias_b = per_head(jnp.where((dist >= 0) & (dist <= WINDOW) & (bpos >= 0), 0.0, NEG_INF))
        dist = tpos_all - newpos
        bias_n = per_head(jnp.where((dist >= 0) & (dist <= WINDOW) & (rnew < t_real), 0.0, NEG_INF))
        wnew = wnew_ref[0]
        sb = _dot(qbd, wbuf_ref[0, 0, 0].reshape(kvw, n_buf).astype(BF16)) + bias_b
        sn = _dot_nt(qbd, wnew[:, :kvw].astype(BF16)) + bias_n
        m_w = jnp.maximum(jnp.max(sb, axis=-1, keepdims=True), jnp.max(sn, axis=-1, keepdims=True))
        pb = jnp.exp(sb - m_w)
        pn = jnp.exp(sn - m_w)
        o_win = ((_dot_nt(pb.astype(BF16), wbuf_ref[0, 0, 1].reshape(kvw, n_buf).astype(BF16))
                  + _dot(pn.astype(BF16), wnew[:, kvw:].astype(BF16)))
                 / (jnp.sum(pb, axis=-1, keepdims=True) + jnp.sum(pn, axis=-1, keepdims=True)))
        heads = []
        for g in range(NSA_KV_HEADS):
            o_cmp = ocmp_ref[g]
            for hh in range(NSA_GROUP):
                head = g * NSA_GROUP + hh
                hr = slice(hh * tq, (hh + 1) * tq)
                ar = slice(g * rows + hh * tq, g * rows + (hh + 1) * tq)
                cs = slice(g * hd, (g + 1) * hd)
                col = (head // (NSA_HEADS // 2)) * LANES + (head % (NSA_HEADS // 2)) * 3
                heads.append(o_cmp[hr] * gate[:, col:col + 1] + o_slc[ar, cs] * gate[:, col + 1:col + 2]
                             + o_win[ar, cs] * gate[:, col + 2:col + 3])
        o_ref[0] = jnp.concatenate(heads, axis=-1)


SAMPLE_PAGES_PER_STEP = 32


def _attn_sample_call(qn, qr, kvc, gate, slc_new, win_buf_t, win_new, pool_t, layer, page_table, t_real):
    b, tq, nq = qn.shape
    n_pages = page_table.shape[1]
    page = pool_t.shape[-1]
    past_len = n_pages * page
    assert past_len % SEL_BLOCK == 0 and t_real <= SEL_BLOCK and past_len % (CMP_BLOCK * LANES) == 0
    n = SAMPLE_PAGES_PER_STEP
    rows = NSA_GROUP * tq
    full = lambda a: pl.BlockSpec((1,) + a.shape[1:], lambda i, c, pt: (i,) + (0,) * (a.ndim - 1))
    wbuf_spec = pl.BlockSpec((1, 1) + win_buf_t.shape[2:], lambda i, c, pt: (layer, i, 0, 0, 0, 0))
    n_cmp = past_len // CMP_BLOCK
    sel_lanes = n_cmp + LANES
    expand = (jnp.arange(n_cmp)[:, None] == (jnp.arange(past_len) // CMP_BLOCK)[None, :]).astype(BF16)
    return pl.pallas_call(
        functools.partial(_attn_sample_kernel, n, t_real, past_len),
        grid_spec=pltpu.PrefetchScalarGridSpec(
            num_scalar_prefetch=1,
            grid=(b, n_pages // n),
            in_specs=[full(a) for a in (qn, qr, kvc, gate, slc_new)] + [wbuf_spec, full(win_new)]
            + [pl.BlockSpec((n_cmp, n * page), lambda i, c, pt: (0, c))]
            + _page_specs(n, layer, pool_t.shape[2:]),
            out_specs=pl.BlockSpec((1, tq, nq), lambda i, c, pt: (i, 0, 0)),
            scratch_shapes=[pltpu.VMEM((NSA_KV_HEADS, tq, sel_lanes), F32),
                            pltpu.VMEM((NSA_KV_HEADS, rows, NSA_HEAD_DIM), F32),
                            pltpu.VMEM((NSA_KV_HEADS * rows, 1), F32),
                            pltpu.VMEM((NSA_KV_HEADS * rows, 1), F32),
                            pltpu.VMEM((NSA_KV_HEADS * rows, NSA_KV_WIDTH), F32),
                            pltpu.VMEM((NSA_KV_HEADS * rows, NSA_KV_WIDTH), BF16),
                            pltpu.VMEM((NSA_KV_HEADS * rows, NSA_KV_WIDTH), BF16)],
        ),
        out_shape=jax.ShapeDtypeStruct((b, tq, nq), F32),
        compiler_params=_cparams(2),
        name="nsa_attn_sample",
    )(page_table, qn, qr, kvc, gate, slc_new, win_buf_t, win_new, expand, *([pool_t] * n))


def _rope_tables(pos):
    half = NSA_HEAD_DIM // 2
    inv = ROPE_THETA ** (-jnp.arange(half, dtype=F32) / half)
    ang = pos.astype(F32)[:, None] * inv[None, :]
    cos, sin = jnp.cos(ang), jnp.sin(ang)
    reps = LANES // NSA_HEAD_DIM
    return jnp.tile(cos, (1, 2 * reps)), jnp.tile(jnp.concatenate([-sin, sin], axis=1), (1, reps))


def _nsa_params(w_in, g_q, g_ks, g_kw, g_kc, pe, w1, w2):
    d = w_in.shape[0]
    body = NSA_HEADS * NSA_HEAD_DIM + 6 * NSA_KV_WIDTH
    half = NSA_HEADS // 2 * 3
    zpad = jnp.zeros((d, LANES - half), w_in.dtype)
    w_pad = jnp.concatenate([w_in[:, :body], w_in[:, body:body + half], zpad, w_in[:, body + half:], zpad], axis=1)
    return dict(
        w_in=w_pad.astype(BF16),
        g_q=jnp.tile(g_q, NSA_HEADS)[None], g_ks=jnp.tile(g_ks, NSA_KV_HEADS)[None],
        g_kw=jnp.tile(g_kw, NSA_KV_HEADS)[None], g_kc=g_kc[None],
        pe=pe.reshape(2, CMP_BLOCK // 2, LANES), w1=w1.astype(BF16), w2=w2.astype(BF16))


def _kv_rows(x, b, t):
    return x.reshape(b, t, 2, NSA_KV_HEADS, NSA_HEAD_DIM)


def _nsa_prompt_layer(h, gmix, prm):
    b, t, d = h.shape
    cos, sin = _rope_tables(jnp.arange(t))
    qn, qr, cmp, slc, wrow, gate = _nsa_proj_call(
        h.reshape(b * t, d), gmix, prm["w_in"], prm["g_q"], prm["g_ks"], prm["g_kw"], cos, sin)
    r3 = lambda x: x.reshape(b, t, x.shape[-1])
    kvc = _compress_rows_call(r3(cmp), prm["pe"], prm["w1"], prm["w2"], prm["g_kc"])
    o = _attn_prompt_call(r3(qn), r3(qr), kvc, r3(slc), r3(wrow), r3(gate))
    keep = min(WINDOW, t)
    return (o.reshape(b * t, -1), _kv_rows(cmp, b, t), _kv_rows(slc, b, t), _kv_rows(wrow, b, t)[:, t - keep:])


def _nsa_sample_layer(h, cmp_pool_t, slc_pool_t, win_buf, win_buf_t, layer, page_table, gmix, prm):
    b, t, d = h.shape
    past_len = page_table.shape[1] * cmp_pool_t.shape[-1]
    cos, sin = _rope_tables(jnp.tile(past_len + jnp.arange(t), b))
    qn, qr, cmp, slc, wrow, gate = _nsa_proj_call(
        h.reshape(b * t, d), gmix, prm["w_in"], prm["g_q"], prm["g_ks"], prm["g_kw"], cos, sin)
    kvc = _compress_pages_call(cmp_pool_t, layer, page_table, prm["pe"], prm["w1"], prm["w2"], prm["g_kc"])
    tq = 8
    pad = lambda x: jnp.pad(x.reshape(b, t, x.shape[-1]), ((0, 0), (0, tq - t), (0, 0)))
    o = _attn_sample_call(pad(qn), pad(qr), kvc, pad(gate), pad(slc), win_buf_t, pad(wrow), slc_pool_t, layer,
                          page_table, t)
    new_win = jnp.concatenate([win_buf, _kv_rows(wrow, b, t)], axis=1)[:, t:]
    return o[:, :t].reshape(b * t, -1), _kv_rows(cmp, b, t), _kv_rows(slc, b, t), new_win


def _softplus(z):
    return jnp.maximum(z, 0.0) + jnp.log1p(jnp.exp(-jnp.abs(z)))


def _rwkv_proj_kernel(batch, h_ref, shift_ref, gmix_ref, mu_ref, wr_ref, wk_ref, wv_ref,
                      w0_ref, w1_ref, w2_ref, a0_ref, a1_ref, a2_ref, g1_ref, g2_ref,
                      r_ref, dec_ref, k_ref, v_ref, a_ref, g_ref, last_ref, carry_ref):
    i = pl.program_id(0)
    u = _rms(h_ref[...]) * gmix_ref[...]
    tm = u.shape[0]

    @pl.when(i == 0)
    def _():
        carry_ref[...] = shift_ref[...]
    prev = jnp.concatenate([carry_ref[...], u[:tm - batch]], axis=0)
    carry_ref[...] = u[tm - batch:]
    last_ref[...] = u[tm - batch:]
    xx = prev - u
    mix = lambda j: (u + xx * mu_ref[j:j + 1]).astype(BF16)
    r_ref[...] = _dot(mix(0), wr_ref[...])
    wl = w0_ref[...] + _dot(jnp.tanh(_dot(mix(1), w1_ref[...])).astype(BF16), w2_ref[...])
    dec_ref[...] = jnp.exp(-jnp.exp(-_softplus(-wl) - 0.5))
    k_ref[...] = _dot(mix(2), wk_ref[...])
    v_ref[...] = _dot(mix(3), wv_ref[...])
    a_ref[...] = jax.nn.sigmoid(a0_ref[...] + _dot(_dot(mix(4), a1_ref[...]).astype(BF16), a2_ref[...]))
    g_ref[...] = _dot(jax.nn.sigmoid(_dot(mix(5), g1_ref[...])).astype(BF16), g2_ref[...])


def _rwkv_proj_call(h, shift, gmix, prm):
    m, d = h.shape
    batch = shift.shape[0]
    tm = min(m, 512)
    assert tm % batch == 0 and tm > batch and batch % 8 == 0
    row = pl.BlockSpec((tm, d), lambda i: (i, 0))
    weights = [prm[n] for n in ("mu", "w_r", "w_k", "w_v", "w0", "w1", "w2", "a0", "a1", "a2", "g1", "g2")]
    outs = pl.pallas_call(
        functools.partial(_rwkv_proj_kernel, batch),
        grid=(m // tm,),
        in_specs=[row, _resident(shift.shape), _resident(gmix.shape)] + [_resident(w.shape) for w in weights],
        out_specs=[row] * 6 + [pl.BlockSpec((batch, d), lambda i: (0, 0))],
        out_shape=[jax.ShapeDtypeStruct((m, d), F32)] * 6 + [jax.ShapeDtypeStruct((batch, d), F32)],
        scratch_shapes=[pltpu.VMEM((batch, d), F32)],
        compiler_params=_cparams(1),
        name="rwkv_proj",
    )(h, shift, gmix, *weights)
    return outs[:6], outs[6]


def _rwkv_scan_kernel(tc, r_ref, dec_ref, k_ref, v_ref, a_ref, kk_ref, ka_ref, rk_ref, lw_ref, lb_ref, s0_ref,
                      y_ref, sout_ref, state_ref, vec_ref, stage_ref, ystage_ref):
    c = pl.program_id(1)
    n = RWKV_HEAD_DIM

    @pl.when(c == 0)
    def _():
        state_ref[...] = s0_ref[...]
        ystage_ref[...] = jnp.zeros_like(ystage_ref)

    def stage_in(t, slot):
        for i, ref in enumerate((r_ref, dec_ref, k_ref, v_ref, a_ref)):
            xa, xb = _rows_to_lanes(ref[t], ref[t + 1])
            stage_ref[i, 2 * slot] = xa
            stage_ref[i, 2 * slot + 1] = xb

    def step(t):
        r = stage_ref[0, t]
        k = stage_ref[2, t]
        v = stage_ref[3, t]
        a = stage_ref[4, t]
        kk = k * kk_ref[...]
        kk = kk / jnp.maximum(jnp.sqrt(jnp.sum(kk * kk, axis=0, keepdims=True)), 1e-12)
        k2 = k * (1.0 + (a - 1.0) * ka_ref[...])
        vec_ref[0] = -kk
        vec_ref[1] = kk * a
        vec_ref[2] = k2
        sa = jnp.zeros((n, LANES), F32)
        for j in range(n):
            sa = sa + state_ref[j] * vec_ref[0, j:j + 1, :]
        y = jnp.zeros((n, LANES), F32)
        for j in range(n):
            s_new = (state_ref[j] * stage_ref[1, t, j:j + 1, :] + sa * vec_ref[1, j:j + 1, :]
                     + v * vec_ref[2, j:j + 1, :])
            state_ref[j] = s_new
            y = y + s_new * stage_ref[0, t, j:j + 1, :]
        mean = jnp.mean(y, axis=0, keepdims=True)
        var = jnp.mean(jnp.square(y - mean), axis=0, keepdims=True)
        y = (y - mean) * lax.rsqrt(var + LNX_EPS) * lw_ref[...] + lb_ref[...]
        ystage_ref[t] = y + jnp.sum(r * k2 * rk_ref[...], axis=0, keepdims=True) * v

    def steps(slot):
        step(2 * slot)
        step(2 * slot + 1)

    def stage_out(slot, t):
        _lanes_to_rows(ystage_ref[2 * slot], ystage_ref[2 * slot + 1], y_ref, t)

    _pipelined_pairs(tc, stage_in, steps, stage_out)

    @pl.when(c == pl.num_programs(1) - 1)
    def _():
        sout_ref[...] = state_ref[...]


def _pipelined_pairs(tc, stage_in, steps, stage_out):
    stage_in(0, 0)

    def body(i, _):
        t = 4 * i
        stage_in(t + 2, 1)
        steps(0)
        stage_out(1, jnp.maximum(t - 2, 0))
        stage_in(jnp.minimum(t + 4, tc - 2), 0)
        steps(1)
        stage_out(0, t)
        return 0

    lax.fori_loop(0, tc // 4, body, 0)
    stage_out(1, tc - 2)


def _rows_to_lanes(xa, xb):
    chunks = xa.shape[1] // LANES
    stack = jnp.concatenate([x[:, c * LANES:(c + 1) * LANES] for x in (xa, xb) for c in range(chunks)], axis=0)
    tr = stack.T
    half = LANES // 2
    top, bot = tr[:half], tr[half:]
    low = _lane_iota() < half
    return (jnp.where(low, top, pltpu.roll(bot, half, axis=1)),
            jnp.where(low, pltpu.roll(top, half, axis=1), bot))


def _lanes_to_rows(ya, yb, out_ref, t):
    half = LANES // 2
    low = _lane_iota() < half
    top = jnp.where(low, ya, pltpu.roll(yb, half, axis=1))
    bot = jnp.where(low, pltpu.roll(ya, half, axis=1), yb)
    tr = jnp.concatenate([top, bot], axis=0).T
    chunks = out_ref.shape[2] // LANES
    for tok in range(2):
        for c in range(chunks):
            r0 = (tok * chunks + c) * 8
            out_ref[t + tok, :, c * LANES:(c + 1) * LANES] = tr[r0:r0 + 8]


def _lane_params(x, heads, width):
    per_chunk = LANES // width if width < LANES else 1
    x = x.reshape(heads // per_chunk, per_chunk, width)
    x = jnp.transpose(x, (2, 1, 0))
    return jnp.repeat(x[..., None], 8, axis=-1).reshape(width, LANES)


def _rwkv_scan_call(r, dec, k, v, a, kk, ka, rk, lw, lb, s0):
    t, b, d = r.shape
    n = RWKV_HEAD_DIM
    assert d == 8 * LANES and b % 8 == 0 and t % 4 == 0
    tc = math.gcd(t, 32)
    seq = pl.BlockSpec((tc, 8, d), lambda g, c: (c, g, 0))
    par = pl.BlockSpec((n, LANES), lambda g, c: (0, 0))
    st = pl.BlockSpec((n, n, LANES), lambda g, c: (0, 0, g))
    return pl.pallas_call(
        functools.partial(_rwkv_scan_kernel, tc),
        grid=(b // 8, t // tc),
        in_specs=[seq] * 5 + [par] * 5 + [st],
        out_specs=[seq, st],
        out_shape=[jax.ShapeDtypeStruct((t, b, d), F32), jax.ShapeDtypeStruct((n, n, b // 8 * LANES), F32)],
        scratch_shapes=[pltpu.VMEM((n, n, LANES), F32), pltpu.VMEM((3, n, LANES), F32),
                        pltpu.VMEM((5, 4, n, LANES), F32), pltpu.VMEM((4, n, LANES), F32)],
        compiler_params=_cparams(2),
        name="rwkv_scan",
    )(r, dec, k, v, a, kk, ka, rk, lw, lb, s0)


def _rwkv_params(mu, w_r, w_k, w_v, w0, w1, w2, a0, a1, a2, g1, g2):
    gpad = -g1.shape[1] % LANES
    return dict(mu=mu, w_r=w_r.astype(BF16), w_k=w_k.astype(BF16), w_v=w_v.astype(BF16), w0=w0[None],
                w1=w1.astype(BF16), w2=w2.astype(BF16), a0=a0[None], a1=a1.astype(BF16), a2=a2.astype(BF16),
                g1=jnp.pad(g1, ((0, 0), (0, gpad))).astype(BF16), g2=jnp.pad(g2, ((0, gpad), (0, 0))).astype(BF16))


def _rwkv_layer(h, batch, shift, s0, gmix, prm, k_k, k_a, r_k, lnx_w, lnx_b):
    m, d = h.shape
    t = m // batch
    n = RWKV_HEAD_DIM
    nh = d // n
    g8 = batch // 8
    (r, dec, k, v, a, g), new_shift = _rwkv_proj_call(h, shift, gmix, prm)
    rows = lambda x: x.reshape(t, batch, d)
    par = lambda x: _lane_params(x.reshape(-1), nh, n)
    st = s0.reshape(g8, 8, nh // 2, 2, n, n).transpose(5, 4, 0, 3, 2, 1).reshape(n, n, g8 * LANES)
    y, s = _rwkv_scan_call(rows(r), rows(dec), rows(k), rows(v), rows(a), par(k_k), par(k_a), par(r_k),
                           par(lnx_w), par(lnx_b), st)
    s = s.reshape(n, n, g8, 2, nh // 2, 8).transpose(2, 5, 4, 3, 1, 0).reshape(batch, nh, n, n)
    return y.reshape(m, d), g, new_shift, s


def _hgrn_proj_kernel(h_ref, gmix_ref, win_ref, lb_ref, omlb_ref, q_ref, f_ref, i_ref, g_ref):
    d = h_ref.shape[1]
    u = (_rms(h_ref[...]) * gmix_ref[...]).astype(BF16)
    z = _dot(u, win_ref[...])
    q, f, g = z[:, :d], z[:, d:2 * d], z[:, 3 * d:]
    q_ref[...] = q * jax.nn.sigmoid(q)
    f_ref[...] = lb_ref[...] + omlb_ref[...] * jax.nn.sigmoid(f)
    i_ref[...] = z[:, 2 * d:3 * d]
    g_ref[...] = g * jax.nn.sigmoid(g)


def _hgrn_proj_call(h, gmix, win, lb, omlb):
    m, d = h.shape
    tm = min(m, 512)
    row = pl.BlockSpec((tm, d), lambda i: (i, 0))
    return pl.pallas_call(
        _hgrn_proj_kernel,
        grid=(m // tm,),
        in_specs=[row, _resident(gmix.shape), _resident(win.shape), _resident(lb.shape), _resident(omlb.shape)],
        out_specs=[row] * 4,
        out_shape=[jax.ShapeDtypeStruct((m, d), F32)] * 4,
        compiler_params=_cparams(1),
        name="hgrn_proj",
    )(h, gmix, win, lb, omlb)


def _rows_to_lanes_dup(xa, xb):
    chunks = xa.shape[1] // LANES
    stack = jnp.concatenate([x[:, c * LANES:(c + 1) * LANES] for x in (xa, xb) for c in range(chunks)], axis=0)
    tr = stack.T
    half = LANES // 2
    low = _lane_iota() < half
    other = pltpu.roll(tr, half, axis=1)
    return jnp.where(low, tr, other), jnp.where(low, other, tr)


def _hgrn_scan_kernel(tc, q_ref, f_ref, v_ref, s0_ref, o_ref, sout_ref, state_ref, keys_ref, vals_ref):
    c = pl.program_id(1)
    dk = state_ref.shape[0]

    @pl.when(c == 0)
    def _():
        state_ref[...] = s0_ref[...]

    def stage_in(tp, _):
        t = 2 * tp
        for i, ref in enumerate((q_ref, f_ref)):
            xa, xb = _rows_to_lanes_dup(ref[t], ref[t + 1])
            keys_ref[i, t] = xa
            keys_ref[i, t + 1] = xb
            if i == 1:
                keys_ref[2, t] = 1.0 - xa
                keys_ref[2, t + 1] = 1.0 - xb
        va, vb = _rows_to_lanes(v_ref[t], v_ref[t + 1])
        vals_ref[t] = va
        vals_ref[t + 1] = vb
        return 0

    lax.fori_loop(0, tc // 2, stage_in, 0)

    def step(t, _):
        v = vals_ref[t]
        o = jnp.zeros(v.shape, F32)
        for d in range(dk):
            s_new = state_ref[d] * keys_ref[1, t, d:d + 1, :] + keys_ref[2, t, d:d + 1, :] * v
            state_ref[d] = s_new
            o = o + s_new * keys_ref[0, t, d:d + 1, :]
        vals_ref[t] = o
        return 0

    lax.fori_loop(0, tc, step, 0)

    def stage_out(tq, _):
        for u in range(2):
            t = 4 * tq + 2 * u
            _lanes_to_rows(vals_ref[t], vals_ref[t + 1], o_ref, t)
        return 0

    lax.fori_loop(0, tc // 4, stage_out, 0)

    @pl.when(c == pl.num_programs(1) - 1)
    def _():
        sout_ref[...] = state_ref[...]


def _hgrn_scan_call(q, f, v, s0):
    t, b, d = q.shape
    dk = HGRN_HEAD_DIM
    dv = dk // 2
    assert d == 8 * LANES and b % 8 == 0 and t % 4 == 0
    tc = math.gcd(t, 32)
    seq = pl.BlockSpec((tc, 8, d), lambda g, c: (c, g, 0))
    st = pl.BlockSpec((dk, dv, LANES), lambda g, c: (0, 0, g))
    return pl.pallas_call(
        functools.partial(_hgrn_scan_kernel, tc),
        grid=(b // 8, t // tc),
        in_specs=[seq] * 3 + [st],
        out_specs=[seq, st],
        out_shape=[jax.ShapeDtypeStruct((t, b, d), F32), jax.ShapeDtypeStruct((dk, dv, b // 8 * LANES), F32)],
        scratch_shapes=[pltpu.VMEM((dk, dv, LANES), F32), pltpu.VMEM((3, tc, dk, LANES), F32),
                        pltpu.VMEM((tc, dv, LANES), F32)],
        compiler_params=_cparams(2),
        name="hgrn_scan",
    )(q, f, v, s0)


def _hgrn_layer(h, batch, s0, gmix, win, lb, omlb):
    m, d = h.shape
    t = m // batch
    n = HGRN_HEAD_DIM
    nh = d // n
    g8 = batch // 8
    q, f, i, g = _hgrn_proj_call(h, gmix, win, lb, omlb)
    rows = lambda x: x.reshape(t, batch, d)
    st = s0.reshape(g8, 8, nh, n, 2, n // 2).transpose(3, 5, 0, 4, 2, 1).reshape(n, n // 2, g8 * LANES)
    o, s = _hgrn_scan_call(rows(q), rows(f), rows(i), st)
    s = s.reshape(n, n // 2, g8, 2, nh, 8).transpose(2, 5, 4, 0, 3, 1).reshape(batch, nh, n, n)
    return o.reshape(m, d), g, s


def kernel(x_prompt, x_sample, cache_cmp, cache_slc, cache_win, state_rwkv_shift, state_rwkv_wkv, state_hgrn,
           page_table, p_prompt, p_sample, norm_mix, norm_ffn, w_up, w_down, w_ple, w_ple_gate,
           nsa_w_in, nsa_g_q, nsa_g_ks, nsa_g_kw, nsa_g_kc, nsa_cmp_pe, nsa_cmp_w1, nsa_cmp_w2, nsa_w_out,
           rwkv_mu, rwkv_w_r, rwkv_w_k, rwkv_w_v, rwkv_w_o, rwkv_w0, rwkv_w1, rwkv_w2, rwkv_a0, rwkv_a1,
           rwkv_a2, rwkv_g1, rwkv_g2, rwkv_k_k, rwkv_k_a, rwkv_r_k, rwkv_lnx_w, rwkv_lnx_b,
           hgrn_w_in, hgrn_gn, hgrn_w_o, hgrn_lower_bounds):
    depth = norm_mix.shape[0]
    bp, tp, d = x_prompt.shape
    bs, ts, _ = x_sample.shape
    lb_soft = jax.nn.softmax(hgrn_lower_bounds.astype(F32), axis=0)
    lower_bound = jnp.cumsum(lb_soft, axis=0) - lb_soft[0]
    rows_minor = lambda x: jnp.transpose(x, (0, 1, 3, 4, 5, 2))
    cmp_pool_t, slc_pool_t, win_buf_t = rows_minor(cache_cmp), rows_minor(cache_slc), rows_minor(cache_win)
    hp, hs = x_prompt.reshape(bp * tp, d), x_sample.reshape(bs * ts, d)
    swap = lambda x, a, b: x.reshape(a, b, x.shape[-1]).transpose(1, 0, 2).reshape(a * b, x.shape[-1])
    time_major = False
    outs = [[] for _ in range(12)]
    for i in range(depth):
        kind, n = i % N_MIXERS, i // N_MIXERS
        gmix = norm_mix[i][None]
        ffn = (p_prompt[i].reshape(bp * tp, -1), p_sample[i].reshape(bs * ts, -1))
        if (kind != 0) != time_major:
            hp = swap(hp, tp, bp) if time_major else swap(hp, bp, tp)
            hs = swap(hs, ts, bs) if time_major else swap(hs, bs, ts)
            time_major = not time_major
        if time_major:
            ffn = (swap(ffn[0], bp, tp), swap(ffn[1], bs, ts))
        hp3, hs3 = hp.reshape(bp, tp, d), hs.reshape(bs, ts, d)
        tail = (norm_ffn[i][None], w_up[i].astype(BF16), w_down[i].astype(BF16), w_ple[i].astype(BF16),
                w_ple_gate[i].astype(BF16))
        if kind == 0:
            prm = _nsa_params(nsa_w_in[n], nsa_g_q[n], nsa_g_ks[n], nsa_g_kw[n], nsa_g_kc[n], nsa_cmp_pe[n],
                              nsa_cmp_w1[n], nsa_cmp_w2[n])
            op, rc, rs, rw = _nsa_prompt_layer(hp3, gmix, prm)
            os_, nc_rows, ns_rows, nw_buf = _nsa_sample_layer(hs3, cmp_pool_t, slc_pool_t, cache_win[n], win_buf_t,
                                                              n, page_table, gmix, prm)
            for lst, v in zip(outs[:6], (rc, nc_rows, rs, ns_rows, rw, nw_buf)):
                lst.append(v)
            wo = nsa_w_out[n].astype(BF16)
            hp = _ffn_call("nsa", hp, op, None, None, ffn[0], wo, *tail)
            hs = _ffn_call("nsa", hs, os_, None, None, ffn[1], wo, *tail)
        elif kind == 1:
            prm = _rwkv_params(rwkv_mu[n], rwkv_w_r[n], rwkv_w_k[n], rwkv_w_v[n], rwkv_w0[n], rwkv_w1[n],
                               rwkv_w2[n], rwkv_a0[n], rwkv_a1[n], rwkv_a2[n], rwkv_g1[n], rwkv_g2[n])
            vecs = (rwkv_k_k[n], rwkv_k_a[n], rwkv_r_k[n], rwkv_lnx_w[n], rwkv_lnx_b[n])
            nh = d // RWKV_HEAD_DIM
            zero_state = jnp.zeros((bp, nh, RWKV_HEAD_DIM, RWKV_HEAD_DIM), F32)
            yp, gp, shp, sp = _rwkv_layer(hp, bp, jnp.zeros((bp, d), F32), zero_state, gmix, prm, *vecs)
            ys, gs, shs, ss = _rwkv_layer(hs, bs, state_rwkv_shift[n], state_rwkv_wkv[n].astype(F32), gmix, prm,
                                          *vecs)
            for lst, v in zip(outs[6:10], (shp, shs, sp, ss)):
                lst.append(v)
            wo = rwkv_w_o[n].astype(BF16)
            hp = _ffn_call("rwkv", hp, yp, gp, None, ffn[0], wo, *tail)
            hs = _ffn_call("rwkv", hs, ys, gs, None, ffn[1], wo, *tail)
        else:
            lb = lower_bound[i][None]
            win = hgrn_w_in[n].astype(BF16)
            nh = d // HGRN_HEAD_DIM
            zero_state = jnp.zeros((bp, nh, HGRN_HEAD_DIM, HGRN_HEAD_DIM), F32)
            op, gp, sp = _hgrn_layer(hp, bp, zero_state, gmix, win, lb, 1.0 - lb)
            os_, gs, ss = _hgrn_layer(hs, bs, state_hgrn[n].astype(F32), gmix, win, lb, 1.0 - lb)
            outs[10].append(sp)
            outs[11].append(ss)
            wo, gn = hgrn_w_o[n].astype(BF16), hgrn_gn[n][None]
            hp = _ffn_call("hgrn", hp, op, gp, gn, ffn[0], wo, *tail)
            hs = _ffn_call("hgrn", hs, os_, gs, gn, ffn[1], wo, *tail)
    if time_major:
        hp, hs = swap(hp, tp, bp), swap(hs, ts, bs)
    return (hp.reshape(bp, tp, d), hs.reshape(bs, ts, d)) + tuple(jnp.stack(o) for o in outs)
```
